```python
import jax, jax.numpy as jnp
from jax import lax
import numpy as np

D_MODEL = 1024
BATCH = 32
SEQ = 2048
DEPTH = 1
DEC_BATCH = 4
DEC_SEQ = 8192
PAST_LEN = 128

F32 = jnp.float32
RMS_EPS = 1e-6
N_MEM = 256
MLA_HEADS = 8
MLA_NOPE = 64
MLA_ROPE = 32
MLA_QK = MLA_NOPE + MLA_ROPE
MLA_V = 64
Q_LORA = 384
KV_LORA = 256
ROPE_THETA = 10000.0
Q_BLOCK = 128
RW_HEADS = 8
RW_HEAD = 64
RW_DIM = RW_HEADS * RW_HEAD
DECAY_LORA = 64
AAA_LORA = 64
GATE_LORA = 128
RW_COLS = 3 * RW_DIM + 2 * DECAY_LORA + 2 * AAA_LORA + GATE_LORA
LNX_EPS = 64e-5
X_HEADS = 4
X_HEAD = 128
X_DIM = X_HEADS * X_HEAD
N_BRANCH = 3
MLA_COLS = Q_LORA + KV_LORA + MLA_ROPE
IN_COLS = MLA_COLS + RW_COLS + X_DIM + N_BRANCH * D_MODEL
D_FF = 2816

kernel_name = 'hybrid_mla_rwkv7_memxattn_convffn_encoder'


def rms_norm(x, g, eps=RMS_EPS):
    xf = x.astype(F32)
    y = xf * lax.rsqrt(jnp.mean(xf * xf, axis=-1, keepdims=True) + eps)
    return (y * g.astype(F32)).astype(x.dtype)


def shift_prev(x):
    return jnp.pad(x, ((0, 0), (1, 0), (0, 0)))[:, :-1]


def shift_next(x):
    return jnp.pad(x, ((0, 0), (0, 1), (0, 0)))[:, 1:]


def rope(x, T):
    half = MLA_ROPE // 2
    inv = jnp.power(ROPE_THETA, -jnp.arange(half, dtype=F32) / half)
    ang = jnp.arange(T, dtype=F32)[:, None] * inv[None, :]
    cos = jnp.cos(ang)[None, :, None, :]
    sin = jnp.sin(ang)[None, :, None, :]
    x1 = x[..., :half].astype(F32)
    x2 = x[..., half:].astype(F32)
    return jnp.concatenate([x1 * cos - x2 * sin, x2 * cos + x1 * sin], axis=-1).astype(x.dtype)


def blocked_attention(q, k, v, scale):
    B, T, H, Dq = q.shape
    nb = T // Q_BLOCK
    qb = jnp.moveaxis(q.reshape(B, nb, Q_BLOCK, H, Dq), 1, 0)

    def one(qblk):
        s = jnp.einsum('bqhd,bkhd->bhqk', qblk, k).astype(F32) * scale
        p = jax.nn.softmax(s, axis=-1)
        return jnp.einsum('bhqk,bkhd->bqhd', p.astype(v.dtype), v)

    o = lax.map(one, qb)
    return jnp.moveaxis(o, 0, 1).reshape(B, T, H, v.shape[-1])


def wkv7_scan(r, decay, k, v, kk, b, reverse):
    B, T, H, N = r.shape
    xs = tuple(jnp.moveaxis(a, 1, 0) for a in (r, decay, k, v, kk, b))

    def step(S, inp):
        r_t, w_t, k_t, v_t, kk_t, b_t = inp
        sa = jnp.einsum('bhvk,bhk->bhv', S, kk_t)
        S = S * w_t[:, :, None, :] - sa[..., None] * b_t[:, :, None, :] + v_t[..., None] * k_t[:, :, None, :]
        y = jnp.einsum('bhvk,bhk->bhv', S, r_t)
        return S, y

    S0 = jnp.zeros((B, H, N, N), F32)
    _, ys = lax.scan(step, S0, xs, reverse=reverse)
    return jnp.moveaxis(ys, 0, 1)


def rwkv7_direction(r, k, v, kk, wl, al, w0, w2, a0, a2, k_a, r_k, reverse):
    B, T, _ = r.shape
    hs = lambda t: t.reshape(B, T, RW_HEADS, RW_HEAD)
    w = -jax.nn.softplus(-(w0.astype(F32) + jnp.tanh(wl) @ w2.astype(F32))) - 0.5
    decay = jnp.exp(-jnp.exp(w))
    a = jax.nn.sigmoid(a0.astype(F32) + al @ a2.astype(F32))
    kd = k * (1.0 + (a - 1.0) * k_a.astype(F32))
    rh, kdh, vh = hs(r), hs(kd), hs(v)
    y = wkv7_scan(rh, hs(decay), kdh, vh, kk, kk * hs(a), reverse)
    bonus = jnp.sum(rh * kdh * r_k.astype(F32), axis=-1, keepdims=True) * vh
    return y + bonus


def encoder_layer(x, mem, p, l):
    B, T, _ = x.shape
    h = rms_norm(x, p['norm_mix_g'][l])
    proj = h @ p['w_in'][l]
    idx = [int(i) for i in np.cumsum([Q_LORA, KV_LORA, MLA_ROPE, RW_COLS, X_DIM])]
    c_q, c_kv, k_r, rw, xq, gate_logits = jnp.split(proj, idx, axis=-1)

    H = MLA_HEADS
    q = (rms_norm(c_q, p['q_norm_g'][l]) @ p['w_uq'][l]).reshape(B, T, H, MLA_QK)
    kv = (rms_norm(c_kv, p['kv_norm_g'][l]) @ p['w_ukv'][l]).reshape(B, T, H, MLA_NOPE + MLA_V)
    k_nope, v_a = kv[..., :MLA_NOPE], kv[..., MLA_NOPE:]
    k = jnp.concatenate([k_nope, jnp.broadcast_to(k_r[:, :, None, :], (B, T, H, MLA_ROPE))], axis=-1)
    q = rms_norm(q, p['mla_qn_g'][l])
    k = rms_norm(k, p['mla_kn_g'][l])
    q = jnp.concatenate([q[..., :MLA_NOPE], rope(q[..., MLA_NOPE:], T)], axis=-1)
    k = jnp.concatenate([k[..., :MLA_NOPE], rope(k[..., MLA_NOPE:], T)], axis=-1)
    o_a = blocked_attention(q, k, v_a, MLA_QK ** -0.5).reshape(B, T, H * MLA_V) @ p['w_o_a'][l]

    rwf = rw.astype(F32)
    rwf = rwf + p['mu_prev'][l].astype(F32) * (shift_prev(rwf) - rwf) + p['mu_next'][l].astype(F32) * (shift_next(rwf) - rwf)
    idx2 = [int(i) for i in np.cumsum([RW_DIM, RW_DIM, RW_DIM, DECAY_LORA, DECAY_LORA, AAA_LORA, AAA_LORA])]
    r7, k7, v7, wl_f, wl_b, al_f, al_b, gl = jnp.split(rwf, idx2, axis=-1)
    kk = (k7 * p['k_k'][l].astype(F32)).reshape(B, T, RW_HEADS, RW_HEAD)
    kk = kk / jnp.maximum(jnp.sqrt(jnp.sum(kk * kk, axis=-1, keepdims=True)), 1e-12)
    y_f = rwkv7_direction(r7, k7, v7, kk, wl_f, al_f, p['w0_f'][l], p['w2_f'][l], p['a0_f'][l], p['a2_f'][l], p['k_a'][l], p['r_k'][l], False)
    y_b = rwkv7_direction(r7, k7, v7, kk, wl_b, al_b, p['w0_b'][l], p['w2_b'][l], p['a0_b'][l], p['a2_b'][l], p['k_a'][l], p['r_k'][l], True)
    y7 = y_f + y_b
    mu = jnp.mean(y7, axis=-1, keepdims=True)
    var = jnp.mean(jnp.square(y7 - mu), axis=-1, keepdims=True)
    y7 = ((y7 - mu) * lax.rsqrt(var + LNX_EPS)).reshape(B, T, RW_DIM)
    y7 = y7 * p['lnx_g'][l].astype(F32) + p['lnx_b'][l].astype(F32)
    g7 = jax.nn.sigmoid(gl) @ p['g2'][l].astype(F32)
    o_b = (y7 * g7).astype(x.dtype) @ p['w_o_b'][l]

    m = rms_norm(mem, p['mem_norm_g'][l])
    mkv = (m @ p['w_mkv'][l]).reshape(B, N_MEM, X_HEADS, 2 * X_HEAD)
    mk, mv = mkv[..., :X_HEAD], mkv[..., X_HEAD:]
    xqh = rms_norm(xq.reshape(B, T, X_HEADS, X_HEAD), p['x_qn_g'][l])
    mk = rms_norm(mk, p['x_kn_g'][l])
    s = jnp.einsum('bqhd,bkhd->bhqk', xqh, mk).astype(F32) * (X_HEAD ** -0.5)
    pr = jax.nn.softmax(s, axis=-1)
    o_c = jnp.einsum('bhqk,bkhd->bqhd', pr.astype(mv.dtype), mv).reshape(B, T, X_DIM) @ p['w_o_c'][l]

    gates = jax.nn.sigmoid(gate_logits.astype(F32)).reshape(B, T, N_BRANCH, D_MODEL).astype(x.dtype)
    merged = gates[:, :, 0] * o_a + gates[:, :, 1] * o_b + gates[:, :, 2] * o_c
    x = x + merged @ p['w_out'][l]

    h2 = rms_norm(x, p['norm_ffn_g'][l])
    up = h2 @ p['w_up'][l]
    u_gate, u_val = up[..., :D_FF], up[..., D_FF:]
    cw = p['conv_w'][l]
    c = cw[0] * shift_prev(u_gate) + cw[1] * u_gate + cw[2] * shift_next(u_gate) + p['conv_b'][l]
    act = jax.nn.gelu(c, approximate=False) * u_val
    return x + act @ p['w_down'][l]


def setup_inputs(seed: int = 0) -> dict:
    key = jax.random.key(seed)
    ks = iter(jax.random.split(key, 64))

    def nrm(shape, scale):
        return jax.random.normal(next(ks), shape, F32) * scale

    def gain(shape):
        return 1.0 + nrm(shape, 0.1)

    L = DEPTH
    return {
        'x_prompt': nrm((BATCH, SEQ, D_MODEL), 1.0),
        'x_sample': nrm((DEC_BATCH, DEC_SEQ, D_MODEL), 1.0),
        'mem_prompt': nrm((BATCH, N_MEM, D_MODEL), 1.0),
        'mem_sample': nrm((DEC_BATCH, N_MEM, D_MODEL), 1.0),
        'norm_mix_g': gain((L, D_MODEL)),
        'w_in': nrm((L, D_MODEL, IN_COLS), D_MODEL ** -0.5),
        'q_norm_g': gain((L, Q_LORA)),
        'w_uq': nrm((L, Q_LORA, MLA_HEADS * MLA_QK), Q_LORA ** -0.5),
        'kv_norm_g': gain((L, KV_LORA)),
        'w_ukv': nrm((L, KV_LORA, MLA_HEADS * (MLA_NOPE + MLA_V)), KV_LORA ** -0.5),
        'mla_qn_g': gain((L, MLA_QK)),
        'mla_kn_g': gain((L, MLA_QK)),
        'w_o_a': nrm((L, MLA_HEADS * MLA_V, D_MODEL), (MLA_HEADS * MLA_V) ** -0.5),
        'mu_prev': 0.3 + nrm((L, RW_COLS), 0.1),
        'mu_next': 0.3 + nrm((L, RW_COLS), 0.1),
        'w0_f': -2.0 + nrm((L, RW_DIM), 1.0),
        'w2_f': nrm((L, DECAY_LORA, RW_DIM), 0.5 * DECAY_LORA ** -0.5),
        'a0_f': nrm((L, RW_DIM), 0.1),
        'a2_f': nrm((L, AAA_LORA, RW_DIM), 0.5 * AAA_LORA ** -0.5),
        'w0_b': -2.0 + nrm((L, RW_DIM), 1.0),
        'w2_b': nrm((L, DECAY_LORA, RW_DIM), 0.5 * DECAY_LORA ** -0.5),
        'a0_b': nrm((L, RW_DIM), 0.1),
        'a2_b': nrm((L, AAA_LORA, RW_DIM), 0.5 * AAA_LORA ** -0.5),
        'g2': nrm((L, GATE_LORA, RW_DIM), GATE_LORA ** -0.5),
        'k_k': 0.85 + nrm((L, RW_DIM), 0.05),
        'k_a': 1.0 + nrm((L, RW_DIM), 0.1),
        'r_k': nrm((L, RW_HEADS, RW_HEAD), 0.1),
        'lnx_g': gain((L, RW_DIM)),
        'lnx_b': nrm((L, RW_DIM), 0.02),
        'w_o_b': nrm((L, RW_DIM, D_MODEL), RW_DIM ** -0.5),
        'mem_norm_g': gain((L, D_MODEL)),
        'w_mkv': nrm((L, D_MODEL, 2 * X_DIM), D_MODEL ** -0.5),
        'x_qn_g': gain((L, X_HEAD)),
        'x_kn_g': gain((L, X_HEAD)),
        'w_o_c': nrm((L, X_DIM, D_MODEL), X_DIM ** -0.5),
        'w_out': nrm((L, D_MODEL, D_MODEL), D_MODEL ** -0.5),
        'norm_ffn_g': gain((L, D_MODEL)),
        'w_up': nrm((L, D_MODEL, 2 * D_FF), D_MODEL ** -0.5),
        'conv_w': nrm((L, 3, D_FF), 3 ** -0.5),
        'conv_b': nrm((L, D_FF), 0.02),
        'w_down': nrm((L, D_FF, D_MODEL), D_FF ** -0.5),
    }


def reference(x_prompt, x_sample, mem_prompt, mem_sample, norm_mix_g, w_in, q_norm_g, w_uq, kv_norm_g, w_ukv,
              mla_qn_g, mla_kn_g, w_o_a, mu_prev, mu_next, w0_f, w2_f, a0_f, a2_f, w0_b, w2_b, a0_b, a2_b,
              g2, k_k, k_a, r_k, lnx_g, lnx_b, w_o_b, mem_norm_g, w_mkv, x_qn_g, x_kn_g, w_o_c, w_out,
              norm_ffn_g, w_up, conv_w, conv_b, w_down):
    p = dict(norm_mix_g=norm_mix_g, w_in=w_in, q_norm_g=q_norm_g, w_uq=w_uq, kv_norm_g=kv_norm_g, w_ukv=w_ukv,
             mla_qn_g=mla_qn_g, mla_kn_g=mla_kn_g, w_o_a=w_o_a, mu_prev=mu_prev, mu_next=mu_next,
             w0_f=w0_f, w2_f=w2_f, a0_f=a0_f, a2_f=a2_f, w0_b=w0_b, w2_b=w2_b, a0_b=a0_b, a2_b=a2_b,
             g2=g2, k_k=k_k, k_a=k_a, r_k=r_k, lnx_g=lnx_g, lnx_b=lnx_b, w_o_b=w_o_b,
             mem_norm_g=mem_norm_g, w_mkv=w_mkv, x_qn_g=x_qn_g, x_kn_g=x_kn_g, w_o_c=w_o_c, w_out=w_out,
             norm_ffn_g=norm_ffn_g, w_up=w_up, conv_w=conv_w, conv_b=conv_b, w_down=w_down)
    y_prompt = x_prompt
    y_sample = x_sample
    for l in range(DEPTH):
        y_prompt = encoder_layer(y_prompt, mem_prompt, p, l)
        y_sample = encoder_layer(y_sample, mem_sample, p, l)
    return (y_prompt, y_sample)
```

```python
import functools

import numpy as np
import jax
import jax.numpy as jnp
from jax import lax
from jax.experimental import pallas as pl
from jax.experimental.pallas import tpu as pltpu

F32 = jnp.float32
BF16 = jnp.bfloat16

D_MODEL = 1024
RMS_EPS = 1e-6
N_MEM = 256
MLA_HEADS = 8
MLA_NOPE = 64
MLA_ROPE = 32
MLA_QK = MLA_NOPE + MLA_ROPE
MLA_V = 64
Q_LORA = 384
KV_LORA = 256
ROPE_THETA = 10000.0
RW_HEADS = 8
RW_HEAD = 64
RW_DIM = RW_HEADS * RW_HEAD
LORA = 64
GATE_LORA = 128
RW_COLS = 3 * RW_DIM + 4 * LORA + GATE_LORA
LNX_EPS = 64e-5
X_HEADS = 4
X_HEAD = 128
X_DIM = X_HEADS * X_HEAD
D_FF = 2816

LANES = 128
SUBLANES = 8
CHUNK = 64
GROUP = 4
GW = GROUP * RW_HEAD
VMEM_LIMIT = 56 * 1024 * 1024


def _cparams(*sem):
    return pltpu.CompilerParams(dimension_semantics=sem, vmem_limit_bytes=VMEM_LIMIT)


def _rms(x, g, eps=RMS_EPS):
    return x * lax.rsqrt(jnp.mean(x * x, axis=-1, keepdims=True) + eps) * g


def _dot(a, b):
    return jnp.dot(a, b, preferred_element_type=F32)


def _dot_nt(a, b):
    return lax.dot_general(a, b, (((1,), (1,)), ((), ())), preferred_element_type=F32)


def _dot_tn(a, b):
    return lax.dot_general(a, b, (((0,), (0,)), ((), ())), preferred_element_type=F32)


def _split2(x):
    hi = x.astype(BF16)
    lo = (x - hi.astype(F32)).astype(BF16)
    return hi, lo


def _dot_0_1(x, e):
    hi, lo = _split2(x)
    return _dot(hi, e) + _dot(lo, e)


def _dot_0_1_l(e, x):
    hi = x.astype(BF16)
    r1 = x - hi.astype(F32)
    mid = r1.astype(BF16)
    lo = (r1 - mid.astype(F32)).astype(BF16)
    return _dot(e, hi) + _dot(e, mid) + _dot(e, lo)


def _sigmoid(z):
    return 1.0 / (1.0 + jnp.exp(-z))


def _full(shape):
    nd = len(shape)
    return pl.BlockSpec(shape, lambda *_: (0,) * nd)


def _mla_prep_kernel(x_ref, g_ref, wc_ref, gq_ref, gkv_ref, wuq_ref, wuk_ref, wuv_ref, gqs_ref, gks_ref,
                     cos_ref, sa_ref, sb_ref, q_out, k_out, v_out):
    h = _rms(x_ref[...], g_ref[...]).astype(BF16)
    c = _dot(h, wc_ref[...])
    cq = _rms(c[:, :Q_LORA], gq_ref[...]).astype(BF16)
    ckv = _rms(c[:, Q_LORA:Q_LORA + KV_LORA], gkv_ref[...]).astype(BF16)
    kr = c[:, Q_LORA + KV_LORA:]
    q = _dot(cq, wuq_ref[...])
    k = _dot(ckv, wuk_ref[...])
    v_out[...] = _dot(ckv, wuv_ref[...]).astype(BF16)
    cos = cos_ref[...]
    sa = sa_ref[...]
    sb = sb_ref[...]
    for hd in range(MLA_HEADS):
        sl = slice(LANES * hd, LANES * (hd + 1))
        for src, gref, out, extra in ((q, gqs_ref, q_out, None), (k, gks_ref, k_out, kr)):
            t = src[:, sl]
            if extra is not None:
                t = t + extra
            ss = jnp.sum(t * t, axis=-1, keepdims=True) * (1.0 / MLA_QK)
            t = t * lax.rsqrt(ss + RMS_EPS) * gref[:, sl]
            t = t * cos + pltpu.roll(t, LANES - MLA_ROPE // 2, 1) * sa + pltpu.roll(t, MLA_ROPE // 2, 1) * sb
            out[:, sl] = t.astype(BF16)


def _mla_prep(x2, T, w, tabs):
    ntok = x2.shape[0]
    tm = min(512, T)
    npos = T // tm
    row = lambda i: (i, 0)
    pos = lambda i: (i % npos, 0)
    return pl.pallas_call(
        _mla_prep_kernel,
        grid=(ntok // tm,),
        in_specs=[pl.BlockSpec((tm, D_MODEL), row), _full((1, D_MODEL)), _full((D_MODEL, 768)),
                  _full((1, Q_LORA)), _full((1, KV_LORA)), _full((Q_LORA, 1024)), _full((KV_LORA, 1024)),
                  _full((KV_LORA, 512)), _full((1, 1024)), _full((1, 1024)),
                  pl.BlockSpec((tm, LANES), pos), pl.BlockSpec((tm, LANES), pos), pl.BlockSpec((tm, LANES), pos)],
        out_specs=[pl.BlockSpec((tm, 1024), row), pl.BlockSpec((tm, 1024), row), pl.BlockSpec((tm, 512), row)],
        out_shape=[jax.ShapeDtypeStruct((ntok, 1024), BF16), jax.ShapeDtypeStruct((ntok, 1024), BF16),
                   jax.ShapeDtypeStruct((ntok, 512), BF16)],
        compiler_params=_cparams("parallel"),
        name="mla_prep",
    )(x2, w["norm_mix_g"], w["w_c"], w["q_norm_g"], w["kv_norm_g"], w["w_uq"], w["w_uk"], w["w_uv"],
      w["gq_slots"], w["gk_slots"], *tabs)


def _flash_kernel(q_ref, k_ref, v_ref, o_ref, m_ref, l_ref, acc_ref, *, tk, nk):
    m_ref[...] = jnp.full(m_ref.shape, -jnp.inf, F32)
    l_ref[...] = jnp.zeros(l_ref.shape, F32)
    acc_ref[...] = jnp.zeros(acc_ref.shape, F32)

    def body(j, carry):
        ks = pl.multiple_of(j * tk, tk)
        kblk = k_ref[pl.ds(ks, tk), :]
        vblk = v_ref[pl.ds(ks, tk), :]
        for h in range(2):
            sl = slice(LANES * h, LANES * (h + 1))
            s = _dot_nt(q_ref[:, sl], kblk[:, sl])
            m_prev = m_ref[h]
            m_new = jnp.maximum(m_prev, jnp.max(s, axis=-1, keepdims=True))
            p = jnp.exp(s - m_new)
            alpha = jnp.exp(m_prev - m_new)
            l_ref[h] = alpha * l_ref[h] + jnp.sum(p, axis=-1, keepdims=True)
            acc_ref[h] = alpha * acc_ref[h] + _dot(p.astype(BF16), vblk)
            m_ref[h] = m_new
        return carry

    lax.fori_loop(0, nk, body, 0)
    lane = lax.broadcasted_iota(jnp.int32, o_ref.shape, 1)
    o = jnp.where(lane < MLA_V, acc_ref[0] / l_ref[0], acc_ref[1] / l_ref[1])
    o_ref[...] = o.astype(BF16)


def _flash(q, k, v, B, T):
    tq = min(512, T)
    tk = min(512, T)
    nq = T // tq
    return pl.pallas_call(
        functools.partial(_flash_kernel, tk=tk, nk=T // tk),
        grid=(B, MLA_HEADS // 2, nq),
        in_specs=[pl.BlockSpec((tq, 2 * LANES), lambda b, hp, i: (b * nq + i, hp)),
                  pl.BlockSpec((T, 2 * LANES), lambda b, hp, i: (b, hp)),
                  pl.BlockSpec((T, LANES), lambda b, hp, i: (b, hp))],
        out_specs=pl.BlockSpec((tq, LANES), lambda b, hp, i: (b * nq + i, hp)),
        out_shape=jax.ShapeDtypeStruct((B * T, MLA_HEADS * MLA_V), BF16),
        scratch_shapes=[pltpu.VMEM((2, tq, 1), F32), pltpu.VMEM((2, tq, 1), F32), pltpu.VMEM((2, tq, LANES), F32)],
        compiler_params=_cparams("parallel", "parallel", "arbitrary"),
        name="mla_flash",
    )(q, k, v)


def _rw_prep_kernel(x_ref, xp_ref, xn_ref, g_ref, wrw_ref, mup_ref, mun_ref, kk_ref, ka_ref, rk_ref,
                    w0_ref, w2_ref, a0_ref, a2_ref, g2_ref, seg_ref, tri_ref, ones_ref,
                    v_out, kkt_out, rt_out, kh_out, bh_out, kw_out, bw_out, wt_out, bonus_out, g7_out,
                    h_scr, rw_scr, *, tm, npos):
    i = pl.program_id(0)
    not_first = (i % npos != 0).astype(F32)
    not_last = (i % npos != npos - 1).astype(F32)
    g = g_ref[...]
    h_scr[0:SUBLANES, :] = (_rms(xp_ref[...], g) * not_first).astype(BF16)
    h_scr[SUBLANES:SUBLANES + tm, :] = _rms(x_ref[...], g).astype(BF16)
    h_scr[SUBLANES + tm:, :] = (_rms(xn_ref[...], g) * not_last).astype(BF16)
    rw_scr[...] = _dot(h_scr[...], wrw_ref[...])
    cur = rw_scr[SUBLANES:SUBLANES + tm, :]
    prev = rw_scr[SUBLANES - 1:SUBLANES - 1 + tm, :]
    nxt = rw_scr[SUBLANES + 1:SUBLANES + 1 + tm, :]
    rwf = cur + mup_ref[...] * (prev - cur) + mun_ref[...] * (nxt - cur)
    r7 = rwf[:, 0:RW_DIM]
    k7 = rwf[:, RW_DIM:2 * RW_DIM]
    v7 = rwf[:, 2 * RW_DIM:3 * RW_DIM]
    wl = rwf[:, 3 * RW_DIM:3 * RW_DIM + 2 * LORA]
    al = rwf[:, 3 * RW_DIM + 2 * LORA:3 * RW_DIM + 4 * LORA]
    gl = rwf[:, 3 * RW_DIM + 4 * LORA:]
    seg = seg_ref[...]
    kx = k7 * kk_ref[...]
    kkn = kx / jnp.maximum(jnp.sqrt(_dot_0_1(kx * kx, seg)), 1e-12)
    v_out[...] = v7.astype(BF16)
    g7_out[...] = _dot(_sigmoid(gl).astype(BF16), g2_ref[...])
    tw = jnp.tanh(wl).astype(BF16)
    alb = al.astype(BF16)
    bonus = jnp.zeros((tm, RW_DIM), F32)
    for d in range(2):
        w = -jax.nn.softplus(-(w0_ref[d] + _dot(tw, w2_ref[d]))) - 0.5
        lw = -jnp.exp(w)
        a = _sigmoid(a0_ref[d] + _dot(alb, a2_ref[d]))
        kd = k7 * (1.0 + (a - 1.0) * ka_ref[...])
        b = kkn * a
        bonus = bonus + _dot_0_1(r7 * kd * rk_ref[...], seg) * v7
        cum = _dot_0_1_l(tri_ref[d], lw)
        tot = _dot_0_1_l(ones_ref[...], lw)
        w_incl = jnp.exp(cum)
        w_excl = jnp.exp(cum - lw)
        w_inv = jnp.exp(-cum)
        w_rest = jnp.exp(tot - cum)
        kkt_out[d] = (kkn * w_excl).astype(BF16)
        rt_out[d] = (r7 * w_incl).astype(BF16)
        kh_out[d] = (kd * w_inv).astype(BF16)
        bh_out[d] = (b * w_inv).astype(BF16)
        kw_out[d] = (kd * w_rest).astype(BF16)
        bw_out[d] = (b * w_rest).astype(BF16)
        wtot = jnp.exp(tot)
        for ci in range(tm // CHUNK):
            wt_out[d, ci] = wtot[CHUNK * ci:CHUNK * ci + 1, :]
    bonus_out[...] = bonus


def _rw_prep(x2, T, w):
    ntok = x2.shape[0]
    tm = min(256, T)
    npos = T // tm
    nblk8 = ntok // SUBLANES
    r8 = tm // SUBLANES
    row = lambda i: (i, 0)
    drow = lambda i: (0, i, 0)
    dspec = pl.BlockSpec((2, tm, RW_DIM), drow)
    dshape = jax.ShapeDtypeStruct((2, ntok, RW_DIM), BF16)
    tri, ones = _chunk_masks(tm)
    outs = pl.pallas_call(
        functools.partial(_rw_prep_kernel, tm=tm, npos=npos),
        grid=(ntok // tm,),
        in_specs=[pl.BlockSpec((tm, D_MODEL), row),
                  pl.BlockSpec((SUBLANES, D_MODEL), lambda i: (jnp.maximum(i * r8 - 1, 0), 0)),
                  pl.BlockSpec((SUBLANES, D_MODEL), lambda i: (jnp.minimum((i + 1) * r8, nblk8 - 1), 0)),
                  _full((1, D_MODEL)), _full((D_MODEL, RW_COLS)), _full((1, RW_COLS)), _full((1, RW_COLS)),
                  _full((1, RW_DIM)), _full((1, RW_DIM)), _full((1, RW_DIM)),
                  _full((2, 1, RW_DIM)), _full((2, 2 * LORA, RW_DIM)), _full((2, 1, RW_DIM)),
                  _full((2, 2 * LORA, RW_DIM)), _full((GATE_LORA, RW_DIM)), _full((RW_DIM, RW_DIM)),
                  _full((2, tm, tm)), _full((tm, tm))],
        out_specs=[pl.BlockSpec((tm, RW_DIM), row), dspec, dspec, dspec, dspec, dspec, dspec,
                   pl.BlockSpec((2, tm // CHUNK, 1, RW_DIM), lambda i: (0, i, 0, 0)),
                   pl.BlockSpec((tm, RW_DIM), row), pl.BlockSpec((tm, RW_DIM), row)],
        out_shape=[jax.ShapeDtypeStruct((ntok, RW_DIM), BF16), dshape, dshape, dshape, dshape, dshape, dshape,
                   jax.ShapeDtypeStruct((2, ntok // CHUNK, 1, RW_DIM), F32),
                   jax.ShapeDtypeStruct((ntok, RW_DIM), F32), jax.ShapeDtypeStruct((ntok, RW_DIM), F32)],
        scratch_shapes=[pltpu.VMEM((tm + 2 * SUBLANES, D_MODEL), BF16), pltpu.VMEM((tm + 2 * SUBLANES, RW_COLS), F32)],
        compiler_params=_cparams("parallel"),
        name="rw_prep",
    )(x2, x2, x2, w["norm_mix_g"], w["w_rw"], w["mu_prev"], w["mu_next"], w["k_k"], w["k_a"], w["r_k"],
      w["w0"], w["w2"], w["a0"], w["a2"], w["g2"], w["seg"], tri, ones)
    return outs


def _chunk_masks(tm):
    t = np.arange(tm)
    same = (t[:, None] // CHUNK) == (t[None, :] // CHUNK)
    fwd = same & (t[None, :] <= t[:, None])
    bwd = same & (t[None, :] >= t[:, None])
    tri = jnp.asarray(np.stack([fwd, bwd]).astype(np.float32), BF16)
    return tri, jnp.asarray(same.astype(np.float32), BF16)


def _rw_chain(kkt, rt, kh, bh, kw, bw, v, wtot, s_ref, strict, incl, head_mask, eye):
    stack = lambda a: jnp.where(head_mask, jnp.concatenate([a] * GROUP, axis=0), jnp.zeros((), BF16))
    xkk, xr, yk, yb, vs, kws, bws = (stack(a) for a in (kkt, rt, kh, bh, v, kw, bw))
    a_all = _dot_nt(jnp.concatenate([xkk, xr], axis=0), jnp.concatenate([yk, yb], axis=0))
    ak = jnp.where(strict, a_all[:GW, :GW], 0.0)
    ab = jnp.where(strict, a_all[:GW, GW:], 0.0)
    bk = jnp.where(incl, a_all[GW:, :GW], 0.0)
    bb = jnp.where(incl, a_all[GW:, GW:], 0.0)
    tinv = eye - ab
    pw = ab
    for _ in range(5):
        pwb = pw.astype(BF16)
        pw = _dot(pwb, pwb)
        tinv = tinv + _dot(tinv.astype(BF16), pw.astype(BF16))
    sb = s_ref[...].astype(BF16)
    rhs = _dot_nt(xkk, sb) + _dot(ak.astype(BF16), vs)
    u = (-_dot(tinv.astype(BF16), rhs.astype(BF16))).astype(BF16)
    y = _dot_nt(xr, sb) + _dot(bk.astype(BF16), vs) + _dot(bb.astype(BF16), u)
    s_ref[...] = s_ref[...] * wtot + _dot_tn(jnp.concatenate([vs, u], axis=0), jnp.concatenate([kws, bws], axis=0))
    return y[0:CHUNK] + y[CHUNK:2 * CHUNK] + y[2 * CHUNK:3 * CHUNK] + y[3 * CHUNK:4 * CHUNK]


def _rw_scan_kernel(*refs):
    ins = refs[:16]
    yf_ref, yb_ref, s_ref = refs[16:]
    c = pl.program_id(1)

    @pl.when(c == 0)
    def _():
        s_ref[...] = jnp.zeros(s_ref.shape, F32)

    ri = lax.broadcasted_iota(jnp.int32, (GW, GW), 0)
    ci = lax.broadcasted_iota(jnp.int32, (GW, GW), 1)
    head_mask = (ri // CHUNK) == (ci // RW_HEAD)
    rt_ = ri % CHUNK
    cs_ = ci % CHUNK
    eye = (ri == ci).astype(F32)
    for d, y_ref in ((0, yf_ref), (1, yb_ref)):
        kkt, rt, kh, bh, kw, bw, v, wt = ins[8 * d:8 * d + 8]
        strict = (cs_ < rt_) if d == 0 else (cs_ > rt_)
        incl = (cs_ <= rt_) if d == 0 else (cs_ >= rt_)
        for gi in range(RW_HEADS // GROUP):
            sl = slice(GW * gi, GW * (gi + 1))
            y = _rw_chain(kkt[:, sl], rt[:, sl], kh[:, sl], bh[:, sl], kw[:, sl], bw[:, sl], v[:, sl],
                          wt[:, sl], s_ref.at[d, gi], strict, incl, head_mask, eye)
            y_ref[:, sl] = y


def _rw_scan(v, kkt, rt, kh, bh, kw, bw, wt, B, T):
    nc = T // CHUNK
    ntok = B * T
    in_specs = []
    args = []
    for d in range(2):
        if d == 0:
            cm = lambda b, c: b * nc + c
        else:
            cm = lambda b, c: b * nc + (nc - 1 - c)
        for arr in (kkt, rt, kh, bh, kw, bw):
            in_specs.append(pl.BlockSpec((None, CHUNK, RW_DIM), lambda b, c, cm=cm, d=d: (d, cm(b, c), 0)))
            args.append(arr)
        in_specs.append(pl.BlockSpec((CHUNK, RW_DIM), lambda b, c, cm=cm: (cm(b, c), 0)))
        args.append(v)
        in_specs.append(pl.BlockSpec((None, None, 1, RW_DIM), lambda b, c, cm=cm, d=d: (d, cm(b, c), 0, 0)))
        args.append(wt)
    return pl.pallas_call(
        _rw_scan_kernel,
        grid=(B, nc),
        in_specs=in_specs,
        out_specs=[pl.BlockSpec((CHUNK, RW_DIM), lambda b, c: (b * nc + c, 0)),
                   pl.BlockSpec((CHUNK, RW_DIM), lambda b, c: (b * nc + (nc - 1 - c), 0))],
        out_shape=[jax.ShapeDtypeStruct((ntok, RW_DIM), F32), jax.ShapeDtypeStruct((ntok, RW_DIM), F32)],
        scratch_shapes=[pltpu.VMEM((2, RW_HEADS // GROUP, GW, GW), F32)],
        compiler_params=_cparams("parallel", "arbitrary"),
        name="rw_scan",
    )(*args)


def _mem_kv_kernel(m_ref, g_ref, wk_ref, wv_ref, gk_ref, k_out, v_out):
    m = _rms(m_ref[...], g_ref[...]).astype(BF16)
    k = _dot(m, wk_ref[...])
    v_out[...] = _dot(m, wv_ref[...]).astype(BF16)
    for hd in range(X_HEADS):
        sl = slice(X_HEAD * hd, X_HEAD * (hd + 1))
        k_out[:, sl] = _rms(k[:, sl], gk_ref[...]).astype(BF16)


def _mem_kv(mem2, w):
    n = mem2.shape[0]
    row = lambda i: (i, 0)
    return pl.pallas_call(
        _mem_kv_kernel,
        grid=(n // N_MEM,),
        in_specs=[pl.BlockSpec((N_MEM, D_MODEL), row), _full((1, D_MODEL)), _full((D_MODEL, X_DIM)),
                  _full((D_MODEL, X_DIM)), _full((1, X_HEAD))],
        out_specs=[pl.BlockSpec((N_MEM, X_DIM), row), pl.BlockSpec((N_MEM, X_DIM), row)],
        out_shape=[jax.ShapeDtypeStruct((n, X_DIM), BF16), jax.ShapeDtypeStruct((n, X_DIM), BF16)],
        compiler_params=_cparams("parallel"),
        name="mem_kv",
    )(mem2, w["mem_norm_g"], w["w_mk"], w["w_mv"], w["x_kn_g"])


def _xattn_kernel(x_ref, g_ref, wq_ref, gq_ref, mk_ref, mv_ref, o_ref):
    h = _rms(x_ref[...], g_ref[...]).astype(BF16)
    q = _dot(h, wq_ref[...])
    for hd in range(X_HEADS):
        sl = slice(X_HEAD * hd, X_HEAD * (hd + 1))
        qh = _rms(q[:, sl], gq_ref[...]).astype(BF16)
        s = _dot_nt(qh, mk_ref[:, sl])
        p = jnp.exp(s - jnp.max(s, axis=-1, keepdims=True))
        o = _dot(p.astype(BF16), mv_ref[:, sl]) / jnp.sum(p, axis=-1, keepdims=True)
        o_ref[:, sl] = o.astype(BF16)


def _xattn(x2, mk, mv, B, T, w):
    tq = min(512, T)
    nq = T // tq
    return pl.pallas_call(
        _xattn_kernel,
        grid=(B, nq),
        in_specs=[pl.BlockSpec((tq, D_MODEL), lambda b, i: (b * nq + i, 0)), _full((1, D_MODEL)),
                  _full((D_MODEL, X_DIM)), _full((1, X_HEAD)),
                  pl.BlockSpec((N_MEM, X_DIM), lambda b, i: (b, 0)), pl.BlockSpec((N_MEM, X_DIM), lambda b, i: (b, 0))],
        out_specs=pl.BlockSpec((tq, X_DIM), lambda b, i: (b * nq + i, 0)),
        out_shape=jax.ShapeDtypeStruct((B * T, X_DIM), BF16),
        compiler_params=_cparams("parallel", "parallel"),
        name="xattn",
    )(x2, w["norm_mix_g"], w["w_xq"], w["x_qn_g_scaled"], mk, mv)


def _merge_kernel(x_ref, oa_ref, yf_ref, yb_ref, bonus_ref, g7_ref, oc_ref, g_ref, wg_ref, woa_ref, wob_ref,
                  woc_ref, wout_ref, lng_ref, lnb_ref, seg_ref, o_ref):
    x = x_ref[...]
    h = _rms(x, g_ref[...]).astype(BF16)
    seg = seg_ref[...]
    y7 = yf_ref[...] + yb_ref[...] + bonus_ref[...]
    mu = _dot_0_1(y7, seg) * (1.0 / RW_HEAD)
    dy = y7 - mu
    var = _dot_0_1(dy * dy, seg) * (1.0 / RW_HEAD)
    y7 = dy * lax.rsqrt(var + LNX_EPS) * lng_ref[...] + lnb_ref[...]
    yb = (y7 * g7_ref[...]).astype(BF16)
    merged = jnp.zeros(x.shape, F32)
    for bi, (br, wo) in enumerate(((oa_ref[...], woa_ref), (yb, wob_ref), (oc_ref[...], woc_ref))):
        gate = _sigmoid(_dot(h, wg_ref[:, D_MODEL * bi:D_MODEL * (bi + 1)]))
        merged = merged + gate * _dot(br, wo[...])
    o_ref[...] = x + _dot(merged.astype(BF16), wout_ref[...])


def _merge(x2, oa, yf, yb, bonus, g7, oc, T, w):
    ntok = x2.shape[0]
    tm = min(256, T)
    row = lambda i: (i, 0)
    half = lambda: pl.BlockSpec((tm, RW_DIM), row)
    return pl.pallas_call(
        _merge_kernel,
        grid=(ntok // tm,),
        in_specs=[pl.BlockSpec((tm, D_MODEL), row), half(), half(), half(), half(), half(), half(),
                  _full((1, D_MODEL)), _full((D_MODEL, 3 * D_MODEL)), _full((RW_DIM, D_MODEL)),
                  _full((RW_DIM, D_MODEL)), _full((RW_DIM, D_MODEL)), _full((D_MODEL, D_MODEL)),
                  _full((1, RW_DIM)), _full((1, RW_DIM)), _full((RW_DIM, RW_DIM))],
        out_specs=pl.BlockSpec((tm, D_MODEL), row),
        out_shape=jax.ShapeDtypeStruct((ntok, D_MODEL), F32),
        compiler_params=_cparams("parallel"),
        name="merge",
    )(x2, oa, yf, yb, bonus, g7, oc, w["norm_mix_g"], w["w_gate"], w["w_o_a"], w["w_o_b"], w["w_o_c"],
      w["w_out"], w["lnx_g"], w["lnx_b"], w["seg"])


def _ffn_kernel(x_ref, xp_ref, xn_ref, g_ref, wug_ref, wuv_ref, cw_ref, cb_ref, wd_ref, o_ref, h_scr, ug_scr,
                *, tm, npos):
    i = pl.program_id(0)
    f = pl.program_id(1)

    @pl.when(f == 0)
    def _():
        not_first = (i % npos != 0).astype(F32)
        not_last = (i % npos != npos - 1).astype(F32)
        g = g_ref[...]
        h_scr[0:SUBLANES, :] = (_rms(xp_ref[...], g) * not_first).astype(BF16)
        h_scr[SUBLANES:SUBLANES + tm, :] = _rms(x_ref[...], g).astype(BF16)
        h_scr[SUBLANES + tm:, :] = (_rms(xn_ref[...], g) * not_last).astype(BF16)

    ug_scr[...] = _dot(h_scr[...], wug_ref[...])
    uv = _dot(h_scr[SUBLANES:SUBLANES + tm, :], wuv_ref[...])
    cw = cw_ref[...]
    c = (cw[0:1] * ug_scr[SUBLANES - 1:SUBLANES - 1 + tm, :] + cw[1:2] * ug_scr[SUBLANES:SUBLANES + tm, :]
         + cw[2:3] * ug_scr[SUBLANES + 1:SUBLANES + 1 + tm, :] + cb_ref[...])
    act = 0.5 * c * (1.0 + lax.erf(c * np.float32(1.0 / np.sqrt(2.0)))) * uv
    part = _dot(act.astype(BF16), wd_ref[...])

    @pl.when(f == 0)
    def _():
        o_ref[...] = x_ref[...] + part

    @pl.when(f != 0)
    def _():
        o_ref[...] += part


def _ffn(x2, T, w):
    ntok = x2.shape[0]
    tm = min(512, T)
    fc = D_FF // 2
    npos = T // tm
    nblk8 = ntok // SUBLANES
    r8 = tm // SUBLANES
    row = lambda i, f: (i, 0)
    return pl.pallas_call(
        functools.partial(_ffn_kernel, tm=tm, npos=npos),
        grid=(ntok // tm, D_FF // fc),
        in_specs=[pl.BlockSpec((tm, D_MODEL), row),
                  pl.BlockSpec((SUBLANES, D_MODEL), lambda i, f: (jnp.maximum(i * r8 - 1, 0), 0)),
                  pl.BlockSpec((SUBLANES, D_MODEL), lambda i, f: (jnp.minimum((i + 1) * r8, nblk8 - 1), 0)),
                  pl.BlockSpec((1, D_MODEL), lambda i, f: (0, 0)),
                  pl.BlockSpec((D_MODEL, fc), lambda i, f: (0, f)),
                  pl.BlockSpec((D_MODEL, fc), lambda i, f: (0, f)),
                  pl.BlockSpec((3, fc), lambda i, f: (0, f)),
                  pl.BlockSpec((1, fc), lambda i, f: (0, f)),
                  pl.BlockSpec((fc, D_MODEL), lambda i, f: (f, 0))],
        out_specs=pl.BlockSpec((tm, D_MODEL), row),
        out_shape=jax.ShapeDtypeStruct((ntok, D_MODEL), F32),
        scratch_shapes=[pltpu.VMEM((tm + 2 * SUBLANES, D_MODEL), BF16), pltpu.VMEM((tm + 2 * SUBLANES, fc), F32)],
        compiler_params=_cparams("parallel", "arbitrary"),
        name="conv_ffn",
    )(x2, x2, x2, w["norm_ffn_g"], w["w_up_gate"], w["w_up_val"], w["conv_w"], w["conv_b"], w["w_down"])


def _prep_weights(p):
    w = {}
    row = lambda a: a.reshape(1, -1).astype(F32)
    w_in = p["w_in"]
    o = np.cumsum([0, Q_LORA, KV_LORA, MLA_ROPE, RW_COLS, X_DIM, 3 * D_MODEL])
    seg = lambda i: w_in[:, o[i]:o[i + 1]]
    zc = lambda n: jnp.zeros((D_MODEL, n), F32)
    w["w_c"] = jnp.concatenate([seg(0), seg(1), zc(MLA_NOPE), seg(2), zc(LANES - MLA_QK)], axis=1).astype(BF16)
    w["w_rw"] = seg(3).astype(BF16)
    w["w_xq"] = seg(4).astype(BF16)
    w["w_gate"] = seg(5).astype(BF16)
    w["norm_mix_g"] = row(p["norm_mix_g"])
    w["q_norm_g"] = row(p["q_norm_g"])
    w["kv_norm_g"] = row(p["kv_norm_g"])
    pad_slot = lambda a: jnp.pad(a, ((0, 0), (0, 0), (0, LANES - a.shape[-1]))).reshape(a.shape[0], -1)
    w["w_uq"] = pad_slot(p["w_uq"].reshape(Q_LORA, MLA_HEADS, MLA_QK)).astype(BF16)
    ukv = p["w_ukv"].reshape(KV_LORA, MLA_HEADS, MLA_NOPE + MLA_V)
    w["w_uk"] = pad_slot(ukv[:, :, :MLA_NOPE]).astype(BF16)
    w["w_uv"] = ukv[:, :, MLA_NOPE:].reshape(KV_LORA, MLA_HEADS * MLA_V).astype(BF16)
    slot_gain = lambda g: jnp.tile(jnp.pad(g.reshape(-1), (0, LANES - MLA_QK)), MLA_HEADS).reshape(1, -1)
    w["gq_slots"] = slot_gain(p["mla_qn_g"]) * np.float32(MLA_QK ** -0.5)
    w["gk_slots"] = slot_gain(p["mla_kn_g"])
    for name in ("mu_prev", "mu_next", "k_k", "k_a", "r_k", "lnx_g", "lnx_b", "mem_norm_g", "x_kn_g", "norm_ffn_g",
                 "conv_b"):
        w[name] = row(p[name])
    w["w0"] = jnp.stack([p["w0_f"], p["w0_b"]]).reshape(2, 1, RW_DIM)
    w["a0"] = jnp.stack([p["a0_f"], p["a0_b"]]).reshape(2, 1, RW_DIM)
    zl = jnp.zeros((LORA, RW_DIM), F32)
    w["w2"] = jnp.stack([jnp.concatenate([p["w2_f"], zl]), jnp.concatenate([zl, p["w2_b"]])]).astype(BF16)
    w["a2"] = jnp.stack([jnp.concatenate([p["a2_f"], zl]), jnp.concatenate([zl, p["a2_b"]])]).astype(BF16)
    w["g2"] = p["g2"].astype(BF16)
    hid = np.arange(RW_DIM) // RW_HEAD
    w["seg"] = jnp.asarray((hid[:, None] == hid[None, :]).astype(np.float32), BF16)
    mkv = p["w_mkv"].reshape(D_MODEL, X_HEADS, 2 * X_HEAD)
    w["w_mk"] = mkv[:, :, :X_HEAD].reshape(D_MODEL, X_DIM).astype(BF16)
    w["w_mv"] = mkv[:, :, X_HEAD:].reshape(D_MODEL, X_DIM).astype(BF16)
    w["x_qn_g_scaled"] = row(p["x_qn_g"]) * np.float32(X_HEAD ** -0.5)
    for name in ("w_o_a", "w_o_b", "w_o_c", "w_out", "w_down"):
        w[name] = p[name].astype(BF16)
    w["w_up_gate"] = p["w_up"][:, :D_FF].astype(BF16)
    w["w_up_val"] = p["w_up"][:, D_FF:].astype(BF16)
    w["conv_w"] = p["conv_w"].astype(F32)
    return w


def _rope_tables(T):
    half = MLA_ROPE // 2
    inv = jnp.power(ROPE_THETA, -jnp.arange(half, dtype=F32) / half)
    ang = jnp.arange(T, dtype=F32)[:, None] * inv[None, :]
    cos, sin = jnp.cos(ang), jnp.sin(ang)
    z = lambda n: jnp.zeros((T, n), F32)
    cos_t = jnp.concatenate([jnp.ones((T, MLA_NOPE), F32), cos, cos, jnp.ones((T, LANES - MLA_QK), F32)], axis=1)
    sin_a = jnp.concatenate([z(MLA_NOPE), -sin, z(half), z(LANES - MLA_QK)], axis=1)
    sin_b = jnp.concatenate([z(MLA_NOPE), z(half), sin, z(LANES - MLA_QK)], axis=1)
    return cos_t, sin_a, sin_b


def _layer(x, mem, w):
    B, T, _ = x.shape
    x2 = x.reshape(B * T, D_MODEL)
    q, k, v = _mla_prep(x2, T, w, _rope_tables(T))
    o_a = _flash(q, k, v, B, T)
    v7, kkt, rt, kh, bh, kw, bw, wt, bonus, g7 = _rw_prep(x2, T, w)
    y_f, y_b = _rw_scan(v7, kkt, rt, kh, bh, kw, bw, wt, B, T)
    mk, mv = _mem_kv(mem.reshape(B * N_MEM, D_MODEL), w)
    o_c = _xattn(x2, mk, mv, B, T, w)
    x1 = _merge(x2, o_a, y_f, y_b, bonus, g7, o_c, T, w)
    return _ffn(x1, T, w).reshape(B, T, D_MODEL)


def kernel(x_prompt, x_sample, mem_prompt, mem_sample, norm_mix_g, w_in, q_norm_g, w_uq, kv_norm_g, w_ukv, mla_qn_g, mla_kn_g, w_o_a, mu_prev, mu_next, w0_f, w2_f, a0_f, a2_f, w0_b, w2_b, a0_b, a2_b, g2, k_k, k_a, r_k, lnx_g, lnx_b, w_o_b, mem_norm_g, w_mkv, x_qn_g, x_kn_g, w_o_c, w_out, norm_ffn_g, w_up, conv_w, conv_b, w_down):
    p = dict(norm_mix_g=norm_mix_g, w_in=w_in, q_norm_g=q_norm_g, w_uq=w_uq, kv_norm_g=kv_norm_g, w_ukv=w_ukv,
             mla_qn_g=mla_qn_g, mla_kn_g=mla_kn_g, w_o_a=w_o_a, mu_prev=mu_prev, mu_next=mu_next,
             w0_f=w0_f, w2_f=w2_f, a0_f=a0_f, a2_f=a2_f, w0_b=w0_b, w2_b=w2_b, a0_b=a0_b, a2_b=a2_b,
             g2=g2, k_k=k_k, k_a=k_a, r_k=r_k, lnx_g=lnx_g, lnx_b=lnx_b, w_o_b=w_o_b,
             mem_norm_g=mem_norm_g, w_mkv=w_mkv, x_qn_g=x_qn_g, x_kn_g=x_kn_g, w_o_c=w_o_c, w_out=w_out,
             norm_ffn_g=norm_ffn_g, w_up=w_up, conv_w=conv_w, conv_b=conv_b, w_down=w_down)
    w = _prep_weights({name: a[0] for name, a in p.items()})
    return (_layer(x_prompt, mem_prompt, w), _layer(x_sample, mem_sample, w))
```

```python
import functools

import numpy as np
import jax
import jax.numpy as jnp
from jax import lax
from jax.experimental import pallas as pl
from jax.experimental.pallas import tpu as pltpu

F32 = jnp.float32
BF16 = jnp.bfloat16

D_MODEL = 1024
RMS_EPS = 1e-6
N_MEM = 256
MLA_HEADS = 8
MLA_NOPE = 64
MLA_ROPE = 32
MLA_QK = MLA_NOPE + MLA_ROPE
MLA_V = 64
Q_LORA = 384
KV_LORA = 256
ROPE_THETA = 10000.0
RW_HEADS = 8
RW_HEAD = 64
RW_DIM = RW_HEADS * RW_HEAD
LORA = 64
GATE_LORA = 128
RW_COLS = 3 * RW_DIM + 4 * LORA + GATE_LORA
LNX_EPS = 64e-5
X_HEADS = 4
X_HEAD = 128
X_DIM = X_HEADS * X_HEAD
D_FF = 2816

LANES = 128
SUBLANES = 8
CHUNK = 64
GROUP = 4
GW = GROUP * RW_HEAD
VMEM_LIMIT = 56 * 1024 * 1024


def _cparams(*sem):
    return pltpu.CompilerParams(dimension_semantics=sem, vmem_limit_bytes=VMEM_LIMIT)


def _rms(x, g, eps=RMS_EPS):
    return x * lax.rsqrt(jnp.mean(x * x, axis=-1, keepdims=True) + eps) * g


def _dot(a, b):
    return jnp.dot(a, b, preferred_element_type=F32)


def _dot_nt(a, b):
    return lax.dot_general(a, b, (((1,), (1,)), ((), ())), preferred_element_type=F32)


def _dot_tn(a, b):
    return lax.dot_general(a, b, (((0,), (0,)), ((), ())), preferred_element_type=F32)


def _split2(x):
    hi = x.astype(BF16)
    lo = (x - hi.astype(F32)).astype(BF16)
    return hi, lo


def _dot_0_1(x, e):
    hi, lo = _split2(x)
    return _dot(hi, e) + _dot(lo, e)


def _dot_0_1_l(e, x):
    hi = x.astype(BF16)
    r1 = x - hi.astype(F32)
    mid = r1.astype(BF16)
    lo = (r1 - mid.astype(F32)).astype(BF16)
    return _dot(e, hi) + _dot(e, mid) + _dot(e, lo)


def _sigmoid(z):
    return 1.0 / (1.0 + jnp.exp(-z))


def _full(shape):
    nd = len(shape)
    return pl.BlockSpec(shape, lambda *_: (0,) * nd)


def _mla_prep_kernel(x_ref, g_ref, wc_ref, gq_ref, gkv_ref, wuq_ref, wuk_ref, wuv_ref, gqs_ref, gks_ref,
                     cos_ref, sa_ref, sb_ref, q_out, k_out, v_out):
    h = _rms(x_ref[...], g_ref[...]).astype(BF16)
    c = _dot(h, wc_ref[...])
    cq = _rms(c[:, :Q_LORA], gq_ref[...]).astype(BF16)
    ckv = _rms(c[:, Q_LORA:Q_LORA + KV_LORA], gkv_ref[...]).astype(BF16)
    kr = c[:, Q_LORA + KV_LORA:]
    q = _dot(cq, wuq_ref[...])
    k = _dot(ckv, wuk_ref[...])
    vt = _dot_nt(wuv_ref[...], ckv)
    slot_row = lax.broadcasted_iota(jnp.int32, vt.shape, 0) % LANES
    v_out[0] = jnp.where(slot_row == MLA_V, 1.0, vt).astype(BF16)
    cos = cos_ref[...]
    sa = sa_ref[...]
    sb = sb_ref[...]
    for hd in range(MLA_HEADS):
        sl = slice(LANES * hd, LANES * (hd + 1))
        for src, gref, out, extra in ((q, gqs_ref, q_out, None), (k, gks_ref, k_out, kr)):
            t = src[:, sl]
            if extra is not None:
                t = t + extra
            ss = jnp.sum(t * t, axis=-1, keepdims=True) * (1.0 / MLA_QK)
            t = t * lax.rsqrt(ss + RMS_EPS) * gref[:, sl]
            t = t * cos + pltpu.roll(t, LANES - MLA_ROPE // 2, 1) * sa + pltpu.roll(t, MLA_ROPE // 2, 1) * sb
            out[:, sl] = t.astype(BF16)


def _mla_prep(x2, T, w, tabs):
    ntok = x2.shape[0]
    tm = _attn_tile(T)
    npos = T // tm
    row = lambda i: (i, 0)
    pos = lambda i: (i % npos, 0)
    return pl.pallas_call(
        _mla_prep_kernel,
        grid=(ntok // tm,),
        in_specs=[pl.BlockSpec((tm, D_MODEL), row), _full((1, D_MODEL)), _full((D_MODEL, 768)),
                  _full((1, Q_LORA)), _full((1, KV_LORA)), _full((Q_LORA, 1024)), _full((KV_LORA, 1024)),
                  _full((1024, KV_LORA)), _full((1, 1024)), _full((1, 1024)),
                  pl.BlockSpec((tm, LANES), pos), pl.BlockSpec((tm, LANES), pos), pl.BlockSpec((tm, LANES), pos)],
        out_specs=[pl.BlockSpec((tm, 1024), row), pl.BlockSpec((tm, 1024), row),
                   pl.BlockSpec((1, 1024, tm), lambda i: (i, 0, 0))],
        out_shape=[jax.ShapeDtypeStruct((ntok, 1024), BF16), jax.ShapeDtypeStruct((ntok, 1024), BF16),
                   jax.ShapeDtypeStruct((ntok // tm, 1024, tm), BF16)],
        compiler_params=_cparams("parallel"),
        name="mla_prep",
    )(x2, w["norm_mix_g"], w["w_c"], w["q_norm_g"], w["kv_norm_g"], w["w_uq"], w["w_uk"], w["w_uv"],
      w["gq_slots"], w["gk_slots"], *tabs)


def _flash_kernel(q_ref, k_ref, vt_ref, o_ref, s0_ref, s1_ref, *, tk, nk):
    tq = q_ref.shape[0]
    slots = [slice(LANES * h, LANES * (h + 1)) for h in range(2)]
    per = tk // vt_ref.shape[2]

    def scores(j, s_ref):
        kblk = k_ref[pl.ds(pl.multiple_of(j * tk, tk), tk), :]
        for h in range(2):
            s_ref[h] = _dot_nt(kblk[:, slots[h]], q_ref[:, slots[h]])

    def consume(j, s_ref, state):
        vt = jnp.concatenate([vt_ref[per * j + r] for r in range(per)], axis=1)
        m_news = [jnp.maximum(state[h][0], jnp.max(s_ref[h], axis=0, keepdims=True)) for h in range(2)]
        pts = [jnp.exp((s_ref[h] - m_news[h]).astype(BF16)) for h in range(2)]
        return tuple((m_news[h], jnp.exp(state[h][0] - m_news[h]) * state[h][1] + _dot(vt[slots[h], :], pts[h]))
                     for h in range(2))

    def body(t, state):
        j = 2 * t
        scores(j + 1, s1_ref)
        state = consume(j, s0_ref, state)
        scores(j + 2, s0_ref)
        return consume(j + 1, s1_ref, state)

    state = tuple((jnp.full((1, tq), -jnp.inf, F32), jnp.zeros((LANES, tq), F32)) for _ in range(2))
    scores(0, s0_ref)
    state = lax.fori_loop(0, nk // 2 - 1, body, state)
    scores(nk - 1, s1_ref)
    state = consume(nk - 2, s0_ref, state)
    state = consume(nk - 1, s1_ref, state)
    outs = [acc[0:MLA_V, :] / acc[MLA_V:MLA_V + 1, :] for _, acc in state]
    o_ref[...] = jnp.concatenate(outs, axis=0).T.astype(BF16)


def _attn_tile(T):
    return min(512, T // 2)


def _flash(q, k, vt, B, T):
    tq = _attn_tile(T)
    tk = min(1024, T // 2)
    tv = _attn_tile(T)
    nq = T // tq
    nk = T // tk
    assert nk % 2 == 0 and tk % tv == 0, (T, tk, tv)
    return pl.pallas_call(
        functools.partial(_flash_kernel, tk=tk, nk=nk),
        grid=(B, MLA_HEADS // 2, nq),
        in_specs=[pl.BlockSpec((tq, 2 * LANES), lambda b, hp, i: (b * nq + i, hp)),
                  pl.BlockSpec((T, 2 * LANES), lambda b, hp, i: (b, hp)),
                  pl.BlockSpec((T // tv, 2 * LANES, tv), lambda b, hp, i: (b, hp, 0))],
        out_specs=pl.BlockSpec((tq, LANES), lambda b, hp, i: (b * nq + i, hp)),
        out_shape=jax.ShapeDtypeStruct((B * T, MLA_HEADS * MLA_V), BF16),
        scratch_shapes=[pltpu.VMEM((2, tk, tq), F32), pltpu.VMEM((2, tk, tq), F32)],
        compiler_params=_cparams("parallel", "parallel", "arbitrary"),
        name="mla_flash",
    )(q, k, vt)


def _rw_prep_kernel(x_ref, xp_ref, xn_ref, g_ref, wrw_ref, mup_ref, mun_ref, kk_ref, ka_ref, rk_ref,
                    w0_ref, w2_ref, a0_ref, a2_ref, g2_ref, seg_ref, tri_ref, ones_ref,
                    v_out, kkt_out, rt_out, kh_out, bh_out, kw_out, bw_out, wt_out, bonus_out, g7_out,
                    h_scr, rw_scr, *, tm, npos):
    i = pl.program_id(0)
    not_first = (i % npos != 0).astype(F32)
    not_last = (i % npos != npos - 1).astype(F32)
    g = g_ref[...]
    h_scr[0:SUBLANES, :] = (_rms(xp_ref[...], g) * not_first).astype(BF16)
    h_scr[SUBLANES:SUBLANES + tm, :] = _rms(x_ref[...], g).astype(BF16)
    h_scr[SUBLANES + tm:, :] = (_rms(xn_ref[...], g) * not_last).astype(BF16)
    rw_scr[...] = _dot(h_scr[...], wrw_ref[...])
    cur = rw_scr[SUBLANES:SUBLANES + tm, :]
    prev = rw_scr[SUBLANES - 1:SUBLANES - 1 + tm, :]
    nxt = rw_scr[SUBLANES + 1:SUBLANES + 1 + tm, :]
    rwf = cur + mup_ref[...] * (prev - cur) + mun_ref[...] * (nxt - cur)
    r7 = rwf[:, 0:RW_DIM]
    k7 = rwf[:, RW_DIM:2 * RW_DIM]
    v7 = rwf[:, 2 * RW_DIM:3 * RW_DIM]
    wl = rwf[:, 3 * RW_DIM:3 * RW_DIM + 2 * LORA]
    al = rwf[:, 3 * RW_DIM + 2 * LORA:3 * RW_DIM + 4 * LORA]
    gl = rwf[:, 3 * RW_DIM + 4 * LORA:]
    seg = seg_ref[...]
    kx = k7 * kk_ref[...]
    kkn = kx / jnp.maximum(jnp.sqrt(_dot_0_1(kx * kx, seg)), 1e-12)
    v_out[...] = v7.astype(BF16)
    g7_out[...] = _dot(_sigmoid(gl).astype(BF16), g2_ref[...])
    tw = jnp.tanh(wl).astype(BF16)
    alb = al.astype(BF16)
    bonus = jnp.zeros((tm, RW_DIM), F32)
    for d in range(2):
        w = -jax.nn.softplus(-(w0_ref[d] + _dot(tw, w2_ref[d]))) - 0.5
        lw = -jnp.exp(w)
        a = _sigmoid(a0_ref[d] + _dot(alb, a2_ref[d]))
        kd = k7 * (1.0 + (a - 1.0) * ka_ref[...])
        b = kkn * a
        bonus = bonus + _dot_0_1(r7 * kd * rk_ref[...], seg) * v7
        cum = _dot_0_1_l(tri_ref[d], lw)
        tot = _dot_0_1_l(ones_ref[...], lw)
        w_incl = jnp.exp(cum)
        w_excl = jnp.exp(cum - lw)
        w_inv = jnp.exp(-cum)
        w_rest = jnp.exp(tot - cum)
        kkt_out[d] = (kkn * w_excl).astype(BF16)
        rt_out[d] = (r7 * w_incl).astype(BF16)
        kh_out[d] = (kd * w_inv).astype(BF16)
        bh_out[d] = (b * w_inv).astype(BF16)
        kw_out[d] = (kd * w_rest).astype(BF16)
        bw_out[d] = (b * w_rest).astype(BF16)
        wtot = jnp.exp(tot)
        for ci in range(tm // CHUNK):
            wt_out[d, ci] = wtot[CHUNK * ci:CHUNK * ci + 1, :]
    bonus_out[...] = bonus


def _rw_prep(x2, T, w):
    ntok = x2.shape[0]
    tm = min(256, T)
    npos = T // tm
    nblk8 = ntok // SUBLANES
    r8 = tm // SUBLANES
    row = lambda i: (i, 0)
    drow = lambda i: (0, i, 0)
    dspec = pl.BlockSpec((2, tm, RW_DIM), drow)
    dshape = jax.ShapeDtypeStruct((2, ntok, RW_DIM), BF16)
    tri, ones = _chunk_masks(tm)
    outs = pl.pallas_call(
        functools.partial(_rw_prep_kernel, tm=tm, npos=npos),
        grid=(ntok // tm,),
        in_specs=[pl.BlockSpec((tm, D_MODEL), row),
                  pl.BlockSpec((SUBLANES, D_MODEL), lambda i: (jnp.maximum(i * r8 - 1, 0), 0)),
                  pl.BlockSpec((SUBLANES, D_MODEL), lambda i: (jnp.minimum((i + 1) * r8, nblk8 - 1), 0)),
                  _full((1, D_MODEL)), _full((D_MODEL, RW_COLS)), _full((1, RW_COLS)), _full((1, RW_COLS)),
                  _full((1, RW_DIM)), _full((1, RW_DIM)), _full((1, RW_DIM)),
                  _full((2, 1, RW_DIM)), _full((2, 2 * LORA, RW_DIM)), _full((2, 1, RW_DIM)),
                  _full((2, 2 * LORA, RW_DIM)), _full((GATE_LORA, RW_DIM)), _full((RW_DIM, RW_DIM)),
                  _full((2, tm, tm)), _full((tm, tm))],
        out_specs=[pl.BlockSpec((tm, RW_DIM), row), dspec, dspec, dspec, dspec, dspec, dspec,
                   pl.BlockSpec((2, tm // CHUNK, 1, RW_DIM), lambda i: (0, i, 0, 0)),
                   pl.BlockSpec((tm, RW_DIM), row), pl.BlockSpec((tm, RW_DIM), row)],
        out_shape=[jax.ShapeDtypeStruct((ntok, RW_DIM), BF16), dshape, dshape, dshape, dshape, dshape, dshape,
                   jax.ShapeDtypeStruct((2, ntok // CHUNK, 1, RW_DIM), F32),
                   jax.ShapeDtypeStruct((ntok, RW_DIM), F32), jax.ShapeDtypeStruct((ntok, RW_DIM), F32)],
        scratch_shapes=[pltpu.VMEM((tm + 2 * SUBLANES, D_MODEL), BF16), pltpu.VMEM((tm + 2 * SUBLANES, RW_COLS), F32)],
        compiler_params=_cparams("parallel"),
        name="rw_prep",
    )(x2, x2, x2, w["norm_mix_g"], w["w_rw"], w["mu_prev"], w["mu_next"], w["k_k"], w["k_a"], w["r_k"],
      w["w0"], w["w2"], w["a0"], w["a2"], w["g2"], w["seg"], tri, ones)
    return outs


def _chunk_masks(tm):
    t = np.arange(tm)
    same = (t[:, None] // CHUNK) == (t[None, :] // CHUNK)
    fwd = same & (t[None, :] <= t[:, None])
    bwd = same & (t[None, :] >= t[:, None])
    tri = jnp.asarray(np.stack([fwd, bwd]).astype(np.float32), BF16)
    return tri, jnp.asarray(same.astype(np.float32), BF16)


def _rw_scan_kernel(*refs):
    ins = refs[:16]
    yf_ref, yb_ref, s_ref = refs[16:]
    c = pl.program_id(1)

    @pl.when(c == 0)
    def _():
        s_ref[...] = jnp.zeros(s_ref.shape, F32)

    ri = lax.broadcasted_iota(jnp.int32, (GW, GW), 0)
    ci = lax.broadcasted_iota(jnp.int32, (GW, GW), 1)
    head_mask = (ri // CHUNK) == (ci // RW_HEAD)
    rt_ = ri % CHUNK
    cs_ = ci % CHUNK
    eye = (ri == ci).astype(F32)
    stack = lambda a: jnp.where(head_mask, jnp.concatenate([a] * GROUP, axis=0), jnp.zeros((), BF16))

    chains = []
    for d, y_ref in ((0, yf_ref), (1, yb_ref)):
        kkt, rt, kh, bh, kw, bw, v, wt = ins[8 * d:8 * d + 8]
        strict = (cs_ < rt_) if d == 0 else (cs_ > rt_)
        incl = (cs_ <= rt_) if d == 0 else (cs_ >= rt_)
        for gi in range(RW_HEADS // GROUP):
            sl = slice(GW * gi, GW * (gi + 1))
            ch = dict(y_ref=y_ref, sl=sl, s_ref=s_ref.at[d, gi], wtot=wt[:, sl])
            xkk, xr, yk, yb, vs, kws, bws = (stack(r[:, sl]) for r in (kkt, rt, kh, bh, v, kw, bw))
            a_all = _dot_nt(jnp.concatenate([xkk, xr], axis=0), jnp.concatenate([yk, yb], axis=0))
            ch.update(xkk=xkk, xr=xr, vs=vs, kbw=jnp.concatenate([kws, bws], axis=0),
                      ak=jnp.where(strict, a_all[:GW, :GW], 0.0).astype(BF16),
                      bk=jnp.where(incl, a_all[GW:, :GW], 0.0).astype(BF16),
                      bb=jnp.where(incl, a_all[GW:, GW:], 0.0).astype(BF16))
            ab = jnp.where(strict, a_all[:GW, GW:], 0.0)
            ch.update(tinv=eye - ab, pw=ab.astype(BF16))
            chains.append(ch)
    for _ in range(5):
        for ch in chains:
            ch["pw"] = _dot(ch["pw"], ch["pw"]).astype(BF16)
        for ch in chains:
            ch["tinv"] = ch["tinv"] + _dot(ch["tinv"].astype(BF16), ch["pw"])
    for ch in chains:
        ch["sb"] = ch["s_ref"][...].astype(BF16)
        ch["rhs"] = (_dot_nt(ch["xkk"], ch["sb"]) + _dot(ch["ak"], ch["vs"])).astype(BF16)
    for ch in chains:
        ch["u"] = (-_dot(ch["tinv"].astype(BF16), ch["rhs"])).astype(BF16)
    for ch in chains:
        y = _dot_nt(ch["xr"], ch["sb"]) + _dot(ch["bk"], ch["vs"]) + _dot(ch["bb"], ch["u"])
        ch["y_ref"][:, ch["sl"]] = y[0:CHUNK] + y[CHUNK:2 * CHUNK] + y[2 * CHUNK:3 * CHUNK] + y[3 * CHUNK:4 * CHUNK]
    for ch in chains:
        ch["s_ref"][...] = ch["s_ref"][...] * ch["wtot"] + _dot_tn(jnp.concatenate([ch["vs"], ch["u"]], axis=0), ch["kbw"])


def _rw_scan(v, kkt, rt, kh, bh, kw, bw, wt, B, T):
    nc = T // CHUNK
    ntok = B * T
    in_specs = []
    args = []
    for d in range(2):
        if d == 0:
            cm = lambda b, c: b * nc + c
        else:
            cm = lambda b, c: b * nc + (nc - 1 - c)
        for arr in (kkt, rt, kh, bh, kw, bw):
            in_specs.append(pl.BlockSpec((None, CHUNK, RW_DIM), lambda b, c, cm=cm, d=d: (d, cm(b, c), 0)))
            args.append(arr)
        in_specs.append(pl.BlockSpec((CHUNK, RW_DIM), lambda b, c, cm=cm: (cm(b, c), 0)))
        args.append(v)
        in_specs.append(pl.BlockSpec((None, None, 1, RW_DIM), lambda b, c, cm=cm, d=d: (d, cm(b, c), 0, 0)))
        args.append(wt)
    return pl.pallas_call(
        _rw_scan_kernel,
        grid=(B, nc),
        in_specs=in_specs,
        out_specs=[pl.BlockSpec((CHUNK, RW_DIM), lambda b, c: (b * nc + c, 0)),
                   pl.BlockSpec((CHUNK, RW_DIM), lambda b, c: (b * nc + (nc - 1 - c), 0))],
        out_shape=[jax.ShapeDtypeStruct((ntok, RW_DIM), F32), jax.ShapeDtypeStruct((ntok, RW_DIM), F32)],
        scratch_shapes=[pltpu.VMEM((2, RW_HEADS // GROUP, GW, GW), F32)],
        compiler_params=_cparams("parallel", "arbitrary"),
        name="rw_scan",
    )(*args)


def _mem_kv_kernel(m_ref, g_ref, wk_ref, wv_ref, gk_ref, k_out, v_out):
    m = _rms(m_ref[...], g_ref[...]).astype(BF16)
    k = _dot(m, wk_ref[...])
    v_out[...] = _dot(m, wv_ref[...]).astype(BF16)
    for hd in range(X_HEADS):
        sl = slice(X_HEAD * hd, X_HEAD * (hd + 1))
        k_out[:, sl] = _rms(k[:, sl], gk_ref[...]).astype(BF16)


def _mem_kv(mem2, w):
    n = mem2.shape[0]
    row = lambda i: (i, 0)
    return pl.pallas_call(
        _mem_kv_kernel,
        grid=(n // N_MEM,),
        in_specs=[pl.BlockSpec((N_MEM, D_MODEL), row), _full((1, D_MODEL)), _full((D_MODEL, X_DIM)),
                  _full((D_MODEL, X_DIM)), _full((1, X_HEAD))],
        out_specs=[pl.BlockSpec((N_MEM, X_DIM), row), pl.BlockSpec((N_MEM, X_DIM), row)],
        out_shape=[jax.ShapeDtypeStruct((n, X_DIM), BF16), jax.ShapeDtypeStruct((n, X_DIM), BF16)],
        compiler_params=_cparams("parallel"),
        name="mem_kv",
    )(mem2, w["mem_norm_g"], w["w_mk"], w["w_mv"], w["x_kn_g"])


def _xattn_kernel(x_ref, g_ref, wq_ref, gq_ref, mk_ref, mv_ref, o_ref):
    h = _rms(x_ref[...], g_ref[...]).astype(BF16)
    q = _dot(h, wq_ref[...])
    for hd in range(X_HEADS):
        sl = slice(X_HEAD * hd, X_HEAD * (hd + 1))
        qh = _rms(q[:, sl], gq_ref[...]).astype(BF16)
        s = _dot_nt(qh, mk_ref[:, sl])
        p = jnp.exp(s - jnp.max(s, axis=-1, keepdims=True))
        o = _dot(p.astype(BF16), mv_ref[:, sl]) / jnp.sum(p, axis=-1, keepdims=True)
        o_ref[:, sl] = o.astype(BF16)


def _xattn(x2, mk, mv, B, T, w):
    tq = min(512, T)
    nq = T // tq
    return pl.pallas_call(
        _xattn_kernel,
        grid=(B, nq),
        in_specs=[pl.BlockSpec((tq, D_MODEL), lambda b, i: (b * nq + i, 0)), _full((1, D_MODEL)),
                  _full((D_MODEL, X_DIM)), _full((1, X_HEAD)),
                  pl.BlockSpec((N_MEM, X_DIM), lambda b, i: (b, 0)), pl.BlockSpec((N_MEM, X_DIM), lambda b, i: (b, 0))],
        out_specs=pl.BlockSpec((tq, X_DIM), lambda b, i: (b * nq + i, 0)),
        out_shape=jax.ShapeDtypeStruct((B * T, X_DIM), BF16),
        compiler_params=_cparams("parallel", "parallel"),
        name="xattn",
    )(x2, w["norm_mix_g"], w["w_xq"], w["x_qn_g_scaled"], mk, mv)


def _merge_kernel(x_ref, oa_ref, yf_ref, yb_ref, bonus_ref, g7_ref, oc_ref, g_ref, wg_ref, woa_ref, wob_ref,
                  woc_ref, wout_ref, lng_ref, lnb_ref, seg_ref, o_ref):
    x = x_ref[...]
    h = _rms(x, g_ref[...]).astype(BF16)
    seg = seg_ref[...]
    y7 = yf_ref[...] + yb_ref[...] + bonus_ref[...]
    mu = _dot_0_1(y7, seg) * (1.0 / RW_HEAD)
    dy = y7 - mu
    var = _dot_0_1(dy * dy, seg) * (1.0 / RW_HEAD)
    y7 = dy * lax.rsqrt(var + LNX_EPS) * lng_ref[...] + lnb_ref[...]
    yb = (y7 * g7_ref[...]).astype(BF16)
    merged = jnp.zeros(x.shape, F32)
    for bi, (br, wo) in enumerate(((oa_ref[...], woa_ref), (yb, wob_ref), (oc_ref[...], woc_ref))):
        gate = _sigmoid(_dot(h, wg_ref[:, D_MODEL * bi:D_MODEL * (bi + 1)]))
        merged = merged + gate * _dot(br, wo[...])
    o_ref[...] = x + _dot(merged.astype(BF16), wout_ref[...])


def _merge(x2, oa, yf, yb, bonus, g7, oc, T, w):
    ntok = x2.shape[0]
    tm = min(256, T)
    row = lambda i: (i, 0)
    half = lambda: pl.BlockSpec((tm, RW_DIM), row)
    return pl.pallas_call(
        _merge_kernel,
        grid=(ntok // tm,),
        in_specs=[pl.BlockSpec((tm, D_MODEL), row), half(), half(), half(), half(), half(), half(),
                  _full((1, D_MODEL)), _full((D_MODEL, 3 * D_MODEL)), _full((RW_DIM, D_MODEL)),
                  _full((RW_DIM, D_MODEL)), _full((RW_DIM, D_MODEL)), _full((D_MODEL, D_MODEL)),
                  _full((1, RW_DIM)), _full((1, RW_DIM)), _full((RW_DIM, RW_DIM))],
        out_specs=pl.BlockSpec((tm, D_MODEL), row),
        out_shape=jax.ShapeDtypeStruct((ntok, D_MODEL), F32),
        compiler_params=_cparams("parallel"),
        name="merge",
    )(x2, oa, yf, yb, bonus, g7, oc, w["norm_mix_g"], w["w_gate"], w["w_o_a"], w["w_o_b"], w["w_o_c"],
      w["w_out"], w["lnx_g"], w["lnx_b"], w["seg"])


def _ffn_kernel(x_ref, xp_ref, xn_ref, g_ref, wug_ref, wuv_ref, cw_ref, cb_ref, wd_ref, o_ref, h_scr, ug_scr,
                *, tm, npos):
    i = pl.program_id(0)
    f = pl.program_id(1)

    @pl.when(f == 0)
    def _():
        not_first = (i % npos != 0).astype(F32)
        not_last = (i % npos != npos - 1).astype(F32)
        g = g_ref[...]
        h_scr[0:SUBLANES, :] = (_rms(xp_ref[...], g) * not_first).astype(BF16)
        h_scr[SUBLANES:SUBLANES + tm, :] = _rms(x_ref[...], g).astype(BF16)
        h_scr[SUBLANES + tm:, :] = (_rms(xn_ref[...], g) * not_last).astype(BF16)

    ug_scr[...] = _dot(h_scr[...], wug_ref[...])
    uv = _dot(h_scr[SUBLANES:SUBLANES + tm, :], wuv_ref[...])
    cw = cw_ref[...]
    c = (cw[0:1] * ug_scr[SUBLANES - 1:SUBLANES - 1 + tm, :] + cw[1:2] * ug_scr[SUBLANES:SUBLANES + tm, :]
         + cw[2:3] * ug_scr[SUBLANES + 1:SUBLANES + 1 + tm, :] + cb_ref[...])
    act = 0.5 * c * (1.0 + lax.erf(c * np.float32(1.0 / np.sqrt(2.0)))) * uv
    part = _dot(act.astype(BF16), wd_ref[...])

    @pl.when(f == 0)
    def _():
        o_ref[...] = x_ref[...] + part

    @pl.when(f != 0)
    def _():
        o_ref[...] += part


def _ffn(x2, T, w):
    ntok = x2.shape[0]
    tm = min(512, T)
    fc = D_FF // 2
    npos = T // tm
    nblk8 = ntok // SUBLANES
    r8 = tm // SUBLANES
    row = lambda i, f: (i, 0)
    return pl.pallas_call(
        functools.partial(_ffn_kernel, tm=tm, npos=npos),
        grid=(ntok // tm, D_FF // fc),
        in_specs=[pl.BlockSpec((tm, D_MODEL), row),
                  pl.BlockSpec((SUBLANES, D_MODEL), lambda i, f: (jnp.maximum(i * r8 - 1, 0), 0)),
                  pl.BlockSpec((SUBLANES, D_MODEL), lambda i, f: (jnp.minimum((i + 1) * r8, nblk8 - 1), 0)),
                  pl.BlockSpec((1, D_MODEL), lambda i, f: (0, 0)),
                  pl.BlockSpec((D_MODEL, fc), lambda i, f: (0, f)),
                  pl.BlockSpec((D_MODEL, fc), lambda i, f: (0, f)),
                  pl.BlockSpec((3, fc), lambda i, f: (0, f)),
                  pl.BlockSpec((1, fc), lambda i, f: (0, f)),
                  pl.BlockSpec((fc, D_MODEL), lambda i, f: (f, 0))],
        out_specs=pl.BlockSpec((tm, D_MODEL), row),
        out_shape=jax.ShapeDtypeStruct((ntok, D_MODEL), F32),
        scratch_shapes=[pltpu.VMEM((tm + 2 * SUBLANES, D_MODEL), BF16), pltpu.VMEM((tm + 2 * SUBLANES, fc), F32)],
        compiler_params=_cparams("parallel", "arbitrary"),
        name="conv_ffn",
    )(x2, x2, x2, w["norm_ffn_g"], w["w_up_gate"], w["w_up_val"], w["conv_w"], w["conv_b"], w["w_down"])


def _prep_weights(p):
    w = {}
    row = lambda a: a.reshape(1, -1).astype(F32)
    w_in = p["w_in"]
    o = np.cumsum([0, Q_LORA, KV_LORA, MLA_ROPE, RW_COLS, X_DIM, 3 * D_MODEL])
    seg = lambda i: w_in[:, o[i]:o[i + 1]]
    zc = lambda n: jnp.zeros((D_MODEL, n), F32)
    w["w_c"] = jnp.concatenate([seg(0), seg(1), zc(MLA_NOPE), seg(2), zc(LANES - MLA_QK)], axis=1).astype(BF16)
    w["w_rw"] = seg(3).astype(BF16)
    w["w_xq"] = seg(4).astype(BF16)
    w["w_gate"] = seg(5).astype(BF16)
    w["norm_mix_g"] = row(p["norm_mix_g"])
    w["q_norm_g"] = row(p["q_norm_g"])
    w["kv_norm_g"] = row(p["kv_norm_g"])
    pad_slot = lambda a: jnp.pad(a, ((0, 0), (0, 0), (0, LANES - a.shape[-1]))).reshape(a.shape[0], -1)
    w["w_uq"] = pad_slot(p["w_uq"].reshape(Q_LORA, MLA_HEADS, MLA_QK)).astype(BF16)
    ukv = p["w_ukv"].reshape(KV_LORA, MLA_HEADS, MLA_NOPE + MLA_V)
    w["w_uk"] = pad_slot(ukv[:, :, :MLA_NOPE]).astype(BF16)
    w["w_uv"] = pad_slot(ukv[:, :, MLA_NOPE:]).T.astype(BF16)
    slot_gain = lambda g: jnp.tile(jnp.pad(g.reshape(-1), (0, LANES - MLA_QK)), MLA_HEADS).reshape(1, -1)
    w["gq_slots"] = slot_gain(p["mla_qn_g"]) * np.float32(MLA_QK ** -0.5)
    w["gk_slots"] = slot_gain(p["mla_kn_g"])
    for name in ("mu_prev", "mu_next", "k_k", "k_a", "r_k", "lnx_g", "lnx_b", "mem_norm_g", "x_kn_g", "norm_ffn_g",
                 "conv_b"):
        w[name] = row(p[name])
    w["w0"] = jnp.stack([p["w0_f"], p["w0_b"]]).reshape(2, 1, RW_DIM)
    w["a0"] = jnp.stack([p["a0_f"], p["a0_b"]]).reshape(2, 1, RW_DIM)
    zl = jnp.zeros((LORA, RW_DIM), F32)
    w["w2"] = jnp.stack([jnp.concatenate([p["w2_f"], zl]), jnp.concatenate([zl, p["w2_b"]])]).astype(BF16)
    w["a2"] = jnp.stack([jnp.concatenate([p["a2_f"], zl]), jnp.concatenate([zl, p["a2_b"]])]).astype(BF16)
    w["g2"] = p["g2"].astype(BF16)
    hid = np.arange(RW_DIM) // RW_HEAD
    w["seg"] = jnp.asarray((hid[:, None] == hid[None, :]).astype(np.float32), BF16)
    mkv = p["w_mkv"].reshape(D_MODEL, X_HEADS, 2 * X_HEAD)
    w["w_mk"] = mkv[:, :, :X_HEAD].reshape(D_MODEL, X_DIM).astype(BF16)
    w["w_mv"] = mkv[:, :, X_HEAD:].reshape(D_MODEL, X_DIM).astype(BF16)
    w["x_qn_g_scaled"] = row(p["x_qn_g"]) * np.float32(X_HEAD ** -0.5)
    for name in ("w_o_a", "w_o_b", "w_o_c", "w_out", "w_down"):
        w[name] = p[name].astype(BF16)
    w["w_up_gate"] = p["w_up"][:, :D_FF].astype(BF16)
    w["w_up_val"] = p["w_up"][:, D_FF:].astype(BF16)
    w["conv_w"] = p["conv_w"].astype(F32)
    return w


def _rope_tables(T):
    half = MLA_ROPE // 2
    inv = jnp.power(ROPE_THETA, -jnp.arange(half, dtype=F32) / half)
    ang = jnp.arange(T, dtype=F32)[:, None] * inv[None, :]
    cos, sin = jnp.cos(ang), jnp.sin(ang)
    z = lambda n: jnp.zeros((T, n), F32)
    cos_t = jnp.concatenate([jnp.ones((T, MLA_NOPE), F32), cos, cos, jnp.ones((T, LANES - MLA_QK), F32)], axis=1)
    sin_a = jnp.concatenate([z(MLA_NOPE), -sin, z(half), z(LANES - MLA_QK)], axis=1)
    sin_b = jnp.concatenate([z(MLA_NOPE), z(half), sin, z(LANES - MLA_QK)], axis=1)
    return cos_t, sin_a, sin_b


def _layer(x, mem, w):
    B, T, _ = x.shape
    x2 = x.reshape(B * T, D_MODEL)
    q, k, v = _mla_prep(x2, T, w, _rope_tables(T))
    o_a = _flash(q, k, v, B, T)
    v7, kkt, rt, kh, bh, kw, bw, wt, bonus, g7 = _rw_prep(x2, T, w)
    y_f, y_b = _rw_scan(v7, kkt, rt, kh, bh, kw, bw, wt, B, T)
    mk, mv = _mem_kv(mem.reshape(B * N_MEM, D_MODEL), w)
    o_c = _xattn(x2, mk, mv, B, T, w)
    x1 = _merge(x2, o_a, y_f, y_b, bonus, g7, o_c, T, w)
    return _ffn(x1, T, w).reshape(B, T, D_MODEL)


def kernel(x_prompt, x_sample, mem_prompt, mem_sample, norm_mix_g, w_in, q_norm_g, w_uq, kv_norm_g, w_ukv, mla_qn_g, mla_kn_g, w_o_a, mu_prev, mu_next, w0_f, w2_f, a0_f, a2_f, w0_b, w2_b, a0_b, a2_b, g2, k_k, k_a, r_k, lnx_g, lnx_b, w_o_b, mem_norm_g, w_mkv, x_qn_g, x_kn_g, w_o_c, w_out, norm_ffn_g, w_up, conv_w, conv_b, w_down):
    p = dict(norm_mix_g=norm_mix_g, w_in=w_in, q_norm_g=q_norm_g, w_uq=w_uq, kv_norm_g=kv_norm_g, w_ukv=w_ukv,
             mla_qn_g=mla_qn_g, mla_kn_g=mla_kn_g, w_o_a=w_o_a, mu_prev=mu_prev, mu_next=mu_next,
             w0_f=w0_f, w2_f=w2_f, a0_f=a0_f, a2_f=a2_f, w0_b=w0_b, w2_b=w2_b, a0_b=a0_b, a2_b=a2_b,
             g2=g2, k_k=k_k, k_a=k_a, r_k=r_k, lnx_g=lnx_g, lnx_b=lnx_b, w_o_b=w_o_b,
             mem_norm_g=mem_norm_g, w_mkv=w_mkv, x_qn_g=x_qn_g, x_kn_g=x_kn_g, w_o_c=w_o_c, w_out=w_out,
             norm_ffn_g=norm_ffn_g, w_up=w_up, conv_w=conv_w, conv_b=conv_b, w_down=w_down)
    w = _prep_weights({name: a[0] for name, a in p.items()})
    return (_layer(x_prompt, mem_prompt, w), _layer(x_sample, mem_sample, w))
```

```python
import functools

import numpy as np
import jax
import jax.numpy as jnp
from jax import lax
from jax.experimental import pallas as pl
from jax.experimental.pallas import tpu as pltpu

F32 = jnp.float32
BF16 = jnp.bfloat16

D_MODEL = 1024
RMS_EPS = 1e-6
N_MEM = 256
MLA_HEADS = 8
MLA_NOPE = 64
MLA_ROPE = 32
MLA_QK = MLA_NOPE + MLA_ROPE
MLA_V = 64
Q_LORA = 384
KV_LORA = 256
ROPE_THETA = 10000.0
RW_HEADS = 8
RW_HEAD = 64
RW_DIM = RW_HEADS * RW_HEAD
LORA = 64
GATE_LORA = 128
RW_COLS = 3 * RW_DIM + 4 * LORA + GATE_LORA
LNX_EPS = 64e-5
X_HEADS = 4
X_HEAD = 128
X_DIM = X_HEADS * X_HEAD
D_FF = 2816

LANES = 128
SUBLANES = 8
CHUNK = 64
GROUP = 2
GW = GROUP * RW_HEAD
VMEM_LIMIT = 56 * 1024 * 1024


def _cparams(*sem):
    return pltpu.CompilerParams(dimension_semantics=sem, vmem_limit_bytes=VMEM_LIMIT)


def _rms(x, g, eps=RMS_EPS):
    return x * lax.rsqrt(jnp.mean(x * x, axis=-1, keepdims=True) + eps) * g


def _dot(a, b):
    return jnp.dot(a, b, preferred_element_type=F32)


def _dot_nt(a, b):
    return lax.dot_general(a, b, (((1,), (1,)), ((), ())), preferred_element_type=F32)


def _dot_tn(a, b):
    return lax.dot_general(a, b, (((0,), (0,)), ((), ())), preferred_element_type=F32)


def _split2(x):
    hi = x.astype(BF16)
    lo = (x - hi.astype(F32)).astype(BF16)
    return hi, lo


def _dot_0_1(x, e):
    hi, lo = _split2(x)
    return _dot(hi, e) + _dot(lo, e)


def _dot_0_1_l(e, x):
    hi = x.astype(BF16)
    r1 = x - hi.astype(F32)
    mid = r1.astype(BF16)
    lo = (r1 - mid.astype(F32)).astype(BF16)
    return _dot(e, hi) + _dot(e, mid) + _dot(e, lo)


def _sigmoid(z):
    return 1.0 / (1.0 + jnp.exp(-z))


def _full(shape):
    nd = len(shape)
    return pl.BlockSpec(shape, lambda *_: (0,) * nd)


def _mla_prep_kernel(x_ref, g_ref, wc_ref, gq_ref, gkv_ref, wuq_ref, wuqr_ref, wuk_ref, wuv_ref,
                     qa_ref, qb_ref, ka_ref, kba_ref, kbb_ref, q_out, k_out, v_out):
    h = _rms(x_ref[...], g_ref[...]).astype(BF16)
    c = _dot(h, wc_ref[...])
    cq = _rms(c[:, :Q_LORA], gq_ref[...]).astype(BF16)
    ckv = _rms(c[:, Q_LORA:Q_LORA + KV_LORA], gkv_ref[...]).astype(BF16)
    kr = c[:, Q_LORA + KV_LORA:]
    q = _dot(cq, wuq_ref[...])
    qp = _dot(cq, wuqr_ref[...])
    k = _dot(ckv, wuk_ref[...])
    vt = _dot_nt(wuv_ref[...], ckv)
    slot_row = lax.broadcasted_iota(jnp.int32, vt.shape, 0) % LANES
    v_out[0] = jnp.where(slot_row == MLA_V, 1.0, vt).astype(BF16)
    qa = qa_ref[...]
    qb = qb_ref[...]
    ka = ka_ref[...]
    half = MLA_ROPE // 2
    kr_roped = kr * ka + pltpu.roll(kr, LANES - half, 1) * kba_ref[...] + pltpu.roll(kr, half, 1) * kbb_ref[...]
    kr_ss = jnp.sum(kr * kr, axis=-1, keepdims=True)
    for hd in range(MLA_HEADS):
        sl = slice(LANES * hd, LANES * (hd + 1))
        t = q[:, sl]
        ss = jnp.sum(t * t, axis=-1, keepdims=True) * (1.0 / MLA_QK)
        q_out[:, sl] = ((t * qa + qp[:, sl] * qb) * lax.rsqrt(ss + RMS_EPS)).astype(BF16)
        t = k[:, sl]
        ss = (jnp.sum(t * t, axis=-1, keepdims=True) + kr_ss) * (1.0 / MLA_QK)
        k_out[:, sl] = ((t * ka + kr_roped) * lax.rsqrt(ss + RMS_EPS)).astype(BF16)


def _mla_prep(x2, T, w, tabs):
    ntok = x2.shape[0]
    tm = _attn_tile(T)
    npos = T // tm
    row = lambda i: (i, 0)
    pos = lambda i: (i % npos, 0)
    return pl.pallas_call(
        _mla_prep_kernel,
        grid=(ntok // tm,),
        in_specs=[pl.BlockSpec((tm, D_MODEL), row), _full((1, D_MODEL)), _full((D_MODEL, 768)),
                  _full((1, Q_LORA)), _full((1, KV_LORA)), _full((Q_LORA, 1024)), _full((Q_LORA, 1024)),
                  _full((KV_LORA, 1024)), _full((1024, KV_LORA))] + [pl.BlockSpec((tm, LANES), pos)] * 5,
        out_specs=[pl.BlockSpec((tm, 1024), row), pl.BlockSpec((tm, 1024), row),
                   pl.BlockSpec((1, 1024, tm), lambda i: (i, 0, 0))],
        out_shape=[jax.ShapeDtypeStruct((ntok, 1024), BF16), jax.ShapeDtypeStruct((ntok, 1024), BF16),
                   jax.ShapeDtypeStruct((ntok // tm, 1024, tm), BF16)],
        compiler_params=_cparams("parallel"),
        name="mla_prep",
    )(x2, w["norm_mix_g"], w["w_c"], w["q_norm_g"], w["kv_norm_g"], w["w_uq"], w["w_uq_rot"], w["w_uk"],
      w["w_uv"], *tabs)


def _flash_kernel(q_ref, k_ref, vt_ref, o_ref, s0_ref, s1_ref, *, tk, nk):
    tq = q_ref.shape[0]
    slots = [slice(LANES * h, LANES * (h + 1)) for h in range(2)]
    per = tk // vt_ref.shape[2]

    def scores(j, s_ref):
        kblk = k_ref[pl.ds(pl.multiple_of(j * tk, tk), tk), :]
        for h in range(2):
            s_ref[h] = _dot_nt(kblk[:, slots[h]], q_ref[:, slots[h]])

    def consume(j, s_ref, state):
        vt = jnp.concatenate([vt_ref[per * j + r] for r in range(per)], axis=1)
        m_news = [jnp.maximum(state[h][0], jnp.max(s_ref[h], axis=0, keepdims=True)) for h in range(2)]
        pts = [jnp.exp((s_ref[h] - m_news[h]).astype(BF16)) for h in range(2)]
        return tuple((m_news[h], jnp.exp(state[h][0] - m_news[h]) * state[h][1] + _dot(vt[slots[h], :], pts[h]))
                     for h in range(2))

    def body(t, state):
        j = 2 * t
        scores(j + 1, s1_ref)
        state = consume(j, s0_ref, state)
        scores(j + 2, s0_ref)
        return consume(j + 1, s1_ref, state)

    state = tuple((jnp.full((1, tq), -jnp.inf, F32), jnp.zeros((LANES, tq), F32)) for _ in range(2))
    scores(0, s0_ref)
    state = lax.fori_loop(0, nk // 2 - 1, body, state)
    scores(nk - 1, s1_ref)
    state = consume(nk - 2, s0_ref, state)
    state = consume(nk - 1, s1_ref, state)
    outs = [acc[0:MLA_V, :] / acc[MLA_V:MLA_V + 1, :] for _, acc in state]
    o_ref[...] = jnp.concatenate(outs, axis=0).T.astype(BF16)


def _attn_tile(T):
    return min(512, T // 2)


def _flash(q, k, vt, B, T):
    tq = _attn_tile(T)
    tk = min(1024, T // 2)
    tv = _attn_tile(T)
    nq = T // tq
    nk = T // tk
    assert nk % 2 == 0 and tk % tv == 0, (T, tk, tv)
    return pl.pallas_call(
        functools.partial(_flash_kernel, tk=tk, nk=nk),
        grid=(B, MLA_HEADS // 2, nq),
        in_specs=[pl.BlockSpec((tq, 2 * LANES), lambda b, hp, i: (b * nq + i, hp)),
                  pl.BlockSpec((T, 2 * LANES), lambda b, hp, i: (b, hp)),
                  pl.BlockSpec((T // tv, 2 * LANES, tv), lambda b, hp, i: (b, hp, 0))],
        out_specs=pl.BlockSpec((tq, LANES), lambda b, hp, i: (b * nq + i, hp)),
        out_shape=jax.ShapeDtypeStruct((B * T, MLA_HEADS * MLA_V), BF16),
        scratch_shapes=[pltpu.VMEM((2, tk, tq), F32), pltpu.VMEM((2, tk, tq), F32)],
        compiler_params=_cparams("parallel", "parallel", "arbitrary"),
        name="mla_flash",
    )(q, k, vt)


def _rw_prep_kernel(x_ref, xp_ref, xn_ref, g_ref, wrw_ref, mup_ref, mun_ref, kk_ref, ka_ref, rk_ref,
                    w0_ref, w2_ref, a0_ref, a2_ref, g2_ref, seg_ref, tri_ref, ones_ref,
                    v_out, kkt_out, rt_out, kh_out, bh_out, kw_out, bw_out, wt_out, bonus_out, g7_out,
                    h_scr, rw_scr, *, tm, npos):
    i = pl.program_id(0)
    not_first = (i % npos != 0).astype(F32)
    not_last = (i % npos != npos - 1).astype(F32)
    g = g_ref[...]
    h_scr[0:SUBLANES, :] = (_rms(xp_ref[...], g) * not_first).astype(BF16)
    h_scr[SUBLANES:SUBLANES + tm, :] = _rms(x_ref[...], g).astype(BF16)
    h_scr[SUBLANES + tm:, :] = (_rms(xn_ref[...], g) * not_last).astype(BF16)
    rw_scr[...] = _dot(h_scr[...], wrw_ref[...])
    cur = rw_scr[SUBLANES:SUBLANES + tm, :]
    prev = rw_scr[SUBLANES - 1:SUBLANES - 1 + tm, :]
    nxt = rw_scr[SUBLANES + 1:SUBLANES + 1 + tm, :]
    rwf = cur + mup_ref[...] * (prev - cur) + mun_ref[...] * (nxt - cur)
    r7 = rwf[:, 0:RW_DIM]
    k7 = rwf[:, RW_DIM:2 * RW_DIM]
    v7 = rwf[:, 2 * RW_DIM:3 * RW_DIM]
    wl = rwf[:, 3 * RW_DIM:3 * RW_DIM + 2 * LORA]
    al = rwf[:, 3 * RW_DIM + 2 * LORA:3 * RW_DIM + 4 * LORA]
    gl = rwf[:, 3 * RW_DIM + 4 * LORA:]
    seg = seg_ref[...]
    kx = k7 * kk_ref[...]
    kkn = kx / jnp.maximum(jnp.sqrt(_dot_0_1(kx * kx, seg)), 1e-12)
    v_out[...] = v7.astype(BF16)
    g7_out[...] = _dot(_sigmoid(gl).astype(BF16), g2_ref[...])
    tw = jnp.tanh(wl).astype(BF16)
    alb = al.astype(BF16)
    bonus = jnp.zeros((tm, RW_DIM), F32)
    for d in range(2):
        w = -jax.nn.softplus(-(w0_ref[d] + _dot(tw, w2_ref[d]))) - 0.5
        lw = -jnp.exp(w)
        a = _sigmoid(a0_ref[d] + _dot(alb, a2_ref[d]))
        kd = k7 * (1.0 + (a - 1.0) * ka_ref[...])
        b = kkn * a
        bonus = bonus + _dot_0_1(r7 * kd * rk_ref[...], seg) * v7
        cum = _dot_0_1_l(tri_ref[d], lw)
        tot = _dot_0_1_l(ones_ref[...], lw)
        w_incl = jnp.exp(cum)
        w_excl = jnp.exp(cum - lw)
        w_inv = jnp.exp(-cum)
        w_rest = jnp.exp(tot - cum)
        kkt_out[d] = (kkn * w_excl).astype(BF16)
        rt_out[d] = (r7 * w_incl).astype(BF16)
        kh_out[d] = (kd * w_inv).astype(BF16)
        bh_out[d] = (b * w_inv).astype(BF16)
        kw_out[d] = (kd * w_rest).astype(BF16)
        bw_out[d] = (b * w_rest).astype(BF16)
        wtot = jnp.exp(tot)
        for ci in range(tm // CHUNK):
            wt_out[d, ci] = wtot[CHUNK * ci:CHUNK * ci + 1, :]
    bonus_out[...] = bonus


def _rw_prep(x2, T, w):
    ntok = x2.shape[0]
    tm = min(256, T)
    npos = T // tm
    nblk8 = ntok // SUBLANES
    r8 = tm // SUBLANES
    row = lambda i: (i, 0)
    drow = lambda i: (0, i, 0)
    dspec = pl.BlockSpec((2, tm, RW_DIM), drow)
    dshape = jax.ShapeDtypeStruct((2, ntok, RW_DIM), BF16)
    tri, ones = _chunk_masks(tm)
    outs = pl.pallas_call(
        functools.partial(_rw_prep_kernel, tm=tm, npos=npos),
        grid=(ntok // tm,),
        in_specs=[pl.BlockSpec((tm, D_MODEL), row),
                  pl.BlockSpec((SUBLANES, D_MODEL), lambda i: (jnp.maximum(i * r8 - 1, 0), 0)),
                  pl.BlockSpec((SUBLANES, D_MODEL), lambda i: (jnp.minimum((i + 1) * r8, nblk8 - 1), 0)),
                  _full((1, D_MODEL)), _full((D_MODEL, RW_COLS)), _full((1, RW_COLS)), _full((1, RW_COLS)),
                  _full((1, RW_DIM)), _full((1, RW_DIM)), _full((1, RW_DIM)),
                  _full((2, 1, RW_DIM)), _full((2, 2 * LORA, RW_DIM)), _full((2, 1, RW_DIM)),
                  _full((2, 2 * LORA, RW_DIM)), _full((GATE_LORA, RW_DIM)), _full((RW_DIM, RW_DIM)),
                  _full((2, tm, tm)), _full((tm, tm))],
        out_specs=[pl.BlockSpec((tm, RW_DIM), row), dspec, dspec, dspec, dspec, dspec, dspec,
                   pl.BlockSpec((2, tm // CHUNK, 1, RW_DIM), lambda i: (0, i, 0, 0)),
                   pl.BlockSpec((tm, RW_DIM), row), pl.BlockSpec((tm, RW_DIM), row)],
        out_shape=[jax.ShapeDtypeStruct((ntok, RW_DIM), BF16), dshape, dshape, dshape, dshape, dshape, dshape,
                   jax.ShapeDtypeStruct((2, ntok // CHUNK, 1, RW_DIM), F32),
                   jax.ShapeDtypeStruct((ntok, RW_DIM), F32), jax.ShapeDtypeStruct((ntok, RW_DIM), F32)],
        scratch_shapes=[pltpu.VMEM((tm + 2 * SUBLANES, D_MODEL), BF16), pltpu.VMEM((tm + 2 * SUBLANES, RW_COLS), F32)],
        compiler_params=_cparams("parallel"),
        name="rw_prep",
    )(x2, x2, x2, w["norm_mix_g"], w["w_rw"], w["mu_prev"], w["mu_next"], w["k_k"], w["k_a"], w["r_k"],
      w["w0"], w["w2"], w["a0"], w["a2"], w["g2"], w["seg"], tri, ones)
    return outs


def _chunk_masks(tm):
    t = np.arange(tm)
    same = (t[:, None] // CHUNK) == (t[None, :] // CHUNK)
    fwd = same & (t[None, :] <= t[:, None])
    bwd = same & (t[None, :] >= t[:, None])
    tri = jnp.asarray(np.stack([fwd, bwd]).astype(np.float32), BF16)
    return tri, jnp.asarray(same.astype(np.float32), BF16)


def _rw_scan_kernel(*refs):
    ins = refs[:16]
    yf_ref, yb_ref, s_ref = refs[16:]
    c = pl.program_id(1)

    @pl.when(c == 0)
    def _():
        s_ref[...] = jnp.zeros(s_ref.shape, F32)

    ri = lax.broadcasted_iota(jnp.int32, (GW, GW), 0)
    ci = lax.broadcasted_iota(jnp.int32, (GW, GW), 1)
    head_mask = (ri // CHUNK) == (ci // RW_HEAD)
    rt_ = ri % CHUNK
    cs_ = ci % CHUNK
    eye = (ri == ci).astype(F32)
    stack = lambda a: jnp.where(head_mask, jnp.concatenate([a] * GROUP, axis=0), jnp.zeros((), BF16))

    chains = []
    for d, y_ref in ((0, yf_ref), (1, yb_ref)):
        kkt, rt, kh, bh, kw, bw, v, wt = ins[8 * d:8 * d + 8]
        strict = (cs_ < rt_) if d == 0 else (cs_ > rt_)
        incl = (cs_ <= rt_) if d == 0 else (cs_ >= rt_)
        for gi in range(RW_HEADS // GROUP):
            sl = slice(GW * gi, GW * (gi + 1))
            ch = dict(y_ref=y_ref, sl=sl, s_ref=s_ref.at[d, gi], wtot=wt[:, sl])
            xkk, xr, yk, yb, vs, kws, bws = (stack(r[:, sl]) for r in (kkt, rt, kh, bh, v, kw, bw))
            a_all = _dot_nt(jnp.concatenate([xkk, xr], axis=0), jnp.concatenate([yk, yb], axis=0))
            ch.update(xkk=xkk, xr=xr, vs=vs, kbw=jnp.concatenate([kws, bws], axis=0),
                      ak=jnp.where(strict, a_all[:GW, :GW], 0.0).astype(BF16),
                      bk=jnp.where(incl, a_all[GW:, :GW], 0.0).astype(BF16),
                      bb=jnp.where(incl, a_all[GW:, GW:], 0.0).astype(BF16))
            ab = jnp.where(strict, a_all[:GW, GW:], 0.0)
            ch.update(tinv=eye - ab, pw=ab.astype(BF16))
            chains.append(ch)
    for _ in range(5):
        for ch in chains:
            ch["pw"] = _dot(ch["pw"], ch["pw"]).astype(BF16)
        for ch in chains:
            ch["tinv"] = ch["tinv"] + _dot(ch["tinv"].astype(BF16), ch["pw"])
    for ch in chains:
        ch["sb"] = ch["s_ref"][...].astype(BF16)
        ch["rhs"] = (_dot_nt(ch["xkk"], ch["sb"]) + _dot(ch["ak"], ch["vs"])).astype(BF16)
    for ch in chains:
        ch["u"] = (-_dot(ch["tinv"].astype(BF16), ch["rhs"])).astype(BF16)
    for ch in chains:
        y = _dot_nt(ch["xr"], ch["sb"]) + _dot(ch["bk"], ch["vs"]) + _dot(ch["bb"], ch["u"])
        ch["y_ref"][:, ch["sl"]] = sum(y[CHUNK * hd:CHUNK * (hd + 1)] for hd in range(GROUP))
    for ch in chains:
        ch["s_ref"][...] = ch["s_ref"][...] * ch["wtot"] + _dot_tn(jnp.concatenate([ch["vs"], ch["u"]], axis=0), ch["kbw"])


def _rw_scan(v, kkt, rt, kh, bh, kw, bw, wt, B, T):
    nc = T // CHUNK
    ntok = B * T
    in_specs = []
    args = []
    for d in range(2):
        if d == 0:
            cm = lambda b, c: b * nc + c
        else:
            cm = lambda b, c: b * nc + (nc - 1 - c)
        for arr in (kkt, rt, kh, bh, kw, bw):
            in_specs.append(pl.BlockSpec((None, CHUNK, RW_DIM), lambda b, c, cm=cm, d=d: (d, cm(b, c), 0)))
            args.append(arr)
        in_specs.append(pl.BlockSpec((CHUNK, RW_DIM), lambda b, c, cm=cm: (cm(b, c), 0)))
        args.append(v)
        in_specs.append(pl.BlockSpec((None, None, 1, RW_DIM), lambda b, c, cm=cm, d=d: (d, cm(b, c), 0, 0)))
        args.append(wt)
    return pl.pallas_call(
        _rw_scan_kernel,
        grid=(B, nc),
        in_specs=in_specs,
        out_specs=[pl.BlockSpec((CHUNK, RW_DIM), lambda b, c: (b * nc + c, 0)),
                   pl.BlockSpec((CHUNK, RW_DIM), lambda b, c: (b * nc + (nc - 1 - c), 0))],
        out_shape=[jax.ShapeDtypeStruct((ntok, RW_DIM), F32), jax.ShapeDtypeStruct((ntok, RW_DIM), F32)],
        scratch_shapes=[pltpu.VMEM((2, RW_HEADS // GROUP, GW, GW), F32)],
        compiler_params=_cparams("parallel", "arbitrary"),
        name="rw_scan",
    )(*args)


def _mem_kv_kernel(m_ref, g_ref, wk_ref, wv_ref, gk_ref, k_out, v_out):
    m = _rms(m_ref[...], g_ref[...]).astype(BF16)
    k = _dot(m, wk_ref[...])
    v_out[...] = _dot(m, wv_ref[...]).astype(BF16)
    for hd in range(X_HEADS):
        sl = slice(X_HEAD * hd, X_HEAD * (hd + 1))
        k_out[:, sl] = _rms(k[:, sl], gk_ref[...]).astype(BF16)


def _mem_kv(mem2, w):
    n = mem2.shape[0]
    row = lambda i: (i, 0)
    return pl.pallas_call(
        _mem_kv_kernel,
        grid=(n // N_MEM,),
        in_specs=[pl.BlockSpec((N_MEM, D_MODEL), row), _full((1, D_MODEL)), _full((D_MODEL, X_DIM)),
                  _full((D_MODEL, X_DIM)), _full((1, X_HEAD))],
        out_specs=[pl.BlockSpec((N_MEM, X_DIM), row), pl.BlockSpec((N_MEM, X_DIM), row)],
        out_shape=[jax.ShapeDtypeStruct((n, X_DIM), BF16), jax.ShapeDtypeStruct((n, X_DIM), BF16)],
        compiler_params=_cparams("parallel"),
        name="mem_kv",
    )(mem2, w["mem_norm_g"], w["w_mk"], w["w_mv"], w["x_kn_g"])


def _xattn_kernel(x_ref, g_ref, wq_ref, gq_ref, mk_ref, mv_ref, o_ref):
    h = _rms(x_ref[...], g_ref[...]).astype(BF16)
    q = _dot(h, wq_ref[...])
    for hd in range(X_HEADS):
        sl = slice(X_HEAD * hd, X_HEAD * (hd + 1))
        qh = _rms(q[:, sl], gq_ref[...]).astype(BF16)
        s = _dot_nt(qh, mk_ref[:, sl])
        p = jnp.exp(s - jnp.max(s, axis=-1, keepdims=True))
        o = _dot(p.astype(BF16), mv_ref[:, sl]) / jnp.sum(p, axis=-1, keepdims=True)
        o_ref[:, sl] = o.astype(BF16)


def _xattn(x2, mk, mv, B, T, w):
    tq = min(512, T)
    nq = T // tq
    return pl.pallas_call(
        _xattn_kernel,
        grid=(B, nq),
        in_specs=[pl.BlockSpec((tq, D_MODEL), lambda b, i: (b * nq + i, 0)), _full((1, D_MODEL)),
                  _full((D_MODEL, X_DIM)), _full((1, X_HEAD)),
                  pl.BlockSpec((N_MEM, X_DIM), lambda b, i: (b, 0)), pl.BlockSpec((N_MEM, X_DIM), lambda b, i: (b, 0))],
        out_specs=pl.BlockSpec((tq, X_DIM), lambda b, i: (b * nq + i, 0)),
        out_shape=jax.ShapeDtypeStruct((B * T, X_DIM), BF16),
        compiler_params=_cparams("parallel", "parallel"),
        name="xattn",
    )(x2, w["norm_mix_g"], w["w_xq"], w["x_qn_g_scaled"], mk, mv)


def _merge_kernel(x_ref, oa_ref, yf_ref, yb_ref, bonus_ref, g7_ref, oc_ref, g_ref, wg_ref, woa_ref, wob_ref,
                  woc_ref, wout_ref, lng_ref, lnb_ref, seg_ref, o_ref):
    x = x_ref[...]
    h = _rms(x, g_ref[...]).astype(BF16)
    seg = seg_ref[...]
    y7 = yf_ref[...] + yb_ref[...] + bonus_ref[...]
    mu = _dot_0_1(y7, seg) * (1.0 / RW_HEAD)
    dy = y7 - mu
    var = _dot_0_1(dy * dy, seg) * (1.0 / RW_HEAD)
    y7 = dy * lax.rsqrt(var + LNX_EPS) * lng_ref[...] + lnb_ref[...]
    yb = (y7 * g7_ref[...]).astype(BF16)
    merged = jnp.zeros(x.shape, F32)
    for bi, (br, wo) in enumerate(((oa_ref[...], woa_ref), (yb, wob_ref), (oc_ref[...], woc_ref))):
        gate = _sigmoid(_dot(h, wg_ref[:, D_MODEL * bi:D_MODEL * (bi + 1)]))
        merged = merged + gate * _dot(br, wo[...])
    o_ref[...] = x + _dot(merged.astype(BF16), wout_ref[...])


def _merge(x2, oa, yf, yb, bonus, g7, oc, T, w):
    ntok = x2.shape[0]
    tm = min(512, T)
    row = lambda i: (i, 0)
    half = lambda: pl.BlockSpec((tm, RW_DIM), row)
    return pl.pallas_call(
        _merge_kernel,
        grid=(ntok // tm,),
        in_specs=[pl.BlockSpec((tm, D_MODEL), row), half(), half(), half(), half(), half(), half(),
                  _full((1, D_MODEL)), _full((D_MODEL, 3 * D_MODEL)), _full((RW_DIM, D_MODEL)),
                  _full((RW_DIM, D_MODEL)), _full((RW_DIM, D_MODEL)), _full((D_MODEL, D_MODEL)),
                  _full((1, RW_DIM)), _full((1, RW_DIM)), _full((RW_DIM, RW_DIM))],
        out_specs=pl.BlockSpec((tm, D_MODEL), row),
        out_shape=jax.ShapeDtypeStruct((ntok, D_MODEL), F32),
        compiler_params=_cparams("parallel"),
        name="merge",
    )(x2, oa, yf, yb, bonus, g7, oc, w["norm_mix_g"], w["w_gate"], w["w_o_a"], w["w_o_b"], w["w_o_c"],
      w["w_out"], w["lnx_g"], w["lnx_b"], w["seg"])


def _ffn_kernel(x_ref, xp_ref, xn_ref, g_ref, wug_ref, wuv_ref, cw_ref, cb_ref, wd_ref, o_ref, h_scr, ug_scr,
                *, tm, npos):
    i = pl.program_id(0)
    f = pl.program_id(1)

    @pl.when(f == 0)
    def _():
        not_first = (i % npos != 0).astype(F32)
        not_last = (i % npos != npos - 1).astype(F32)
        g = g_ref[...]
        h_scr[0:SUBLANES, :] = (_rms(xp_ref[...], g) * not_first).astype(BF16)
        h_scr[SUBLANES:SUBLANES + tm, :] = _rms(x_ref[...], g).astype(BF16)
        h_scr[SUBLANES + tm:, :] = (_rms(xn_ref[...], g) * not_last).astype(BF16)

    ug_scr[...] = _dot(h_scr[...], wug_ref[...])
    uv = _dot(h_scr[SUBLANES:SUBLANES + tm, :], wuv_ref[...])
    cw = cw_ref[...]
    c = (cw[0:1] * ug_scr[SUBLANES - 1:SUBLANES - 1 + tm, :] + cw[1:2] * ug_scr[SUBLANES:SUBLANES + tm, :]
         + cw[2:3] * ug_scr[SUBLANES + 1:SUBLANES + 1 + tm, :] + cb_ref[...])
    act = 0.5 * c * (1.0 + lax.erf(c * np.float32(1.0 / np.sqrt(2.0)))) * uv
    part = _dot(act.astype(BF16), wd_ref[...])

    @pl.when(f == 0)
    def _():
        o_ref[...] = x_ref[...] + part

    @pl.when(f != 0)
    def _():
        o_ref[...] += part


def _ffn(x2, T, w):
    ntok = x2.shape[0]
    tm = min(512, T)
    fc = D_FF // 2
    npos = T // tm
    nblk8 = ntok // SUBLANES
    r8 = tm // SUBLANES
    row = lambda i, f: (i, 0)
    return pl.pallas_call(
        functools.partial(_ffn_kernel, tm=tm, npos=npos),
        grid=(ntok // tm, D_FF // fc),
        in_specs=[pl.BlockSpec((tm, D_MODEL), row),
                  pl.BlockSpec((SUBLANES, D_MODEL), lambda i, f: (jnp.maximum(i * r8 - 1, 0), 0)),
                  pl.BlockSpec((SUBLANES, D_MODEL), lambda i, f: (jnp.minimum((i + 1) * r8, nblk8 - 1), 0)),
                  pl.BlockSpec((1, D_MODEL), lambda i, f: (0, 0)),
                  pl.BlockSpec((D_MODEL, fc), lambda i, f: (0, f)),
                  pl.BlockSpec((D_MODEL, fc), lambda i, f: (0, f)),
                  pl.BlockSpec((3, fc), lambda i, f: (0, f)),
                  pl.BlockSpec((1, fc), lambda i, f: (0, f)),
                  pl.BlockSpec((fc, D_MODEL), lambda i, f: (f, 0))],
        out_specs=pl.BlockSpec((tm, D_MODEL), row),
        out_shape=jax.ShapeDtypeStruct((ntok, D_MODEL), F32),
        scratch_shapes=[pltpu.VMEM((tm + 2 * SUBLANES, D_MODEL), BF16), pltpu.VMEM((tm + 2 * SUBLANES, fc), F32)],
        compiler_params=_cparams("parallel", "arbitrary"),
        name="conv_ffn",
    )(x2, x2, x2, w["norm_ffn_g"], w["w_up_gate"], w["w_up_val"], w["conv_w"], w["conv_b"], w["w_down"])


def _prep_weights(p):
    w = {}
    row = lambda a: a.reshape(1, -1).astype(F32)
    w_in = p["w_in"]
    o = np.cumsum([0, Q_LORA, KV_LORA, MLA_ROPE, RW_COLS, X_DIM, 3 * D_MODEL])
    seg = lambda i: w_in[:, o[i]:o[i + 1]]
    zc = lambda n: jnp.zeros((D_MODEL, n), F32)
    w["w_c"] = jnp.concatenate([seg(0), seg(1), zc(MLA_NOPE), seg(2), zc(LANES - MLA_QK)], axis=1).astype(BF16)
    w["w_rw"] = seg(3).astype(BF16)
    w["w_xq"] = seg(4).astype(BF16)
    w["w_gate"] = seg(5).astype(BF16)
    w["norm_mix_g"] = row(p["norm_mix_g"])
    w["q_norm_g"] = row(p["q_norm_g"])
    w["kv_norm_g"] = row(p["kv_norm_g"])
    pad_slot = lambda a: jnp.pad(a, ((0, 0), (0, 0), (0, LANES - a.shape[-1]))).reshape(a.shape[0], -1)
    uq = p["w_uq"].reshape(Q_LORA, MLA_HEADS, MLA_QK)
    w["w_uq"] = pad_slot(uq).astype(BF16)
    half = MLA_ROPE // 2
    uq_rot = jnp.concatenate([jnp.zeros_like(uq[:, :, :MLA_NOPE]), -uq[:, :, MLA_NOPE + half:],
                              uq[:, :, MLA_NOPE:MLA_NOPE + half]], axis=-1)
    w["w_uq_rot"] = pad_slot(uq_rot).astype(BF16)
    ukv =p["w_ukv"].reshape(KV_LORA, MLA_HEADS, MLA_NOPE + MLA_V)
    w["w_uk"] = pad_slot(ukv[:, :, :MLA_NOPE]).astype(BF16)
    w["w_uv"] = pad_slot(ukv[:, :, MLA_NOPE:]).T.astype(BF16)
    w["mla_qn_g_scaled"] = p["mla_qn_g"].reshape(-1) * np.float32(MLA_QK ** -0.5)
    w["mla_kn_g"] = p["mla_kn_g"].reshape(-1)
    for name in ("mu_prev", "mu_next", "k_k", "k_a", "r_k", "lnx_g", "lnx_b", "mem_norm_g", "x_kn_g", "norm_ffn_g",
                 "conv_b"):
        w[name] = row(p[name])
    w["w0"] = jnp.stack([p["w0_f"], p["w0_b"]]).reshape(2, 1, RW_DIM)
    w["a0"] = jnp.stack([p["a0_f"], p["a0_b"]]).reshape(2, 1, RW_DIM)
    zl = jnp.zeros((LORA, RW_DIM), F32)
    w["w2"] = jnp.stack([jnp.concatenate([p["w2_f"], zl]), jnp.concatenate([zl, p["w2_b"]])]).astype(BF16)
    w["a2"] = jnp.stack([jnp.concatenate([p["a2_f"], zl]), jnp.concatenate([zl, p["a2_b"]])]).astype(BF16)
    w["g2"] = p["g2"].astype(BF16)
    hid = np.arange(RW_DIM) // RW_HEAD
    w["seg"] = jnp.asarray((hid[:, None] == hid[None, :]).astype(np.float32), BF16)
    mkv = p["w_mkv"].reshape(D_MODEL, X_HEADS, 2 * X_HEAD)
    w["w_mk"] = mkv[:, :, :X_HEAD].reshape(D_MODEL, X_DIM).astype(BF16)
    w["w_mv"] = mkv[:, :, X_HEAD:].reshape(D_MODEL, X_DIM).astype(BF16)
    w["x_qn_g_scaled"] = row(p["x_qn_g"]) * np.float32(X_HEAD ** -0.5)
    for name in ("w_o_a", "w_o_b", "w_o_c", "w_out", "w_down"):
        w[name] = p[name].astype(BF16)
    w["w_up_gate"] = p["w_up"][:, :D_FF].astype(BF16)
    w["w_up_val"] = p["w_up"][:, D_FF:].astype(BF16)
    w["conv_w"] = p["conv_w"].astype(F32)
    return w


def _rope_tables(T, gq, gk):
    half = MLA_ROPE // 2
    inv = jnp.power(ROPE_THETA, -jnp.arange(half, dtype=F32) / half)
    ang = jnp.arange(T, dtype=F32)[:, None] * inv[None, :]
    cos, sin = jnp.cos(ang), jnp.sin(ang)
    z = lambda n: jnp.zeros((T, n), F32)
    pad = z(LANES - MLA_QK)

    def own(g):
        g1, g2 = g[MLA_NOPE:MLA_NOPE + half], g[MLA_NOPE + half:]
        return jnp.concatenate([jnp.broadcast_to(g[:MLA_NOPE], (T, MLA_NOPE)), cos * g1, cos * g2, pad], axis=1)

    g1, g2 = gq[MLA_NOPE:MLA_NOPE + half], gq[MLA_NOPE + half:]
    qb = jnp.concatenate([z(MLA_NOPE), sin * g2, sin * g1, pad], axis=1)
    g1, g2 = gk[MLA_NOPE:MLA_NOPE + half], gk[MLA_NOPE + half:]
    kba = jnp.concatenate([z(MLA_NOPE), -sin * g2, z(half), pad], axis=1)
    kbb = jnp.concatenate([z(MLA_NOPE), z(half), sin * g1, pad], axis=1)
    return own(gq), qb, own(gk), kba, kbb


def _layer(x, mem, w):
    B, T, _ = x.shape
    x2 = x.reshape(B * T, D_MODEL)
    q, k, v = _mla_prep(x2, T, w, _rope_tables(T, w["mla_qn_g_scaled"], w["mla_kn_g"]))
    o_a = _flash(q, k, v, B, T)
    v7, kkt, rt, kh, bh, kw, bw, wt, bonus, g7 = _rw_prep(x2, T, w)
    y_f, y_b = _rw_scan(v7, kkt, rt, kh, bh, kw, bw, wt, B, T)
    mk, mv = _mem_kv(mem.reshape(B * N_MEM, D_MODEL), w)
    o_c = _xattn(x2, mk, mv, B, T, w)
    x1 = _merge(x2, o_a, y_f, y_b, bonus, g7, o_c, T, w)
    return _ffn(x1, T, w).reshape(B, T, D_MODEL)


def kernel(x_prompt, x_sample, mem_prompt, mem_sample, norm_mix_g, w_in, q_norm_g, w_uq, kv_norm_g, w_ukv, mla_qn_g, mla_kn_g, w_o_a, mu_prev, mu_next, w0_f, w2_f, a0_f, a2_f, w0_b, w2_b, a0_b, a2_b, g2, k_k, k_a, r_k, lnx_g, lnx_b, w_o_b, mem_norm_g, w_mkv, x_qn_g, x_kn_g, w_o_c, w_out, norm_ffn_g, w_up, conv_w, conv_b, w_down):
    p = dict(norm_mix_g=norm_mix_g, w_in=w_in, q_norm_g=q_norm_g, w_uq=w_uq, kv_norm_g=kv_norm_g, w_ukv=w_ukv,
             mla_qn_g=mla_qn_g, mla_kn_g=mla_kn_g, w_o_a=w_o_a, mu_prev=mu_prev, mu_next=mu_next,
             w0_f=w0_f, w2_f=w2_f, a0_f=a0_f, a2_f=a2_f, w0_b=w0_b, w2_b=w2_b, a0_b=a0_b, a2_b=a2_b,
             g2=g2, k_k=k_k, k_a=k_a, r_k=r_k, lnx_g=lnx_g, lnx_b=lnx_b, w_o_b=w_o_b,
             mem_norm_g=mem_norm_g, w_mkv=w_mkv, x_qn_g=x_qn_g, x_kn_g=x_kn_g, w_o_c=w_o_c, w_out=w_out,
             norm_ffn_g=norm_ffn_g, w_up=w_up, conv_w=conv_w, conv_b=conv_b, w_down=w_down)
    w = _prep_weights({name: a[0] for name, a in p.items()})
    return (_layer(x_prompt, mem_prompt, w), _layer(x_sample, mem_sample, w))
```

```python
import functools

import numpy as np
import jax
import jax.numpy as jnp
from jax import lax
from jax.experimental import pallas as pl
from jax.experimental.pallas import tpu as pltpu

F32 = jnp.float32
BF16 = jnp.bfloat16

D_MODEL = 1024
RMS_EPS = 1e-6
N_MEM = 256
MLA_HEADS = 8
MLA_NOPE = 64
MLA_ROPE = 32
MLA_QK = MLA_NOPE + MLA_ROPE
MLA_V = 64
Q_LORA = 384
KV_LORA = 256
ROPE_THETA = 10000.0
RW_HEADS = 8
RW_HEAD = 64
RW_DIM = RW_HEADS * RW_HEAD
LORA = 64
GATE_LORA = 128
RW_COLS = 3 * RW_DIM + 4 * LORA + GATE_LORA
LNX_EPS = 64e-5
X_HEADS = 4
X_HEAD = 128
X_DIM = X_HEADS * X_HEAD
D_FF = 2816

LANES = 128
SUBLANES = 8
CHUNK = 64
GROUP = 2
GW = GROUP * RW_HEAD
SCAN_CHUNKS = 2
VMEM_LIMIT = 56 * 1024 * 1024


def _cparams(*sem):
    return pltpu.CompilerParams(dimension_semantics=sem, vmem_limit_bytes=VMEM_LIMIT)


def _rms(x, g, eps=RMS_EPS):
    return x * lax.rsqrt(jnp.mean(x * x, axis=-1, keepdims=True) + eps) * g


def _dot(a, b):
    return jnp.dot(a, b, preferred_element_type=F32)


def _dot_nt(a, b):
    return lax.dot_general(a, b, (((1,), (1,)), ((), ())), preferred_element_type=F32)


def _dot_tn(a, b):
    return lax.dot_general(a, b, (((0,), (0,)), ((), ())), preferred_element_type=F32)


def _dot_0_1(x, e):
    return _dot(x.astype(BF16), e)


def _dot_0_1_l(e, x):
    hi = x.astype(BF16)
    lo = (x - hi.astype(F32)).astype(BF16)
    return _dot(e, hi) + _dot(e, lo)


def _sigmoid(z):
    return 1.0 / (1.0 + jnp.exp(-z))


def _full(shape):
    nd = len(shape)
    return pl.BlockSpec(shape, lambda *_: (0,) * nd)


def _mla_prep_kernel(x_ref, g_ref, wc_ref, gq_ref, gkv_ref, wuq_ref, wuqr_ref, wuk_ref, wuv_ref,
                     qa_ref, qb_ref, ka_ref, kba_ref, kbb_ref, q_out, k_out, v_out):
    h = _rms(x_ref[...], g_ref[...]).astype(BF16)
    c = _dot(h, wc_ref[...])
    cq = _rms(c[:, :Q_LORA], gq_ref[...]).astype(BF16)
    ckv = _rms(c[:, Q_LORA:Q_LORA + KV_LORA], gkv_ref[...]).astype(BF16)
    kr = c[:, Q_LORA + KV_LORA:]
    q = _dot(cq, wuq_ref[...])
    qp = _dot(cq, wuqr_ref[...])
    k = _dot(ckv, wuk_ref[...])
    vt = _dot_nt(wuv_ref[...], ckv)
    slot_row = lax.broadcasted_iota(jnp.int32, vt.shape, 0) % LANES
    v_out[0] = jnp.where(slot_row == MLA_V, 1.0, vt).astype(BF16)
    qa = qa_ref[...]
    qb = qb_ref[...]
    ka = ka_ref[...]
    half = MLA_ROPE // 2
    kr_roped = kr * ka + pltpu.roll(kr, LANES - half, 1) * kba_ref[...] + pltpu.roll(kr, half, 1) * kbb_ref[...]
    kr_ss = jnp.sum(kr * kr, axis=-1, keepdims=True)
    for hd in range(MLA_HEADS):
        sl = slice(LANES * hd, LANES * (hd + 1))
        t = q[:, sl]
        ss = jnp.sum(t * t, axis=-1, keepdims=True) * (1.0 / MLA_QK)
        q_out[:, sl] = ((t * qa + qp[:, sl] * qb) * lax.rsqrt(ss + RMS_EPS)).astype(BF16)
        t = k[:, sl]
        ss = (jnp.sum(t * t, axis=-1, keepdims=True) + kr_ss) * (1.0 / MLA_QK)
        k_out[:, sl] = ((t * ka + kr_roped) * lax.rsqrt(ss + RMS_EPS)).astype(BF16)


def _mla_prep(x2, T, w, tabs):
    ntok = x2.shape[0]
    tm = _attn_tile(T)
    npos = T // tm
    row = lambda i: (i, 0)
    pos = lambda i: (i % npos, 0)
    return pl.pallas_call(
        _mla_prep_kernel,
        grid=(ntok // tm,),
        in_specs=[pl.BlockSpec((tm, D_MODEL), row), _full((1, D_MODEL)), _full((D_MODEL, 768)),
                  _full((1, Q_LORA)), _full((1, KV_LORA)), _full((Q_LORA, 1024)), _full((Q_LORA, 1024)),
                  _full((KV_LORA, 1024)), _full((1024, KV_LORA))] + [pl.BlockSpec((tm, LANES), pos)] * 5,
        out_specs=[pl.BlockSpec((tm, 1024), row), pl.BlockSpec((tm, 1024), row),
                   pl.BlockSpec((1, 1024, tm), lambda i: (i, 0, 0))],
        out_shape=[jax.ShapeDtypeStruct((ntok, 1024), BF16), jax.ShapeDtypeStruct((ntok, 1024), BF16),
                   jax.ShapeDtypeStruct((ntok // tm, 1024, tm), BF16)],
        compiler_params=_cparams("parallel"),
        name="mla_prep",
    )(x2, w["norm_mix_g"], w["w_c"], w["q_norm_g"], w["kv_norm_g"], w["w_uq"], w["w_uq_rot"], w["w_uk"],
      w["w_uv"], *tabs)


def _flash_kernel(q_ref, k_ref, vt_ref, o_ref, s0_ref, s1_ref, *, tk, nk):
    tq = q_ref.shape[0]
    slots = [slice(LANES * h, LANES * (h + 1)) for h in range(2)]
    per = tk // vt_ref.shape[2]

    def scores(j, s_ref):
        kblk = k_ref[pl.ds(pl.multiple_of(j * tk, tk), tk), :]
        for h in range(2):
            s_ref[h] = _dot_nt(kblk[:, slots[h]], q_ref[:, slots[h]])

    def consume(j, s_ref, state):
        vt = jnp.concatenate([vt_ref[per * j + r] for r in range(per)], axis=1)
        m_news = [jnp.maximum(state[h][0], jnp.max(s_ref[h], axis=0, keepdims=True)) for h in range(2)]
        pts = [jnp.exp2((s_ref[h] - m_news[h]).astype(BF16)) for h in range(2)]
        return tuple((m_news[h], jnp.exp2(state[h][0] - m_news[h]) * state[h][1] + _dot(vt[slots[h], :], pts[h]))
                     for h in range(2))

    def body(t, state):
        j = 2 * t
        scores(j + 1, s1_ref)
        state = consume(j, s0_ref, state)
        scores(j + 2, s0_ref)
        return consume(j + 1, s1_ref, state)

    state = tuple((jnp.full((1, tq), -jnp.inf, F32), jnp.zeros((LANES, tq), F32)) for _ in range(2))
    scores(0, s0_ref)
    state = lax.fori_loop(0, nk // 2 - 1, body, state)
    scores(nk - 1, s1_ref)
    state = consume(nk - 2, s0_ref, state)
    state = consume(nk - 1, s1_ref, state)
    outs = [acc[0:MLA_V, :] / acc[MLA_V:MLA_V + 1, :] for _, acc in state]
    o_ref[...] = jnp.concatenate(outs, axis=0).T.astype(BF16)


def _attn_tile(T):
    return min(512, T // 2)


def _flash(q, k, vt, B, T):
    tq = _attn_tile(T)
    tk = min(1024, T // 2)
    tv = _attn_tile(T)
    nq = T // tq
    nk = T // tk
    assert nk % 2 == 0 and tk % tv == 0, (T, tk, tv)
    return pl.pallas_call(
        functools.partial(_flash_kernel, tk=tk, nk=nk),
        grid=(B, MLA_HEADS // 2, nq),
        in_specs=[pl.BlockSpec((tq, 2 * LANES), lambda b, hp, i: (b * nq + i, hp)),
                  pl.BlockSpec((T, 2 * LANES), lambda b, hp, i: (b, hp)),
                  pl.BlockSpec((T // tv, 2 * LANES, tv), lambda b, hp, i: (b, hp, 0))],
        out_specs=pl.BlockSpec((tq, LANES), lambda b, hp, i: (b * nq + i, hp)),
        out_shape=jax.ShapeDtypeStruct((B * T, MLA_HEADS * MLA_V), BF16),
        scratch_shapes=[pltpu.VMEM((2, tk, tq), F32), pltpu.VMEM((2, tk, tq), F32)],
        compiler_params=_cparams("parallel", "parallel", "arbitrary"),
        name="mla_flash",
    )(q, k, vt)


def _rw_prep_kernel(x_ref, xp_ref, xn_ref, g_ref, wrw_ref, mup_ref, mun_ref, kk_ref, ka_ref, rk_ref,
                    w0_ref, w2_ref, a0_ref, a2_ref, g2_ref, seg_ref, tri_ref,
                    v_out, kkt_out, rt_out, kh_out, bh_out, kw_out, bw_out, wt_out, bonus_out, g7_out,
                    h_scr, rw_scr, *, tm, npos):
    i = pl.program_id(0)
    not_first = (i % npos != 0).astype(F32)
    not_last = (i % npos != npos - 1).astype(F32)
    g = g_ref[...]
    h_scr[0:SUBLANES, :] = (_rms(xp_ref[...], g) * not_first).astype(BF16)
    h_scr[SUBLANES:SUBLANES + tm, :] = _rms(x_ref[...], g).astype(BF16)
    h_scr[SUBLANES + tm:, :] = (_rms(xn_ref[...], g) * not_last).astype(BF16)
    rw_scr[...] = _dot(h_scr[...], wrw_ref[...])
    cur = rw_scr[SUBLANES:SUBLANES + tm, :]
    prev = rw_scr[SUBLANES - 1:SUBLANES - 1 + tm, :]
    nxt = rw_scr[SUBLANES + 1:SUBLANES + 1 + tm, :]
    rwf = cur + mup_ref[...] * (prev - cur) + mun_ref[...] * (nxt - cur)
    r7 = rwf[:, 0:RW_DIM]
    k7 = rwf[:, RW_DIM:2 * RW_DIM]
    v7 = rwf[:, 2 * RW_DIM:3 * RW_DIM]
    wl = rwf[:, 3 * RW_DIM:3 * RW_DIM + 2 * LORA]
    al = rwf[:, 3 * RW_DIM + 2 * LORA:3 * RW_DIM + 4 * LORA]
    gl = rwf[:, 3 * RW_DIM + 4 * LORA:]
    seg = seg_ref[...]
    kx = k7 * kk_ref[...]
    kkn = kx / jnp.maximum(jnp.sqrt(_dot_0_1(kx * kx, seg)), 1e-12)
    v_out[...] = v7.astype(BF16)
    g7_out[...] = _dot(_sigmoid(gl).astype(BF16), g2_ref[...])
    tw = jnp.tanh(wl).astype(BF16)
    alb = al.astype(BF16)
    bonus = jnp.zeros((tm, RW_DIM), F32)
    for d in range(2):
        w = -jax.nn.softplus(-(w0_ref[d] + _dot(tw, w2_ref[d]))) - 0.5
        lw = -jnp.exp(w)
        a = _sigmoid(a0_ref[d] + _dot(alb, a2_ref[d]))
        kd = k7 * (1.0 + (a - 1.0) * ka_ref[...])
        b = kkn * a
        bonus = bonus + _dot_0_1(r7 * kd * rk_ref[...], seg) * v7
        cum = _dot_0_1_l(tri_ref[d], lw)
        last = CHUNK - 1 if d == 0 else 0
        tot_rows = [cum[CHUNK * ci + last:CHUNK * ci + last + 1, :] for ci in range(tm // CHUNK)]
        tot = jnp.concatenate([jnp.broadcast_to(r, (CHUNK, RW_DIM)) for r in tot_rows], axis=0)
        w_incl = jnp.exp(cum)
        w_excl = jnp.exp(cum - lw)
        w_inv = jnp.exp(-cum)
        w_rest = jnp.exp(tot - cum)
        kkt_out[d] = (kkn * w_excl).astype(BF16)
        rt_out[d] = (r7 * w_incl).astype(BF16)
        kh_out[d] = (kd * w_inv).astype(BF16)
        bh_out[d] = (b * w_inv).astype(BF16)
        kw_out[d] = (kd * w_rest).astype(BF16)
        bw_out[d] = (b * w_rest).astype(BF16)
        for ci in range(tm // CHUNK):
            wt_out[d, ci] = jnp.exp(tot_rows[ci])
    bonus_out[...] = bonus


def _rw_prep(x2, T, w):
    ntok = x2.shape[0]
    tm = min(256, T)
    npos = T // tm
    nblk8 = ntok // SUBLANES
    r8 = tm // SUBLANES
    row = lambda i: (i, 0)
    drow = lambda i: (0, i, 0)
    dspec = pl.BlockSpec((2, tm, RW_DIM), drow)
    dshape = jax.ShapeDtypeStruct((2, ntok, RW_DIM), BF16)
    tri = _chunk_masks(tm)
    outs = pl.pallas_call(
        functools.partial(_rw_prep_kernel, tm=tm, npos=npos),
        grid=(ntok // tm,),
        in_specs=[pl.BlockSpec((tm, D_MODEL), row),
                  pl.BlockSpec((SUBLANES, D_MODEL), lambda i: (jnp.maximum(i * r8 - 1, 0), 0)),
                  pl.BlockSpec((SUBLANES, D_MODEL), lambda i: (jnp.minimum((i + 1) * r8, nblk8 - 1), 0)),
                  _full((1, D_MODEL)), _full((D_MODEL, RW_COLS)), _full((1, RW_COLS)), _full((1, RW_COLS)),
                  _full((1, RW_DIM)), _full((1, RW_DIM)), _full((1, RW_DIM)),
                  _full((2, 1, RW_DIM)), _full((2, 2 * LORA, RW_DIM)), _full((2, 1, RW_DIM)),
                  _full((2, 2 * LORA, RW_DIM)), _full((GATE_LORA, RW_DIM)), _full((RW_DIM, RW_DIM)),
                  _full((2, tm, tm))],
        out_specs=[pl.BlockSpec((tm, RW_DIM), row), dspec, dspec, dspec, dspec, dspec, dspec,
                   pl.BlockSpec((2, tm // CHUNK, 1, RW_DIM), lambda i: (0, i, 0, 0)),
                   pl.BlockSpec((tm, RW_DIM), row), pl.BlockSpec((tm, RW_DIM), row)],
        out_shape=[jax.ShapeDtypeStruct((ntok, RW_DIM), BF16), dshape, dshape, dshape, dshape, dshape, dshape,
                   jax.ShapeDtypeStruct((2, ntok // CHUNK, 1, RW_DIM), F32),
                   jax.ShapeDtypeStruct((ntok, RW_DIM), F32), jax.ShapeDtypeStruct((ntok, RW_DIM), F32)],
        scratch_shapes=[pltpu.VMEM((tm + 2 * SUBLANES, D_MODEL), BF16), pltpu.VMEM((tm + 2 * SUBLANES, RW_COLS), F32)],
        compiler_params=_cparams("parallel"),
        name="rw_prep",
    )(x2, x2, x2, w["norm_mix_g"], w["w_rw"], w["mu_prev"], w["mu_next"], w["k_k"], w["k_a"], w["r_k"],
      w["w0"], w["w2"], w["a0"], w["a2"], w["g2"], w["seg"], tri)
    return outs


def _chunk_masks(tm):
    t = np.arange(tm)
    same = (t[:, None] // CHUNK) == (t[None, :] // CHUNK)
    fwd = same & (t[None, :] <= t[:, None])
    bwd = same & (t[None, :] >= t[:, None])
    return jnp.asarray(np.stack([fwd, bwd]).astype(np.float32), BF16)


def _rw_scan_kernel(*refs):
    ins = refs[:16]
    yf_ref, yb_ref, s_ref = refs[16:]
    c = pl.program_id(1)

    @pl.when(c == 0)
    def _():
        s_ref[...] = jnp.zeros(s_ref.shape, F32)

    ri = lax.broadcasted_iota(jnp.int32, (GW, GW), 0)
    ci = lax.broadcasted_iota(jnp.int32, (GW, GW), 1)
    head_mask = (ri // CHUNK) == (ci // RW_HEAD)
    rt_ = ri % CHUNK
    cs_ = ci % CHUNK
    eye = (ri == ci).astype(F32)
    stack = lambda a: jnp.where(head_mask, jnp.concatenate([a] * GROUP, axis=0), jnp.zeros((), BF16))

    items = []
    for d, y_ref in ((0, yf_ref), (1, yb_ref)):
        kkt, rt, kh, bh, kw, bw, v, wt = ins[8 * d:8 * d + 8]
        strict = (cs_ < rt_) if d == 0 else (cs_ > rt_)
        incl = (cs_ <= rt_) if d == 0 else (cs_ >= rt_)
        for gi in range(RW_HEADS // GROUP):
            sl = slice(GW * gi, GW * (gi + 1))
            for sub in range(SCAN_CHUNKS):
                rows = slice(CHUNK * sub, CHUNK * (sub + 1))
                it = dict(y_ref=y_ref, sl=sl, rows=rows, s_ref=s_ref.at[d, gi], wtot=wt[sub][:, sl],
                          order=sub if d == 0 else SCAN_CHUNKS - 1 - sub)
                xkk, xr, yk, yb, vs, kws, bws = (stack(r[rows, sl]) for r in (kkt, rt, kh, bh, v, kw, bw))
                a_all = _dot_nt(jnp.concatenate([xkk, xr], axis=0), jnp.concatenate([yk, yb], axis=0))
                it.update(xkk=xkk, xr=xr, vs=vs, kbw=jnp.concatenate([kws, bws], axis=0),
                          ak=jnp.where(strict, a_all[:GW, :GW], 0.0).astype(BF16),
                          bk=jnp.where(incl, a_all[GW:, :GW], 0.0).astype(BF16),
                          bb=jnp.where(incl, a_all[GW:, GW:], 0.0).astype(BF16))
                ab = jnp.where(strict, a_all[:GW, GW:], 0.0)
                it.update(tinv=eye - ab, pw=ab.astype(BF16))
                items.append(it)
    for _ in range(5):
        for it in items:
            it["pw"] = _dot(it["pw"], it["pw"]).astype(BF16)
        for it in items:
            it["tinv"] = it["tinv"] + _dot(it["tinv"].astype(BF16), it["pw"])
    for it in items:
        it["tinv"] = it["tinv"].astype(BF16)
    for step in range(SCAN_CHUNKS):
        chains = [it for it in items if it["order"] == step]
        for ch in chains:
            ch["sb"] = ch["s_ref"][...].astype(BF16)
            ch["rhs"] = (_dot_nt(ch["xkk"], ch["sb"]) + _dot(ch["ak"], ch["vs"])).astype(BF16)
        for ch in chains:
            ch["u"] = (-_dot(ch["tinv"], ch["rhs"])).astype(BF16)
        for ch in chains:
            y = _dot_nt(ch["xr"], ch["sb"]) + _dot(ch["bk"], ch["vs"]) + _dot(ch["bb"], ch["u"])
            ch["y_ref"][ch["rows"], ch["sl"]] = sum(y[CHUNK * hd:CHUNK * (hd + 1)] for hd in range(GROUP))
        for ch in chains:
            ch["s_ref"][...] = (ch["s_ref"][...] * ch["wtot"]
                                + _dot_tn(jnp.concatenate([ch["vs"], ch["u"]], axis=0), ch["kbw"]))


def _rw_scan(v, kkt, rt, kh, bh, kw, bw, wt, B, T):
    ntok = B * T
    rows = SCAN_CHUNKS * CHUNK
    nb = T // rows
    in_specs = []
    args = []
    for d in range(2):
        if d == 0:
            cm = lambda b, c: b * nb + c
        else:
            cm = lambda b, c: b * nb + (nb - 1 - c)
        for arr in (kkt, rt, kh, bh, kw, bw):
            in_specs.append(pl.BlockSpec((None, rows, RW_DIM), lambda b, c, cm=cm, d=d: (d, cm(b, c), 0)))
            args.append(arr)
        in_specs.append(pl.BlockSpec((rows, RW_DIM), lambda b, c, cm=cm: (cm(b, c), 0)))
        args.append(v)
        in_specs.append(pl.BlockSpec((None, SCAN_CHUNKS, 1, RW_DIM), lambda b, c, cm=cm, d=d: (d, cm(b, c), 0, 0)))
        args.append(wt)
    return pl.pallas_call(
        _rw_scan_kernel,
        grid=(B, nb),
        in_specs=in_specs,
        out_specs=[pl.BlockSpec((rows, RW_DIM), lambda b, c: (b * nb + c, 0)),
                   pl.BlockSpec((rows, RW_DIM), lambda b, c: (b * nb + (nb - 1 - c), 0))],
        out_shape=[jax.ShapeDtypeStruct((ntok, RW_DIM), F32), jax.ShapeDtypeStruct((ntok, RW_DIM), F32)],
        scratch_shapes=[pltpu.VMEM((2, RW_HEADS // GROUP, GW, GW), F32)],
        compiler_params=_cparams("parallel", "arbitrary"),
        name="rw_scan",
    )(*args)


def _mem_kv_kernel(m_ref, g_ref, wk_ref, wv_ref, gk_ref, k_out, v_out):
    m = _rms(m_ref[...], g_ref[...]).astype(BF16)
    k = _dot(m, wk_ref[...])
    v_out[...] = _dot(m, wv_ref[...]).astype(BF16)
    for hd in range(X_HEADS):
        sl = slice(X_HEAD * hd, X_HEAD * (hd + 1))
        k_out[:, sl] = _rms(k[:, sl], gk_ref[...]).astype(BF16)


def _mem_kv(mem2, w):
    n = mem2.shape[0]
    row = lambda i: (i, 0)
    return pl.pallas_call(
        _mem_kv_kernel,
        grid=(n // N_MEM,),
        in_specs=[pl.BlockSpec((N_MEM, D_MODEL), row), _full((1, D_MODEL)), _full((D_MODEL, X_DIM)),
                  _full((D_MODEL, X_DIM)), _full((1, X_HEAD))],
        out_specs=[pl.BlockSpec((N_MEM, X_DIM), row), pl.BlockSpec((N_MEM, X_DIM), row)],
        out_shape=[jax.ShapeDtypeStruct((n, X_DIM), BF16), jax.ShapeDtypeStruct((n, X_DIM), BF16)],
        compiler_params=_cparams("parallel"),
        name="mem_kv",
    )(mem2, w["mem_norm_g"], w["w_mk"], w["w_mv"], w["x_kn_g"])


def _xattn_kernel(x_ref, g_ref, wq_ref, gq_ref, mk_ref, mv_ref, o_ref):
    h = _rms(x_ref[...], g_ref[...]).astype(BF16)
    q = _dot(h, wq_ref[...])
    for hd in range(X_HEADS):
        sl = slice(X_HEAD * hd, X_HEAD * (hd + 1))
        qh = _rms(q[:, sl], gq_ref[...]).astype(BF16)
        s = _dot_nt(qh, mk_ref[:, sl])
        p = jnp.exp(s - jnp.max(s, axis=-1, keepdims=True))
        o = _dot(p.astype(BF16), mv_ref[:, sl]) / jnp.sum(p, axis=-1, keepdims=True)
        o_ref[:, sl] = o.astype(BF16)


def _xattn(x2, mk, mv, B, T, w):
    tq = min(512, T)
    nq = T // tq
    return pl.pallas_call(
        _xattn_kernel,
        grid=(B, nq),
        in_specs=[pl.BlockSpec((tq, D_MODEL), lambda b, i: (b * nq + i, 0)), _full((1, D_MODEL)),
                  _full((D_MODEL, X_DIM)), _full((1, X_HEAD)),
                  pl.BlockSpec((N_MEM, X_DIM), lambda b, i: (b, 0)), pl.BlockSpec((N_MEM, X_DIM), lambda b, i: (b, 0))],
        out_specs=pl.BlockSpec((tq, X_DIM), lambda b, i: (b * nq + i, 0)),
        out_shape=jax.ShapeDtypeStruct((B * T, X_DIM), BF16),
        compiler_params=_cparams("parallel", "parallel"),
        name="xattn",
    )(x2, w["norm_mix_g"], w["w_xq"], w["x_qn_g_scaled"], mk, mv)


def _merge_kernel(x_ref, oa_ref, yf_ref, yb_ref, bonus_ref, g7_ref, oc_ref, g_ref, wg_ref, woa_ref, wob_ref,
                  woc_ref, wout_ref, lng_ref, lnb_ref, seg_ref, o_ref):
    x = x_ref[...]
    h = _rms(x, g_ref[...]).astype(BF16)
    seg = seg_ref[...]
    y7 = yf_ref[...] + yb_ref[...] + bonus_ref[...]
    mu = _dot_0_1(y7, seg) * (1.0 / RW_HEAD)
    dy = y7 - mu
    var = _dot_0_1(dy * dy, seg) * (1.0 / RW_HEAD)
    y7 = dy * lax.rsqrt(var + LNX_EPS) * lng_ref[...] + lnb_ref[...]
    yb = (y7 * g7_ref[...]).astype(BF16)
    merged = jnp.zeros(x.shape, F32)
    for bi, (br, wo) in enumerate(((oa_ref[...], woa_ref), (yb, wob_ref), (oc_ref[...], woc_ref))):
        gate = _sigmoid(_dot(h, wg_ref[:, D_MODEL * bi:D_MODEL * (bi + 1)]))
        merged = merged + gate * _dot(br, wo[...])
    o_ref[...] = x + _dot(merged.astype(BF16), wout_ref[...])


def _merge(x2, oa, yf, yb, bonus, g7, oc, T, w):
    ntok = x2.shape[0]
    tm = min(512, T)
    row = lambda i: (i, 0)
    half = lambda: pl.BlockSpec((tm, RW_DIM), row)
    return pl.pallas_call(
        _merge_kernel,
        grid=(ntok // tm,),
        in_specs=[pl.BlockSpec((tm, D_MODEL), row), half(), half(), half(), half(), half(), half(),
                  _full((1, D_MODEL)), _full((D_MODEL, 3 * D_MODEL)), _full((RW_DIM, D_MODEL)),
                  _full((RW_DIM, D_MODEL)), _full((RW_DIM, D_MODEL)), _full((D_MODEL, D_MODEL)),
                  _full((1, RW_DIM)), _full((1, RW_DIM)), _full((RW_DIM, RW_DIM))],
        out_specs=pl.BlockSpec((tm, D_MODEL), row),
        out_shape=jax.ShapeDtypeStruct((ntok, D_MODEL), F32),
        compiler_params=_cparams("parallel"),
        name="merge",
    )(x2, oa, yf, yb, bonus, g7, oc, w["norm_mix_g"], w["w_gate"], w["w_o_a"], w["w_o_b"], w["w_o_c"],
      w["w_out"], w["lnx_g"], w["lnx_b"], w["seg"])


def _ffn_kernel(x_ref, xp_ref, xn_ref, g_ref, wug_ref, wuv_ref, cw_ref, cb_ref, wd_ref, o_ref, h_scr, ug_scr,
                *, tm, npos):
    i = pl.program_id(0)
    f = pl.program_id(1)

    @pl.when(f == 0)
    def _():
        not_first = (i % npos != 0).astype(F32)
        not_last = (i % npos != npos - 1).astype(F32)
        g = g_ref[...]
        h_scr[0:SUBLANES, :] = (_rms(xp_ref[...], g) * not_first).astype(BF16)
        h_scr[SUBLANES:SUBLANES + tm, :] = _rms(x_ref[...], g).astype(BF16)
        h_scr[SUBLANES + tm:, :] = (_rms(xn_ref[...], g) * not_last).astype(BF16)

    ug_scr[...] = _dot(h_scr[...], wug_ref[...])
    uv = _dot(h_scr[SUBLANES:SUBLANES + tm, :], wuv_ref[...])
    cw = cw_ref[...]
    c = (cw[0:1] * ug_scr[SUBLANES - 1:SUBLANES - 1 + tm, :] + cw[1:2] * ug_scr[SUBLANES:SUBLANES + tm, :]
         + cw[2:3] * ug_scr[SUBLANES + 1:SUBLANES + 1 + tm, :] + cb_ref[...])
    act = 0.5 * c * (1.0 + lax.erf(c * np.float32(1.0 / np.sqrt(2.0)))) * uv
    part = _dot(act.astype(BF16), wd_ref[...])

    @pl.when(f == 0)
    def _():
        o_ref[...] = x_ref[...] + part

    @pl.when(f != 0)
    def _():
        o_ref[...] += part


def _ffn(x2, T, w):
    ntok = x2.shape[0]
    tm = min(512, T)
    fc = D_FF // 2
    npos = T // tm
    nblk8 = ntok // SUBLANES
    r8 = tm // SUBLANES
    row = lambda i, f: (i, 0)
    return pl.pallas_call(
        functools.partial(_ffn_kernel, tm=tm, npos=npos),
        grid=(ntok // tm, D_FF // fc),
        in_specs=[pl.BlockSpec((tm, D_MODEL), row),
                  pl.BlockSpec((SUBLANES, D_MODEL), lambda i, f: (jnp.maximum(i * r8 - 1, 0), 0)),
                  pl.BlockSpec((SUBLANES, D_MODEL), lambda i, f: (jnp.minimum((i + 1) * r8, nblk8 - 1), 0)),
                  pl.BlockSpec((1, D_MODEL), lambda i, f: (0, 0)),
                  pl.BlockSpec((D_MODEL, fc), lambda i, f: (0, f)),
                  pl.BlockSpec((D_MODEL, fc), lambda i, f: (0, f)),
                  pl.BlockSpec((3, fc), lambda i, f: (0, f)),
                  pl.BlockSpec((1, fc), lambda i, f: (0, f)),
                  pl.BlockSpec((fc, D_MODEL), lambda i, f: (f, 0))],
        out_specs=pl.BlockSpec((tm, D_MODEL), row),
        out_shape=jax.ShapeDtypeStruct((ntok, D_MODEL), F32),
        scratch_shapes=[pltpu.VMEM((tm + 2 * SUBLANES, D_MODEL), BF16), pltpu.VMEM((tm + 2 * SUBLANES, fc), F32)],
        compiler_params=_cparams("parallel", "arbitrary"),
        name="conv_ffn",
    )(x2, x2, x2, w["norm_ffn_g"], w["w_up_gate"], w["w_up_val"], w["conv_w"], w["conv_b"], w["w_down"])


def _prep_weights(p):
    w = {}
    row = lambda a: a.reshape(1, -1).astype(F32)
    w_in = p["w_in"]
    o = np.cumsum([0, Q_LORA, KV_LORA, MLA_ROPE, RW_COLS, X_DIM, 3 * D_MODEL])
    seg = lambda i: w_in[:, o[i]:o[i + 1]]
    zc = lambda n: jnp.zeros((D_MODEL, n), F32)
    w["w_c"] = jnp.concatenate([seg(0), seg(1), zc(MLA_NOPE), seg(2), zc(LANES - MLA_QK)], axis=1).astype(BF16)
    w["w_rw"] = seg(3).astype(BF16)
    w["w_xq"] = seg(4).astype(BF16)
    w["w_gate"] = seg(5).astype(BF16)
    w["norm_mix_g"] = row(p["norm_mix_g"])
    w["q_norm_g"] = row(p["q_norm_g"])
    w["kv_norm_g"] = row(p["kv_norm_g"])
    pad_slot = lambda a: jnp.pad(a, ((0, 0), (0, 0), (0, LANES - a.shape[-1]))).reshape(a.shape[0], -1)
    uq = p["w_uq"].reshape(Q_LORA, MLA_HEADS, MLA_QK)
    w["w_uq"] = pad_slot(uq).astype(BF16)
    half = MLA_ROPE // 2
    uq_rot = jnp.concatenate([jnp.zeros_like(uq[:, :, :MLA_NOPE]), -uq[:, :, MLA_NOPE + half:],
                              uq[:, :, MLA_NOPE:MLA_NOPE + half]], axis=-1)
    w["w_uq_rot"] = pad_slot(uq_rot).astype(BF16)
    ukv =p["w_ukv"].reshape(KV_LORA, MLA_HEADS, MLA_NOPE + MLA_V)
    w["w_uk"] = pad_slot(ukv[:, :, :MLA_NOPE]).astype(BF16)
    w["w_uv"] = pad_slot(ukv[:, :, MLA_NOPE:]).T.astype(BF16)
    w["mla_qn_g_scaled"] = p["mla_qn_g"].reshape(-1) * np.float32(MLA_QK ** -0.5 * np.log2(np.e))
    w["mla_kn_g"] = p["mla_kn_g"].reshape(-1)
    for name in ("mu_prev", "mu_next", "k_k", "k_a", "r_k", "lnx_g", "lnx_b", "mem_norm_g", "x_kn_g", "norm_ffn_g",
                 "conv_b"):
        w[name] = row(p[name])
    w["w0"] = jnp.stack([p["w0_f"], p["w0_b"]]).reshape(2, 1, RW_DIM)
    w["a0"] = jnp.stack([p["a0_f"], p["a0_b"]]).reshape(2, 1, RW_DIM)
    zl = jnp.zeros((LORA, RW_DIM), F32)
    w["w2"] = jnp.stack([jnp.concatenate([p["w2_f"], zl]), jnp.concatenate([zl, p["w2_b"]])]).astype(BF16)
    w["a2"] = jnp.stack([jnp.concatenate([p["a2_f"], zl]), jnp.concatenate([zl, p["a2_b"]])]).astype(BF16)
    w["g2"] = p["g2"].astype(BF16)
    hid = np.arange(RW_DIM) // RW_HEAD
    w["seg"] = jnp.asarray((hid[:, None] == hid[None, :]).astype(np.float32), BF16)
    mkv = p["w_mkv"].reshape(D_MODEL, X_HEADS, 2 * X_HEAD)
    w["w_mk"] = mkv[:, :, :X_HEAD].reshape(D_MODEL, X_DIM).astype(BF16)
    w["w_mv"] = mkv[:, :, X_HEAD:].reshape(D_MODEL, X_DIM).astype(BF16)
    w["x_qn_g_scaled"] = row(p["x_qn_g"]) * np.float32(X_HEAD ** -0.5)
    for name in ("w_o_a", "w_o_b", "w_o_c", "w_out", "w_down"):
        w[name] = p[name].astype(BF16)
    w["w_up_gate"] = p["w_up"][:, :D_FF].astype(BF16)
    w["w_up_val"] = p["w_up"][:, D_FF:].astype(BF16)
    w["conv_w"] = p["conv_w"].astype(F32)
    return w


def _rope_tables(T, gq, gk):
    half = MLA_ROPE // 2
    inv = jnp.power(ROPE_THETA, -jnp.arange(half, dtype=F32) / half)
    ang = jnp.arange(T, dtype=F32)[:, None] * inv[None, :]
    cos, sin = jnp.cos(ang), jnp.sin(ang)
    z = lambda n: jnp.zeros((T, n), F32)
    pad = z(LANES - MLA_QK)

    def own(g):
        g1, g2 = g[MLA_NOPE:MLA_NOPE + half], g[MLA_NOPE + half:]
        return jnp.concatenate([jnp.broadcast_to(g[:MLA_NOPE], (T, MLA_NOPE)), cos * g1, cos * g2, pad], axis=1)

    g1, g2 = gq[MLA_NOPE:MLA_NOPE + half], gq[MLA_NOPE + half:]
    qb = jnp.concatenate([z(MLA_NOPE), sin * g2, sin * g1, pad], axis=1)
    g1, g2 = gk[MLA_NOPE:MLA_NOPE + half], gk[MLA_NOPE + half:]
    kba = jnp.concatenate([z(MLA_NOPE), -sin * g2, z(half), pad], axis=1)
    kbb = jnp.concatenate([z(MLA_NOPE), z(half), sin * g1, pad], axis=1)
    return own(gq), qb, own(gk), kba, kbb


def _layer(x, mem, w):
    B, T, _ = x.shape
    x2 = x.reshape(B * T, D_MODEL)
    q, k, v = _mla_prep(x2, T, w, _rope_tables(T, w["mla_qn_g_scaled"], w["mla_kn_g"]))
    o_a = _flash(q, k, v, B, T)
    v7, kkt, rt, kh, bh, kw, bw, wt, bonus, g7 = _rw_prep(x2, T, w)
    y_f, y_b = _rw_scan(v7, kkt, rt, kh, bh, kw, bw, wt, B, T)
    mk, mv = _mem_kv(mem.reshape(B * N_MEM, D_MODEL), w)
    o_c = _xattn(x2, mk, mv, B, T, w)
    x1 = _merge(x2, o_a, y_f, y_b, bonus, g7, o_c, T, w)
    return _ffn(x1, T, w).reshape(B, T, D_MODEL)


def kernel(x_prompt, x_sample, mem_prompt, mem_sample, norm_mix_g, w_in, q_norm_g, w_uq, kv_norm_g, w_ukv, mla_qn_g, mla_kn_g, w_o_a, mu_prev, mu_next, w0_f, w2_f, a0_f, a2_f, w0_b, w2_b, a0_b, a2_b, g2, k_k, k_a, r_k, lnx_g, lnx_b, w_o_b, mem_norm_g, w_mkv, x_qn_g, x_kn_g, w_o_c, w_out, norm_ffn_g, w_up, conv_w, conv_b, w_down):
    p = dict(norm_mix_g=norm_mix_g, w_in=w_in, q_norm_g=q_norm_g, w_uq=w_uq, kv_norm_g=kv_norm_g, w_ukv=w_ukv,
             mla_qn_g=mla_qn_g, mla_kn_g=mla_kn_g, w_o_a=w_o_a, mu_prev=mu_prev, mu_next=mu_next,
             w0_f=w0_f, w2_f=w2_f, a0_f=a0_f, a2_f=a2_f, w0_b=w0_b, w2_b=w2_b, a0_b=a0_b, a2_b=a2_b,
             g2=g2, k_k=k_k, k_a=k_a, r_k=r_k, lnx_g=lnx_g, lnx_b=lnx_b, w_o_b=w_o_b,
             mem_norm_g=mem_norm_g, w_mkv=w_mkv, x_qn_g=x_qn_g, x_kn_g=x_kn_g, w_o_c=w_o_c, w_out=w_out,
             norm_ffn_g=norm_ffn_g, w_up=w_up, conv_w=conv_w, conv_b=conv_b, w_down=w_down)
    w = _prep_weights({name: a[0] for name, a in p.items()})
    return (_layer(x_prompt, mem_prompt, w), _layer(x_sample, mem_sample, w))
```

```python
import functools

import numpy as np
import jax
import jax.numpy as jnp
from jax import lax
from jax.experimental import pallas as pl
from jax.experimental.pallas import tpu as pltpu

F32 = jnp.float32
BF16 = jnp.bfloat16

D_MODEL = 1024
RMS_EPS = 1e-6
N_MEM = 256
MLA_HEADS = 8
MLA_NOPE = 64
MLA_ROPE = 32
MLA_QK = MLA_NOPE + MLA_ROPE
MLA_V = 64
Q_LORA = 384
KV_LORA = 256
ROPE_THETA = 10000.0
RW_HEADS = 8
RW_HEAD = 64
RW_DIM = RW_HEADS * RW_HEAD
LORA = 64
GATE_LORA = 128
RW_COLS = 3 * RW_DIM + 4 * LORA + GATE_LORA
LNX_EPS = 64e-5
X_HEADS = 4
X_HEAD = 128
X_DIM = X_HEADS * X_HEAD
D_FF = 2816

LANES = 128
SUBLANES = 8
CHUNK = 64
GROUP = 2
GW = GROUP * RW_HEAD
SCAN_CHUNKS = 2
VMEM_LIMIT = 56 * 1024 * 1024


def _cparams(*sem):
    return pltpu.CompilerParams(dimension_semantics=sem, vmem_limit_bytes=VMEM_LIMIT)


def _rms(x, g, eps=RMS_EPS):
    return x * lax.rsqrt(jnp.mean(x * x, axis=-1, keepdims=True) + eps) * g


def _dot(a, b):
    return jnp.dot(a, b, preferred_element_type=F32)


def _dot_nt(a, b):
    return lax.dot_general(a, b, (((1,), (1,)), ((), ())), preferred_element_type=F32)


def _dot_tn(a, b):
    return lax.dot_general(a, b, (((0,), (0,)), ((), ())), preferred_element_type=F32)


def _dot_0_1(x, e):
    return _dot(x.astype(BF16), e)


def _dot_0_1_l(e, x):
    hi = x.astype(BF16)
    lo = (x - hi.astype(F32)).astype(BF16)
    return _dot(e, hi) + _dot(e, lo)


def _sigmoid(z):
    return 1.0 / (1.0 + jnp.exp(-z))


def _full(shape):
    nd = len(shape)
    return pl.BlockSpec(shape, lambda *_: (0,) * nd, pipeline_mode=pl.Buffered(1))


def _mla_prep_kernel(x_ref, g_ref, wc_ref, gq_ref, gkv_ref, wuq_ref, wuqr_ref, wuk_ref, wuv_ref,
                     qa_ref, qb_ref, ka_ref, kba_ref, kbb_ref, q_out, k_out, v_out):
    h = _rms(x_ref[...], g_ref[...]).astype(BF16)
    c = _dot(h, wc_ref[...])
    cq = _rms(c[:, :Q_LORA], gq_ref[...]).astype(BF16)
    ckv = _rms(c[:, Q_LORA:Q_LORA + KV_LORA], gkv_ref[...]).astype(BF16)
    kr = c[:, Q_LORA + KV_LORA:]
    q = _dot(cq, wuq_ref[...])
    qp = _dot(cq, wuqr_ref[...])
    k = _dot(ckv, wuk_ref[...])
    vt = _dot_nt(wuv_ref[...], ckv)
    slot_row = lax.broadcasted_iota(jnp.int32, vt.shape, 0) % LANES
    v_out[0] = jnp.where(slot_row == MLA_V, 1.0, vt).astype(BF16)
    qa = qa_ref[...]
    qb = qb_ref[...]
    ka = ka_ref[...]
    half = MLA_ROPE // 2
    kr_roped = kr * ka + pltpu.roll(kr, LANES - half, 1) * kba_ref[...] + pltpu.roll(kr, half, 1) * kbb_ref[...]
    kr_ss = jnp.sum(kr * kr, axis=-1, keepdims=True)
    for hd in range(MLA_HEADS):
        sl = slice(LANES * hd, LANES * (hd + 1))
        t = q[:, sl]
        ss = jnp.sum(t * t, axis=-1, keepdims=True) * (1.0 / MLA_QK)
        q_out[:, sl] = ((t * qa + qp[:, sl] * qb) * lax.rsqrt(ss + RMS_EPS)).astype(BF16)
        t = k[:, sl]
        ss = (jnp.sum(t * t, axis=-1, keepdims=True) + kr_ss) * (1.0 / MLA_QK)
        k_out[:, sl] = ((t * ka + kr_roped) * lax.rsqrt(ss + RMS_EPS)).astype(BF16)


def _mla_prep(x2, T, w, tabs):
    ntok = x2.shape[0]
    tm = _attn_tile(T)
    npos = T // tm
    row = lambda i: (i, 0)
    pos = lambda i: (i % npos, 0)
    return pl.pallas_call(
        _mla_prep_kernel,
        grid=(ntok // tm,),
        in_specs=[pl.BlockSpec((tm, D_MODEL), row), _full((1, D_MODEL)), _full((D_MODEL, 768)),
                  _full((1, Q_LORA)), _full((1, KV_LORA)), _full((Q_LORA, 1024)), _full((Q_LORA, 1024)),
                  _full((KV_LORA, 1024)), _full((1024, KV_LORA))] + [pl.BlockSpec((tm, LANES), pos)] * 5,
        out_specs=[pl.BlockSpec((tm, 1024), row), pl.BlockSpec((tm, 1024), row),
                   pl.BlockSpec((1, 1024, tm), lambda i: (i, 0, 0))],
        out_shape=[jax.ShapeDtypeStruct((ntok, 1024), BF16), jax.ShapeDtypeStruct((ntok, 1024), BF16),
                   jax.ShapeDtypeStruct((ntok // tm, 1024, tm), BF16)],
        compiler_params=_cparams("parallel"),
        name="mla_prep",
    )(x2, w["norm_mix_g"], w["w_c"], w["q_norm_g"], w["kv_norm_g"], w["w_uq"], w["w_uq_rot"], w["w_uk"],
      w["w_uv"], *tabs)


def _flash_kernel(q_ref, k_ref, vt_ref, o_ref, s0_ref, s1_ref, *, tk, nk):
    tq = q_ref.shape[0]
    slots = [slice(LANES * h, LANES * (h + 1)) for h in range(2)]
    per = tk // vt_ref.shape[2]

    def scores(j, s_ref):
        kblk = k_ref[pl.ds(pl.multiple_of(j * tk, tk), tk), :]
        for h in range(2):
            s_ref[h] = _dot_nt(kblk[:, slots[h]], q_ref[:, slots[h]])

    def consume(j, s_ref, state):
        vt = jnp.concatenate([vt_ref[per * j + r] for r in range(per)], axis=1)
        m_news = [jnp.maximum(state[h][0], jnp.max(s_ref[h], axis=0, keepdims=True)) for h in range(2)]
        pts = [jnp.exp2((s_ref[h] - m_news[h]).astype(BF16)) for h in range(2)]
        return tuple((m_news[h], jnp.exp2(state[h][0] - m_news[h]) * state[h][1] + _dot(vt[slots[h], :], pts[h]))
                     for h in range(2))

    def body(t, state):
        j = 2 * t
        scores(j + 1, s1_ref)
        state = consume(j, s0_ref, state)
        scores(j + 2, s0_ref)
        return consume(j + 1, s1_ref, state)

    state = tuple((jnp.full((1, tq), -jnp.inf, F32), jnp.zeros((LANES, tq), F32)) for _ in range(2))
    scores(0, s0_ref)
    state = lax.fori_loop(0, nk // 2 - 1, body, state)
    scores(nk - 1, s1_ref)
    state = consume(nk - 2, s0_ref, state)
    state = consume(nk - 1, s1_ref, state)
    outs = [acc[0:MLA_V, :] / acc[MLA_V:MLA_V + 1, :] for _, acc in state]
    o_ref[...] = jnp.concatenate(outs, axis=0).T.astype(BF16)


def _attn_tile(T):
    return min(512, T // 2)


def _flash(q, k, vt, B, T):
    tq = _attn_tile(T)
    tk = min(1024, T // 2)
    tv = _attn_tile(T)
    nq = T // tq
    nk = T // tk
    assert nk % 2 == 0 and tk % tv == 0, (T, tk, tv)
    return pl.pallas_call(
        functools.partial(_flash_kernel, tk=tk, nk=nk),
        grid=(B, MLA_HEADS // 2, nq),
        in_specs=[pl.BlockSpec((tq, 2 * LANES), lambda b, hp, i: (b * nq + i, hp)),
                  pl.BlockSpec((T, 2 * LANES), lambda b, hp, i: (b, hp)),
                  pl.BlockSpec((T // tv, 2 * LANES, tv), lambda b, hp, i: (b, hp, 0))],
        out_specs=pl.BlockSpec((tq, LANES), lambda b, hp, i: (b * nq + i, hp)),
        out_shape=jax.ShapeDtypeStruct((B * T, MLA_HEADS * MLA_V), BF16),
        scratch_shapes=[pltpu.VMEM((2, tk, tq), F32), pltpu.VMEM((2, tk, tq), F32)],
        compiler_params=_cparams("parallel", "parallel", "arbitrary"),
        name="mla_flash",
    )(q, k, vt)


def _rw_prep_kernel(x_ref, xp_ref, xn_ref, g_ref, wrw_ref, mup_ref, mun_ref, kk_ref, ka_ref, rk_ref,
                    w0_ref, w2_ref, a0_ref, a2_ref, g2_ref, seg_ref, tri_ref,
                    v_out, kkt_out, rt_out, kh_out, bh_out, kw_out, bw_out, wt_out, bonus_out, g7_out,
                    h_scr, rw_scr, *, tm, npos):
    i = pl.program_id(0)
    not_first = (i % npos != 0).astype(F32)
    not_last = (i % npos != npos - 1).astype(F32)
    g = g_ref[...]
    h_scr[0:SUBLANES, :] = (_rms(xp_ref[...], g) * not_first).astype(BF16)
    h_scr[SUBLANES:SUBLANES + tm, :] = _rms(x_ref[...], g).astype(BF16)
    h_scr[SUBLANES + tm:, :] = (_rms(xn_ref[...], g) * not_last).astype(BF16)
    rw_scr[...] = _dot(h_scr[...], wrw_ref[...])
    cur = rw_scr[SUBLANES:SUBLANES + tm, :]
    prev = rw_scr[SUBLANES - 1:SUBLANES - 1 + tm, :]
    nxt = rw_scr[SUBLANES + 1:SUBLANES + 1 + tm, :]
    mup = mup_ref[...]
    mun = mun_ref[...]
    rwf = cur * (1.0 - mup - mun) + mup * prev + mun * nxt
    r7 = rwf[:, 0:RW_DIM]
    k7 = rwf[:, RW_DIM:2 * RW_DIM]
    v7 = rwf[:, 2 * RW_DIM:3 * RW_DIM]
    wl = rwf[:, 3 * RW_DIM:3 * RW_DIM + 2 * LORA]
    al = rwf[:, 3 * RW_DIM + 2 * LORA:3 * RW_DIM + 4 * LORA]
    gl = rwf[:, 3 * RW_DIM + 4 * LORA:]
    seg = seg_ref[...]
    kx = k7 * kk_ref[...]
    kkn = kx * lax.rsqrt(jnp.maximum(_dot_0_1(kx * kx, seg), 1e-24))
    v_out[...] = v7.astype(BF16)
    g7_out[...] = _dot(_sigmoid(gl).astype(BF16), g2_ref[...])
    tw = jnp.tanh(wl).astype(BF16)
    alb = al.astype(BF16)
    bonus = jnp.zeros((tm, RW_DIM), F32)
    for d in range(2):
        z = -(w0_ref[d] + _dot(tw, w2_ref[d]))
        softplus = jnp.maximum(z, 0.0) + jnp.log(1.0 + jnp.exp(-jnp.abs(z)))
        lw = -jnp.exp(-softplus - 0.5)
        a = _sigmoid(a0_ref[d] + _dot(alb, a2_ref[d]))
        kd = k7 * (1.0 + (a - 1.0) * ka_ref[...])
        b = kkn * a
        bonus = bonus + _dot_0_1(r7 * kd * rk_ref[...], seg) * v7
        cum = _dot_0_1_l(tri_ref[d], lw)
        last = CHUNK - 1 if d == 0 else 0
        tot_rows = [cum[CHUNK * ci + last:CHUNK * ci + last + 1, :] for ci in range(tm // CHUNK)]
        tot = jnp.concatenate([jnp.broadcast_to(r, (CHUNK, RW_DIM)) for r in tot_rows], axis=0)
        w_incl = jnp.exp(cum)
        w_excl = jnp.exp(cum - lw)
        w_inv = jnp.exp(-cum)
        w_rest = jnp.exp(tot - cum)
        kkt_out[d] = (kkn * w_excl).astype(BF16)
        rt_out[d] = (r7 * w_incl).astype(BF16)
        kh_out[d] = (kd * w_inv).astype(BF16)
        bh_out[d] = (b * w_inv).astype(BF16)
        kw_out[d] = (kd * w_rest).astype(BF16)
        bw_out[d] = (b * w_rest).astype(BF16)
        for ci in range(tm // CHUNK):
            wt_out[d, ci] = jnp.exp(tot_rows[ci])
    bonus_out[...] = bonus


def _rw_prep(x2, T, w):
    ntok = x2.shape[0]
    tm = min(256, T)
    npos = T // tm
    nblk8 = ntok // SUBLANES
    r8 = tm // SUBLANES
    row = lambda i: (i, 0)
    drow = lambda i: (0, i, 0)
    dspec = pl.BlockSpec((2, tm, RW_DIM), drow)
    dshape = jax.ShapeDtypeStruct((2, ntok, RW_DIM), BF16)
    tri = _chunk_masks(tm)
    outs = pl.pallas_call(
        functools.partial(_rw_prep_kernel, tm=tm, npos=npos),
        grid=(ntok // tm,),
        in_specs=[pl.BlockSpec((tm, D_MODEL), row),
                  pl.BlockSpec((SUBLANES, D_MODEL), lambda i: (jnp.maximum(i * r8 - 1, 0), 0)),
                  pl.BlockSpec((SUBLANES, D_MODEL), lambda i: (jnp.minimum((i + 1) * r8, nblk8 - 1), 0)),
                  _full((1, D_MODEL)), _full((D_MODEL, RW_COLS)), _full((1, RW_COLS)), _full((1, RW_COLS)),
                  _full((1, RW_DIM)), _full((1, RW_DIM)), _full((1, RW_DIM)),
                  _full((2, 1, RW_DIM)), _full((2, 2 * LORA, RW_DIM)), _full((2, 1, RW_DIM)),
                  _full((2, 2 * LORA, RW_DIM)), _full((GATE_LORA, RW_DIM)), _full((RW_DIM, RW_DIM)),
                  _full((2, tm, tm))],
        out_specs=[pl.BlockSpec((tm, RW_DIM), row), dspec, dspec, dspec, dspec, dspec, dspec,
                   pl.BlockSpec((2, tm // CHUNK, 1, RW_DIM), lambda i: (0, i, 0, 0)),
                   pl.BlockSpec((tm, RW_DIM), row), pl.BlockSpec((tm, RW_DIM), row)],
        out_shape=[jax.ShapeDtypeStruct((ntok, RW_DIM), BF16), dshape, dshape, dshape, dshape, dshape, dshape,
                   jax.ShapeDtypeStruct((2, ntok // CHUNK, 1, RW_DIM), F32),
                   jax.ShapeDtypeStruct((ntok, RW_DIM), F32), jax.ShapeDtypeStruct((ntok, RW_DIM), F32)],
        scratch_shapes=[pltpu.VMEM((tm + 2 * SUBLANES, D_MODEL), BF16), pltpu.VMEM((tm + 2 * SUBLANES, RW_COLS), F32)],
        compiler_params=_cparams("parallel"),
        name="rw_prep",
    )(x2, x2, x2, w["norm_mix_g"], w["w_rw"], w["mu_prev"], w["mu_next"], w["k_k"], w["k_a"], w["r_k"],
      w["w0"], w["w2"], w["a0"], w["a2"], w["g2"], w["seg"], tri)
    return outs


def _chunk_masks(tm):
    t = np.arange(tm)
    same = (t[:, None] // CHUNK) == (t[None, :] // CHUNK)
    fwd = same & (t[None, :] <= t[:, None])
    bwd = same & (t[None, :] >= t[:, None])
    return jnp.asarray(np.stack([fwd, bwd]).astype(np.float32), BF16)


def _rw_scan_kernel(*refs):
    ins = refs[:16]
    yf_ref, yb_ref, s_ref = refs[16:]
    c = pl.program_id(1)

    @pl.when(c == 0)
    def _():
        s_ref[...] = jnp.zeros(s_ref.shape, F32)

    ri = lax.broadcasted_iota(jnp.int32, (GW, GW), 0)
    ci = lax.broadcasted_iota(jnp.int32, (GW, GW), 1)
    head_mask = (ri // CHUNK) == (ci // RW_HEAD)
    rt_ = ri % CHUNK
    cs_ = ci % CHUNK
    eye = (ri == ci).astype(F32)
    stack = lambda a: jnp.where(head_mask, jnp.concatenate([a] * GROUP, axis=0), jnp.zeros((), BF16))

    items = []
    for d, y_ref in ((0, yf_ref), (1, yb_ref)):
        kkt, rt, kh, bh, kw, bw, v, wt = ins[8 * d:8 * d + 8]
        strict = (cs_ < rt_) if d == 0 else (cs_ > rt_)
        incl = (cs_ <= rt_) if d == 0 else (cs_ >= rt_)
        for gi in range(RW_HEADS // GROUP):
            sl = slice(GW * gi, GW * (gi + 1))
            for sub in range(SCAN_CHUNKS):
                rows = slice(CHUNK * sub, CHUNK * (sub + 1))
                it = dict(y_ref=y_ref, sl=sl, rows=rows, s_ref=s_ref.at[d, gi], wtot=wt[sub][:, sl],
                          order=sub if d == 0 else SCAN_CHUNKS - 1 - sub)
                xkk, xr, yk, yb, vs, kws, bws = (stack(r[rows, sl]) for r in (kkt, rt, kh, bh, v, kw, bw))
                a_all = _dot_nt(jnp.concatenate([xkk, xr], axis=0), jnp.concatenate([yk, yb], axis=0))
                it.update(xkk=xkk, xr=xr, vs=vs, kbw=jnp.concatenate([kws, bws], axis=0),
                          ak=jnp.where(strict, a_all[:GW, :GW], 0.0).astype(BF16),
                          bk=jnp.where(incl, a_all[GW:, :GW], 0.0).astype(BF16),
                          bb=jnp.where(incl, a_all[GW:, GW:], 0.0).astype(BF16))
                ab = jnp.where(strict, a_all[:GW, GW:], 0.0)
                it.update(tinv=eye - ab, pw=ab.astype(BF16))
                items.append(it)
    for _ in range(5):
        for it in items:
            it["pw"] = _dot(it["pw"], it["pw"]).astype(BF16)
        for it in items:
            it["tinv"] = it["tinv"] + _dot(it["tinv"].astype(BF16), it["pw"])
    for it in items:
        it["tinv"] = it["tinv"].astype(BF16)
    for step in range(SCAN_CHUNKS):
        chains = [it for it in items if it["order"] == step]
        for ch in chains:
            ch["sb"] = ch["s_ref"][...].astype(BF16)
            ch["rhs"] = (_dot_nt(ch["xkk"], ch["sb"]) + _dot(ch["ak"], ch["vs"])).astype(BF16)
        for ch in chains:
            ch["u"] = (-_dot(ch["tinv"], ch["rhs"])).astype(BF16)
        for ch in chains:
            y = _dot_nt(ch["xr"], ch["sb"]) + _dot(ch["bk"], ch["vs"]) + _dot(ch["bb"], ch["u"])
            ch["y_ref"][ch["rows"], ch["sl"]] = sum(y[CHUNK * hd:CHUNK * (hd + 1)] for hd in range(GROUP))
        for ch in chains:
            ch["s_ref"][...] = (ch["s_ref"][...] * ch["wtot"]
                                + _dot_tn(jnp.concatenate([ch["vs"], ch["u"]], axis=0), ch["kbw"]))


def _rw_scan(v, kkt, rt, kh, bh, kw, bw, wt, B, T):
    ntok = B * T
    rows = SCAN_CHUNKS * CHUNK
    nb = T // rows
    in_specs = []
    args = []
    for d in range(2):
        if d == 0:
            cm = lambda b, c: b * nb + c
        else:
            cm = lambda b, c: b * nb + (nb - 1 - c)
        for arr in (kkt, rt, kh, bh, kw, bw):
            in_specs.append(pl.BlockSpec((None, rows, RW_DIM), lambda b, c, cm=cm, d=d: (d, cm(b, c), 0)))
            args.append(arr)
        in_specs.append(pl.BlockSpec((rows, RW_DIM), lambda b, c, cm=cm: (cm(b, c), 0)))
        args.append(v)
        in_specs.append(pl.BlockSpec((None, SCAN_CHUNKS, 1, RW_DIM), lambda b, c, cm=cm, d=d: (d, cm(b, c), 0, 0)))
        args.append(wt)
    return pl.pallas_call(
        _rw_scan_kernel,
        grid=(B, nb),
        in_specs=in_specs,
        out_specs=[pl.BlockSpec((rows, RW_DIM), lambda b, c: (b * nb + c, 0)),
                   pl.BlockSpec((rows, RW_DIM), lambda b, c: (b * nb + (nb - 1 - c), 0))],
        out_shape=[jax.ShapeDtypeStruct((ntok, RW_DIM), F32), jax.ShapeDtypeStruct((ntok, RW_DIM), F32)],
        scratch_shapes=[pltpu.VMEM((2, RW_HEADS // GROUP, GW, GW), F32)],
        compiler_params=_cparams("parallel", "arbitrary"),
        name="rw_scan",
    )(*args)


def _mem_kv_kernel(m_ref, g_ref, wk_ref, wv_ref, gk_ref, k_out, v_out):
    m = _rms(m_ref[...], g_ref[...]).astype(BF16)
    k = _dot(m, wk_ref[...])
    v_out[...] = _dot(m, wv_ref[...]).astype(BF16)
    for hd in range(X_HEADS):
        sl = slice(X_HEAD * hd, X_HEAD * (hd + 1))
        k_out[:, sl] = _rms(k[:, sl], gk_ref[...]).astype(BF16)


def _mem_kv(mem2, w):
    n = mem2.shape[0]
    row = lambda i: (i, 0)
    return pl.pallas_call(
        _mem_kv_kernel,
        grid=(n // N_MEM,),
        in_specs=[pl.BlockSpec((N_MEM, D_MODEL), row), _full((1, D_MODEL)), _full((D_MODEL, X_DIM)),
                  _full((D_MODEL, X_DIM)), _full((1, X_HEAD))],
        out_specs=[pl.BlockSpec((N_MEM, X_DIM), row), pl.BlockSpec((N_MEM, X_DIM), row)],
        out_shape=[jax.ShapeDtypeStruct((n, X_DIM), BF16), jax.ShapeDtypeStruct((n, X_DIM), BF16)],
        compiler_params=_cparams("parallel"),
        name="mem_kv",
    )(mem2, w["mem_norm_g"], w["w_mk"], w["w_mv"], w["x_kn_g"])


def _xattn_kernel(x_ref, g_ref, wq_ref, gq_ref, mk_ref, mv_ref, o_ref):
    h = _rms(x_ref[...], g_ref[...]).astype(BF16)
    q = _dot(h, wq_ref[...])
    for hd in range(X_HEADS):
        sl = slice(X_HEAD * hd, X_HEAD * (hd + 1))
        qh = _rms(q[:, sl], gq_ref[...]).astype(BF16)
        s = _dot_nt(qh, mk_ref[:, sl])
        p = jnp.exp(s - jnp.max(s, axis=-1, keepdims=True))
        o = _dot(p.astype(BF16), mv_ref[:, sl]) / jnp.sum(p, axis=-1, keepdims=True)
        o_ref[:, sl] = o.astype(BF16)


def _xattn(x2, mk, mv, B, T, w):
    tq = min(512, T)
    nq = T // tq
    return pl.pallas_call(
        _xattn_kernel,
        grid=(B, nq),
        in_specs=[pl.BlockSpec((tq, D_MODEL), lambda b, i: (b * nq + i, 0)), _full((1, D_MODEL)),
                  _full((D_MODEL, X_DIM)), _full((1, X_HEAD)),
                  pl.BlockSpec((N_MEM, X_DIM), lambda b, i: (b, 0)), pl.BlockSpec((N_MEM, X_DIM), lambda b, i: (b, 0))],
        out_specs=pl.BlockSpec((tq, X_DIM), lambda b, i: (b * nq + i, 0)),
        out_shape=jax.ShapeDtypeStruct((B * T, X_DIM), BF16),
        compiler_params=_cparams("parallel", "parallel"),
        name="xattn",
    )(x2, w["norm_mix_g"], w["w_xq"], w["x_qn_g_scaled"], mk, mv)


def _merge_kernel(x_ref, oa_ref, yf_ref, yb_ref, bonus_ref, g7_ref, oc_ref, g_ref, wg_ref, woa_ref, wob_ref,
                  woc_ref, wout_ref, lng_ref, lnb_ref, seg_ref, o_ref):
    x = x_ref[...]
    h = _rms(x, g_ref[...]).astype(BF16)
    seg = seg_ref[...]
    y7 = yf_ref[...] + yb_ref[...] + bonus_ref[...]
    mu = _dot_0_1(y7, seg) * (1.0 / RW_HEAD)
    dy = y7 - mu
    var = _dot_0_1(dy * dy, seg) * (1.0 / RW_HEAD)
    y7 = dy * lax.rsqrt(var + LNX_EPS) * lng_ref[...] + lnb_ref[...]
    yb = (y7 * g7_ref[...]).astype(BF16)
    merged = jnp.zeros(x.shape, F32)
    for bi, (br, wo) in enumerate(((oa_ref[...], woa_ref), (yb, wob_ref), (oc_ref[...], woc_ref))):
        gate = _sigmoid(_dot(h, wg_ref[:, D_MODEL * bi:D_MODEL * (bi + 1)]))
        merged = merged + gate * _dot(br, wo[...])
    o_ref[...] = x + _dot(merged.astype(BF16), wout_ref[...])


def _merge(x2, oa, yf, yb, bonus, g7, oc, T, w):
    ntok = x2.shape[0]
    tm = min(512, T)
    row = lambda i: (i, 0)
    half = lambda: pl.BlockSpec((tm, RW_DIM), row)
    return pl.pallas_call(
        _merge_kernel,
        grid=(ntok // tm,),
        in_specs=[pl.BlockSpec((tm, D_MODEL), row), half(), half(), half(), half(), half(), half(),
                  _full((1, D_MODEL)), _full((D_MODEL, 3 * D_MODEL)), _full((RW_DIM, D_MODEL)),
                  _full((RW_DIM, D_MODEL)), _full((RW_DIM, D_MODEL)), _full((D_MODEL, D_MODEL)),
                  _full((1, RW_DIM)), _full((1, RW_DIM)), _full((RW_DIM, RW_DIM))],
        out_specs=pl.BlockSpec((tm, D_MODEL), row),
        out_shape=jax.ShapeDtypeStruct((ntok, D_MODEL), F32),
        compiler_params=_cparams("parallel"),
        name="merge",
    )(x2, oa, yf, yb, bonus, g7, oc, w["norm_mix_g"], w["w_gate"], w["w_o_a"], w["w_o_b"], w["w_o_c"],
      w["w_out"], w["lnx_g"], w["lnx_b"], w["seg"])


MXU_TILE = 256
FFN_SPLITS = (0, 5 * MXU_TILE, D_FF)


def _ffn_kernel(x_ref, xp_ref, xn_ref, g_ref, wug_ref, wuv_ref, cw_ref, cb_ref, wd_ref, o_ref, h_scr, ug_scr,
                *, tm, npos):
    i = pl.program_id(0)
    not_first = (i % npos != 0).astype(F32)
    not_last = (i % npos != npos - 1).astype(F32)
    g = g_ref[...]
    h_scr[0:SUBLANES, :] = (_rms(xp_ref[...], g) * not_first).astype(BF16)
    h_scr[SUBLANES:SUBLANES + tm, :] = _rms(x_ref[...], g).astype(BF16)
    h_scr[SUBLANES + tm:, :] = (_rms(xn_ref[...], g) * not_last).astype(BF16)
    out = x_ref[...]
    for lo, hi in zip(FFN_SPLITS[:-1], FFN_SPLITS[1:]):
        n = hi - lo
        ug_scr[:, 0:n] = _dot(h_scr[...], wug_ref[:, lo:hi])
        uv = _dot(h_scr[SUBLANES:SUBLANES + tm, :], wuv_ref[:, lo:hi])
        cw = cw_ref[:, lo:hi]
        c = (cw[0:1] * ug_scr[SUBLANES - 1:SUBLANES - 1 + tm, 0:n] + cw[1:2] * ug_scr[SUBLANES:SUBLANES + tm, 0:n]
             + cw[2:3] * ug_scr[SUBLANES + 1:SUBLANES + 1 + tm, 0:n] + cb_ref[:, lo:hi])
        act = 0.5 * c * (1.0 + lax.erf(c * np.float32(1.0 / np.sqrt(2.0)))) * uv
        out = out + _dot(act.astype(BF16), wd_ref[lo:hi, :])
    o_ref[...] = out


def _ffn(x2, T, w):
    ntok = x2.shape[0]
    tm = min(512, T)
    fc = max(hi - lo for lo, hi in zip(FFN_SPLITS[:-1], FFN_SPLITS[1:]))
    npos = T // tm
    nblk8 = ntok // SUBLANES
    r8 = tm // SUBLANES
    row = lambda i: (i, 0)
    return pl.pallas_call(
        functools.partial(_ffn_kernel, tm=tm, npos=npos),
        grid=(ntok // tm,),
        in_specs=[pl.BlockSpec((tm, D_MODEL), row),
                  pl.BlockSpec((SUBLANES, D_MODEL), lambda i: (jnp.maximum(i * r8 - 1, 0), 0)),
                  pl.BlockSpec((SUBLANES, D_MODEL), lambda i: (jnp.minimum((i + 1) * r8, nblk8 - 1), 0)),
                  _full((1, D_MODEL)), _full((D_MODEL, D_FF)), _full((D_MODEL, D_FF)), _full((3, D_FF)),
                  _full((1, D_FF)), _full((D_FF, D_MODEL))],
        out_specs=pl.BlockSpec((tm, D_MODEL), row),
        out_shape=jax.ShapeDtypeStruct((ntok, D_MODEL), F32),
        scratch_shapes=[pltpu.VMEM((tm + 2 * SUBLANES, D_MODEL), BF16), pltpu.VMEM((tm + 2 * SUBLANES, fc), F32)],
        compiler_params=_cparams("parallel"),
        name="conv_ffn",
    )(x2, x2, x2, w["norm_ffn_g"], w["w_up_gate"], w["w_up_val"], w["conv_w"], w["conv_b"], w["w_down"])


def _prep_weights(p):
    w = {}
    row = lambda a: a.reshape(1, -1).astype(F32)
    w_in = p["w_in"]
    o = np.cumsum([0, Q_LORA, KV_LORA, MLA_ROPE, RW_COLS, X_DIM, 3 * D_MODEL])
    seg = lambda i: w_in[:, o[i]:o[i + 1]]
    zc = lambda n: jnp.zeros((D_MODEL, n), F32)
    w["w_c"] = jnp.concatenate([seg(0), seg(1), zc(MLA_NOPE), seg(2), zc(LANES - MLA_QK)], axis=1).astype(BF16)
    w["w_rw"] = seg(3).astype(BF16)
    w["w_xq"] = seg(4).astype(BF16)
    w["w_gate"] = seg(5).astype(BF16)
    w["norm_mix_g"] = row(p["norm_mix_g"])
    w["q_norm_g"] = row(p["q_norm_g"])
    w["kv_norm_g"] = row(p["kv_norm_g"])
    pad_slot = lambda a: jnp.pad(a, ((0, 0), (0, 0), (0, LANES - a.shape[-1]))).reshape(a.shape[0], -1)
    uq = p["w_uq"].reshape(Q_LORA, MLA_HEADS, MLA_QK)
    w["w_uq"] = pad_slot(uq).astype(BF16)
    half = MLA_ROPE // 2
    uq_rot = jnp.concatenate([jnp.zeros_like(uq[:, :, :MLA_NOPE]), -uq[:, :, MLA_NOPE + half:],
                              uq[:, :, MLA_NOPE:MLA_NOPE + half]], axis=-1)
    w["w_uq_rot"] = pad_slot(uq_rot).astype(BF16)
    ukv =p["w_ukv"].reshape(KV_LORA, MLA_HEADS, MLA_NOPE + MLA_V)
    w["w_uk"] = pad_slot(ukv[:, :, :MLA_NOPE]).astype(BF16)
    w["w_uv"] = pad_slot(ukv[:, :, MLA_NOPE:]).T.astype(BF16)
    w["mla_qn_g_scaled"] = p["mla_qn_g"].reshape(-1) * np.float32(MLA_QK ** -0.5 * np.log2(np.e))
    w["mla_kn_g"] = p["mla_kn_g"].reshape(-1)
    for name in ("mu_prev", "mu_next", "k_k", "k_a", "r_k", "lnx_g", "lnx_b", "mem_norm_g", "x_kn_g", "norm_ffn_g",
                 "conv_b"):
        w[name] = row(p[name])
    w["w0"] = jnp.stack([p["w0_f"], p["w0_b"]]).reshape(2, 1, RW_DIM)
    w["a0"] = jnp.stack([p["a0_f"], p["a0_b"]]).reshape(2, 1, RW_DIM)
    zl = jnp.zeros((LORA, RW_DIM), F32)
    w["w2"] = jnp.stack([jnp.concatenate([p["w2_f"], zl]), jnp.concatenate([zl, p["w2_b"]])]).astype(BF16)
    w["a2"] = jnp.stack([jnp.concatenate([p["a2_f"], zl]), jnp.concatenate([zl, p["a2_b"]])]).astype(BF16)
    w["g2"] = p["g2"].astype(BF16)
    hid = np.arange(RW_DIM) // RW_HEAD
    w["seg"] = jnp.asarray((hid[:, None] == hid[None, :]).astype(np.float32), BF16)
    mkv = p["w_mkv"].reshape(D_MODEL, X_HEADS, 2 * X_HEAD)
    w["w_mk"] = mkv[:, :, :X_HEAD].reshape(D_MODEL, X_DIM).astype(BF16)
    w["w_mv"] = mkv[:, :, X_HEAD:].reshape(D_MODEL, X_DIM).astype(BF16)
    w["x_qn_g_scaled"] = row(p["x_qn_g"]) * np.float32(X_HEAD ** -0.5)
    for name in ("w_o_a", "w_o_b", "w_o_c", "w_out", "w_down"):
        w[name] = p[name].astype(BF16)
    w["w_up_gate"] = p["w_up"][:, :D_FF].astype(BF16)
    w["w_up_val"] = p["w_up"][:, D_FF:].astype(BF16)
    w["conv_w"] = p["conv_w"].astype(F32)
    return w


def _rope_tables(T, gq, gk):
    half = MLA_ROPE // 2
    inv = jnp.power(ROPE_THETA, -jnp.arange(half, dtype=F32) / half)
    ang = jnp.arange(T, dtype=F32)[:, None] * inv[None, :]
    cos, sin = jnp.cos(ang), jnp.sin(ang)
    z = lambda n: jnp.zeros((T, n), F32)
    pad = z(LANES - MLA_QK)

    def own(g):
        g1, g2 = g[MLA_NOPE:MLA_NOPE + half], g[MLA_NOPE + half:]
        return jnp.concatenate([jnp.broadcast_to(g[:MLA_NOPE], (T, MLA_NOPE)), cos * g1, cos * g2, pad], axis=1)

    g1, g2 = gq[MLA_NOPE:MLA_NOPE + half], gq[MLA_NOPE + half:]
    qb = jnp.concatenate([z(MLA_NOPE), sin * g2, sin * g1, pad], axis=1)
    g1, g2 = gk[MLA_NOPE:MLA_NOPE + half], gk[MLA_NOPE + half:]
    kba = jnp.concatenate([z(MLA_NOPE), -sin * g2, z(half), pad], axis=1)
    kbb = jnp.concatenate([z(MLA_NOPE), z(half), sin * g1, pad], axis=1)
    return own(gq), qb, own(gk), kba, kbb


def _layer(x, mem, w):
    B, T, _ = x.shape
    x2 = x.reshape(B * T, D_MODEL)
    q, k, v = _mla_prep(x2, T, w, _rope_tables(T, w["mla_qn_g_scaled"], w["mla_kn_g"]))
    o_a = _flash(q, k, v, B, T)
    v7, kkt, rt, kh, bh, kw, bw, wt, bonus, g7 = _rw_prep(x2, T, w)
    y_f, y_b = _rw_scan(v7, kkt, rt, kh, bh, kw, bw, wt, B, T)
    mk, mv = _mem_kv(mem.reshape(B * N_MEM, D_MODEL), w)
    o_c = _xattn(x2, mk, mv, B, T, w)
    x1 = _merge(x2, o_a, y_f, y_b, bonus, g7, o_c, T, w)
    return _ffn(x1, T, w).reshape(B, T, D_MODEL)


def kernel(x_prompt, x_sample, mem_prompt, mem_sample, norm_mix_g, w_in, q_norm_g, w_uq, kv_norm_g, w_ukv, mla_qn_g, mla_kn_g, w_o_a, mu_prev, mu_next, w0_f, w2_f, a0_f, a2_f, w0_b, w2_b, a0_b, a2_b, g2, k_k, k_a, r_k, lnx_g, lnx_b, w_o_b, mem_norm_g, w_mkv, x_qn_g, x_kn_g, w_o_c, w_out, norm_ffn_g, w_up, conv_w, conv_b, w_down):
    p = dict(norm_mix_g=norm_mix_g, w_in=w_in, q_norm_g=q_norm_g, w_uq=w_uq, kv_norm_g=kv_norm_g, w_ukv=w_ukv,
             mla_qn_g=mla_qn_g, mla_kn_g=mla_kn_g, w_o_a=w_o_a, mu_prev=mu_prev, mu_next=mu_next,
             w0_f=w0_f, w2_f=w2_f, a0_f=a0_f, a2_f=a2_f, w0_b=w0_b, w2_b=w2_b, a0_b=a0_b, a2_b=a2_b,
             g2=g2, k_k=k_k, k_a=k_a, r_k=r_k, lnx_g=lnx_g, lnx_b=lnx_b, w_o_b=w_o_b,
             mem_norm_g=mem_norm_g, w_mkv=w_mkv, x_qn_g=x_qn_g, x_kn_g=x_kn_g, w_o_c=w_o_c, w_out=w_out,
             norm_ffn_g=norm_ffn_g, w_up=w_up, conv_w=conv_w, conv_b=conv_b, w_down=w_down)
    w = _prep_weights({name: a[0] for name, a in p.items()})
    return (_layer(x_prompt, mem_prompt, w), _layer(x_sample, mem_sample, w))
```

```python
import functools

import numpy as np
import jax
import jax.numpy as jnp
from jax import lax
from jax.experimental import pallas as pl
from jax.experimental.pallas import tpu as pltpu

F32 = jnp.float32
BF16 = jnp.bfloat16

D_MODEL = 1024
RMS_EPS = 1e-6
N_MEM = 256
MLA_HEADS = 8
MLA_NOPE = 64
MLA_ROPE = 32
MLA_QK = MLA_NOPE + MLA_ROPE
MLA_V = 64
Q_LORA = 384
KV_LORA = 256
ROPE_THETA = 10000.0
RW_HEADS = 8
RW_HEAD = 64
RW_DIM = RW_HEADS * RW_HEAD
LORA = 64
GATE_LORA = 128
RW_COLS = 3 * RW_DIM + 4 * LORA + GATE_LORA
LNX_EPS = 64e-5
X_HEADS = 4
X_HEAD = 128
X_DIM = X_HEADS * X_HEAD
D_FF = 2816

LANES = 128
SUBLANES = 8
CHUNK = 64
GROUP = 2
GW = GROUP * RW_HEAD
SCAN_CHUNKS = 4
VMEM_LIMIT = 56 * 1024 * 1024


def _cparams(*sem):
    return pltpu.CompilerParams(dimension_semantics=sem, vmem_limit_bytes=VMEM_LIMIT)


def _rms(x, g, eps=RMS_EPS):
    return x * lax.rsqrt(jnp.mean(x * x, axis=-1, keepdims=True) + eps) * g


def _dot(a, b):
    return jnp.dot(a, b, preferred_element_type=F32)


def _dot_nt(a, b):
    return lax.dot_general(a, b, (((1,), (1,)), ((), ())), preferred_element_type=F32)


def _dot_tn(a, b):
    return lax.dot_general(a, b, (((0,), (0,)), ((), ())), preferred_element_type=F32)


def _dot_0_1(x, e):
    return _dot(x.astype(BF16), e)


def _dot_0_1_l(e, x):
    hi = x.astype(BF16)
    lo = (x - hi.astype(F32)).astype(BF16)
    return _dot(e, hi) + _dot(e, lo)


def _sigmoid(z):
    return 1.0 / (1.0 + jnp.exp(-z))


def _full(shape):
    nd = len(shape)
    return pl.BlockSpec(shape, lambda *_: (0,) * nd, pipeline_mode=pl.Buffered(1))


def _mla_prep_kernel(x_ref, g_ref, wc_ref, gq_ref, gkv_ref, wuq_ref, wuqr_ref, wuk_ref, wuv_ref,
                     qa_ref, qb_ref, ka_ref, kba_ref, kbb_ref, q_out, k_out, v_out):
    h = _rms(x_ref[...], g_ref[...]).astype(BF16)
    c = _dot(h, wc_ref[...])
    cq = _rms(c[:, :Q_LORA], gq_ref[...]).astype(BF16)
    ckv = _rms(c[:, Q_LORA:Q_LORA + KV_LORA], gkv_ref[...]).astype(BF16)
    kr = c[:, Q_LORA + KV_LORA:]
    q = _dot(cq, wuq_ref[...])
    qp = _dot(cq, wuqr_ref[...])
    k = _dot(ckv, wuk_ref[...])
    vt = _dot_nt(wuv_ref[...], ckv)
    slot_row = lax.broadcasted_iota(jnp.int32, vt.shape, 0) % LANES
    v_out[0] = jnp.where(slot_row == MLA_V, 1.0, vt).astype(BF16)
    qa = qa_ref[...]
    qb = qb_ref[...]
    ka = ka_ref[...]
    half = MLA_ROPE // 2
    kr_roped = kr * ka + pltpu.roll(kr, LANES - half, 1) * kba_ref[...] + pltpu.roll(kr, half, 1) * kbb_ref[...]
    kr_ss = jnp.sum(kr * kr, axis=-1, keepdims=True)
    for hd in range(MLA_HEADS):
        sl = slice(LANES * hd, LANES * (hd + 1))
        t = q[:, sl]
        ss = jnp.sum(t * t, axis=-1, keepdims=True) * (1.0 / MLA_QK)
        q_out[:, sl] = ((t * qa + qp[:, sl] * qb) * lax.rsqrt(ss + RMS_EPS)).astype(BF16)
        t = k[:, sl]
        ss = (jnp.sum(t * t, axis=-1, keepdims=True) + kr_ss) * (1.0 / MLA_QK)
        k_out[:, sl] = ((t * ka + kr_roped) * lax.rsqrt(ss + RMS_EPS)).astype(BF16)


def _mla_prep(x2, T, w, tabs):
    ntok = x2.shape[0]
    tm = _attn_tile(T)
    npos = T // tm
    row = lambda i: (i, 0)
    pos = lambda i: (i % npos, 0)
    return pl.pallas_call(
        _mla_prep_kernel,
        grid=(ntok // tm,),
        in_specs=[pl.BlockSpec((tm, D_MODEL), row), _full((1, D_MODEL)), _full((D_MODEL, 768)),
                  _full((1, Q_LORA)), _full((1, KV_LORA)), _full((Q_LORA, 1024)), _full((Q_LORA, 1024)),
                  _full((KV_LORA, 1024)), _full((1024, KV_LORA))] + [pl.BlockSpec((tm, LANES), pos)] * 5,
        out_specs=[pl.BlockSpec((tm, 1024), row), pl.BlockSpec((tm, 1024), row),
                   pl.BlockSpec((1, 1024, tm), lambda i: (i, 0, 0))],
        out_shape=[jax.ShapeDtypeStruct((ntok, 1024), BF16), jax.ShapeDtypeStruct((ntok, 1024), BF16),
                   jax.ShapeDtypeStruct((ntok // tm, 1024, tm), BF16)],
        compiler_params=_cparams("parallel"),
        name="mla_prep",
    )(x2, w["norm_mix_g"], w["w_c"], w["q_norm_g"], w["kv_norm_g"], w["w_uq"], w["w_uq_rot"], w["w_uk"],
      w["w_uv"], *tabs)


def _flash_kernel(q_ref, k_ref, vt_ref, o_ref, s0_ref, s1_ref, *, tk, nk):
    tq = q_ref.shape[0]
    slots = [slice(LANES * h, LANES * (h + 1)) for h in range(2)]
    per = tk // vt_ref.shape[2]

    def scores(j, s_ref):
        kblk = k_ref[pl.ds(pl.multiple_of(j * tk, tk), tk), :]
        for h in range(2):
            s_ref[h] = _dot_nt(kblk[:, slots[h]], q_ref[:, slots[h]])

    def consume(j, s_ref, state):
        vt = jnp.concatenate([vt_ref[per * j + r] for r in range(per)], axis=1)
        m_news = [jnp.maximum(state[h][0], jnp.max(s_ref[h], axis=0, keepdims=True)) for h in range(2)]
        pts = [jnp.exp2((s_ref[h] - m_news[h]).astype(BF16)) for h in range(2)]
        return tuple((m_news[h], jnp.exp2(state[h][0] - m_news[h]) * state[h][1] + _dot(vt[slots[h], :], pts[h]))
                     for h in range(2))

    def body(t, state):
        j = 2 * t
        scores(j + 1, s1_ref)
        state = consume(j, s0_ref, state)
        scores(j + 2, s0_ref)
        return consume(j + 1, s1_ref, state)

    state = tuple((jnp.full((1, tq), -jnp.inf, F32), jnp.zeros((LANES, tq), F32)) for _ in range(2))
    scores(0, s0_ref)
    state = lax.fori_loop(0, nk // 2 - 1, body, state)
    scores(nk - 1, s1_ref)
    state = consume(nk - 2, s0_ref, state)
    state = consume(nk - 1, s1_ref, state)
    outs = [acc[0:MLA_V, :] / acc[MLA_V:MLA_V + 1, :] for _, acc in state]
    o_ref[...] = jnp.concatenate(outs, axis=0).T.astype(BF16)


def _attn_tile(T):
    return min(512, T // 4)


def _flash(q, k, vt, B, T):
    tq = _attn_tile(T)
    tk = min(1024, T // 4)
    tv = _attn_tile(T)
    nq = T // tq
    nk = T // tk
    assert nk % 2 == 0 and tk % tv == 0, (T, tk, tv)
    return pl.pallas_call(
        functools.partial(_flash_kernel, tk=tk, nk=nk),
        grid=(B, MLA_HEADS // 2, nq),
        in_specs=[pl.BlockSpec((tq, 2 * LANES), lambda b, hp, i: (b * nq + i, hp)),
                  pl.BlockSpec((T, 2 * LANES), lambda b, hp, i: (b, hp)),
                  pl.BlockSpec((T // tv, 2 * LANES, tv), lambda b, hp, i: (b, hp, 0))],
        out_specs=pl.BlockSpec((tq, LANES), lambda b, hp, i: (b * nq + i, hp)),
        out_shape=jax.ShapeDtypeStruct((B * T, MLA_HEADS * MLA_V), BF16),
        scratch_shapes=[pltpu.VMEM((2, tk, tq), F32), pltpu.VMEM((2, tk, tq), F32)],
        compiler_params=_cparams("parallel", "parallel", "arbitrary"),
        name="mla_flash",
    )(q, k, vt)


def _rw_prep_kernel(x_ref, xp_ref, xn_ref, g_ref, wrw_ref, mup_ref, mun_ref, kk_ref, ka_ref, rk_ref,
                    w0_ref, w2_ref, a0_ref, a2_ref, g2_ref, seg_ref, tri_ref,
                    v_out, kkt_out, rt_out, kh_out, bh_out, kw_out, bw_out, wt_out, bonus_out, g7_out,
                    h_scr, rw_scr, *, tm, npos):
    i = pl.program_id(0)
    not_first = (i % npos != 0).astype(F32)
    not_last = (i % npos != npos - 1).astype(F32)
    g = g_ref[...]
    h_scr[0:SUBLANES, :] = (_rms(xp_ref[...], g) * not_first).astype(BF16)
    h_scr[SUBLANES:SUBLANES + tm, :] = _rms(x_ref[...], g).astype(BF16)
    h_scr[SUBLANES + tm:, :] = (_rms(xn_ref[...], g) * not_last).astype(BF16)
    rw_scr[...] = _dot(h_scr[...], wrw_ref[...])
    cur = rw_scr[SUBLANES:SUBLANES + tm, :]
    prev = rw_scr[SUBLANES - 1:SUBLANES - 1 + tm, :]
    nxt = rw_scr[SUBLANES + 1:SUBLANES + 1 + tm, :]
    mup = mup_ref[...]
    mun = mun_ref[...]
    rwf = cur * (1.0 - mup - mun) + mup * prev + mun * nxt
    r7 = rwf[:, 0:RW_DIM]
    k7 = rwf[:, RW_DIM:2 * RW_DIM]
    v7 = rwf[:, 2 * RW_DIM:3 * RW_DIM]
    wl = rwf[:, 3 * RW_DIM:3 * RW_DIM + 2 * LORA]
    al = rwf[:, 3 * RW_DIM + 2 * LORA:3 * RW_DIM + 4 * LORA]
    gl = rwf[:, 3 * RW_DIM + 4 * LORA:]
    seg = seg_ref[...]
    kx = k7 * kk_ref[...]
    kkn = kx * lax.rsqrt(jnp.maximum(_dot_0_1(kx * kx, seg), 1e-24))
    v_out[...] = v7.astype(BF16)
    g7_out[...] = _dot(_sigmoid(gl).astype(BF16), g2_ref[...])
    tw = jnp.tanh(wl).astype(BF16)
    alb = al.astype(BF16)
    bonus = jnp.zeros((tm, RW_DIM), F32)
    for d in range(2):
        z = -(w0_ref[d] + _dot(tw, w2_ref[d]))
        softplus = jnp.maximum(z, 0.0) + jnp.log(1.0 + jnp.exp(-jnp.abs(z)))
        lw = -jnp.exp(-softplus - 0.5)
        a = _sigmoid(a0_ref[d] + _dot(alb, a2_ref[d]))
        kd = k7 * (1.0 + (a - 1.0) * ka_ref[...])
        b = kkn * a
        bonus = bonus + _dot_0_1(r7 * kd * rk_ref[...], seg) * v7
        cum = _dot_0_1_l(tri_ref[d], lw)
        last = CHUNK - 1 if d == 0 else 0
        tot_rows = [cum[CHUNK * ci + last:CHUNK * ci + last + 1, :] for ci in range(tm // CHUNK)]
        tot = jnp.concatenate([jnp.broadcast_to(r, (CHUNK, RW_DIM)) for r in tot_rows], axis=0)
        w_incl = jnp.exp(cum)
        w_excl = jnp.exp(cum - lw)
        w_inv = jnp.exp(-cum)
        w_rest = jnp.exp(tot - cum)
        kkt_out[d] = (kkn * w_excl).astype(BF16)
        rt_out[d] = (r7 * w_incl).astype(BF16)
        kh_out[d] = (kd * w_inv).astype(BF16)
        bh_out[d] = (b * w_inv).astype(BF16)
        kw_out[d] = (kd * w_rest).astype(BF16)
        bw_out[d] = (b * w_rest).astype(BF16)
        for ci in range(tm // CHUNK):
            wt_out[d, ci] = jnp.exp(tot_rows[ci])
    bonus_out[...] = bonus


def _rw_prep(x2, T, w):
    ntok = x2.shape[0]
    tm = min(256, T)
    npos = T // tm
    nblk8 = ntok // SUBLANES
    r8 = tm // SUBLANES
    row = lambda i: (i, 0)
    drow = lambda i: (0, i, 0)
    dspec = pl.BlockSpec((2, tm, RW_DIM), drow)
    dshape = jax.ShapeDtypeStruct((2, ntok, RW_DIM), BF16)
    tri = _chunk_masks(tm)
    outs = pl.pallas_call(
        functools.partial(_rw_prep_kernel, tm=tm, npos=npos),
        grid=(ntok // tm,),
        in_specs=[pl.BlockSpec((tm, D_MODEL), row),
                  pl.BlockSpec((SUBLANES, D_MODEL), lambda i: (jnp.maximum(i * r8 - 1, 0), 0)),
                  pl.BlockSpec((SUBLANES, D_MODEL), lambda i: (jnp.minimum((i + 1) * r8, nblk8 - 1), 0)),
                  _full((1, D_MODEL)), _full((D_MODEL, RW_COLS)), _full((1, RW_COLS)), _full((1, RW_COLS)),
                  _full((1, RW_DIM)), _full((1, RW_DIM)), _full((1, RW_DIM)),
                  _full((2, 1, RW_DIM)), _full((2, 2 * LORA, RW_DIM)), _full((2, 1, RW_DIM)),
                  _full((2, 2 * LORA, RW_DIM)), _full((GATE_LORA, RW_DIM)), _full((RW_DIM, RW_DIM)),
                  _full((2, tm, tm))],
        out_specs=[pl.BlockSpec((tm, RW_DIM), row), dspec, dspec, dspec, dspec, dspec, dspec,
                   pl.BlockSpec((2, tm // CHUNK, 1, RW_DIM), lambda i: (0, i, 0, 0)),
                   pl.BlockSpec((tm, RW_DIM), row), pl.BlockSpec((tm, RW_DIM), row)],
        out_shape=[jax.ShapeDtypeStruct((ntok, RW_DIM), BF16), dshape, dshape, dshape, dshape, dshape, dshape,
                   jax.ShapeDtypeStruct((2, ntok // CHUNK, 1, RW_DIM), F32),
                   jax.ShapeDtypeStruct((ntok, RW_DIM), F32), jax.ShapeDtypeStruct((ntok, RW_DIM), F32)],
        scratch_shapes=[pltpu.VMEM((tm + 2 * SUBLANES, D_MODEL), BF16), pltpu.VMEM((tm + 2 * SUBLANES, RW_COLS), F32)],
        compiler_params=_cparams("parallel"),
        name="rw_prep",
    )(x2, x2, x2, w["norm_mix_g"], w["w_rw"], w["mu_prev"], w["mu_next"], w["k_k"], w["k_a"], w["r_k"],
      w["w0"], w["w2"], w["a0"], w["a2"], w["g2"], w["seg"], tri)
    return outs


def _chunk_masks(tm):
    t = np.arange(tm)
    same = (t[:, None] // CHUNK) == (t[None, :] // CHUNK)
    fwd = same & (t[None, :] <= t[:, None])
    bwd = same & (t[None, :] >= t[:, None])
    return jnp.asarray(np.stack([fwd, bwd]).astype(np.float32), BF16)


def _rw_scan_kernel(*refs):
    ins = refs[:16]
    yf_ref, yb_ref, s_ref = refs[16:]
    c = pl.program_id(1)

    @pl.when(c == 0)
    def _():
        s_ref[...] = jnp.zeros(s_ref.shape, F32)

    t_pos = lax.broadcasted_iota(jnp.int32, (CHUNK, GW), 0)
    s_pos = lax.broadcasted_iota(jnp.int32, (CHUNK, GW), 1) % CHUNK
    eye_c = (t_pos == s_pos).astype(F32)
    ri = lax.broadcasted_iota(jnp.int32, (GW, GW), 0)
    ci = lax.broadcasted_iota(jnp.int32, (GW, GW), 1)
    head_mask = (ri // CHUNK) == (ci // RW_HEAD)
    stack = lambda a: jnp.where(head_mask, jnp.concatenate([a] * GROUP, axis=0), jnp.zeros((), BF16))

    def prepare(orders):
        items = []
        for d, gi, order in ((d, gi, o) for d in range(2) for gi in range(RW_HEADS // GROUP) for o in orders):
            kkt, rt, kh, bh, kw, bw, v, wt = ins[8 * d:8 * d + 8]
            strict = (s_pos < t_pos) if d == 0 else (s_pos > t_pos)
            incl = (s_pos <= t_pos) if d == 0 else (s_pos >= t_pos)
            sub = order if d == 0 else SCAN_CHUNKS - 1 - order
            rows = slice(CHUNK * sub, CHUNK * (sub + 1))
            sl = slice(GW * gi, GW * (gi + 1))
            it = dict(y_ref=(yf_ref, yb_ref)[d], sl=sl, rows=rows, s_ref=s_ref.at[d, gi], wtot=wt[sub][:, sl],
                      order=order)
            kkt_c, rt_c, v_c = kkt[rows, sl], rt[rows, sl], v[rows, sl]
            a_all = _dot_nt(jnp.concatenate([kkt_c, rt_c], axis=0),
                            jnp.concatenate([stack(kh[rows, sl]), stack(bh[rows, sl])], axis=0))
            it.update(kkt=kkt_c, rt=rt_c, v=v_c, vs=stack(v_c),
                      kbw=jnp.concatenate([kw[rows, sl], bw[rows, sl]], axis=0),
                      ak=jnp.where(strict, a_all[:CHUNK, :GW], 0.0).astype(BF16),
                      bk=jnp.where(incl, a_all[CHUNK:, :GW], 0.0).astype(BF16),
                      bb=jnp.where(incl, a_all[CHUNK:, GW:], 0.0).astype(BF16))
            ab = jnp.where(strict, a_all[:CHUNK, GW:], 0.0)
            it.update(tinv=eye_c - ab, pw=ab.astype(BF16))
            items.append(it)
        for it in items:
            it["pw"] = _dot(it["pw"], stack(it["pw"])).astype(BF16)
        for k in range(5):
            for it in items:
                sk = stack(it["pw"])
                if k < 4:
                    res = _dot(jnp.concatenate([it["pw"], it["tinv"].astype(BF16)], axis=0), sk)
                    it["pw"] = res[:CHUNK].astype(BF16)
                    it["tinv"] = it["tinv"] + res[CHUNK:]
                else:
                    it["tinv"] = (it["tinv"] + _dot(it["tinv"].astype(BF16), sk)).astype(BF16)
        return items

    def advance(chains):
        for ch in chains:
            ch["sb"] = ch["s_ref"][...].astype(BF16)
            ch["rhs"] = (_dot_nt(ch["kkt"], ch["sb"]) + _dot(ch["ak"], ch["vs"])).astype(BF16)
        for ch in chains:
            ch["u"] = (-_dot(ch["tinv"], stack(ch["rhs"]))).astype(BF16)
        for ch in chains:
            ch["y_ref"][ch["rows"], ch["sl"]] = (_dot_nt(ch["rt"], ch["sb"]) + _dot(ch["bk"], ch["vs"])
                                                 + _dot(ch["bb"], stack(ch["u"])))
        for ch in chains:
            upd = _dot_tn(jnp.concatenate([ch["v"], ch["u"]], axis=0), ch["kbw"])
            ch["s_ref"][...] = ch["s_ref"][...] * ch["wtot"] + jnp.where(head_mask, upd, 0.0)

    prepared = prepare(range(SCAN_CHUNKS))
    for order in range(SCAN_CHUNKS):
        advance([it for it in prepared if it["order"] == order])


def _rw_scan(v, kkt, rt, kh, bh, kw, bw, wt, B, T):
    ntok = B * T
    rows = SCAN_CHUNKS * CHUNK
    nb = T // rows
    in_specs = []
    args = []
    for d in range(2):
        if d == 0:
            cm = lambda b, c: b * nb + c
        else:
            cm = lambda b, c: b * nb + (nb - 1 - c)
        for arr in (kkt, rt, kh, bh, kw, bw):
            in_specs.append(pl.BlockSpec((None, rows, RW_DIM), lambda b, c, cm=cm, d=d: (d, cm(b, c), 0)))
            args.append(arr)
        in_specs.append(pl.BlockSpec((rows, RW_DIM), lambda b, c, cm=cm: (cm(b, c), 0)))
        args.append(v)
        in_specs.append(pl.BlockSpec((None, SCAN_CHUNKS, 1, RW_DIM), lambda b, c, cm=cm, d=d: (d, cm(b, c), 0, 0)))
        args.append(wt)
    return pl.pallas_call(
        _rw_scan_kernel,
        grid=(B, nb),
        in_specs=in_specs,
        out_specs=[pl.BlockSpec((rows, RW_DIM), lambda b, c: (b * nb + c, 0)),
                   pl.BlockSpec((rows, RW_DIM), lambda b, c: (b * nb + (nb - 1 - c), 0))],
        out_shape=[jax.ShapeDtypeStruct((ntok, RW_DIM), F32), jax.ShapeDtypeStruct((ntok, RW_DIM), F32)],
        scratch_shapes=[pltpu.VMEM((2, RW_HEADS // GROUP, GW, GW), F32)],
        compiler_params=_cparams("parallel", "arbitrary"),
        name="rw_scan",
    )(*args)


def _mem_kv_kernel(m_ref, g_ref, wk_ref, wv_ref, gk_ref, k_out, v_out):
    m = _rms(m_ref[...], g_ref[...]).astype(BF16)
    k = _dot(m, wk_ref[...])
    v_out[...] = _dot(m, wv_ref[...]).astype(BF16)
    for hd in range(X_HEADS):
        sl = slice(X_HEAD * hd, X_HEAD * (hd + 1))
        k_out[:, sl] = _rms(k[:, sl], gk_ref[...]).astype(BF16)


def _mem_kv(mem2, w):
    n = mem2.shape[0]
    row = lambda i: (i, 0)
    return pl.pallas_call(
        _mem_kv_kernel,
        grid=(n // N_MEM,),
        in_specs=[pl.BlockSpec((N_MEM, D_MODEL), row), _full((1, D_MODEL)), _full((D_MODEL, X_DIM)),
                  _full((D_MODEL, X_DIM)), _full((1, X_HEAD))],
        out_specs=[pl.BlockSpec((N_MEM, X_DIM), row), pl.BlockSpec((N_MEM, X_DIM), row)],
        out_shape=[jax.ShapeDtypeStruct((n, X_DIM), BF16), jax.ShapeDtypeStruct((n, X_DIM), BF16)],
        compiler_params=_cparams("parallel"),
        name="mem_kv",
    )(mem2, w["mem_norm_g"], w["w_mk"], w["w_mv"], w["x_kn_g"])


def _xattn_kernel(x_ref, g_ref, wq_ref, gq_ref, mk_ref, mv_ref, o_ref):
    h = _rms(x_ref[...], g_ref[...]).astype(BF16)
    q = _dot(h, wq_ref[...])
    for hd in range(X_HEADS):
        sl = slice(X_HEAD * hd, X_HEAD * (hd + 1))
        qh = _rms(q[:, sl], gq_ref[...]).astype(BF16)
        s = _dot_nt(qh, mk_ref[:, sl])
        p = jnp.exp(s - jnp.max(s, axis=-1, keepdims=True))
        o = _dot(p.astype(BF16), mv_ref[:, sl]) / jnp.sum(p, axis=-1, keepdims=True)
        o_ref[:, sl] = o.astype(BF16)


def _xattn(x2, mk, mv, B, T, w):
    tq = min(512, T)
    nq = T // tq
    return pl.pallas_call(
        _xattn_kernel,
        grid=(B, nq),
        in_specs=[pl.BlockSpec((tq, D_MODEL), lambda b, i: (b * nq + i, 0)), _full((1, D_MODEL)),
                  _full((D_MODEL, X_DIM)), _full((1, X_HEAD)),
                  pl.BlockSpec((N_MEM, X_DIM), lambda b, i: (b, 0)), pl.BlockSpec((N_MEM, X_DIM), lambda b, i: (b, 0))],
        out_specs=pl.BlockSpec((tq, X_DIM), lambda b, i: (b * nq + i, 0)),
        out_shape=jax.ShapeDtypeStruct((B * T, X_DIM), BF16),
        compiler_params=_cparams("parallel", "parallel"),
        name="xattn",
    )(x2, w["norm_mix_g"], w["w_xq"], w["x_qn_g_scaled"], mk, mv)


def _merge_kernel(x_ref, oa_ref, yf_ref, yb_ref, bonus_ref, g7_ref, oc_ref, g_ref, wg_ref, woa_ref, wob_ref,
                  woc_ref, wout_ref, lng_ref, lnb_ref, seg_ref, o_ref):
    x = x_ref[...]
    h = _rms(x, g_ref[...]).astype(BF16)
    seg = seg_ref[...]
    y7 = yf_ref[...] + yb_ref[...] + bonus_ref[...]
    mu = _dot_0_1(y7, seg) * (1.0 / RW_HEAD)
    dy = y7 - mu
    var = _dot_0_1(dy * dy, seg) * (1.0 / RW_HEAD)
    y7 = dy * lax.rsqrt(var + LNX_EPS) * lng_ref[...] + lnb_ref[...]
    yb = (y7 * g7_ref[...]).astype(BF16)
    merged = jnp.zeros(x.shape, F32)
    for bi, (br, wo) in enumerate(((oa_ref[...], woa_ref), (yb, wob_ref), (oc_ref[...], woc_ref))):
        gate = _sigmoid(_dot(h, wg_ref[:, D_MODEL * bi:D_MODEL * (bi + 1)]))
        merged = merged + gate * _dot(br, wo[...])
    o_ref[...] = x + _dot(merged.astype(BF16), wout_ref[...])


def _merge(x2, oa, yf, yb, bonus, g7, oc, T, w):
    ntok = x2.shape[0]
    tm = min(512, T)
    row = lambda i: (i, 0)
    half = lambda: pl.BlockSpec((tm, RW_DIM), row)
    return pl.pallas_call(
        _merge_kernel,
        grid=(ntok // tm,),
        in_specs=[pl.BlockSpec((tm, D_MODEL), row), half(), half(), half(), half(), half(), half(),
                  _full((1, D_MODEL)), _full((D_MODEL, 3 * D_MODEL)), _full((RW_DIM, D_MODEL)),
                  _full((RW_DIM, D_MODEL)), _full((RW_DIM, D_MODEL)), _full((D_MODEL, D_MODEL)),
                  _full((1, RW_DIM)), _full((1, RW_DIM)), _full((RW_DIM, RW_DIM))],
        out_specs=pl.BlockSpec((tm, D_MODEL), row),
        out_shape=jax.ShapeDtypeStruct((ntok, D_MODEL), F32),
        compiler_params=_cparams("parallel"),
        name="merge",
    )(x2, oa, yf, yb, bonus, g7, oc, w["norm_mix_g"], w["w_gate"], w["w_o_a"], w["w_o_b"], w["w_o_c"],
      w["w_out"], w["lnx_g"], w["lnx_b"], w["seg"])


MXU_TILE = 256
FFN_SPLITS = (0, 5 * MXU_TILE, D_FF)


def _ffn_kernel(x_ref, xp_ref, xn_ref, g_ref, wug_ref, wuv_ref, cw_ref, cb_ref, wd_ref, o_ref, h_scr, ug_scr,
                *, tm, npos):
    i = pl.program_id(0)
    not_first = (i % npos != 0).astype(F32)
    not_last = (i % npos != npos - 1).astype(F32)
    g = g_ref[...]
    h_scr[0:SUBLANES, :] = (_rms(xp_ref[...], g) * not_first).astype(BF16)
    h_scr[SUBLANES:SUBLANES + tm, :] = _rms(x_ref[...], g).astype(BF16)
    h_scr[SUBLANES + tm:, :] = (_rms(xn_ref[...], g) * not_last).astype(BF16)
    out = x_ref[...]
    for lo, hi in zip(FFN_SPLITS[:-1], FFN_SPLITS[1:]):
        n = hi - lo
        ug_scr[:, 0:n] = _dot(h_scr[...], wug_ref[:, lo:hi])
        uv = _dot(h_scr[SUBLANES:SUBLANES + tm, :], wuv_ref[:, lo:hi])
        cw = cw_ref[:, lo:hi]
        c = (cw[0:1] * ug_scr[SUBLANES - 1:SUBLANES - 1 + tm, 0:n] + cw[1:2] * ug_scr[SUBLANES:SUBLANES + tm, 0:n]
             + cw[2:3] * ug_scr[SUBLANES + 1:SUBLANES + 1 + tm, 0:n] + cb_ref[:, lo:hi])
        act = 0.5 * c * (1.0 + lax.erf(c * np.float32(1.0 / np.sqrt(2.0)))) * uv
        out = out + _dot(act.astype(BF16), wd_ref[lo:hi, :])
    o_ref[...] = out


def _ffn(x2, T, w):
    ntok = x2.shape[0]
    tm = min(512, T)
    fc = max(hi - lo for lo, hi in zip(FFN_SPLITS[:-1], FFN_SPLITS[1:]))
    npos = T // tm
    nblk8 = ntok // SUBLANES
    r8 = tm // SUBLANES
    row = lambda i: (i, 0)
    return pl.pallas_call(
        functools.partial(_ffn_kernel, tm=tm, npos=npos),
        grid=(ntok // tm,),
        in_specs=[pl.BlockSpec((tm, D_MODEL), row),
                  pl.BlockSpec((SUBLANES, D_MODEL), lambda i: (jnp.maximum(i * r8 - 1, 0), 0)),
                  pl.BlockSpec((SUBLANES, D_MODEL), lambda i: (jnp.minimum((i + 1) * r8, nblk8 - 1), 0)),
                  _full((1, D_MODEL)), _full((D_MODEL, D_FF)), _full((D_MODEL, D_FF)), _full((3, D_FF)),
                  _full((1, D_FF)), _full((D_FF, D_MODEL))],
        out_specs=pl.BlockSpec((tm, D_MODEL), row),
        out_shape=jax.ShapeDtypeStruct((ntok, D_MODEL), F32),
        scratch_shapes=[pltpu.VMEM((tm + 2 * SUBLANES, D_MODEL), BF16), pltpu.VMEM((tm + 2 * SUBLANES, fc), F32)],
        compiler_params=_cparams("parallel"),
        name="conv_ffn",
    )(x2, x2, x2, w["norm_ffn_g"], w["w_up_gate"], w["w_up_val"], w["conv_w"], w["conv_b"], w["w_down"])


def _prep_weights(p):
    w = {}
    row = lambda a: a.reshape(1, -1).astype(F32)
    w_in = p["w_in"]
    o = np.cumsum([0, Q_LORA, KV_LORA, MLA_ROPE, RW_COLS, X_DIM, 3 * D_MODEL])
    seg = lambda i: w_in[:, o[i]:o[i + 1]]
    zc = lambda n: jnp.zeros((D_MODEL, n), F32)
    w["w_c"] = jnp.concatenate([seg(0), seg(1), zc(MLA_NOPE), seg(2), zc(LANES - MLA_QK)], axis=1).astype(BF16)
    w["w_rw"] = seg(3).astype(BF16)
    w["w_xq"] = seg(4).astype(BF16)
    w["w_gate"] = seg(5).astype(BF16)
    w["norm_mix_g"] = row(p["norm_mix_g"])
    w["q_norm_g"] = row(p["q_norm_g"])
    w["kv_norm_g"] = row(p["kv_norm_g"])
    pad_slot = lambda a: jnp.pad(a, ((0, 0), (0, 0), (0, LANES - a.shape[-1]))).reshape(a.shape[0], -1)
    uq = p["w_uq"].reshape(Q_LORA, MLA_HEADS, MLA_QK)
    w["w_uq"] = pad_slot(uq).astype(BF16)
    half = MLA_ROPE // 2
    uq_rot = jnp.concatenate([jnp.zeros_like(uq[:, :, :MLA_NOPE]), -uq[:, :, MLA_NOPE + half:],
                              uq[:, :, MLA_NOPE:MLA_NOPE + half]], axis=-1)
    w["w_uq_rot"] = pad_slot(uq_rot).astype(BF16)
    ukv =p["w_ukv"].reshape(KV_LORA, MLA_HEADS, MLA_NOPE + MLA_V)
    w["w_uk"] = pad_slot(ukv[:, :, :MLA_NOPE]).astype(BF16)
    w["w_uv"] = pad_slot(ukv[:, :, MLA_NOPE:]).T.astype(BF16)
    w["mla_qn_g_scaled"] = p["mla_qn_g"].reshape(-1) * np.float32(MLA_QK ** -0.5 * np.log2(np.e))
    w["mla_kn_g"] = p["mla_kn_g"].reshape(-1)
    for name in ("mu_prev", "mu_next", "k_k", "k_a", "r_k", "lnx_g", "lnx_b", "mem_norm_g", "x_kn_g", "norm_ffn_g",
                 "conv_b"):
        w[name] = row(p[name])
    w["w0"] = jnp.stack([p["w0_f"], p["w0_b"]]).reshape(2, 1, RW_DIM)
    w["a0"] = jnp.stack([p["a0_f"], p["a0_b"]]).reshape(2, 1, RW_DIM)
    zl = jnp.zeros((LORA, RW_DIM), F32)
    w["w2"] = jnp.stack([jnp.concatenate([p["w2_f"], zl]), jnp.concatenate([zl, p["w2_b"]])]).astype(BF16)
    w["a2"] = jnp.stack([jnp.concatenate([p["a2_f"], zl]), jnp.concatenate([zl, p["a2_b"]])]).astype(BF16)
    w["g2"] = p["g2"].astype(BF16)
    hid = np.arange(RW_DIM) // RW_HEAD
    w["seg"] = jnp.asarray((hid[:, None] == hid[None, :]).astype(np.float32), BF16)
    mkv = p["w_mkv"].reshape(D_MODEL, X_HEADS, 2 * X_HEAD)
    w["w_mk"] = mkv[:, :, :X_HEAD].reshape(D_MODEL, X_DIM).astype(BF16)
    w["w_mv"] = mkv[:, :, X_HEAD:].reshape(D_MODEL, X_DIM).astype(BF16)
    w["x_qn_g_scaled"] = row(p["x_qn_g"]) * np.float32(X_HEAD ** -0.5)
    for name in ("w_o_a", "w_o_b", "w_o_c", "w_out", "w_down"):
        w[name] = p[name].astype(BF16)
    w["w_up_gate"] = p["w_up"][:, :D_FF].astype(BF16)
    w["w_up_val"] = p["w_up"][:, D_FF:].astype(BF16)
    w["conv_w"] = p["conv_w"].astype(F32)
    return w


def _rope_tables(T, gq, gk):
    half = MLA_ROPE // 2
    inv = jnp.power(ROPE_THETA, -jnp.arange(half, dtype=F32) / half)
    ang = jnp.arange(T, dtype=F32)[:, None] * inv[None, :]
    cos, sin = jnp.cos(ang), jnp.sin(ang)
    z = lambda n: jnp.zeros((T, n), F32)
    pad = z(LANES - MLA_QK)

    def own(g):
        g1, g2 = g[MLA_NOPE:MLA_NOPE + half], g[MLA_NOPE + half:]
        return jnp.concatenate([jnp.broadcast_to(g[:MLA_NOPE], (T, MLA_NOPE)), cos * g1, cos * g2, pad], axis=1)

    g1, g2 = gq[MLA_NOPE:MLA_NOPE + half], gq[MLA_NOPE + half:]
    qb = jnp.concatenate([z(MLA_NOPE), sin * g2, sin * g1, pad], axis=1)
    g1, g2 = gk[MLA_NOPE:MLA_NOPE + half], gk[MLA_NOPE + half:]
    kba = jnp.concatenate([z(MLA_NOPE), -sin * g2, z(half), pad], axis=1)
    kbb = jnp.concatenate([z(MLA_NOPE), z(half), sin * g1, pad], axis=1)
    return own(gq), qb, own(gk), kba, kbb


def _layer(x, mem, w):
    B, T, _ = x.shape
    x2 = x.reshape(B * T, D_MODEL)
    q, k, v = _mla_prep(x2, T, w, _rope_tables(T, w["mla_qn_g_scaled"], w["mla_kn_g"]))
    o_a = _flash(q, k, v, B, T)
    v7, kkt, rt, kh, bh, kw, bw, wt, bonus, g7 = _rw_prep(x2, T, w)
    y_f, y_b = _rw_scan(v7, kkt, rt, kh, bh, kw, bw, wt, B, T)
    mk, mv = _mem_kv(mem.reshape(B * N_MEM, D_MODEL), w)
    o_c = _xattn(x2, mk, mv, B, T, w)
    x1 = _merge(x2, o_a, y_f, y_b, bonus, g7, o_c, T, w)
    return _ffn(x1, T, w).reshape(B, T, D_MODEL)


def kernel(x_prompt, x_sample, mem_prompt, mem_sample, norm_mix_g, w_in, q_norm_g, w_uq, kv_norm_g, w_ukv, mla_qn_g, mla_kn_g, w_o_a, mu_prev, mu_next, w0_f, w2_f, a0_f, a2_f, w0_b, w2_b, a0_b, a2_b, g2, k_k, k_a, r_k, lnx_g, lnx_b, w_o_b, mem_norm_g, w_mkv, x_qn_g, x_kn_g, w_o_c, w_out, norm_ffn_g, w_up, conv_w, conv_b, w_down):
    p = dict(norm_mix_g=norm_mix_g, w_in=w_in, q_norm_g=q_norm_g, w_uq=w_uq, kv_norm_g=kv_norm_g, w_ukv=w_ukv,
             mla_qn_g=mla_qn_g, mla_kn_g=mla_kn_g, w_o_a=w_o_a, mu_prev=mu_prev, mu_next=mu_next,
             w0_f=w0_f, w2_f=w2_f, a0_f=a0_f, a2_f=a2_f, w0_b=w0_b, w2_b=w2_b, a0_b=a0_b, a2_b=a2_b,
             g2=g2, k_k=k_k, k_a=k_a, r_k=r_k, lnx_g=lnx_g, lnx_b=lnx_b, w_o_b=w_o_b,
             mem_norm_g=mem_norm_g, w_mkv=w_mkv, x_qn_g=x_qn_g, x_kn_g=x_kn_g, w_o_c=w_o_c, w_out=w_out,
             norm_ffn_g=norm_ffn_g, w_up=w_up, conv_w=conv_w, conv_b=conv_b, w_down=w_down)
    w = _prep_weights({name: a[0] for name, a in p.items()})
    return (_layer(x_prompt, mem_prompt, w), _layer(x_sample, mem_sample, w))
```

```python
import functools

import numpy as np
import jax
import jax.numpy as jnp
from jax import lax
from jax.experimental import pallas as pl
from jax.experimental.pallas import tpu as pltpu

F32 = jnp.float32
BF16 = jnp.bfloat16

D_MODEL = 1024
RMS_EPS = 1e-6
N_MEM = 256
MLA_HEADS = 8
MLA_NOPE = 64
MLA_ROPE = 32
MLA_QK = MLA_NOPE + MLA_ROPE
MLA_V = 64
Q_LORA = 384
KV_LORA = 256
ROPE_THETA = 10000.0
RW_HEADS = 8
RW_HEAD = 64
RW_DIM = RW_HEADS * RW_HEAD
LORA = 64
GATE_LORA = 128
RW_COLS = 3 * RW_DIM + 4 * LORA + GATE_LORA
LNX_EPS = 64e-5
X_HEADS = 4
X_HEAD = 128
X_DIM = X_HEADS * X_HEAD
D_FF = 2816

LANES = 128
SUBLANES = 8
CHUNK = 64
GROUP = 2
GW = GROUP * RW_HEAD
SLOT_COLS = MLA_HEADS * LANES
MLA_C_COLS = Q_LORA + KV_LORA + LANES
BF16_ROWS = 16
V_ROWS = -(-(MLA_V + 1) // BF16_ROWS) * BF16_ROWS
SCAN_CHUNKS = 4
VMEM_LIMIT = 56 * 1024 * 1024


def _cparams(*sem):
    return pltpu.CompilerParams(dimension_semantics=sem, vmem_limit_bytes=VMEM_LIMIT)


def _rms(x, g, eps=RMS_EPS):
    return x * lax.rsqrt(jnp.mean(x * x, axis=-1, keepdims=True) + eps) * g


def _dot(a, b):
    return jnp.dot(a, b, preferred_element_type=F32)


def _dot_nt(a, b):
    return lax.dot_general(a, b, (((1,), (1,)), ((), ())), preferred_element_type=F32)


def _dot_tn(a, b):
    return lax.dot_general(a, b, (((0,), (0,)), ((), ())), preferred_element_type=F32)


def _dot_0_1(x, e):
    return _dot(x.astype(BF16), e)


def _dot_0_1_l(e, x):
    hi = x.astype(BF16)
    lo = (x - hi.astype(F32)).astype(BF16)
    return _dot(e, hi) + _dot(e, lo)


def _sigmoid(z):
    return 1.0 / (1.0 + jnp.exp(-z))


def _full(shape):
    nd = len(shape)
    return pl.BlockSpec(shape, lambda *_: (0,) * nd, pipeline_mode=pl.Buffered(1))


def _mla_prep_kernel(x_ref, g_ref, wc_ref, gq_ref, gkv_ref, wuq_ref, wuqr_ref, wuk_ref, wuv_ref,
                     qa_ref, qb_ref, ka_ref, kba_ref, kbb_ref, q_out, k_out, v_out):
    h = _rms(x_ref[...], g_ref[...]).astype(BF16)
    c = _dot(h, wc_ref[...])
    cq = _rms(c[:, :Q_LORA], gq_ref[...]).astype(BF16)
    ckv = _rms(c[:, Q_LORA:Q_LORA + KV_LORA], gkv_ref[...]).astype(BF16)
    kr = c[:, Q_LORA + KV_LORA:]
    q = _dot(cq, wuq_ref[...])
    qp = _dot(cq, wuqr_ref[...])
    k = _dot(ckv, wuk_ref[...])
    vt = _dot_nt(wuv_ref[...], ckv)
    slot_row = lax.broadcasted_iota(jnp.int32, vt.shape, 0) % LANES
    v_out[0] = jnp.where(slot_row == MLA_V, 1.0, vt).astype(BF16)
    qa = qa_ref[...]
    qb = qb_ref[...]
    ka = ka_ref[...]
    half = MLA_ROPE // 2
    kr_roped = kr * ka + pltpu.roll(kr, LANES - half, 1) * kba_ref[...] + pltpu.roll(kr, half, 1) * kbb_ref[...]
    kr_ss = jnp.sum(kr * kr, axis=-1, keepdims=True)
    for hd in range(MLA_HEADS):
        sl = slice(LANES * hd, LANES * (hd + 1))
        t = q[:, sl]
        ss = jnp.sum(t * t, axis=-1, keepdims=True) * (1.0 / MLA_QK)
        q_out[:, sl] = ((t * qa + qp[:, sl] * qb) * lax.rsqrt(ss + RMS_EPS)).astype(BF16)
        t = k[:, sl]
        ss = (jnp.sum(t * t, axis=-1, keepdims=True) + kr_ss) * (1.0 / MLA_QK)
        k_out[:, sl] = ((t * ka + kr_roped) * lax.rsqrt(ss + RMS_EPS)).astype(BF16)


def _mla_prep(x2, T, w, tabs):
    ntok = x2.shape[0]
    tm = _attn_tile(T)
    npos = T // tm
    row = lambda i: (i, 0)
    pos = lambda i: (i % npos, 0)
    return pl.pallas_call(
        _mla_prep_kernel,
        grid=(ntok // tm,),
        in_specs=[pl.BlockSpec((tm, D_MODEL), row), _full((1, D_MODEL)), _full((D_MODEL, MLA_C_COLS)),
                  _full((1, Q_LORA)), _full((1, KV_LORA)), _full((Q_LORA, SLOT_COLS)), _full((Q_LORA, SLOT_COLS)),
                  _full((KV_LORA, SLOT_COLS)), _full((SLOT_COLS, KV_LORA))] + [pl.BlockSpec((tm, LANES), pos)] * 5,
        out_specs=[pl.BlockSpec((tm, SLOT_COLS), row), pl.BlockSpec((tm, SLOT_COLS), row),
                   pl.BlockSpec((1, SLOT_COLS, tm), lambda i: (i, 0, 0))],
        out_shape=[jax.ShapeDtypeStruct((ntok, SLOT_COLS), BF16), jax.ShapeDtypeStruct((ntok, SLOT_COLS), BF16),
                   jax.ShapeDtypeStruct((ntok // tm, SLOT_COLS, tm), BF16)],
        compiler_params=_cparams("parallel"),
        name="mla_prep",
    )(x2, w["norm_mix_g"], w["w_c"], w["q_norm_g"], w["kv_norm_g"], w["w_uq"], w["w_uq_rot"], w["w_uk"],
      w["w_uv"], *tabs)


def _flash_kernel(q_ref, k_ref, vt_ref, o_ref, s0_ref, s1_ref, *, tk, nk):
    tq = q_ref.shape[0]
    slots = [slice(LANES * h, LANES * (h + 1)) for h in range(2)]
    vrows = [slice(LANES * h, LANES * h + V_ROWS) for h in range(2)]
    per = tk // vt_ref.shape[2]

    def scores(j, s_ref):
        kblk = k_ref[pl.ds(pl.multiple_of(j * tk, tk), tk), :]
        for h in range(2):
            s_ref[h] = _dot_nt(kblk[:, slots[h]], q_ref[:, slots[h]])

    def consume(j, s_ref, state):
        vt = jnp.concatenate([vt_ref[per * j + r] for r in range(per)], axis=1)
        m_news = [jnp.maximum(state[h][0], jnp.max(s_ref[h], axis=0, keepdims=True)) for h in range(2)]
        pts = [jnp.exp2((s_ref[h] - m_news[h]).astype(BF16)) for h in range(2)]
        return tuple((m_news[h], jnp.exp2(state[h][0] - m_news[h]) * state[h][1] + _dot(vt[vrows[h], :], pts[h]))
                     for h in range(2))

    def body(t, state):
        j = 2 * t
        scores(j + 1, s1_ref)
        state = consume(j, s0_ref, state)
        scores(j + 2, s0_ref)
        return consume(j + 1, s1_ref, state)

    state = tuple((jnp.full((1, tq), -jnp.inf, F32), jnp.zeros((V_ROWS, tq), F32)) for _ in range(2))
    scores(0, s0_ref)
    state = lax.fori_loop(0, nk // 2 - 1, body, state)
    scores(nk - 1, s1_ref)
    state = consume(nk - 2, s0_ref, state)
    state = consume(nk - 1, s1_ref, state)
    outs = [acc[0:MLA_V, :] / acc[MLA_V:MLA_V + 1, :] for _, acc in state]
    o_ref[...] = jnp.concatenate(outs, axis=0).T.astype(BF16)


def _attn_tile(T):
    return min(512, T // 4)


def _flash(q, k, vt, B, T):
    tv = _attn_tile(T)
    tq, tk = (tv // 2, 2 * tv) if T // tv >= 16 else (tv, tv)
    nq = T // tq
    nk = T // tk
    assert nk % 2 == 0 and tk % tv == 0, (T, tk, tv)
    return pl.pallas_call(
        functools.partial(_flash_kernel, tk=tk, nk=nk),
        grid=(B, MLA_HEADS // 2, nq),
        in_specs=[pl.BlockSpec((tq, 2 * LANES), lambda b, hp, i: (b * nq + i, hp)),
                  pl.BlockSpec((T, 2 * LANES), lambda b, hp, i: (b, hp)),
                  pl.BlockSpec((T // tv, 2 * LANES, tv), lambda b, hp, i: (b, hp, 0))],
        out_specs=pl.BlockSpec((tq, LANES), lambda b, hp, i: (b * nq + i, hp)),
        out_shape=jax.ShapeDtypeStruct((B * T, MLA_HEADS * MLA_V), BF16),
        scratch_shapes=[pltpu.VMEM((2, tk, tq), F32), pltpu.VMEM((2, tk, tq), F32)],
        compiler_params=_cparams("parallel", "parallel", "arbitrary"),
        name="mla_flash",
    )(q, k, vt)


def _rw_prep_kernel(x_ref, xp_ref, xn_ref, g_ref, wrw_ref, mup_ref, mun_ref, kk_ref, ka_ref, rk_ref,
                    w0_ref, w2_ref, a0_ref, a2_ref, g2_ref, seg_ref, tri_ref,
                    v_out, kkt_out, rt_out, kh_out, bh_out, kw_out, bw_out, wt_out, bonus_out, g7_out,
                    h_scr, rw_scr, *, tm, npos):
    i = pl.program_id(0)
    not_first = (i % npos != 0).astype(F32)
    not_last = (i % npos != npos - 1).astype(F32)
    g = g_ref[...]
    h_scr[0:SUBLANES, :] = (_rms(xp_ref[...], g) * not_first).astype(BF16)
    h_scr[SUBLANES:SUBLANES + tm, :] = _rms(x_ref[...], g).astype(BF16)
    h_scr[SUBLANES + tm:, :] = (_rms(xn_ref[...], g) * not_last).astype(BF16)
    rw_scr[...] = _dot(h_scr[...], wrw_ref[...])
    cur = rw_scr[SUBLANES:SUBLANES + tm, :]
    prev = rw_scr[SUBLANES - 1:SUBLANES - 1 + tm, :]
    nxt = rw_scr[SUBLANES + 1:SUBLANES + 1 + tm, :]
    mup = mup_ref[...]
    mun = mun_ref[...]
    rwf = cur * (1.0 - mup - mun) + mup * prev + mun * nxt
    r7 = rwf[:, 0:RW_DIM]
    k7 = rwf[:, RW_DIM:2 * RW_DIM]
    v7 = rwf[:, 2 * RW_DIM:3 * RW_DIM]
    wl = rwf[:, 3 * RW_DIM:3 * RW_DIM + 2 * LORA]
    al = rwf[:, 3 * RW_DIM + 2 * LORA:3 * RW_DIM + 4 * LORA]
    gl = rwf[:, 3 * RW_DIM + 4 * LORA:]
    seg = seg_ref[...]
    kx = k7 * kk_ref[...]
    kkn = kx * lax.rsqrt(jnp.maximum(_dot_0_1(kx * kx, seg), 1e-24))
    v_out[...] = v7.astype(BF16)
    g7_out[...] = _dot(_sigmoid(gl).astype(BF16), g2_ref[...])
    tw = jnp.tanh(wl).astype(BF16)
    alb = al.astype(BF16)
    bonus = jnp.zeros((tm, RW_DIM), F32)
    for d in range(2):
        z = -(w0_ref[d] + _dot(tw, w2_ref[d]))
        softplus = jnp.maximum(z, 0.0) + jnp.log(1.0 + jnp.exp(-jnp.abs(z)))
        lw = -jnp.exp(-softplus - 0.5)
        a = _sigmoid(a0_ref[d] + _dot(alb, a2_ref[d]))
        kd = k7 * (1.0 + (a - 1.0) * ka_ref[...])
        b = kkn * a
        bonus = bonus + _dot_0_1(r7 * kd * rk_ref[...], seg) * v7
        cum = _dot_0_1_l(tri_ref[d], lw)
        last = CHUNK - 1 if d == 0 else 0
        tot_rows = [cum[CHUNK * ci + last:CHUNK * ci + last + 1, :] for ci in range(tm // CHUNK)]
        tot = jnp.concatenate([jnp.broadcast_to(r, (CHUNK, RW_DIM)) for r in tot_rows], axis=0)
        w_incl = jnp.exp(cum)
        w_excl = jnp.exp(cum - lw)
        w_inv = jnp.exp(-cum)
        w_rest = jnp.exp(tot - cum)
        kkt_out[d] = (kkn * w_excl).astype(BF16)
        rt_out[d] = (r7 * w_incl).astype(BF16)
        kh_out[d] = (kd * w_inv).astype(BF16)
        bh_out[d] = (b * w_inv).astype(BF16)
        kw_out[d] = (kd * w_rest).astype(BF16)
        bw_out[d] = (b * w_rest).astype(BF16)
        for ci in range(tm // CHUNK):
            wt_out[d, ci] = jnp.exp(tot_rows[ci])
    bonus_out[...] = bonus


def _rw_prep(x2, T, w):
    ntok = x2.shape[0]
    tm = min(256, T)
    npos = T // tm
    nblk8 = ntok // SUBLANES
    r8 = tm // SUBLANES
    row = lambda i: (i, 0)
    drow = lambda i: (0, i, 0)
    dspec = pl.BlockSpec((2, tm, RW_DIM), drow)
    dshape = jax.ShapeDtypeStruct((2, ntok, RW_DIM), BF16)
    tri = _chunk_masks(tm)
    outs = pl.pallas_call(
        functools.partial(_rw_prep_kernel, tm=tm, npos=npos),
        grid=(ntok // tm,),
        in_specs=[pl.BlockSpec((tm, D_MODEL), row),
                  pl.BlockSpec((SUBLANES, D_MODEL), lambda i: (jnp.maximum(i * r8 - 1, 0), 0)),
                  pl.BlockSpec((SUBLANES, D_MODEL), lambda i: (jnp.minimum((i + 1) * r8, nblk8 - 1), 0)),
                  _full((1, D_MODEL)), _full((D_MODEL, RW_COLS)), _full((1, RW_COLS)), _full((1, RW_COLS)),
                  _full((1, RW_DIM)), _full((1, RW_DIM)), _full((1, RW_DIM)),
                  _full((2, 1, RW_DIM)), _full((2, 2 * LORA, RW_DIM)), _full((2, 1, RW_DIM)),
                  _full((2, 2 * LORA, RW_DIM)), _full((GATE_LORA, RW_DIM)), _full((RW_DIM, RW_DIM)),
                  _full((2, tm, tm))],
        out_specs=[pl.BlockSpec((tm, RW_DIM), row), dspec, dspec, dspec, dspec, dspec, dspec,
                   pl.BlockSpec((2, tm // CHUNK, 1, RW_DIM), lambda i: (0, i, 0, 0)),
                   pl.BlockSpec((tm, RW_DIM), row), pl.BlockSpec((tm, RW_DIM), row)],
        out_shape=[jax.ShapeDtypeStruct((ntok, RW_DIM), BF16), dshape, dshape, dshape, dshape, dshape, dshape,
                   jax.ShapeDtypeStruct((2, ntok // CHUNK, 1, RW_DIM), F32),
                   jax.ShapeDtypeStruct((ntok, RW_DIM), F32), jax.ShapeDtypeStruct((ntok, RW_DIM), F32)],
        scratch_shapes=[pltpu.VMEM((tm + 2 * SUBLANES, D_MODEL), BF16), pltpu.VMEM((tm + 2 * SUBLANES, RW_COLS), F32)],
        compiler_params=_cparams("parallel"),
        name="rw_prep",
    )(x2, x2, x2, w["norm_mix_g"], w["w_rw"], w["mu_prev"], w["mu_next"], w["k_k"], w["k_a"], w["r_k"],
      w["w0"], w["w2"], w["a0"], w["a2"], w["g2"], w["seg"], tri)
    return outs


def _chunk_masks(tm):
    t = np.arange(tm)
    same = (t[:, None] // CHUNK) == (t[None, :] // CHUNK)
    fwd = same & (t[None, :] <= t[:, None])
    bwd = same & (t[None, :] >= t[:, None])
    return jnp.asarray(np.stack([fwd, bwd]).astype(np.float32), BF16)


def _rw_scan_kernel(*refs):
    ins = refs[:16]
    yf_ref, yb_ref, s_ref = refs[16:]
    c = pl.program_id(1)

    @pl.when(c == 0)
    def _():
        s_ref[...] = jnp.zeros(s_ref.shape, F32)

    t_pos = lax.broadcasted_iota(jnp.int32, (CHUNK, GW), 0)
    s_pos = lax.broadcasted_iota(jnp.int32, (CHUNK, GW), 1) % CHUNK
    eye_c = (t_pos == s_pos).astype(F32)
    ri = lax.broadcasted_iota(jnp.int32, (GW, GW), 0)
    ci = lax.broadcasted_iota(jnp.int32, (GW, GW), 1)
    head_mask = (ri // CHUNK) == (ci // RW_HEAD)
    stack = lambda a: jnp.where(head_mask, jnp.concatenate([a] * GROUP, axis=0), jnp.zeros((), BF16))

    def prepare(orders):
        items = []
        for d, gi, order in ((d, gi, o) for d in range(2) for gi in range(RW_HEADS // GROUP) for o in orders):
            kkt, rt, kh, bh, kw, bw, v, wt = ins[8 * d:8 * d + 8]
            strict = (s_pos < t_pos) if d == 0 else (s_pos > t_pos)
            incl = (s_pos <= t_pos) if d == 0 else (s_pos >= t_pos)
            sub = order if d == 0 else SCAN_CHUNKS - 1 - order
            rows = slice(CHUNK * sub, CHUNK * (sub + 1))
            sl = slice(GW * gi, GW * (gi + 1))
            it = dict(y_ref=(yf_ref, yb_ref)[d], sl=sl, rows=rows, s_ref=s_ref.at[d, gi], wtot=wt[sub][:, sl],
                      order=order)
            kkt_c, rt_c, v_c = kkt[rows, sl], rt[rows, sl], v[rows, sl]
            a_all = _dot_nt(jnp.concatenate([kkt_c, rt_c], axis=0),
                            jnp.concatenate([stack(kh[rows, sl]), stack(bh[rows, sl])], axis=0))
            it.update(kkt=kkt_c, rt=rt_c, v=v_c, vs=stack(v_c),
                      kbw=jnp.concatenate([kw[rows, sl], bw[rows, sl]], axis=0),
                      ak=jnp.where(strict, a_all[:CHUNK, :GW], 0.0).astype(BF16),
                      bk=jnp.where(incl, a_all[CHUNK:, :GW], 0.0).astype(BF16),
                      bb=jnp.where(incl, a_all[CHUNK:, GW:], 0.0).astype(BF16))
            ab = jnp.where(strict, a_all[:CHUNK, GW:], 0.0)
            it.update(tinv=eye_c - ab, pw=ab.astype(BF16))
            items.append(it)
        for it in items:
            it["pw"] = _dot(it["pw"], stack(it["pw"])).astype(BF16)
        for k in range(5):
            for it in items:
                sk = stack(it["pw"])
                if k < 4:
                    res = _dot(jnp.concatenate([it["pw"], it["tinv"].astype(BF16)], axis=0), sk)
                    it["pw"] = res[:CHUNK].astype(BF16)
                    it["tinv"] = it["tinv"] + res[CHUNK:]
                else:
                    it["tinv"] = (it["tinv"] + _dot(it["tinv"].astype(BF16), sk)).astype(BF16)
        return items

    def advance(chains):
        for ch in chains:
            ch["sb"] = ch["s_ref"][...].astype(BF16)
            ch["rhs"] = (_dot_nt(ch["kkt"], ch["sb"]) + _dot(ch["ak"], ch["vs"])).astype(BF16)
        for ch in chains:
            ch["u"] = (-_dot(ch["tinv"], stack(ch["rhs"]))).astype(BF16)
        for ch in chains:
            ch["y_ref"][ch["rows"], ch["sl"]] = (_dot_nt(ch["rt"], ch["sb"]) + _dot(ch["bk"], ch["vs"])
                                                 + _dot(ch["bb"], stack(ch["u"])))
        for ch in chains:
            upd = _dot_tn(jnp.concatenate([ch["v"], ch["u"]], axis=0), ch["kbw"])
            ch["s_ref"][...] = ch["s_ref"][...] * ch["wtot"] + jnp.where(head_mask, upd, 0.0)

    prepared = prepare(range(SCAN_CHUNKS))
    for order in range(SCAN_CHUNKS):
        advance([it for it in prepared if it["order"] == order])


def _rw_scan(v, kkt, rt, kh, bh, kw, bw, wt, B, T):
    ntok = B * T
    rows = SCAN_CHUNKS * CHUNK
    nb = T // rows
    in_specs = []
    args = []
    for d in range(2):
        if d == 0:
            cm = lambda b, c: b * nb + c
        else:
            cm = lambda b, c: b * nb + (nb - 1 - c)
        for arr in (kkt, rt, kh, bh, kw, bw):
            in_specs.append(pl.BlockSpec((None, rows, RW_DIM), lambda b, c, cm=cm, d=d: (d, cm(b, c), 0)))
            args.append(arr)
        in_specs.append(pl.BlockSpec((rows, RW_DIM), lambda b, c, cm=cm: (cm(b, c), 0)))
        args.append(v)
        in_specs.append(pl.BlockSpec((None, SCAN_CHUNKS, 1, RW_DIM), lambda b, c, cm=cm, d=d: (d, cm(b, c), 0, 0)))
        args.append(wt)
    return pl.pallas_call(
        _rw_scan_kernel,
        grid=(B, nb),
        in_specs=in_specs,
        out_specs=[pl.BlockSpec((rows, RW_DIM), lambda b, c: (b * nb + c, 0)),
                   pl.BlockSpec((rows, RW_DIM), lambda b, c: (b * nb + (nb - 1 - c), 0))],
        out_shape=[jax.ShapeDtypeStruct((ntok, RW_DIM), F32), jax.ShapeDtypeStruct((ntok, RW_DIM), F32)],
        scratch_shapes=[pltpu.VMEM((2, RW_HEADS // GROUP, GW, GW), F32)],
        compiler_params=_cparams("parallel", "arbitrary"),
        name="rw_scan",
    )(*args)


def _mem_kv_kernel(m_ref, g_ref, wk_ref, wv_ref, gk_ref, k_out, v_out):
    m = _rms(m_ref[...], g_ref[...]).astype(BF16)
    k = _dot(m, wk_ref[...])
    v_out[...] = _dot(m, wv_ref[...]).astype(BF16)
    for hd in range(X_HEADS):
        sl = slice(X_HEAD * hd, X_HEAD * (hd + 1))
        k_out[:, sl] = _rms(k[:, sl], gk_ref[...]).astype(BF16)


def _mem_kv(mem2, w):
    n = mem2.shape[0]
    row = lambda i: (i, 0)
    return pl.pallas_call(
        _mem_kv_kernel,
        grid=(n // N_MEM,),
        in_specs=[pl.BlockSpec((N_MEM, D_MODEL), row), _full((1, D_MODEL)), _full((D_MODEL, X_DIM)),
                  _full((D_MODEL, X_DIM)), _full((1, X_HEAD))],
        out_specs=[pl.BlockSpec((N_MEM, X_DIM), row), pl.BlockSpec((N_MEM, X_DIM), row)],
        out_shape=[jax.ShapeDtypeStruct((n, X_DIM), BF16), jax.ShapeDtypeStruct((n, X_DIM), BF16)],
        compiler_params=_cparams("parallel"),
        name="mem_kv",
    )(mem2, w["mem_norm_g"], w["w_mk"], w["w_mv"], w["x_kn_g"])


def _xattn_kernel(x_ref, g_ref, wq_ref, gq_ref, mk_ref, mv_ref, o_ref):
    h = _rms(x_ref[...], g_ref[...]).astype(BF16)
    q = _dot(h, wq_ref[...])
    for hd in range(X_HEADS):
        sl = slice(X_HEAD * hd, X_HEAD * (hd + 1))
        qh = _rms(q[:, sl], gq_ref[...]).astype(BF16)
        s = _dot_nt(qh, mk_ref[:, sl])
        p = jnp.exp(s - jnp.max(s, axis=-1, keepdims=True))
        o = _dot(p.astype(BF16), mv_ref[:, sl]) / jnp.sum(p, axis=-1, keepdims=True)
        o_ref[:, sl] = o.astype(BF16)


def _xattn(x2, mk, mv, B, T, w):
    tq = min(512, T)
    nq = T // tq
    return pl.pallas_call(
        _xattn_kernel,
        grid=(B, nq),
        in_specs=[pl.BlockSpec((tq, D_MODEL), lambda b, i: (b * nq + i, 0)), _full((1, D_MODEL)),
                  _full((D_MODEL, X_DIM)), _full((1, X_HEAD)),
                  pl.BlockSpec((N_MEM, X_DIM), lambda b, i: (b, 0)), pl.BlockSpec((N_MEM, X_DIM), lambda b, i: (b, 0))],
        out_specs=pl.BlockSpec((tq, X_DIM), lambda b, i: (b * nq + i, 0)),
        out_shape=jax.ShapeDtypeStruct((B * T, X_DIM), BF16),
        compiler_params=_cparams("parallel", "parallel"),
        name="xattn",
    )(x2, w["norm_mix_g"], w["w_xq"], w["x_qn_g_scaled"], mk, mv)


def _merge_kernel(x_ref, oa_ref, yf_ref, yb_ref, bonus_ref, g7_ref, oc_ref, g_ref, wg_ref, woa_ref, wob_ref,
                  woc_ref, wout_ref, lng_ref, lnb_ref, seg_ref, o_ref):
    x = x_ref[...]
    h = _rms(x, g_ref[...]).astype(BF16)
    seg = seg_ref[...]
    y7 = yf_ref[...] + yb_ref[...] + bonus_ref[...]
    mu = _dot_0_1(y7, seg) * (1.0 / RW_HEAD)
    dy = y7 - mu
    var = _dot_0_1(dy * dy, seg) * (1.0 / RW_HEAD)
    y7 = dy * lax.rsqrt(var + LNX_EPS) * lng_ref[...] + lnb_ref[...]
    yb = (y7 * g7_ref[...]).astype(BF16)
    merged = jnp.zeros(x.shape, F32)
    for bi, (br, wo) in enumerate(((oa_ref[...], woa_ref), (yb, wob_ref), (oc_ref[...], woc_ref))):
        gate = _sigmoid(_dot(h, wg_ref[:, D_MODEL * bi:D_MODEL * (bi + 1)]))
        merged = merged + gate * _dot(br, wo[...])
    o_ref[...] = x + _dot(merged.astype(BF16), wout_ref[...])


def _merge(x2, oa, yf, yb, bonus, g7, oc, T, w):
    ntok = x2.shape[0]
    tm = min(512, T)
    row = lambda i: (i, 0)
    half = lambda: pl.BlockSpec((tm, RW_DIM), row)
    return pl.pallas_call(
        _merge_kernel,
        grid=(ntok // tm,),
        in_specs=[pl.BlockSpec((tm, D_MODEL), row), half(), half(), half(), half(), half(), half(),
                  _full((1, D_MODEL)), _full((D_MODEL, 3 * D_MODEL)), _full((RW_DIM, D_MODEL)),
                  _full((RW_DIM, D_MODEL)), _full((RW_DIM, D_MODEL)), _full((D_MODEL, D_MODEL)),
                  _full((1, RW_DIM)), _full((1, RW_DIM)), _full((RW_DIM, RW_DIM))],
        out_specs=pl.BlockSpec((tm, D_MODEL), row),
        out_shape=jax.ShapeDtypeStruct((ntok, D_MODEL), F32),
        compiler_params=_cparams("parallel"),
        name="merge",
    )(x2, oa, yf, yb, bonus, g7, oc, w["norm_mix_g"], w["w_gate"], w["w_o_a"], w["w_o_b"], w["w_o_c"],
      w["w_out"], w["lnx_g"], w["lnx_b"], w["seg"])


MXU_TILE = 256
FFN_SPLITS = (0, 5 * MXU_TILE, D_FF)


def _ffn_kernel(x_ref, xp_ref, xn_ref, g_ref, wug_ref, wuv_ref, cw_ref, cb_ref, wd_ref, o_ref, h_scr, ug_scr,
                *, tm, npos):
    i = pl.program_id(0)
    not_first = (i % npos != 0).astype(F32)
    not_last = (i % npos != npos - 1).astype(F32)
    g = g_ref[...]
    h_scr[0:SUBLANES, :] = (_rms(xp_ref[...], g) * not_first).astype(BF16)
    h_scr[SUBLANES:SUBLANES + tm, :] = _rms(x_ref[...], g).astype(BF16)
    h_scr[SUBLANES + tm:, :] = (_rms(xn_ref[...], g) * not_last).astype(BF16)
    out = x_ref[...]
    for lo, hi in zip(FFN_SPLITS[:-1], FFN_SPLITS[1:]):
        n = hi - lo
        ug_scr[:, 0:n] = _dot(h_scr[...], wug_ref[:, lo:hi])
        uv = _dot(h_scr[SUBLANES:SUBLANES + tm, :], wuv_ref[:, lo:hi])
        cw = cw_ref[:, lo:hi]
        c = (cw[0:1] * ug_scr[SUBLANES - 1:SUBLANES - 1 + tm, 0:n] + cw[1:2] * ug_scr[SUBLANES:SUBLANES + tm, 0:n]
             + cw[2:3] * ug_scr[SUBLANES + 1:SUBLANES + 1 + tm, 0:n] + cb_ref[:, lo:hi])
        act = 0.5 * c * (1.0 + lax.erf(c * np.float32(1.0 / np.sqrt(2.0)))) * uv
        out = out + _dot(act.astype(BF16), wd_ref[lo:hi, :])
    o_ref[...] = out


def _ffn(x2, T, w):
    ntok = x2.shape[0]
    tm = min(512, T)
    fc = max(hi - lo for lo, hi in zip(FFN_SPLITS[:-1], FFN_SPLITS[1:]))
    npos = T // tm
    nblk8 = ntok // SUBLANES
    r8 = tm // SUBLANES
    row = lambda i: (i, 0)
    return pl.pallas_call(
        functools.partial(_ffn_kernel, tm=tm, npos=npos),
        grid=(ntok // tm,),
        in_specs=[pl.BlockSpec((tm, D_MODEL), row),
                  pl.BlockSpec((SUBLANES, D_MODEL), lambda i: (jnp.maximum(i * r8 - 1, 0), 0)),
                  pl.BlockSpec((SUBLANES, D_MODEL), lambda i: (jnp.minimum((i + 1) * r8, nblk8 - 1), 0)),
                  _full((1, D_MODEL)), _full((D_MODEL, D_FF)), _full((D_MODEL, D_FF)), _full((3, D_FF)),
                  _full((1, D_FF)), _full((D_FF, D_MODEL))],
        out_specs=pl.BlockSpec((tm, D_MODEL), row),
        out_shape=jax.ShapeDtypeStruct((ntok, D_MODEL), F32),
        scratch_shapes=[pltpu.VMEM((tm + 2 * SUBLANES, D_MODEL), BF16), pltpu.VMEM((tm + 2 * SUBLANES, fc), F32)],
        compiler_params=_cparams("parallel"),
        name="conv_ffn",
    )(x2, x2, x2, w["norm_ffn_g"], w["w_up_gate"], w["w_up_val"], w["conv_w"], w["conv_b"], w["w_down"])


def _prep_weights(p):
    w = {}
    row = lambda a: a.reshape(1, -1).astype(F32)
    w_in = p["w_in"]
    o = np.cumsum([0, Q_LORA, KV_LORA, MLA_ROPE, RW_COLS, X_DIM, 3 * D_MODEL])
    seg = lambda i: w_in[:, o[i]:o[i + 1]]
    zc = lambda n: jnp.zeros((D_MODEL, n), F32)
    w["w_c"] = jnp.concatenate([seg(0), seg(1), zc(MLA_NOPE), seg(2), zc(LANES - MLA_QK)], axis=1).astype(BF16)
    w["w_rw"] = seg(3).astype(BF16)
    w["w_xq"] = seg(4).astype(BF16)
    w["w_gate"] = seg(5).astype(BF16)
    w["norm_mix_g"] = row(p["norm_mix_g"])
    w["q_norm_g"] = row(p["q_norm_g"])
    w["kv_norm_g"] = row(p["kv_norm_g"])
    pad_slot = lambda a: jnp.pad(a, ((0, 0), (0, 0), (0, LANES - a.shape[-1]))).reshape(a.shape[0], -1)
    uq = p["w_uq"].reshape(Q_LORA, MLA_HEADS, MLA_QK)
    w["w_uq"] = pad_slot(uq).astype(BF16)
    half = MLA_ROPE // 2
    uq_rot = jnp.concatenate([jnp.zeros_like(uq[:, :, :MLA_NOPE]), -uq[:, :, MLA_NOPE + half:],
                              uq[:, :, MLA_NOPE:MLA_NOPE + half]], axis=-1)
    w["w_uq_rot"] = pad_slot(uq_rot).astype(BF16)
    ukv =p["w_ukv"].reshape(KV_LORA, MLA_HEADS, MLA_NOPE + MLA_V)
    w["w_uk"] = pad_slot(ukv[:, :, :MLA_NOPE]).astype(BF16)
    w["w_uv"] = pad_slot(ukv[:, :, MLA_NOPE:]).T.astype(BF16)
    w["mla_qn_g_scaled"] = p["mla_qn_g"].reshape(-1) * np.float32(MLA_QK ** -0.5 * np.log2(np.e))
    w["mla_kn_g"] = p["mla_kn_g"].reshape(-1)
    for name in ("mu_prev", "mu_next", "k_k", "k_a", "r_k", "lnx_g", "lnx_b", "mem_norm_g", "x_kn_g", "norm_ffn_g",
                 "conv_b"):
        w[name] = row(p[name])
    w["w0"] = jnp.stack([p["w0_f"], p["w0_b"]]).reshape(2, 1, RW_DIM)
    w["a0"] = jnp.stack([p["a0_f"], p["a0_b"]]).reshape(2, 1, RW_DIM)
    zl = jnp.zeros((LORA, RW_DIM), F32)
    w["w2"] = jnp.stack([jnp.concatenate([p["w2_f"], zl]), jnp.concatenate([zl, p["w2_b"]])]).astype(BF16)
    w["a2"] = jnp.stack([jnp.concatenate([p["a2_f"], zl]), jnp.concatenate([zl, p["a2_b"]])]).astype(BF16)
    w["g2"] = p["g2"].astype(BF16)
    hid = np.arange(RW_DIM) // RW_HEAD
    w["seg"] = jnp.asarray((hid[:, None] == hid[None, :]).astype(np.float32), BF16)
    mkv = p["w_mkv"].reshape(D_MODEL, X_HEADS, 2 * X_HEAD)
    w["w_mk"] = mkv[:, :, :X_HEAD].reshape(D_MODEL, X_DIM).astype(BF16)
    w["w_mv"] = mkv[:, :, X_HEAD:].reshape(D_MODEL, X_DIM).astype(BF16)
    w["x_qn_g_scaled"] = row(p["x_qn_g"]) * np.float32(X_HEAD ** -0.5)
    for name in ("w_o_a", "w_o_b", "w_o_c", "w_out", "w_down"):
        w[name] = p[name].astype(BF16)
    w["w_up_gate"] = p["w_up"][:, :D_FF].astype(BF16)
    w["w_up_val"] = p["w_up"][:, D_FF:].astype(BF16)
    w["conv_w"] = p["conv_w"].astype(F32)
    return w


def _rope_tables(T, gq, gk):
    half = MLA_ROPE // 2
    inv = jnp.power(ROPE_THETA, -jnp.arange(half, dtype=F32) / half)
    ang = jnp.arange(T, dtype=F32)[:, None] * inv[None, :]
    cos, sin = jnp.cos(ang), jnp.sin(ang)
    z = lambda n: jnp.zeros((T, n), F32)
    pad = z(LANES - MLA_QK)

    def own(g):
        g1, g2 = g[MLA_NOPE:MLA_NOPE + half], g[MLA_NOPE + half:]
        return jnp.concatenate([jnp.broadcast_to(g[:MLA_NOPE], (T, MLA_NOPE)), cos * g1, cos * g2, pad], axis=1)

    g1, g2 = gq[MLA_NOPE:MLA_NOPE + half], gq[MLA_NOPE + half:]
    qb = jnp.concatenate([z(MLA_NOPE), sin * g2, sin * g1, pad], axis=1)
    g1, g2 = gk[MLA_NOPE:MLA_NOPE + half], gk[MLA_NOPE + half:]
    kba = jnp.concatenate([z(MLA_NOPE), -sin * g2, z(half), pad], axis=1)
    kbb = jnp.concatenate([z(MLA_NOPE), z(half), sin * g1, pad], axis=1)
    return own(gq), qb, own(gk), kba, kbb


def _layer(x, mem, w):
    B, T, _ = x.shape
    x2 = x.reshape(B * T, D_MODEL)
    q, k, v = _mla_prep(x2, T, w, _rope_tables(T, w["mla_qn_g_scaled"], w["mla_kn_g"]))
    o_a = _flash(q, k, v, B, T)
    v7, kkt, rt, kh, bh, kw, bw, wt, bonus, g7 = _rw_prep(x2, T, w)
    y_f, y_b = _rw_scan(v7, kkt, rt, kh, bh, kw, bw, wt, B, T)
    mk, mv = _mem_kv(mem.reshape(B * N_MEM, D_MODEL), w)
    o_c = _xattn(x2, mk, mv, B, T, w)
    x1 = _merge(x2, o_a, y_f, y_b, bonus, g7, o_c, T, w)
    return _ffn(x1, T, w).reshape(B, T, D_MODEL)


def kernel(x_prompt, x_sample, mem_prompt, mem_sample, norm_mix_g, w_in, q_norm_g, w_uq, kv_norm_g, w_ukv, mla_qn_g, mla_kn_g, w_o_a, mu_prev, mu_next, w0_f, w2_f, a0_f, a2_f, w0_b, w2_b, a0_b, a2_b, g2, k_k, k_a, r_k, lnx_g, lnx_b, w_o_b, mem_norm_g, w_mkv, x_qn_g, x_kn_g, w_o_c, w_out, norm_ffn_g, w_up, conv_w, conv_b, w_down):
    p = dict(norm_mix_g=norm_mix_g, w_in=w_in, q_norm_g=q_norm_g, w_uq=w_uq, kv_norm_g=kv_norm_g, w_ukv=w_ukv,
             mla_qn_g=mla_qn_g, mla_kn_g=mla_kn_g, w_o_a=w_o_a, mu_prev=mu_prev, mu_next=mu_next,
             w0_f=w0_f, w2_f=w2_f, a0_f=a0_f, a2_f=a2_f, w0_b=w0_b, w2_b=w2_b, a0_b=a0_b, a2_b=a2_b,
             g2=g2, k_k=k_k, k_a=k_a, r_k=r_k, lnx_g=lnx_g, lnx_b=lnx_b, w_o_b=w_o_b,
             mem_norm_g=mem_norm_g, w_mkv=w_mkv, x_qn_g=x_qn_g, x_kn_g=x_kn_g, w_o_c=w_o_c, w_out=w_out,
             norm_ffn_g=norm_ffn_g, w_up=w_up, conv_w=conv_w, conv_b=conv_b, w_down=w_down)
    w = _prep_weights({name: a[0] for name, a in p.items()})
    return (_layer(x_prompt, mem_prompt, w), _layer(x_sample, mem_sample, w))
```

```python
import functools

import numpy as np
import jax
import jax.numpy as jnp
from jax import lax
from jax.experimental import pallas as pl
from jax.experimental.pallas import tpu as pltpu

F32 = jnp.float32
BF16 = jnp.bfloat16

D_MODEL = 1024
RMS_EPS = 1e-6
N_MEM = 256
MLA_HEADS = 8
MLA_NOPE = 64
MLA_ROPE = 32
MLA_QK = MLA_NOPE + MLA_ROPE
MLA_V = 64
Q_LORA = 384
KV_LORA = 256
ROPE_THETA = 10000.0
RW_HEADS = 8
RW_HEAD = 64
RW_DIM = RW_HEADS * RW_HEAD
LORA = 64
GATE_LORA = 128
RW_COLS = 3 * RW_DIM + 4 * LORA + GATE_LORA
LNX_EPS = 64e-5
X_HEADS = 4
X_HEAD = 128
X_DIM = X_HEADS * X_HEAD
D_FF = 2816

LANES = 128
SUBLANES = 8
CHUNK = 64
GROUP = 2
GW = GROUP * RW_HEAD
SLOT_COLS = MLA_HEADS * LANES
MLA_C_COLS = Q_LORA + KV_LORA + LANES
BF16_ROWS = 16
V_ROWS = -(-(MLA_V + 1) // BF16_ROWS) * BF16_ROWS
SCAN_CHUNKS = 4
VMEM_LIMIT = 56 * 1024 * 1024


def _cparams(*sem):
    return pltpu.CompilerParams(dimension_semantics=sem, vmem_limit_bytes=VMEM_LIMIT)


def _rms(x, g, eps=RMS_EPS):
    return x * lax.rsqrt(jnp.mean(x * x, axis=-1, keepdims=True) + eps) * g


def _dot(a, b):
    return jnp.dot(a, b, preferred_element_type=F32)


def _dot_nt(a, b):
    return lax.dot_general(a, b, (((1,), (1,)), ((), ())), preferred_element_type=F32)


def _dot_tn(a, b):
    return lax.dot_general(a, b, (((0,), (0,)), ((), ())), preferred_element_type=F32)


def _dot_0_1(x, e):
    return _dot(x.astype(BF16), e)


def _dot_0_1_l(e, x):
    hi = x.astype(BF16)
    lo = (x - hi.astype(F32)).astype(BF16)
    return _dot(e, hi) + _dot(e, lo)


def _sigmoid(z):
    return 1.0 / (1.0 + jnp.exp(-z))


def _full(shape):
    nd = len(shape)
    return pl.BlockSpec(shape, lambda *_: (0,) * nd, pipeline_mode=pl.Buffered(1))


def _mla_prep_kernel(x_ref, g_ref, wc_ref, gq_ref, gkv_ref, wuq_ref, wuqr_ref, wuk_ref, wuv_ref,
                     qa_ref, qb_ref, ka_ref, kba_ref, kbb_ref, q_out, k_out, v_out):
    tm = x_ref.shape[0]
    half = MLA_ROPE // 2
    halves = [slice(0, tm // 2), slice(tm // 2, tm)]
    cs = [_dot(_rms(x_ref[r, :], g_ref[...]).astype(BF16), wc_ref[...]) for r in halves]
    proj = []
    for c in cs:
        cq = _rms(c[:, :Q_LORA], gq_ref[...]).astype(BF16)
        ckv = _rms(c[:, Q_LORA:Q_LORA + KV_LORA], gkv_ref[...]).astype(BF16)
        proj.append((_dot(cq, wuq_ref[...]), _dot(cq, wuqr_ref[...]), _dot(ckv, wuk_ref[...]),
                     _dot_nt(wuv_ref[...], ckv)))
    for r, c, (q, qp, k, vt) in zip(halves, cs, proj):
        slot_row = lax.broadcasted_iota(jnp.int32, vt.shape, 0) % LANES
        v_out[0, :, r] = jnp.where(slot_row == MLA_V, 1.0, vt).astype(BF16)
        kr = c[:, Q_LORA + KV_LORA:]
        qa = qa_ref[r, :]
        qb = qb_ref[r, :]
        ka = ka_ref[r, :]
        kr_roped = (kr * ka + pltpu.roll(kr, LANES - half, 1) * kba_ref[r, :]
                    + pltpu.roll(kr, half, 1) * kbb_ref[r, :])
        kr_ss = jnp.sum(kr * kr, axis=-1, keepdims=True)
        for hd in range(MLA_HEADS):
            sl = slice(LANES * hd, LANES * (hd + 1))
            t = q[:, sl]
            ss = jnp.sum(t * t, axis=-1, keepdims=True) * (1.0 / MLA_QK)
            q_out[r, sl] = ((t * qa + qp[:, sl] * qb) * lax.rsqrt(ss + RMS_EPS)).astype(BF16)
            t = k[:, sl]
            ss = (jnp.sum(t * t, axis=-1, keepdims=True) + kr_ss) * (1.0 / MLA_QK)
            k_out[r, sl] = ((t * ka + kr_roped) * lax.rsqrt(ss + RMS_EPS)).astype(BF16)


def _mla_prep(x2, T, w, tabs):
    ntok = x2.shape[0]
    tm = _attn_tile(T)
    npos = T // tm
    row = lambda i: (i, 0)
    pos = lambda i: (i % npos, 0)
    return pl.pallas_call(
        _mla_prep_kernel,
        grid=(ntok // tm,),
        in_specs=[pl.BlockSpec((tm, D_MODEL), row), _full((1, D_MODEL)), _full((D_MODEL, MLA_C_COLS)),
                  _full((1, Q_LORA)), _full((1, KV_LORA)), _full((Q_LORA, SLOT_COLS)), _full((Q_LORA, SLOT_COLS)),
                  _full((KV_LORA, SLOT_COLS)), _full((SLOT_COLS, KV_LORA))] + [pl.BlockSpec((tm, LANES), pos)] * 5,
        out_specs=[pl.BlockSpec((tm, SLOT_COLS), row), pl.BlockSpec((tm, SLOT_COLS), row),
                   pl.BlockSpec((1, SLOT_COLS, tm), lambda i: (i, 0, 0))],
        out_shape=[jax.ShapeDtypeStruct((ntok, SLOT_COLS), BF16), jax.ShapeDtypeStruct((ntok, SLOT_COLS), BF16),
                   jax.ShapeDtypeStruct((ntok // tm, SLOT_COLS, tm), BF16)],
        compiler_params=_cparams("parallel"),
        name="mla_prep",
    )(x2, w["norm_mix_g"], w["w_c"], w["q_norm_g"], w["kv_norm_g"], w["w_uq"], w["w_uq_rot"], w["w_uk"],
      w["w_uv"], *tabs)


def _flash_kernel(q_ref, k_ref, vt_ref, o_ref, s0_ref, s1_ref, *, tk, nk):
    tq = q_ref.shape[0]
    slots = [slice(LANES * h, LANES * (h + 1)) for h in range(2)]
    vrows = [slice(LANES * h, LANES * h + V_ROWS) for h in range(2)]
    per = tk // vt_ref.shape[2]

    def scores(j, s_ref):
        kblk = k_ref[pl.ds(pl.multiple_of(j * tk, tk), tk), :]
        for h in range(2):
            s_ref[h] = _dot_nt(kblk[:, slots[h]], q_ref[:, slots[h]])

    def consume(j, s_ref, state):
        vt = jnp.concatenate([vt_ref[per * j + r] for r in range(per)], axis=1)
        m_news = [jnp.maximum(state[h][0], jnp.max(s_ref[h], axis=0, keepdims=True)) for h in range(2)]
        pts = [jnp.exp2((s_ref[h] - m_news[h]).astype(BF16)) for h in range(2)]
        return tuple((m_news[h], jnp.exp2(state[h][0] - m_news[h]) * state[h][1] + _dot(vt[vrows[h], :], pts[h]))
                     for h in range(2))

    def body(t, state):
        j = 2 * t
        scores(j + 1, s1_ref)
        state = consume(j, s0_ref, state)
        scores(j + 2, s0_ref)
        return consume(j + 1, s1_ref, state)

    state = tuple((jnp.full((1, tq), -jnp.inf, F32), jnp.zeros((V_ROWS, tq), F32)) for _ in range(2))
    scores(0, s0_ref)
    state = lax.fori_loop(0, nk // 2 - 1, body, state)
    scores(nk - 1, s1_ref)
    state = consume(nk - 2, s0_ref, state)
    state = consume(nk - 1, s1_ref, state)
    outs = [acc[0:MLA_V, :] / acc[MLA_V:MLA_V + 1, :] for _, acc in state]
    o_ref[...] = jnp.concatenate(outs, axis=0).T.astype(BF16)


def _attn_tile(T):
    return min(512, T // 4)


def _flash(q, k, vt, B, T):
    tv = _attn_tile(T)
    tq, tk = (tv // 2, 2 * tv) if T // tv >= 16 else (tv, tv)
    nq = T // tq
    nk = T // tk
    assert nk % 2 == 0 and tk % tv == 0, (T, tk, tv)
    return pl.pallas_call(
        functools.partial(_flash_kernel, tk=tk, nk=nk),
        grid=(B, MLA_HEADS // 2, nq),
        in_specs=[pl.BlockSpec((tq, 2 * LANES), lambda b, hp, i: (b * nq + i, hp)),
                  pl.BlockSpec((T, 2 * LANES), lambda b, hp, i: (b, hp)),
                  pl.BlockSpec((T // tv, 2 * LANES, tv), lambda b, hp, i: (b, hp, 0))],
        out_specs=pl.BlockSpec((tq, LANES), lambda b, hp, i: (b * nq + i, hp)),
        out_shape=jax.ShapeDtypeStruct((B * T, MLA_HEADS * MLA_V), BF16),
        scratch_shapes=[pltpu.VMEM((2, tk, tq), F32), pltpu.VMEM((2, tk, tq), F32)],
        compiler_params=_cparams("parallel", "parallel", "arbitrary"),
        name="mla_flash",
    )(q, k, vt)


def _rw_prep_kernel(x_ref, xp_ref, xn_ref, g_ref, wrw_ref, mup_ref, mun_ref, kk_ref, ka_ref, rk_ref,
                    w0_ref, w2_ref, a0_ref, a2_ref, g2_ref, seg_ref, tri_ref,
                    v_out, kkt_out, rt_out, kh_out, bh_out, kw_out, bw_out, wt_out, bonus_out, g7_out,
                    h_scr, rw_scr, *, tm, npos):
    i = pl.program_id(0)
    not_first = (i % npos != 0).astype(F32)
    not_last = (i % npos != npos - 1).astype(F32)
    g = g_ref[...]
    h_scr[0:SUBLANES, :] = (_rms(xp_ref[...], g) * not_first).astype(BF16)
    h_scr[SUBLANES:SUBLANES + tm, :] = _rms(x_ref[...], g).astype(BF16)
    h_scr[SUBLANES + tm:, :] = (_rms(xn_ref[...], g) * not_last).astype(BF16)
    rw_scr[...] = _dot(h_scr[...], wrw_ref[...])
    cur = rw_scr[SUBLANES:SUBLANES + tm, :]
    prev = rw_scr[SUBLANES - 1:SUBLANES - 1 + tm, :]
    nxt = rw_scr[SUBLANES + 1:SUBLANES + 1 + tm, :]
    mup = mup_ref[...]
    mun = mun_ref[...]
    rwf = cur * (1.0 - mup - mun) + mup * prev + mun * nxt
    r7 = rwf[:, 0:RW_DIM]
    k7 = rwf[:, RW_DIM:2 * RW_DIM]
    v7 = rwf[:, 2 * RW_DIM:3 * RW_DIM]
    wl = rwf[:, 3 * RW_DIM:3 * RW_DIM + 2 * LORA]
    al = rwf[:, 3 * RW_DIM + 2 * LORA:3 * RW_DIM + 4 * LORA]
    gl = rwf[:, 3 * RW_DIM + 4 * LORA:]
    seg = seg_ref[...]
    kx = k7 * kk_ref[...]
    kkn = kx * lax.rsqrt(jnp.maximum(_dot_0_1(kx * kx, seg), 1e-24))
    v_out[...] = v7.astype(BF16)
    g7_out[...] = _dot(_sigmoid(gl).astype(BF16), g2_ref[...])
    tw = jnp.tanh(wl).astype(BF16)
    alb = al.astype(BF16)
    bonus = jnp.zeros((tm, RW_DIM), F32)
    for d in range(2):
        z = -(w0_ref[d] + _dot(tw, w2_ref[d]))
        softplus = jnp.maximum(z, 0.0) + jnp.log(1.0 + jnp.exp(-jnp.abs(z)))
        lw = -jnp.exp(-softplus - 0.5)
        a = _sigmoid(a0_ref[d] + _dot(alb, a2_ref[d]))
        kd = k7 * (1.0 + (a - 1.0) * ka_ref[...])
        b = kkn * a
        bonus = bonus + _dot_0_1(r7 * kd * rk_ref[...], seg) * v7
        cum = _dot_0_1_l(tri_ref[d], lw)
        last = CHUNK - 1 if d == 0 else 0
        tot_rows = [cum[CHUNK * ci + last:CHUNK * ci + last + 1, :] for ci in range(tm // CHUNK)]
        tot = jnp.concatenate([jnp.broadcast_to(r, (CHUNK, RW_DIM)) for r in tot_rows], axis=0)
        w_incl = jnp.exp(cum)
        w_excl = jnp.exp(cum - lw)
        w_inv = jnp.exp(-cum)
        w_rest = jnp.exp(tot - cum)
        kkt_out[d] = (kkn * w_excl).astype(BF16)
        rt_out[d] = (r7 * w_incl).astype(BF16)
        kh_out[d] = (kd * w_inv).astype(BF16)
        bh_out[d] = (b * w_inv).astype(BF16)
        kw_out[d] = (kd * w_rest).astype(BF16)
        bw_out[d] = (b * w_rest).astype(BF16)
        for ci in range(tm // CHUNK):
            wt_out[d, ci] = jnp.exp(tot_rows[ci])
    bonus_out[...] = bonus


def _rw_prep(x2, T, w):
    ntok = x2.shape[0]
    tm = min(256, T)
    npos = T // tm
    nblk8 = ntok // SUBLANES
    r8 = tm // SUBLANES
    row = lambda i: (i, 0)
    drow = lambda i: (0, i, 0)
    dspec = pl.BlockSpec((2, tm, RW_DIM), drow)
    dshape = jax.ShapeDtypeStruct((2, ntok, RW_DIM), BF16)
    tri = _chunk_masks(tm)
    outs = pl.pallas_call(
        functools.partial(_rw_prep_kernel, tm=tm, npos=npos),
        grid=(ntok // tm,),
        in_specs=[pl.BlockSpec((tm, D_MODEL), row),
                  pl.BlockSpec((SUBLANES, D_MODEL), lambda i: (jnp.maximum(i * r8 - 1, 0), 0)),
                  pl.BlockSpec((SUBLANES, D_MODEL), lambda i: (jnp.minimum((i + 1) * r8, nblk8 - 1), 0)),
                  _full((1, D_MODEL)), _full((D_MODEL, RW_COLS)), _full((1, RW_COLS)), _full((1, RW_COLS)),
                  _full((1, RW_DIM)), _full((1, RW_DIM)), _full((1, RW_DIM)),
                  _full((2, 1, RW_DIM)), _full((2, 2 * LORA, RW_DIM)), _full((2, 1, RW_DIM)),
                  _full((2, 2 * LORA, RW_DIM)), _full((GATE_LORA, RW_DIM)), _full((RW_DIM, RW_DIM)),
                  _full((2, tm, tm))],
        out_specs=[pl.BlockSpec((tm, RW_DIM), row), dspec, dspec, dspec, dspec, dspec, dspec,
                   pl.BlockSpec((2, tm // CHUNK, 1, RW_DIM), lambda i: (0, i, 0, 0)),
                   pl.BlockSpec((tm, RW_DIM), row), pl.BlockSpec((tm, RW_DIM), row)],
        out_shape=[jax.ShapeDtypeStruct((ntok, RW_DIM), BF16), dshape, dshape, dshape, dshape, dshape, dshape,
                   jax.ShapeDtypeStruct((2, ntok // CHUNK, 1, RW_DIM), F32),
                   jax.ShapeDtypeStruct((ntok, RW_DIM), F32), jax.ShapeDtypeStruct((ntok, RW_DIM), F32)],
        scratch_shapes=[pltpu.VMEM((tm + 2 * SUBLANES, D_MODEL), BF16), pltpu.VMEM((tm + 2 * SUBLANES, RW_COLS), F32)],
        compiler_params=_cparams("parallel"),
        name="rw_prep",
    )(x2, x2, x2, w["norm_mix_g"], w["w_rw"], w["mu_prev"], w["mu_next"], w["k_k"], w["k_a"], w["r_k"],
      w["w0"], w["w2"], w["a0"], w["a2"], w["g2"], w["seg"], tri)
    return outs


def _chunk_masks(tm):
    t = np.arange(tm)
    same = (t[:, None] // CHUNK) == (t[None, :] // CHUNK)
    fwd = same & (t[None, :] <= t[:, None])
    bwd = same & (t[None, :] >= t[:, None])
    return jnp.asarray(np.stack([fwd, bwd]).astype(np.float32), BF16)


def _rw_scan_kernel(*refs):
    ins = refs[:16]
    yf_ref, yb_ref, s_ref = refs[16:]
    c = pl.program_id(1)

    @pl.when(c == 0)
    def _():
        s_ref[...] = jnp.zeros(s_ref.shape, F32)

    t_pos = lax.broadcasted_iota(jnp.int32, (CHUNK, GW), 0)
    s_pos = lax.broadcasted_iota(jnp.int32, (CHUNK, GW), 1) % CHUNK
    eye_c = (t_pos == s_pos).astype(F32)
    ri = lax.broadcasted_iota(jnp.int32, (GW, GW), 0)
    ci = lax.broadcasted_iota(jnp.int32, (GW, GW), 1)
    head_mask = (ri // CHUNK) == (ci // RW_HEAD)
    stack = lambda a: jnp.where(head_mask, jnp.concatenate([a] * GROUP, axis=0), jnp.zeros((), BF16))

    def prepare(orders):
        items = []
        for d, gi, order in ((d, gi, o) for d in range(2) for gi in range(RW_HEADS // GROUP) for o in orders):
            kkt, rt, kh, bh, kw, bw, v, wt = ins[8 * d:8 * d + 8]
            strict = (s_pos < t_pos) if d == 0 else (s_pos > t_pos)
            incl = (s_pos <= t_pos) if d == 0 else (s_pos >= t_pos)
            sub = order if d == 0 else SCAN_CHUNKS - 1 - order
            rows = slice(CHUNK * sub, CHUNK * (sub + 1))
            sl = slice(GW * gi, GW * (gi + 1))
            it = dict(y_ref=(yf_ref, yb_ref)[d], sl=sl, rows=rows, s_ref=s_ref.at[d, gi], wtot=wt[sub][:, sl],
                      order=order)
            v_c = v[rows, sl]
            kkt_rt = jnp.concatenate([kkt[rows, sl], rt[rows, sl]], axis=0)
            a_all = _dot_nt(kkt_rt, jnp.concatenate([stack(kh[rows, sl]), stack(bh[rows, sl])], axis=0))
            ak = jnp.where(strict, a_all[:CHUNK, :GW], 0.0).astype(BF16)
            bk = jnp.where(incl, a_all[CHUNK:, :GW], 0.0).astype(BF16)
            it.update(kkt_rt=kkt_rt, v=v_c, kbw=jnp.concatenate([kw[rows, sl], bw[rows, sl]], axis=0),
                      abk=jnp.concatenate([ak, bk], axis=0), bb=jnp.where(incl, a_all[CHUNK:, GW:], 0.0).astype(BF16))
            ab = jnp.where(strict, a_all[:CHUNK, GW:], 0.0)
            it.update(tinv=eye_c - ab, pw=ab.astype(BF16))
            items.append(it)
        for it in items:
            it["abk_v"] = _dot(it["abk"], stack(it["v"]))
        for it in items:
            it["pw"] = _dot(it["pw"], stack(it["pw"])).astype(BF16)
        for k in range(5):
            for it in items:
                sk = stack(it["pw"])
                if k < 4:
                    res = _dot(jnp.concatenate([it["pw"], it["tinv"].astype(BF16)], axis=0), sk)
                    it["pw"] = res[:CHUNK].astype(BF16)
                    it["tinv"] = it["tinv"] + res[CHUNK:]
                else:
                    it["tinv"] = (it["tinv"] + _dot(it["tinv"].astype(BF16), sk)).astype(BF16)
        return items

    def advance(chains):
        for ch in chains:
            sb = ch["s_ref"][...].astype(BF16)
            ch["zs"] = _dot_nt(ch["kkt_rt"], sb) + ch["abk_v"]
        for ch in chains:
            ch["u"] = (-_dot(ch["tinv"], stack(ch["zs"][:CHUNK].astype(BF16)))).astype(BF16)
        for ch in chains:
            ch["y_ref"][ch["rows"], ch["sl"]] = ch["zs"][CHUNK:] + _dot(ch["bb"], stack(ch["u"]))
        for ch in chains:
            upd = _dot_tn(jnp.concatenate([ch["v"], ch["u"]], axis=0), ch["kbw"])
            ch["s_ref"][...] = ch["s_ref"][...] * ch["wtot"] + jnp.where(head_mask, upd, 0.0)

    prepared = prepare(range(SCAN_CHUNKS))
    for order in range(SCAN_CHUNKS):
        advance([it for it in prepared if it["order"] == order])


def _rw_scan(v, kkt, rt, kh, bh, kw, bw, wt, B, T):
    ntok = B * T
    rows = SCAN_CHUNKS * CHUNK
    nb = T // rows
    in_specs = []
    args = []
    for d in range(2):
        if d == 0:
            cm = lambda b, c: b * nb + c
        else:
            cm = lambda b, c: b * nb + (nb - 1 - c)
        for arr in (kkt, rt, kh, bh, kw, bw):
            in_specs.append(pl.BlockSpec((None, rows, RW_DIM), lambda b, c, cm=cm, d=d: (d, cm(b, c), 0)))
            args.append(arr)
        in_specs.append(pl.BlockSpec((rows, RW_DIM), lambda b, c, cm=cm: (cm(b, c), 0)))
        args.append(v)
        in_specs.append(pl.BlockSpec((None, SCAN_CHUNKS, 1, RW_DIM), lambda b, c, cm=cm, d=d: (d, cm(b, c), 0, 0)))
        args.append(wt)
    return pl.pallas_call(
        _rw_scan_kernel,
        grid=(B, nb),
        in_specs=in_specs,
        out_specs=[pl.BlockSpec((rows, RW_DIM), lambda b, c: (b * nb + c, 0)),
                   pl.BlockSpec((rows, RW_DIM), lambda b, c: (b * nb + (nb - 1 - c), 0))],
        out_shape=[jax.ShapeDtypeStruct((ntok, RW_DIM), F32), jax.ShapeDtypeStruct((ntok, RW_DIM), F32)],
        scratch_shapes=[pltpu.VMEM((2, RW_HEADS // GROUP, GW, GW), F32)],
        compiler_params=_cparams("parallel", "arbitrary"),
        name="rw_scan",
    )(*args)


def _mem_kv_kernel(m_ref, g_ref, wk_ref, wv_ref, gk_ref, k_out, v_out):
    m = _rms(m_ref[...], g_ref[...]).astype(BF16)
    k = _dot(m, wk_ref[...])
    v_out[...] = _dot(m, wv_ref[...]).astype(BF16)
    for hd in range(X_HEADS):
        sl = slice(X_HEAD * hd, X_HEAD * (hd + 1))
        k_out[:, sl] = _rms(k[:, sl], gk_ref[...]).astype(BF16)


def _mem_kv(mem2, w):
    n = mem2.shape[0]
    row = lambda i: (i, 0)
    return pl.pallas_call(
        _mem_kv_kernel,
        grid=(n // N_MEM,),
        in_specs=[pl.BlockSpec((N_MEM, D_MODEL), row), _full((1, D_MODEL)), _full((D_MODEL, X_DIM)),
                  _full((D_MODEL, X_DIM)), _full((1, X_HEAD))],
        out_specs=[pl.BlockSpec((N_MEM, X_DIM), row), pl.BlockSpec((N_MEM, X_DIM), row)],
        out_shape=[jax.ShapeDtypeStruct((n, X_DIM), BF16), jax.ShapeDtypeStruct((n, X_DIM), BF16)],
        compiler_params=_cparams("parallel"),
        name="mem_kv",
    )(mem2, w["mem_norm_g"], w["w_mk"], w["w_mv"], w["x_kn_g"])


def _xattn_kernel(x_ref, g_ref, wq_ref, gq_ref, mk_ref, mv_ref, o_ref):
    h = _rms(x_ref[...], g_ref[...]).astype(BF16)
    q = _dot(h, wq_ref[...])
    for hd in range(X_HEADS):
        sl = slice(X_HEAD * hd, X_HEAD * (hd + 1))
        qh = _rms(q[:, sl], gq_ref[...]).astype(BF16)
        s = _dot_nt(qh, mk_ref[:, sl])
        p = jnp.exp(s - jnp.max(s, axis=-1, keepdims=True))
        o = _dot(p.astype(BF16), mv_ref[:, sl]) / jnp.sum(p, axis=-1, keepdims=True)
        o_ref[:, sl] = o.astype(BF16)


def _xattn(x2, mk, mv, B, T, w):
    tq = min(512, T)
    nq = T // tq
    return pl.pallas_call(
        _xattn_kernel,
        grid=(B, nq),
        in_specs=[pl.BlockSpec((tq, D_MODEL), lambda b, i: (b * nq + i, 0)), _full((1, D_MODEL)),
                  _full((D_MODEL, X_DIM)), _full((1, X_HEAD)),
                  pl.BlockSpec((N_MEM, X_DIM), lambda b, i: (b, 0)), pl.BlockSpec((N_MEM, X_DIM), lambda b, i: (b, 0))],
        out_specs=pl.BlockSpec((tq, X_DIM), lambda b, i: (b * nq + i, 0)),
        out_shape=jax.ShapeDtypeStruct((B * T, X_DIM), BF16),
        compiler_params=_cparams("parallel", "parallel"),
        name="xattn",
    )(x2, w["norm_mix_g"], w["w_xq"], w["x_qn_g_scaled"], mk, mv)


def _merge_kernel(x_ref, oa_ref, yf_ref, yb_ref, bonus_ref, g7_ref, oc_ref, g_ref, wg_ref, woa_ref, wob_ref,
                  woc_ref, wout_ref, lng_ref, lnb_ref, seg_ref, o_ref):
    x = x_ref[...]
    h = _rms(x, g_ref[...]).astype(BF16)
    seg = seg_ref[...]
    y7 = yf_ref[...] + yb_ref[...] + bonus_ref[...]
    mu = _dot_0_1(y7, seg) * (1.0 / RW_HEAD)
    dy = y7 - mu
    var = _dot_0_1(dy * dy, seg) * (1.0 / RW_HEAD)
    y7 = dy * lax.rsqrt(var + LNX_EPS) * lng_ref[...] + lnb_ref[...]
    yb = (y7 * g7_ref[...]).astype(BF16)
    merged = jnp.zeros(x.shape, F32)
    for bi, (br, wo) in enumerate(((oa_ref[...], woa_ref), (yb, wob_ref), (oc_ref[...], woc_ref))):
        gate = _sigmoid(_dot(h, wg_ref[:, D_MODEL * bi:D_MODEL * (bi + 1)]))
        merged = merged + gate * _dot(br, wo[...])
    o_ref[...] = x + _dot(merged.astype(BF16), wout_ref[...])


def _merge(x2, oa, yf, yb, bonus, g7, oc, T, w):
    ntok = x2.shape[0]
    tm = min(512, T)
    row = lambda i: (i, 0)
    half = lambda: pl.BlockSpec((tm, RW_DIM), row)
    return pl.pallas_call(
        _merge_kernel,
        grid=(ntok // tm,),
        in_specs=[pl.BlockSpec((tm, D_MODEL), row), half(), half(), half(), half(), half(), half(),
                  _full((1, D_MODEL)), _full((D_MODEL, 3 * D_MODEL)), _full((RW_DIM, D_MODEL)),
                  _full((RW_DIM, D_MODEL)), _full((RW_DIM, D_MODEL)), _full((D_MODEL, D_MODEL)),
                  _full((1, RW_DIM)), _full((1, RW_DIM)), _full((RW_DIM, RW_DIM))],
        out_specs=pl.BlockSpec((tm, D_MODEL), row),
        out_shape=jax.ShapeDtypeStruct((ntok, D_MODEL), F32),
        compiler_params=_cparams("parallel"),
        name="merge",
    )(x2, oa, yf, yb, bonus, g7, oc, w["norm_mix_g"], w["w_gate"], w["w_o_a"], w["w_o_b"], w["w_o_c"],
      w["w_out"], w["lnx_g"], w["lnx_b"], w["seg"])


MXU_TILE = 256
FFN_SPLITS = (0, 3 * MXU_TILE, 6 * MXU_TILE, 9 * MXU_TILE, D_FF)


def _ffn_kernel(x_ref, xp_ref, xn_ref, g_ref, wug_ref, wuv_ref, cw_ref, cb_ref, wd_ref, o_ref, h_scr, ug0_scr,
                ug1_scr, *, tm, npos):
    i = pl.program_id(0)
    not_first = (i % npos != 0).astype(F32)
    not_last = (i % npos != npos - 1).astype(F32)
    g = g_ref[...]
    h_scr[0:SUBLANES, :] = (_rms(xp_ref[...], g) * not_first).astype(BF16)
    h_scr[SUBLANES:SUBLANES + tm, :] = _rms(x_ref[...], g).astype(BF16)
    h_scr[SUBLANES + tm:, :] = (_rms(xn_ref[...], g) * not_last).astype(BF16)
    chunks = list(zip(FFN_SPLITS[:-1], FFN_SPLITS[1:]))
    bufs = (ug0_scr, ug1_scr)

    def up(j):
        lo, hi = chunks[j]
        bufs[j % 2][:, 0:hi - lo] = _dot(h_scr[...], wug_ref[:, lo:hi])
        return _dot(h_scr[SUBLANES:SUBLANES + tm, :], wuv_ref[:, lo:hi])

    def down(j, uv):
        lo, hi = chunks[j]
        n = hi - lo
        ug = bufs[j % 2]
        cw = cw_ref[:, lo:hi]
        c = (cw[0:1] * ug[SUBLANES - 1:SUBLANES - 1 + tm, 0:n] + cw[1:2] * ug[SUBLANES:SUBLANES + tm, 0:n]
             + cw[2:3] * ug[SUBLANES + 1:SUBLANES + 1 + tm, 0:n] + cb_ref[:, lo:hi])
        act = 0.5 * c * (1.0 + lax.erf(c * np.float32(1.0 / np.sqrt(2.0)))) * uv
        return _dot(act.astype(BF16), wd_ref[lo:hi, :])

    out = x_ref[...]
    uv = up(0)
    for j in range(len(chunks)):
        uv_next = up(j + 1) if j + 1 < len(chunks) else None
        out = out + down(j, uv)
        uv = uv_next
    o_ref[...] = out


def _ffn(x2, T, w):
    ntok = x2.shape[0]
    tm = min(512, T)
    fc = max(hi - lo for lo, hi in zip(FFN_SPLITS[:-1], FFN_SPLITS[1:]))
    npos = T // tm
    nblk8 = ntok // SUBLANES
    r8 = tm // SUBLANES
    row = lambda i: (i, 0)
    return pl.pallas_call(
        functools.partial(_ffn_kernel, tm=tm, npos=npos),
        grid=(ntok // tm,),
        in_specs=[pl.BlockSpec((tm, D_MODEL), row),
                  pl.BlockSpec((SUBLANES, D_MODEL), lambda i: (jnp.maximum(i * r8 - 1, 0), 0)),
                  pl.BlockSpec((SUBLANES, D_MODEL), lambda i: (jnp.minimum((i + 1) * r8, nblk8 - 1), 0)),
                  _full((1, D_MODEL)), _full((D_MODEL, D_FF)), _full((D_MODEL, D_FF)), _full((3, D_FF)),
                  _full((1, D_FF)), _full((D_FF, D_MODEL))],
        out_specs=pl.BlockSpec((tm, D_MODEL), row),
        out_shape=jax.ShapeDtypeStruct((ntok, D_MODEL), F32),
        scratch_shapes=[pltpu.VMEM((tm + 2 * SUBLANES, D_MODEL), BF16), pltpu.VMEM((tm + 2 * SUBLANES, fc), F32),
                        pltpu.VMEM((tm + 2 * SUBLANES, fc), F32)],
        compiler_params=_cparams("parallel"),
        name="conv_ffn",
    )(x2, x2, x2, w["norm_ffn_g"], w["w_up_gate"], w["w_up_val"], w["conv_w"], w["conv_b"], w["w_down"])


def _prep_weights(p):
    w = {}
    row = lambda a: a.reshape(1, -1).astype(F32)
    w_in = p["w_in"]
    o = np.cumsum([0, Q_LORA, KV_LORA, MLA_ROPE, RW_COLS, X_DIM, 3 * D_MODEL])
    seg = lambda i: w_in[:, o[i]:o[i + 1]]
    zc = lambda n: jnp.zeros((D_MODEL, n), F32)
    w["w_c"] = jnp.concatenate([seg(0), seg(1), zc(MLA_NOPE), seg(2), zc(LANES - MLA_QK)], axis=1).astype(BF16)
    w["w_rw"] = seg(3).astype(BF16)
    w["w_xq"] = seg(4).astype(BF16)
    w["w_gate"] = seg(5).astype(BF16)
    w["norm_mix_g"] = row(p["norm_mix_g"])
    w["q_norm_g"] = row(p["q_norm_g"])
    w["kv_norm_g"] = row(p["kv_norm_g"])
    pad_slot = lambda a: jnp.pad(a, ((0, 0), (0, 0), (0, LANES - a.shape[-1]))).reshape(a.shape[0], -1)
    uq = p["w_uq"].reshape(Q_LORA, MLA_HEADS, MLA_QK)
    w["w_uq"] = pad_slot(uq).astype(BF16)
    half = MLA_ROPE // 2
    uq_rot = jnp.concatenate([jnp.zeros_like(uq[:, :, :MLA_NOPE]), -uq[:, :, MLA_NOPE + half:],
                              uq[:, :, MLA_NOPE:MLA_NOPE + half]], axis=-1)
    w["w_uq_rot"] = pad_slot(uq_rot).astype(BF16)
    ukv =p["w_ukv"].reshape(KV_LORA, MLA_HEADS, MLA_NOPE + MLA_V)
    w["w_uk"] = pad_slot(ukv[:, :, :MLA_NOPE]).astype(BF16)
    w["w_uv"] = pad_slot(ukv[:, :, MLA_NOPE:]).T.astype(BF16)
    w["mla_qn_g_scaled"] = p["mla_qn_g"].reshape(-1) * np.float32(MLA_QK ** -0.5 * np.log2(np.e))
    w["mla_kn_g"] = p["mla_kn_g"].reshape(-1)
    for name in ("mu_prev", "mu_next", "k_k", "k_a", "r_k", "lnx_g", "lnx_b", "mem_norm_g", "x_kn_g", "norm_ffn_g",
                 "conv_b"):
        w[name] = row(p[name])
    w["w0"] = jnp.stack([p["w0_f"], p["w0_b"]]).reshape(2, 1, RW_DIM)
    w["a0"] = jnp.stack([p["a0_f"], p["a0_b"]]).reshape(2, 1, RW_DIM)
    zl = jnp.zeros((LORA, RW_DIM), F32)
    w["w2"] = jnp.stack([jnp.concatenate([p["w2_f"], zl]), jnp.concatenate([zl, p["w2_b"]])]).astype(BF16)
    w["a2"] = jnp.stack([jnp.concatenate([p["a2_f"], zl]), jnp.concatenate([zl, p["a2_b"]])]).astype(BF16)
    w["g2"] = p["g2"].astype(BF16)
    hid = np.arange(RW_DIM) // RW_HEAD
    w["seg"] = jnp.asarray((hid[:, None] == hid[None, :]).astype(np.float32), BF16)
    mkv = p["w_mkv"].reshape(D_MODEL, X_HEADS, 2 * X_HEAD)
    w["w_mk"] = mkv[:, :, :X_HEAD].reshape(D_MODEL, X_DIM).astype(BF16)
    w["w_mv"] = mkv[:, :, X_HEAD:].reshape(D_MODEL, X_DIM).astype(BF16)
    w["x_qn_g_scaled"] = row(p["x_qn_g"]) * np.float32(X_HEAD ** -0.5)
    for name in ("w_o_a", "w_o_b", "w_o_c", "w_out", "w_down"):
        w[name] = p[name].astype(BF16)
    w["w_up_gate"] = p["w_up"][:, :D_FF].astype(BF16)
    w["w_up_val"] = p["w_up"][:, D_FF:].astype(BF16)
    w["conv_w"] = p["conv_w"].astype(F32)
    return w


def _rope_tables(T, gq, gk):
    half = MLA_ROPE // 2
    inv = jnp.power(ROPE_THETA, -jnp.arange(half, dtype=F32) / half)
    ang = jnp.arange(T, dtype=F32)[:, None] * inv[None, :]
    cos, sin = jnp.cos(ang), jnp.sin(ang)
    z = lambda n: jnp.zeros((T, n), F32)
    pad = z(LANES - MLA_QK)

    def own(g):
        g1, g2 = g[MLA_NOPE:MLA_NOPE + half], g[MLA_NOPE + half:]
        return jnp.concatenate([jnp.broadcast_to(g[:MLA_NOPE], (T, MLA_NOPE)), cos * g1, cos * g2, pad], axis=1)

    g1, g2 = gq[MLA_NOPE:MLA_NOPE + half], gq[MLA_NOPE + half:]
    qb = jnp.concatenate([z(MLA_NOPE), sin * g2, sin * g1, pad], axis=1)
    g1, g2 = gk[MLA_NOPE:MLA_NOPE + half], gk[MLA_NOPE + half:]
    kba = jnp.concatenate([z(MLA_NOPE), -sin * g2, z(half), pad], axis=1)
    kbb = jnp.concatenate([z(MLA_NOPE), z(half), sin * g1, pad], axis=1)
    return own(gq), qb, own(gk), kba, kbb


def _layer(x, mem, w):
    B, T, _ = x.shape
    x2 = x.reshape(B * T, D_MODEL)
    q, k, v = _mla_prep(x2, T, w, _rope_tables(T, w["mla_qn_g_scaled"], w["mla_kn_g"]))
    o_a = _flash(q, k, v, B, T)
    v7, kkt, rt, kh, bh, kw, bw, wt, bonus, g7 = _rw_prep(x2, T, w)
    y_f, y_b = _rw_scan(v7, kkt, rt, kh, bh, kw, bw, wt, B, T)
    mk, mv = _mem_kv(mem.reshape(B * N_MEM, D_MODEL), w)
    o_c = _xattn(x2, mk, mv, B, T, w)
    x1 = _merge(x2, o_a, y_f, y_b, bonus, g7, o_c, T, w)
    return _ffn(x1, T, w).reshape(B, T, D_MODEL)


def kernel(x_prompt, x_sample, mem_prompt, mem_sample, norm_mix_g, w_in, q_norm_g, w_uq, kv_norm_g, w_ukv, mla_qn_g, mla_kn_g, w_o_a, mu_prev, mu_next, w0_f, w2_f, a0_f, a2_f, w0_b, w2_b, a0_b, a2_b, g2, k_k, k_a, r_k, lnx_g, lnx_b, w_o_b, mem_norm_g, w_mkv, x_qn_g, x_kn_g, w_o_c, w_out, norm_ffn_g, w_up, conv_w, conv_b, w_down):
    p = dict(norm_mix_g=norm_mix_g, w_in=w_in, q_norm_g=q_norm_g, w_uq=w_uq, kv_norm_g=kv_norm_g, w_ukv=w_ukv,
             mla_qn_g=mla_qn_g, mla_kn_g=mla_kn_g, w_o_a=w_o_a, mu_prev=mu_prev, mu_next=mu_next,
             w0_f=w0_f, w2_f=w2_f, a0_f=a0_f, a2_f=a2_f, w0_b=w0_b, w2_b=w2_b, a0_b=a0_b, a2_b=a2_b,
             g2=g2, k_k=k_k, k_a=k_a, r_k=r_k, lnx_g=lnx_g, lnx_b=lnx_b, w_o_b=w_o_b,
             mem_norm_g=mem_norm_g, w_mkv=w_mkv, x_qn_g=x_qn_g, x_kn_g=x_kn_g, w_o_c=w_o_c, w_out=w_out,
             norm_ffn_g=norm_ffn_g, w_up=w_up, conv_w=conv_w, conv_b=conv_b, w_down=w_down)
    w = _prep_weights({name: a[0] for name, a in p.items()})
    return (_layer(x_prompt, mem_prompt, w), _layer(x_sample, mem_sample, w))
```

```python
import functools

import numpy as np
import jax
import jax.numpy as jnp
from jax import lax
from jax.experimental import pallas as pl
from jax.experimental.pallas import tpu as pltpu

F32 = jnp.float32
BF16 = jnp.bfloat16

D_MODEL = 1024
RMS_EPS = 1e-6
N_MEM = 256
MLA_HEADS = 8
MLA_NOPE = 64
MLA_ROPE = 32
MLA_QK = MLA_NOPE + MLA_ROPE
MLA_V = 64
Q_LORA = 384
KV_LORA = 256
ROPE_THETA = 10000.0
RW_HEADS = 8
RW_HEAD = 64
RW_DIM = RW_HEADS * RW_HEAD
LORA = 64
GATE_LORA = 128
RW_COLS = 3 * RW_DIM + 4 * LORA + GATE_LORA
LNX_EPS = 64e-5
X_HEADS = 4
X_HEAD = 128
X_DIM = X_HEADS * X_HEAD
D_FF = 2816

LANES = 128
SUBLANES = 8
CHUNK = 64
GROUP = 2
GW = GROUP * RW_HEAD
SLOT_COLS = MLA_HEADS * LANES
MLA_C_COLS = Q_LORA + KV_LORA + LANES
SCAN_CHUNKS = 4
VMEM_LIMIT = 56 * 1024 * 1024


def _cparams(*sem):
    return pltpu.CompilerParams(dimension_semantics=sem, vmem_limit_bytes=VMEM_LIMIT)


def _rms(x, g, eps=RMS_EPS):
    return x * lax.rsqrt(jnp.mean(x * x, axis=-1, keepdims=True) + eps) * g


def _dot(a, b):
    return jnp.dot(a, b, preferred_element_type=F32)


def _dot_nt(a, b):
    return lax.dot_general(a, b, (((1,), (1,)), ((), ())), preferred_element_type=F32)


def _dot_tn(a, b):
    return lax.dot_general(a, b, (((0,), (0,)), ((), ())), preferred_element_type=F32)


def _dot_0_1(x, e):
    return _dot(x.astype(BF16), e)


def _dot_0_1_l(e, x):
    hi = x.astype(BF16)
    lo = (x - hi.astype(F32)).astype(BF16)
    return _dot(e, hi) + _dot(e, lo)


def _sigmoid(z):
    return 1.0 / (1.0 + jnp.exp(-z))


def _full(shape):
    nd = len(shape)
    return pl.BlockSpec(shape, lambda *_: (0,) * nd, pipeline_mode=pl.Buffered(1))


def _mla_prep_kernel(x_ref, g_ref, wc_ref, gq_ref, gkv_ref, wuq_ref, wuqr_ref, wuk_ref, wuv_ref,
                     qa_ref, qb_ref, ka_ref, kba_ref, kbb_ref, q_out, k_out, v_out):
    tm = x_ref.shape[0]
    half = MLA_ROPE // 2
    halves = [slice(0, tm // 2), slice(tm // 2, tm)]
    cs = [_dot(_rms(x_ref[r, :], g_ref[...]).astype(BF16), wc_ref[...]) for r in halves]
    proj = []
    for c in cs:
        cq = _rms(c[:, :Q_LORA], gq_ref[...]).astype(BF16)
        ckv = _rms(c[:, Q_LORA:Q_LORA + KV_LORA], gkv_ref[...]).astype(BF16)
        proj.append((_dot(cq, wuq_ref[...]), _dot(cq, wuqr_ref[...]), _dot(ckv, wuk_ref[...]),
                     _dot_nt(wuv_ref[...], ckv)))
    for r, c, (q, qp, k, vt) in zip(halves, cs, proj):
        slot_row = lax.broadcasted_iota(jnp.int32, vt.shape, 0) % LANES
        v_out[0, :, r] = jnp.where(slot_row == MLA_V, 1.0, vt).astype(BF16)
        kr = c[:, Q_LORA + KV_LORA:]
        qa = qa_ref[r, :]
        qb = qb_ref[r, :]
        ka = ka_ref[r, :]
        kr_roped = (kr * ka + pltpu.roll(kr, LANES - half, 1) * kba_ref[r, :]
                    + pltpu.roll(kr, half, 1) * kbb_ref[r, :])
        kr_ss = jnp.sum(kr * kr, axis=-1, keepdims=True)
        for hd in range(MLA_HEADS):
            sl = slice(LANES * hd, LANES * (hd + 1))
            t = q[:, sl]
            ss = jnp.sum(t * t, axis=-1, keepdims=True) * (1.0 / MLA_QK)
            q_out[r, sl] = ((t * qa + qp[:, sl] * qb) * lax.rsqrt(ss + RMS_EPS)).astype(BF16)
            t = k[:, sl]
            ss = (jnp.sum(t * t, axis=-1, keepdims=True) + kr_ss) * (1.0 / MLA_QK)
            k_out[r, sl] = ((t * ka + kr_roped) * lax.rsqrt(ss + RMS_EPS)).astype(BF16)


def _mla_prep(x2, T, w, tabs):
    ntok = x2.shape[0]
    tm = _attn_tile(T)
    npos = T // tm
    row = lambda i: (i, 0)
    pos = lambda i: (i % npos, 0)
    return pl.pallas_call(
        _mla_prep_kernel,
        grid=(ntok // tm,),
        in_specs=[pl.BlockSpec((tm, D_MODEL), row), _full((1, D_MODEL)), _full((D_MODEL, MLA_C_COLS)),
                  _full((1, Q_LORA)), _full((1, KV_LORA)), _full((Q_LORA, SLOT_COLS)), _full((Q_LORA, SLOT_COLS)),
                  _full((KV_LORA, SLOT_COLS)), _full((SLOT_COLS, KV_LORA))] + [pl.BlockSpec((tm, LANES), pos)] * 5,
        out_specs=[pl.BlockSpec((tm, SLOT_COLS), row), pl.BlockSpec((tm, SLOT_COLS), row),
                   pl.BlockSpec((1, SLOT_COLS, tm), lambda i: (i, 0, 0))],
        out_shape=[jax.ShapeDtypeStruct((ntok, SLOT_COLS), BF16), jax.ShapeDtypeStruct((ntok, SLOT_COLS), BF16),
                   jax.ShapeDtypeStruct((ntok // tm, SLOT_COLS, tm), BF16)],
        compiler_params=_cparams("parallel"),
        name="mla_prep",
    )(x2, w["norm_mix_g"], w["w_c"], w["q_norm_g"], w["kv_norm_g"], w["w_uq"], w["w_uq_rot"], w["w_uk"],
      w["w_uv"], *tabs)


def _flash_kernel(q_ref, k_ref, vt_ref, o_ref, s0_ref, s1_ref, *, tk, nk):
    tq = q_ref.shape[0]
    slots = [slice(LANES * h, LANES * (h + 1)) for h in range(2)]
    per = tk // vt_ref.shape[2]

    def scores(j, s_ref):
        kblk = k_ref[pl.ds(pl.multiple_of(j * tk, tk), tk), :]
        for h in range(2):
            s_ref[h] = _dot_nt(kblk[:, slots[h]], q_ref[:, slots[h]])

    def consume(j, s_ref, state):
        vt = jnp.concatenate([vt_ref[per * j + r] for r in range(per)], axis=1)
        m_news = [jnp.maximum(state[h][0], jnp.max(s_ref[h], axis=0, keepdims=True)) for h in range(2)]
        pts = [jnp.exp2((s_ref[h] - m_news[h]).astype(BF16)) for h in range(2)]
        return tuple((m_news[h], jnp.exp2(state[h][0] - m_news[h]) * state[h][1] + _dot(vt[slots[h], :], pts[h]))
                     for h in range(2))

    def body(t, state):
        j = 2 * t
        scores(j + 1, s1_ref)
        state = consume(j, s0_ref, state)
        scores(j + 2, s0_ref)
        return consume(j + 1, s1_ref, state)

    state = tuple((jnp.full((1, tq), -jnp.inf, F32), jnp.zeros((LANES, tq), F32)) for _ in range(2))
    scores(0, s0_ref)
    state = lax.fori_loop(0, nk // 2 - 1, body, state)
    scores(nk - 1, s1_ref)
    state = consume(nk - 2, s0_ref, state)
    state = consume(nk - 1, s1_ref, state)
    outs = [acc[0:MLA_V, :] / acc[MLA_V:MLA_V + 1, :] for _, acc in state]
    o_ref[...] = jnp.concatenate(outs, axis=0).T.astype(BF16)


def _attn_tile(T):
    return min(512, T // 4)


def _flash(q, k, vt, B, T):
    tv = _attn_tile(T)
    tq, tk = (tv // 2, 2 * tv) if T // tv >= 16 else (tv, tv)
    nq = T // tq
    nk = T // tk
    assert nk % 2 == 0 and tk % tv == 0, (T, tk, tv)
    return pl.pallas_call(
        functools.partial(_flash_kernel, tk=tk, nk=nk),
        grid=(B, MLA_HEADS // 2, nq),
        in_specs=[pl.BlockSpec((tq, 2 * LANES), lambda b, hp, i: (b * nq + i, hp)),
                  pl.BlockSpec((T, 2 * LANES), lambda b, hp, i: (b, hp)),
                  pl.BlockSpec((T // tv, 2 * LANES, tv), lambda b, hp, i: (b, hp, 0))],
        out_specs=pl.BlockSpec((tq, LANES), lambda b, hp, i: (b * nq + i, hp)),
        out_shape=jax.ShapeDtypeStruct((B * T, MLA_HEADS * MLA_V), BF16),
        scratch_shapes=[pltpu.VMEM((2, tk, tq), F32), pltpu.VMEM((2, tk, tq), F32)],
        compiler_params=_cparams("parallel", "parallel", "arbitrary"),
        name="mla_flash",
    )(q, k, vt)


def _rw_prep_kernel(x_ref, xp_ref, xn_ref, g_ref, wrw_ref, mup_ref, mun_ref, kk_ref, ka_ref, rk_ref,
                    w0_ref, w2_ref, a0_ref, a2_ref, g2_ref, seg_ref, tri_ref,
                    v_out, kkt_out, rt_out, kh_out, bh_out, kw_out, bw_out, wt_out, bonus_out, g7_out,
                    h_scr, rw_scr, *, tm, npos):
    i = pl.program_id(0)
    not_first = (i % npos != 0).astype(F32)
    not_last = (i % npos != npos - 1).astype(F32)
    g = g_ref[...]
    h_scr[0:SUBLANES, :] = (_rms(xp_ref[...], g) * not_first).astype(BF16)
    h_scr[SUBLANES:SUBLANES + tm, :] = _rms(x_ref[...], g).astype(BF16)
    h_scr[SUBLANES + tm:, :] = (_rms(xn_ref[...], g) * not_last).astype(BF16)
    rw_scr[...] = _dot(h_scr[...], wrw_ref[...])
    cur = rw_scr[SUBLANES:SUBLANES + tm, :]
    prev = rw_scr[SUBLANES - 1:SUBLANES - 1 + tm, :]
    nxt = rw_scr[SUBLANES + 1:SUBLANES + 1 + tm, :]
    mup = mup_ref[...]
    mun = mun_ref[...]
    rwf = cur * (1.0 - mup - mun) + mup * prev + mun * nxt
    r7 = rwf[:, 0:RW_DIM]
    k7 = rwf[:, RW_DIM:2 * RW_DIM]
    v7 = rwf[:, 2 * RW_DIM:3 * RW_DIM]
    wl = rwf[:, 3 * RW_DIM:3 * RW_DIM + 2 * LORA]
    al = rwf[:, 3 * RW_DIM + 2 * LORA:3 * RW_DIM + 4 * LORA]
    gl = rwf[:, 3 * RW_DIM + 4 * LORA:]
    seg = seg_ref[...]
    kx = k7 * kk_ref[...]
    kkn = kx * lax.rsqrt(jnp.maximum(_dot_0_1(kx * kx, seg), 1e-24))
    v_out[...] = v7.astype(BF16)
    g7_out[...] = _dot(_sigmoid(gl).astype(BF16), g2_ref[...])
    tw = jnp.tanh(wl).astype(BF16)
    alb = al.astype(BF16)
    bonus = jnp.zeros((tm, RW_DIM), F32)
    for d in range(2):
        z = -(w0_ref[d] + _dot(tw, w2_ref[d]))
        softplus = jnp.maximum(z, 0.0) + jnp.log(1.0 + jnp.exp(-jnp.abs(z)))
        lw = -jnp.exp(-softplus - 0.5)
        a = _sigmoid(a0_ref[d] + _dot(alb, a2_ref[d]))
        kd = k7 * (1.0 + (a - 1.0) * ka_ref[...])
        b = kkn * a
        bonus = bonus + _dot_0_1(r7 * kd * rk_ref[...], seg) * v7
        cum = _dot_0_1_l(tri_ref[d], lw)
        last = CHUNK - 1 if d == 0 else 0
        tot_rows = [cum[CHUNK * ci + last:CHUNK * ci + last + 1, :] for ci in range(tm // CHUNK)]
        tot = jnp.concatenate([jnp.broadcast_to(r, (CHUNK, RW_DIM)) for r in tot_rows], axis=0)
        w_incl = jnp.exp(cum)
        w_excl = jnp.exp(cum - lw)
        w_inv = jnp.exp(-cum)
        w_rest = jnp.exp(tot - cum)
        kkt_out[d] = (kkn * w_excl).astype(BF16)
        rt_out[d] = (r7 * w_incl).astype(BF16)
        kh_out[d] = (kd * w_inv).astype(BF16)
        bh_out[d] = (b * w_inv).astype(BF16)
        kw_out[d] = (kd * w_rest).astype(BF16)
        bw_out[d] = (b * w_rest).astype(BF16)
        for ci in range(tm // CHUNK):
            wt_out[d, ci] = jnp.exp(tot_rows[ci])
    bonus_out[...] = bonus


def _rw_prep(x2, T, w):
    ntok = x2.shape[0]
    tm = min(256, T)
    npos = T // tm
    nblk8 = ntok // SUBLANES
    r8 = tm // SUBLANES
    row = lambda i: (i, 0)
    drow = lambda i: (0, i, 0)
    dspec = pl.BlockSpec((2, tm, RW_DIM), drow)
    dshape = jax.ShapeDtypeStruct((2, ntok, RW_DIM), BF16)
    tri = _chunk_masks(tm)
    outs = pl.pallas_call(
        functools.partial(_rw_prep_kernel, tm=tm, npos=npos),
        grid=(ntok // tm,),
        in_specs=[pl.BlockSpec((tm, D_MODEL), row),
                  pl.BlockSpec((SUBLANES, D_MODEL), lambda i: (jnp.maximum(i * r8 - 1, 0), 0)),
                  pl.BlockSpec((SUBLANES, D_MODEL), lambda i: (jnp.minimum((i + 1) * r8, nblk8 - 1), 0)),
                  _full((1, D_MODEL)), _full((D_MODEL, RW_COLS)), _full((1, RW_COLS)), _full((1, RW_COLS)),
                  _full((1, RW_DIM)), _full((1, RW_DIM)), _full((1, RW_DIM)),
                  _full((2, 1, RW_DIM)), _full((2, 2 * LORA, RW_DIM)), _full((2, 1, RW_DIM)),
                  _full((2, 2 * LORA, RW_DIM)), _full((GATE_LORA, RW_DIM)), _full((RW_DIM, RW_DIM)),
                  _full((2, tm, tm))],
        out_specs=[pl.BlockSpec((tm, RW_DIM), row), dspec, dspec, dspec, dspec, dspec, dspec,
                   pl.BlockSpec((2, tm // CHUNK, 1, RW_DIM), lambda i: (0, i, 0, 0)),
                   pl.BlockSpec((tm, RW_DIM), row), pl.BlockSpec((tm, RW_DIM), row)],
        out_shape=[jax.ShapeDtypeStruct((ntok, RW_DIM), BF16), dshape, dshape, dshape, dshape, dshape, dshape,
                   jax.ShapeDtypeStruct((2, ntok // CHUNK, 1, RW_DIM), F32),
                   jax.ShapeDtypeStruct((ntok, RW_DIM), F32), jax.ShapeDtypeStruct((ntok, RW_DIM), F32)],
        scratch_shapes=[pltpu.VMEM((tm + 2 * SUBLANES, D_MODEL), BF16), pltpu.VMEM((tm + 2 * SUBLANES, RW_COLS), F32)],
        compiler_params=_cparams("parallel"),
        name="rw_prep",
    )(x2, x2, x2, w["norm_mix_g"], w["w_rw"], w["mu_prev"], w["mu_next"], w["k_k"], w["k_a"], w["r_k"],
      w["w0"], w["w2"], w["a0"], w["a2"], w["g2"], w["seg"], tri)
    return outs


def _chunk_masks(tm):
    t = np.arange(tm)
    same = (t[:, None] // CHUNK) == (t[None, :] // CHUNK)
    fwd = same & (t[None, :] <= t[:, None])
    bwd = same & (t[None, :] >= t[:, None])
    return jnp.asarray(np.stack([fwd, bwd]).astype(np.float32), BF16)


def _rw_scan_kernel(*refs):
    ins = refs[:16]
    yf_ref, yb_ref, s_ref = refs[16:]
    c = pl.program_id(1)

    @pl.when(c == 0)
    def _():
        s_ref[...] = jnp.zeros(s_ref.shape, F32)

    t_pos = lax.broadcasted_iota(jnp.int32, (CHUNK, GW), 0)
    s_pos = lax.broadcasted_iota(jnp.int32, (CHUNK, GW), 1) % CHUNK
    eye_c = (t_pos == s_pos).astype(F32)
    ri = lax.broadcasted_iota(jnp.int32, (GW, GW), 0)
    ci = lax.broadcasted_iota(jnp.int32, (GW, GW), 1)
    head_mask = (ri // CHUNK) == (ci // RW_HEAD)
    stack = lambda a: jnp.where(head_mask, jnp.concatenate([a] * GROUP, axis=0), jnp.zeros((), BF16))

    def prepare(orders):
        items = []
        for d, gi, order in ((d, gi, o) for d in range(2) for gi in range(RW_HEADS // GROUP) for o in orders):
            kkt, rt, kh, bh, kw, bw, v, wt = ins[8 * d:8 * d + 8]
            strict = (s_pos < t_pos) if d == 0 else (s_pos > t_pos)
            incl = (s_pos <= t_pos) if d == 0 else (s_pos >= t_pos)
            sub = order if d == 0 else SCAN_CHUNKS - 1 - order
            rows = slice(CHUNK * sub, CHUNK * (sub + 1))
            sl = slice(GW * gi, GW * (gi + 1))
            it = dict(y_ref=(yf_ref, yb_ref)[d], sl=sl, rows=rows, s_ref=s_ref.at[d, gi], wtot=wt[sub][:, sl],
                      order=order)
            v_c = v[rows, sl]
            kkt_rt = jnp.concatenate([kkt[rows, sl], rt[rows, sl]], axis=0)
            a_all = _dot_nt(kkt_rt, jnp.concatenate([stack(kh[rows, sl]), stack(bh[rows, sl])], axis=0))
            ak = jnp.where(strict, a_all[:CHUNK, :GW], 0.0).astype(BF16)
            bk = jnp.where(incl, a_all[CHUNK:, :GW], 0.0).astype(BF16)
            it.update(kkt_rt=kkt_rt, v=v_c, kbw=jnp.concatenate([kw[rows, sl], bw[rows, sl]], axis=0),
                      abk=jnp.concatenate([ak, bk], axis=0), bb=jnp.where(incl, a_all[CHUNK:, GW:], 0.0).astype(BF16))
            ab = jnp.where(strict, a_all[:CHUNK, GW:], 0.0)
            it.update(tinv=eye_c - ab, pw=ab.astype(BF16))
            items.append(it)
        for it in items:
            it["abk_v"] = _dot(it["abk"], stack(it["v"]))
        for it in items:
            it["pw"] = _dot(it["pw"], stack(it["pw"])).astype(BF16)
        for k in range(5):
            for it in items:
                sk = stack(it["pw"])
                if k < 4:
                    res = _dot(jnp.concatenate([it["pw"], it["tinv"].astype(BF16)], axis=0), sk)
                    it["pw"] = res[:CHUNK].astype(BF16)
                    it["tinv"] = it["tinv"] + res[CHUNK:]
                else:
                    it["tinv"] = (it["tinv"] + _dot(it["tinv"].astype(BF16), sk)).astype(BF16)
        return items

    def advance(chains):
        for ch in chains:
            sb = ch["s_ref"][...].astype(BF16)
            ch["zs"] = _dot_nt(ch["kkt_rt"], sb) + ch["abk_v"]
        for ch in chains:
            ch["u"] = (-_dot(ch["tinv"], stack(ch["zs"][:CHUNK].astype(BF16)))).astype(BF16)
        for ch in chains:
            ch["y_ref"][ch["rows"], ch["sl"]] = ch["zs"][CHUNK:] + _dot(ch["bb"], stack(ch["u"]))
        for ch in chains:
            upd = _dot_tn(jnp.concatenate([ch["v"], ch["u"]], axis=0), ch["kbw"])
            ch["s_ref"][...] = ch["s_ref"][...] * ch["wtot"] + jnp.where(head_mask, upd, 0.0)

    prepared = prepare(range(SCAN_CHUNKS))
    for order in range(SCAN_CHUNKS):
        advance([it for it in prepared if it["order"] == order])


def _rw_scan(v, kkt, rt, kh, bh, kw, bw, wt, B, T):
    ntok = B * T
    rows = SCAN_CHUNKS * CHUNK
    nb = T // rows
    in_specs = []
    args = []
    for d in range(2):
        if d == 0:
            cm = lambda b, c: b * nb + c
        else:
            cm = lambda b, c: b * nb + (nb - 1 - c)
        for arr in (kkt, rt, kh, bh, kw, bw):
            in_specs.append(pl.BlockSpec((None, rows, RW_DIM), lambda b, c, cm=cm, d=d: (d, cm(b, c), 0)))
            args.append(arr)
        in_specs.append(pl.BlockSpec((rows, RW_DIM), lambda b, c, cm=cm: (cm(b, c), 0)))
        args.append(v)
        in_specs.append(pl.BlockSpec((None, SCAN_CHUNKS, 1, RW_DIM), lambda b, c, cm=cm, d=d: (d, cm(b, c), 0, 0)))
        args.append(wt)
    return pl.pallas_call(
        _rw_scan_kernel,
        grid=(B, nb),
        in_specs=in_specs,
        out_specs=[pl.BlockSpec((rows, RW_DIM), lambda b, c: (b * nb + c, 0)),
                   pl.BlockSpec((rows, RW_DIM), lambda b, c: (b * nb + (nb - 1 - c), 0))],
        out_shape=[jax.ShapeDtypeStruct((ntok, RW_DIM), F32), jax.ShapeDtypeStruct((ntok, RW_DIM), F32)],
        scratch_shapes=[pltpu.VMEM((2, RW_HEADS // GROUP, GW, GW), F32)],
        compiler_params=_cparams("parallel", "arbitrary"),
        name="rw_scan",
    )(*args)


def _mem_kv_kernel(m_ref, g_ref, wk_ref, wv_ref, gk_ref, k_out, v_out):
    m = _rms(m_ref[...], g_ref[...]).astype(BF16)
    k = _dot(m, wk_ref[...])
    v_out[...] = _dot(m, wv_ref[...]).astype(BF16)
    for hd in range(X_HEADS):
        sl = slice(X_HEAD * hd, X_HEAD * (hd + 1))
        k_out[:, sl] = _rms(k[:, sl], gk_ref[...]).astype(BF16)


def _mem_kv(mem2, w):
    n = mem2.shape[0]
    row = lambda i: (i, 0)
    return pl.pallas_call(
        _mem_kv_kernel,
        grid=(n // N_MEM,),
        in_specs=[pl.BlockSpec((N_MEM, D_MODEL), row), _full((1, D_MODEL)), _full((D_MODEL, X_DIM)),
                  _full((D_MODEL, X_DIM)), _full((1, X_HEAD))],
        out_specs=[pl.BlockSpec((N_MEM, X_DIM), row), pl.BlockSpec((N_MEM, X_DIM), row)],
        out_shape=[jax.ShapeDtypeStruct((n, X_DIM), BF16), jax.ShapeDtypeStruct((n, X_DIM), BF16)],
        compiler_params=_cparams("parallel"),
        name="mem_kv",
    )(mem2, w["mem_norm_g"], w["w_mk"], w["w_mv"], w["x_kn_g"])


def _xattn_kernel(x_ref, g_ref, wq_ref, gq_ref, mk_ref, mv_ref, o_ref):
    h = _rms(x_ref[...], g_ref[...]).astype(BF16)
    q = _dot(h, wq_ref[...])
    for hd in range(X_HEADS):
        sl = slice(X_HEAD * hd, X_HEAD * (hd + 1))
        qh = _rms(q[:, sl], gq_ref[...]).astype(BF16)
        s = _dot_nt(qh, mk_ref[:, sl])
        p = jnp.exp(s - jnp.max(s, axis=-1, keepdims=True))
        o = _dot(p.astype(BF16), mv_ref[:, sl]) / jnp.sum(p, axis=-1, keepdims=True)
        o_ref[:, sl] = o.astype(BF16)


def _xattn(x2, mk, mv, B, T, w):
    tq = min(512, T)
    nq = T // tq
    return pl.pallas_call(
        _xattn_kernel,
        grid=(B, nq),
        in_specs=[pl.BlockSpec((tq, D_MODEL), lambda b, i: (b * nq + i, 0)), _full((1, D_MODEL)),
                  _full((D_MODEL, X_DIM)), _full((1, X_HEAD)),
                  pl.BlockSpec((N_MEM, X_DIM), lambda b, i: (b, 0)), pl.BlockSpec((N_MEM, X_DIM), lambda b, i: (b, 0))],
        out_specs=pl.BlockSpec((tq, X_DIM), lambda b, i: (b * nq + i, 0)),
        out_shape=jax.ShapeDtypeStruct((B * T, X_DIM), BF16),
        compiler_params=_cparams("parallel", "parallel"),
        name="xattn",
    )(x2, w["norm_mix_g"], w["w_xq"], w["x_qn_g_scaled"], mk, mv)


def _merge_kernel(x_ref, oa_ref, yf_ref, yb_ref, bonus_ref, g7_ref, oc_ref, g_ref, wg_ref, woa_ref, wob_ref,
                  woc_ref, wout_ref, lng_ref, lnb_ref, seg_ref, o_ref):
    x = x_ref[...]
    h = _rms(x, g_ref[...]).astype(BF16)
    seg = seg_ref[...]
    gate = lambda bi: _sigmoid(_dot(h, wg_ref[:, D_MODEL * bi:D_MODEL * (bi + 1)]))
    y7 = yf_ref[...] + yb_ref[...] + bonus_ref[...]
    mu = _dot_0_1(y7, seg) * (1.0 / RW_HEAD)
    merged = gate(0) * _dot(oa_ref[...], woa_ref[...])
    dy = y7 - mu
    var = _dot_0_1(dy * dy, seg) * (1.0 / RW_HEAD)
    merged = merged + gate(2) * _dot(oc_ref[...], woc_ref[...])
    gate_b = gate(1)
    y7 = dy * lax.rsqrt(var + LNX_EPS) * lng_ref[...] + lnb_ref[...]
    merged = merged + gate_b * _dot((y7 * g7_ref[...]).astype(BF16), wob_ref[...])
    o_ref[...] = x + _dot(merged.astype(BF16), wout_ref[...])


def _merge(x2, oa, yf, yb, bonus, g7, oc, T, w):
    ntok = x2.shape[0]
    tm = min(512, T)
    row = lambda i: (i, 0)
    half = lambda: pl.BlockSpec((tm, RW_DIM), row)
    return pl.pallas_call(
        _merge_kernel,
        grid=(ntok // tm,),
        in_specs=[pl.BlockSpec((tm, D_MODEL), row), half(), half(), half(), half(), half(), half(),
                  _full((1, D_MODEL)), _full((D_MODEL, 3 * D_MODEL)), _full((RW_DIM, D_MODEL)),
                  _full((RW_DIM, D_MODEL)), _full((RW_DIM, D_MODEL)), _full((D_MODEL, D_MODEL)),
                  _full((1, RW_DIM)), _full((1, RW_DIM)), _full((RW_DIM, RW_DIM))],
        out_specs=pl.BlockSpec((tm, D_MODEL), row),
        out_shape=jax.ShapeDtypeStruct((ntok, D_MODEL), F32),
        compiler_params=_cparams("parallel"),
        name="merge",
    )(x2, oa, yf, yb, bonus, g7, oc, w["norm_mix_g"], w["w_gate"], w["w_o_a"], w["w_o_b"], w["w_o_c"],
      w["w_out"], w["lnx_g"], w["lnx_b"], w["seg"])


MXU_TILE = 256
FFN_SPLITS = (0, 5 * MXU_TILE, D_FF)


def _ffn_kernel(x_ref, xp_ref, xn_ref, g_ref, wug_ref, wuv_ref, cw_ref, cb_ref, wd_ref, o_ref, h_scr, ug_scr,
                *, tm, npos):
    i = pl.program_id(0)
    not_first = (i % npos != 0).astype(F32)
    not_last = (i % npos != npos - 1).astype(F32)
    g = g_ref[...]
    h_scr[0:SUBLANES, :] = (_rms(xp_ref[...], g) * not_first).astype(BF16)
    h_scr[SUBLANES:SUBLANES + tm, :] = _rms(x_ref[...], g).astype(BF16)
    h_scr[SUBLANES + tm:, :] = (_rms(xn_ref[...], g) * not_last).astype(BF16)
    out = x_ref[...]
    for lo, hi in zip(FFN_SPLITS[:-1], FFN_SPLITS[1:]):
        n = hi - lo
        ug_scr[:, 0:n] = _dot(h_scr[...], wug_ref[:, lo:hi])
        uv = _dot(h_scr[SUBLANES:SUBLANES + tm, :], wuv_ref[:, lo:hi])
        cw = cw_ref[:, lo:hi]
        c = (cw[0:1] * ug_scr[SUBLANES - 1:SUBLANES - 1 + tm, 0:n] + cw[1:2] * ug_scr[SUBLANES:SUBLANES + tm, 0:n]
             + cw[2:3] * ug_scr[SUBLANES + 1:SUBLANES + 1 + tm, 0:n] + cb_ref[:, lo:hi])
        act = 0.5 * c * (1.0 + lax.erf(c * np.float32(1.0 / np.sqrt(2.0)))) * uv
        out = out + _dot(act.astype(BF16), wd_ref[lo:hi, :])
    o_ref[...] = out


def _ffn(x2, T, w):
    ntok = x2.shape[0]
    tm = min(512, T)
    fc = max(hi - lo for lo, hi in zip(FFN_SPLITS[:-1], FFN_SPLITS[1:]))
    npos = T // tm
    nblk8 = ntok // SUBLANES
    r8 = tm // SUBLANES
    row = lambda i: (i, 0)
    return pl.pallas_call(
        functools.partial(_ffn_kernel, tm=tm, npos=npos),
        grid=(ntok // tm,),
        in_specs=[pl.BlockSpec((tm, D_MODEL), row),
                  pl.BlockSpec((SUBLANES, D_MODEL), lambda i: (jnp.maximum(i * r8 - 1, 0), 0)),
                  pl.BlockSpec((SUBLANES, D_MODEL), lambda i: (jnp.minimum((i + 1) * r8, nblk8 - 1), 0)),
                  _full((1, D_MODEL)), _full((D_MODEL, D_FF)), _full((D_MODEL, D_FF)), _full((3, D_FF)),
                  _full((1, D_FF)), _full((D_FF, D_MODEL))],
        out_specs=pl.BlockSpec((tm, D_MODEL), row),
        out_shape=jax.ShapeDtypeStruct((ntok, D_MODEL), F32),
        scratch_shapes=[pltpu.VMEM((tm + 2 * SUBLANES, D_MODEL), BF16), pltpu.VMEM((tm + 2 * SUBLANES, fc), F32)],
        compiler_params=_cparams("parallel"),
        name="conv_ffn",
    )(x2, x2, x2, w["norm_ffn_g"], w["w_up_gate"], w["w_up_val"], w["conv_w"], w["conv_b"], w["w_down"])


def _prep_weights(p):
    w = {}
    row = lambda a: a.reshape(1, -1).astype(F32)
    w_in = p["w_in"]
    o = np.cumsum([0, Q_LORA, KV_LORA, MLA_ROPE, RW_COLS, X_DIM, 3 * D_MODEL])
    seg = lambda i: w_in[:, o[i]:o[i + 1]]
    zc = lambda n: jnp.zeros((D_MODEL, n), F32)
    w["w_c"] = jnp.concatenate([seg(0), seg(1), zc(MLA_NOPE), seg(2), zc(LANES - MLA_QK)], axis=1).astype(BF16)
    w["w_rw"] = seg(3).astype(BF16)
    w["w_xq"] = seg(4).astype(BF16)
    w["w_gate"] = seg(5).astype(BF16)
    w["norm_mix_g"] = row(p["norm_mix_g"])
    w["q_norm_g"] = row(p["q_norm_g"])
    w["kv_norm_g"] = row(p["kv_norm_g"])
    pad_slot = lambda a: jnp.pad(a, ((0, 0), (0, 0), (0, LANES - a.shape[-1]))).reshape(a.shape[0], -1)
    uq = p["w_uq"].reshape(Q_LORA, MLA_HEADS, MLA_QK)
    w["w_uq"] = pad_slot(uq).astype(BF16)
    half = MLA_ROPE // 2
    uq_rot = jnp.concatenate([jnp.zeros_like(uq[:, :, :MLA_NOPE]), -uq[:, :, MLA_NOPE + half:],
                              uq[:, :, MLA_NOPE:MLA_NOPE + half]], axis=-1)
    w["w_uq_rot"] = pad_slot(uq_rot).astype(BF16)
    ukv =p["w_ukv"].reshape(KV_LORA, MLA_HEADS, MLA_NOPE + MLA_V)
    w["w_uk"] = pad_slot(ukv[:, :, :MLA_NOPE]).astype(BF16)
    w["w_uv"] = pad_slot(ukv[:, :, MLA_NOPE:]).T.astype(BF16)
    w["mla_qn_g_scaled"] = p["mla_qn_g"].reshape(-1) * np.float32(MLA_QK ** -0.5 * np.log2(np.e))
    w["mla_kn_g"] = p["mla_kn_g"].reshape(-1)
    for name in ("mu_prev", "mu_next", "k_k", "k_a", "r_k", "lnx_g", "lnx_b", "mem_norm_g", "x_kn_g", "norm_ffn_g",
                 "conv_b"):
        w[name] = row(p[name])
    w["w0"] = jnp.stack([p["w0_f"], p["w0_b"]]).reshape(2, 1, RW_DIM)
    w["a0"] = jnp.stack([p["a0_f"], p["a0_b"]]).reshape(2, 1, RW_DIM)
    zl = jnp.zeros((LORA, RW_DIM), F32)
    w["w2"] = jnp.stack([jnp.concatenate([p["w2_f"], zl]), jnp.concatenate([zl, p["w2_b"]])]).astype(BF16)
    w["a2"] = jnp.stack([jnp.concatenate([p["a2_f"], zl]), jnp.concatenate([zl, p["a2_b"]])]).astype(BF16)
    w["g2"] = p["g2"].astype(BF16)
    hid = np.arange(RW_DIM) // RW_HEAD
    w["seg"] = jnp.asarray((hid[:, None] == hid[None, :]).astype(np.float32), BF16)
    mkv = p["w_mkv"].reshape(D_MODEL, X_HEADS, 2 * X_HEAD)
    w["w_mk"] = mkv[:, :, :X_HEAD].reshape(D_MODEL, X_DIM).astype(BF16)
    w["w_mv"] = mkv[:, :, X_HEAD:].reshape(D_MODEL, X_DIM).astype(BF16)
    w["x_qn_g_scaled"] = row(p["x_qn_g"]) * np.float32(X_HEAD ** -0.5)
    for name in ("w_o_a", "w_o_b", "w_o_c", "w_out", "w_down"):
        w[name] = p[name].astype(BF16)
    w["w_up_gate"] = p["w_up"][:, :D_FF].astype(BF16)
    w["w_up_val"] = p["w_up"][:, D_FF:].astype(BF16)
    w["conv_w"] = p["conv_w"].astype(F32)
    return w


def _rope_tables(T, gq, gk):
    half = MLA_ROPE // 2
    inv = jnp.power(ROPE_THETA, -jnp.arange(half, dtype=F32) / half)
    ang = jnp.arange(T, dtype=F32)[:, None] * inv[None, :]
    cos, sin = jnp.cos(ang), jnp.sin(ang)
    z = lambda n: jnp.zeros((T, n), F32)
    pad = z(LANES - MLA_QK)

    def own(g):
        g1, g2 = g[MLA_NOPE:MLA_NOPE + half], g[MLA_NOPE + half:]
        return jnp.concatenate([jnp.broadcast_to(g[:MLA_NOPE], (T, MLA_NOPE)), cos * g1, cos * g2, pad], axis=1)

    g1, g2 = gq[MLA_NOPE:MLA_NOPE + half], gq[MLA_NOPE + half:]
    qb = jnp.concatenate([z(MLA_NOPE), sin * g2, sin * g1, pad], axis=1)
    g1, g2 = gk[MLA_NOPE:MLA_NOPE + half], gk[MLA_NOPE + half:]
    kba = jnp.concatenate([z(MLA_NOPE), -sin * g2, z(half), pad], axis=1)
    kbb = jnp.concatenate([z(MLA_NOPE), z(half), sin * g1, pad], axis=1)
    return own(gq), qb, own(gk), kba, kbb


def _layer(x, mem, w):
    B, T, _ = x.shape
    x2 = x.reshape(B * T, D_MODEL)
    q, k, v = _mla_prep(x2, T, w, _rope_tables(T, w["mla_qn_g_scaled"], w["mla_kn_g"]))
    o_a = _flash(q, k, v, B, T)
    v7, kkt, rt, kh, bh, kw, bw, wt, bonus, g7 = _rw_prep(x2, T, w)
    y_f, y_b = _rw_scan(v7, kkt, rt, kh, bh, kw, bw, wt, B, T)
    mk, mv = _mem_kv(mem.reshape(B * N_MEM, D_MODEL), w)
    o_c = _xattn(x2, mk, mv, B, T, w)
    x1 = _merge(x2, o_a, y_f, y_b, bonus, g7, o_c, T, w)
    return _ffn(x1, T, w).reshape(B, T, D_MODEL)


def kernel(x_prompt, x_sample, mem_prompt, mem_sample, norm_mix_g, w_in, q_norm_g, w_uq, kv_norm_g, w_ukv, mla_qn_g, mla_kn_g, w_o_a, mu_prev, mu_next, w0_f, w2_f, a0_f, a2_f, w0_b, w2_b, a0_b, a2_b, g2, k_k, k_a, r_k, lnx_g, lnx_b, w_o_b, mem_norm_g, w_mkv, x_qn_g, x_kn_g, w_o_c, w_out, norm_ffn_g, w_up, conv_w, conv_b, w_down):
    p = dict(norm_mix_g=norm_mix_g, w_in=w_in, q_norm_g=q_norm_g, w_uq=w_uq, kv_norm_g=kv_norm_g, w_ukv=w_ukv,
             mla_qn_g=mla_qn_g, mla_kn_g=mla_kn_g, w_o_a=w_o_a, mu_prev=mu_prev, mu_next=mu_next,
             w0_f=w0_f, w2_f=w2_f, a0_f=a0_f, a2_f=a2_f, w0_b=w0_b, w2_b=w2_b, a0_b=a0_b, a2_b=a2_b,
             g2=g2, k_k=k_k, k_a=k_a, r_k=r_k, lnx_g=lnx_g, lnx_b=lnx_b, w_o_b=w_o_b,
             mem_norm_g=mem_norm_g, w_mkv=w_mkv, x_qn_g=x_qn_g, x_kn_g=x_kn_g, w_o_c=w_o_c, w_out=w_out,
             norm_ffn_g=norm_ffn_g, w_up=w_up, conv_w=conv_w, conv_b=conv_b, w_down=w_down)
    w = _prep_weights({name: a[0] for name, a in p.items()})
    return (_layer(x_prompt, mem_prompt, w), _layer(x_sample, mem_sample, w))
```

```python
import functools

import numpy as np
import jax
import jax.numpy as jnp
from jax import lax
from jax.experimental import pallas as pl
from jax.experimental.pallas import tpu as pltpu

F32 = jnp.float32
BF16 = jnp.bfloat16

D_MODEL = 1024
RMS_EPS = 1e-6
N_MEM = 256
MLA_HEADS = 8
MLA_NOPE = 64
MLA_ROPE = 32
MLA_QK = MLA_NOPE + MLA_ROPE
MLA_V = 64
Q_LORA = 384
KV_LORA = 256
ROPE_THETA = 10000.0
RW_HEADS = 8
RW_HEAD = 64
RW_DIM = RW_HEADS * RW_HEAD
LORA = 64
GATE_LORA = 128
RW_COLS = 3 * RW_DIM + 4 * LORA + GATE_LORA
LNX_EPS = 64e-5
X_HEADS = 4
X_HEAD = 128
X_DIM = X_HEADS * X_HEAD
D_FF = 2816

LANES = 128
SUBLANES = 8
CHUNK = 64
GROUP = 2
GW = GROUP * RW_HEAD
SLOT_COLS = MLA_HEADS * LANES
MLA_C_COLS = Q_LORA + KV_LORA + LANES
SCAN_CHUNKS = 8
VMEM_LIMIT = 56 * 1024 * 1024


def _cparams(*sem):
    return pltpu.CompilerParams(dimension_semantics=sem, vmem_limit_bytes=VMEM_LIMIT)


def _rms(x, g, eps=RMS_EPS):
    return x * lax.rsqrt(jnp.mean(x * x, axis=-1, keepdims=True) + eps) * g


def _dot(a, b):
    return jnp.dot(a, b, preferred_element_type=F32)


def _dot_nt(a, b):
    return lax.dot_general(a, b, (((1,), (1,)), ((), ())), preferred_element_type=F32)


def _dot_tn(a, b):
    return lax.dot_general(a, b, (((0,), (0,)), ((), ())), preferred_element_type=F32)


def _dot_0_1(x, e):
    return _dot(x.astype(BF16), e)


def _dot_0_1_l(e, x):
    hi = x.astype(BF16)
    lo = (x - hi.astype(F32)).astype(BF16)
    return _dot(e, hi) + _dot(e, lo)


def _sigmoid(z):
    return 1.0 / (1.0 + jnp.exp(-z))


def _full(shape):
    nd = len(shape)
    return pl.BlockSpec(shape, lambda *_: (0,) * nd, pipeline_mode=pl.Buffered(1))


def _mla_prep_kernel(x_ref, g_ref, wc_ref, gq_ref, gkv_ref, wuq_ref, wuqr_ref, wuk_ref, wuv_ref,
                     qa_ref, qb_ref, ka_ref, kba_ref, kbb_ref, q_out, k_out, v_out):
    tm = x_ref.shape[0]
    half = MLA_ROPE // 2
    halves = [slice(0, tm // 2), slice(tm // 2, tm)]
    cs = [_dot(_rms(x_ref[r, :], g_ref[...]).astype(BF16), wc_ref[...]) for r in halves]
    proj = []
    for c in cs:
        cq = _rms(c[:, :Q_LORA], gq_ref[...]).astype(BF16)
        ckv = _rms(c[:, Q_LORA:Q_LORA + KV_LORA], gkv_ref[...]).astype(BF16)
        proj.append((_dot(cq, wuq_ref[...]), _dot(cq, wuqr_ref[...]), _dot(ckv, wuk_ref[...]),
                     _dot_nt(wuv_ref[...], ckv)))
    for r, c, (q, qp, k, vt) in zip(halves, cs, proj):
        slot_row = lax.broadcasted_iota(jnp.int32, vt.shape, 0) % LANES
        v_out[0, :, r] = jnp.where(slot_row == MLA_V, 1.0, vt).astype(BF16)
        kr = c[:, Q_LORA + KV_LORA:]
        qa = qa_ref[r, :]
        qb = qb_ref[r, :]
        ka = ka_ref[r, :]
        kr_roped = (kr * ka + pltpu.roll(kr, LANES - half, 1) * kba_ref[r, :]
                    + pltpu.roll(kr, half, 1) * kbb_ref[r, :])
        kr_ss = jnp.sum(kr * kr, axis=-1, keepdims=True)
        for hd in range(MLA_HEADS):
            sl = slice(LANES * hd, LANES * (hd + 1))
            t = q[:, sl]
            ss = jnp.sum(t * t, axis=-1, keepdims=True) * (1.0 / MLA_QK)
            q_out[r, sl] = ((t * qa + qp[:, sl] * qb) * lax.rsqrt(ss + RMS_EPS)).astype(BF16)
            t = k[:, sl]
            ss = (jnp.sum(t * t, axis=-1, keepdims=True) + kr_ss) * (1.0 / MLA_QK)
            k_out[r, sl] = ((t * ka + kr_roped) * lax.rsqrt(ss + RMS_EPS)).astype(BF16)


def _mla_prep(x2, T, w, tabs):
    ntok = x2.shape[0]
    tm = _attn_tile(T)
    npos = T // tm
    row = lambda i: (i, 0)
    pos = lambda i: (i % npos, 0)
    return pl.pallas_call(
        _mla_prep_kernel,
        grid=(ntok // tm,),
        in_specs=[pl.BlockSpec((tm, D_MODEL), row), _full((1, D_MODEL)), _full((D_MODEL, MLA_C_COLS)),
                  _full((1, Q_LORA)), _full((1, KV_LORA)), _full((Q_LORA, SLOT_COLS)), _full((Q_LORA, SLOT_COLS)),
                  _full((KV_LORA, SLOT_COLS)), _full((SLOT_COLS, KV_LORA))] + [pl.BlockSpec((tm, LANES), pos)] * 5,
        out_specs=[pl.BlockSpec((tm, SLOT_COLS), row), pl.BlockSpec((tm, SLOT_COLS), row),
                   pl.BlockSpec((1, SLOT_COLS, tm), lambda i: (i, 0, 0))],
        out_shape=[jax.ShapeDtypeStruct((ntok, SLOT_COLS), BF16), jax.ShapeDtypeStruct((ntok, SLOT_COLS), BF16),
                   jax.ShapeDtypeStruct((ntok // tm, SLOT_COLS, tm), BF16)],
        compiler_params=_cparams("parallel"),
        name="mla_prep",
    )(x2, w["norm_mix_g"], w["w_c"], w["q_norm_g"], w["kv_norm_g"], w["w_uq"], w["w_uq_rot"], w["w_uk"],
      w["w_uv"], *tabs)


def _flash_kernel(q_ref, k_ref, vt_ref, o_ref, s0_ref, s1_ref, *, tk, nk):
    tq = q_ref.shape[0]
    slots = [slice(LANES * h, LANES * (h + 1)) for h in range(2)]
    per = tk // vt_ref.shape[2]

    def scores(j, s_ref):
        kblk = k_ref[pl.ds(pl.multiple_of(j * tk, tk), tk), :]
        for h in range(2):
            s_ref[h] = _dot_nt(kblk[:, slots[h]], q_ref[:, slots[h]])

    def consume(j, s_ref, state):
        vt = jnp.concatenate([vt_ref[per * j + r] for r in range(per)], axis=1)
        m_news = [jnp.maximum(state[h][0], jnp.max(s_ref[h], axis=0, keepdims=True)) for h in range(2)]
        pts = [jnp.exp2((s_ref[h] - m_news[h]).astype(BF16)) for h in range(2)]
        return tuple((m_news[h], jnp.exp2(state[h][0] - m_news[h]) * state[h][1] + _dot(vt[slots[h], :], pts[h]))
                     for h in range(2))

    def body(t, state):
        j = 2 * t
        scores(j + 1, s1_ref)
        state = consume(j, s0_ref, state)
        scores(j + 2, s0_ref)
        return consume(j + 1, s1_ref, state)

    state = tuple((jnp.full((1, tq), -jnp.inf, F32), jnp.zeros((LANES, tq), F32)) for _ in range(2))
    scores(0, s0_ref)
    state = lax.fori_loop(0, nk // 2 - 1, body, state)
    scores(nk - 1, s1_ref)
    state = consume(nk - 2, s0_ref, state)
    state = consume(nk - 1, s1_ref, state)
    outs = [acc[0:MLA_V, :] / acc[MLA_V:MLA_V + 1, :] for _, acc in state]
    o_ref[...] = jnp.concatenate(outs, axis=0).T.astype(BF16)


def _attn_tile(T):
    return min(512, T // 4)


def _flash(q, k, vt, B, T):
    tv = _attn_tile(T)
    tq, tk = (tv // 2, 2 * tv) if T // tv >= 16 else (tv, tv)
    nq = T // tq
    nk = T // tk
    assert nk % 2 == 0 and tk % tv == 0, (T, tk, tv)
    return pl.pallas_call(
        functools.partial(_flash_kernel, tk=tk, nk=nk),
        grid=(B, MLA_HEADS // 2, nq),
        in_specs=[pl.BlockSpec((tq, 2 * LANES), lambda b, hp, i: (b * nq + i, hp)),
                  pl.BlockSpec((T, 2 * LANES), lambda b, hp, i: (b, hp)),
                  pl.BlockSpec((T // tv, 2 * LANES, tv), lambda b, hp, i: (b, hp, 0))],
        out_specs=pl.BlockSpec((tq, LANES), lambda b, hp, i: (b * nq + i, hp)),
        out_shape=jax.ShapeDtypeStruct((B * T, MLA_HEADS * MLA_V), BF16),
        scratch_shapes=[pltpu.VMEM((2, tk, tq), F32), pltpu.VMEM((2, tk, tq), F32)],
        compiler_params=_cparams("parallel", "parallel", "arbitrary"),
        name="mla_flash",
    )(q, k, vt)


def _rw_prep_kernel(x_ref, xp_ref, xn_ref, g_ref, wrw_ref, mup_ref, mun_ref, kk_ref, ka_ref, rk_ref,
                    w0_ref, w2_ref, a0_ref, a2_ref, g2_ref, seg_ref, tri_ref,
                    v_out, kkt_out, rt_out, kh_out, bh_out, kw_out, bw_out, wt_out, bonus_out, g7_out,
                    h_scr, rw_scr, *, tm, npos):
    i = pl.program_id(0)
    not_first = (i % npos != 0).astype(F32)
    not_last = (i % npos != npos - 1).astype(F32)
    g = g_ref[...]
    h_scr[0:SUBLANES, :] = (_rms(xp_ref[...], g) * not_first).astype(BF16)
    h_scr[SUBLANES:SUBLANES + tm, :] = _rms(x_ref[...], g).astype(BF16)
    h_scr[SUBLANES + tm:, :] = (_rms(xn_ref[...], g) * not_last).astype(BF16)
    rw_scr[...] = _dot(h_scr[...], wrw_ref[...])
    mup = mup_ref[...]
    mun = mun_ref[...]
    cur = rw_scr[SUBLANES:SUBLANES + tm, :]
    prev = rw_scr[SUBLANES - 1:SUBLANES - 1 + tm, :]
    nxt = rw_scr[SUBLANES + 1:SUBLANES + 1 + tm, :]
    rwf = cur * (1.0 - mup - mun) + mup * prev + mun * nxt
    r7 = rwf[:, 0:RW_DIM]
    k7 = rwf[:, RW_DIM:2 * RW_DIM]
    v7 = rwf[:, 2 * RW_DIM:3 * RW_DIM]
    wl = rwf[:, 3 * RW_DIM:3 * RW_DIM + 2 * LORA]
    al = rwf[:, 3 * RW_DIM + 2 * LORA:3 * RW_DIM + 4 * LORA]
    gl = rwf[:, 3 * RW_DIM + 4 * LORA:]
    seg = seg_ref[...]
    kx = k7 * kk_ref[...]
    kkn = kx * lax.rsqrt(jnp.maximum(_dot_0_1(kx * kx, seg), 1e-24))
    v_out[...] = v7.astype(BF16)
    g7_out[...] = _dot(_sigmoid(gl).astype(BF16), g2_ref[...])
    tw = jnp.tanh(wl).astype(BF16)
    alb = al.astype(BF16)
    bonus = jnp.zeros((tm, RW_DIM), F32)
    for d in range(2):
        z = -(w0_ref[d] + _dot(tw, w2_ref[d]))
        softplus = jnp.maximum(z, 0.0) + jnp.log(1.0 + jnp.exp(-jnp.abs(z)))
        lw = -jnp.exp(-softplus - 0.5)
        a = _sigmoid(a0_ref[d] + _dot(alb, a2_ref[d]))
        kd = k7 * (1.0 + (a - 1.0) * ka_ref[...])
        b = kkn * a
        bonus = bonus + _dot_0_1(r7 * kd * rk_ref[...], seg) * v7
        cum = _dot_0_1_l(tri_ref[d], lw)
        last = CHUNK - 1 if d == 0 else 0
        wtot_rows = [jnp.exp(cum[CHUNK * ci + last:CHUNK * ci + last + 1, :]) for ci in range(tm // CHUNK)]
        wtot = jnp.concatenate([jnp.broadcast_to(r, (CHUNK, RW_DIM)) for r in wtot_rows], axis=0)
        w_incl = jnp.exp(cum)
        w_excl = jnp.exp(cum - lw)
        w_inv = 1.0 / w_incl
        w_rest = wtot * w_inv
        kkt_out[d] = (kkn * w_excl).astype(BF16)
        rt_out[d] = (r7 * w_incl).astype(BF16)
        kh_out[d] = (kd * w_inv).astype(BF16)
        bh_out[d] = (b * w_inv).astype(BF16)
        kw_out[d] = (kd * w_rest).astype(BF16)
        bw_out[d] = (b * w_rest).astype(BF16)
        for ci in range(tm // CHUNK):
            wt_out[d, ci] = wtot_rows[ci]
    bonus_out[...] = bonus


def _rw_prep(x2, T, w):
    ntok = x2.shape[0]
    tm = min(512, T)
    npos = T // tm
    nblk8 = ntok // SUBLANES
    r8 = tm // SUBLANES
    row = lambda i: (i, 0)
    drow = lambda i: (0, i, 0)
    dspec = pl.BlockSpec((2, tm, RW_DIM), drow)
    dshape = jax.ShapeDtypeStruct((2, ntok, RW_DIM), BF16)
    tri = _chunk_masks(tm)
    outs = pl.pallas_call(
        functools.partial(_rw_prep_kernel, tm=tm, npos=npos),
        grid=(ntok // tm,),
        in_specs=[pl.BlockSpec((tm, D_MODEL), row),
                  pl.BlockSpec((SUBLANES, D_MODEL), lambda i: (jnp.maximum(i * r8 - 1, 0), 0)),
                  pl.BlockSpec((SUBLANES, D_MODEL), lambda i: (jnp.minimum((i + 1) * r8, nblk8 - 1), 0)),
                  _full((1, D_MODEL)), _full((D_MODEL, RW_COLS)), _full((1, RW_COLS)), _full((1, RW_COLS)),
                  _full((1, RW_DIM)), _full((1, RW_DIM)), _full((1, RW_DIM)),
                  _full((2, 1, RW_DIM)), _full((2, 2 * LORA, RW_DIM)), _full((2, 1, RW_DIM)),
                  _full((2, 2 * LORA, RW_DIM)), _full((GATE_LORA, RW_DIM)), _full((RW_DIM, RW_DIM)),
                  _full((2, tm, tm))],
        out_specs=[pl.BlockSpec((tm, RW_DIM), row), dspec, dspec, dspec, dspec, dspec, dspec,
                   pl.BlockSpec((2, tm // CHUNK, 1, RW_DIM), lambda i: (0, i, 0, 0)),
                   pl.BlockSpec((tm, RW_DIM), row), pl.BlockSpec((tm, RW_DIM), row)],
        out_shape=[jax.ShapeDtypeStruct((ntok, RW_DIM), BF16), dshape, dshape, dshape, dshape, dshape, dshape,
                   jax.ShapeDtypeStruct((2, ntok // CHUNK, 1, RW_DIM), F32),
                   jax.ShapeDtypeStruct((ntok, RW_DIM), F32), jax.ShapeDtypeStruct((ntok, RW_DIM), F32)],
        scratch_shapes=[pltpu.VMEM((tm + 2 * SUBLANES, D_MODEL), BF16), pltpu.VMEM((tm + 2 * SUBLANES, RW_COLS), F32)],
        compiler_params=_cparams("parallel"),
        name="rw_prep",
    )(x2, x2, x2, w["norm_mix_g"], w["w_rw"], w["mu_prev"], w["mu_next"], w["k_k"], w["k_a"], w["r_k"],
      w["w0"], w["w2"], w["a0"], w["a2"], w["g2"], w["seg"], tri)
    return outs


def _chunk_masks(tm):
    t = np.arange(tm)
    same = (t[:, None] // CHUNK) == (t[None, :] // CHUNK)
    fwd = same & (t[None, :] <= t[:, None])
    bwd = same & (t[None, :] >= t[:, None])
    return jnp.asarray(np.stack([fwd, bwd]).astype(np.float32), BF16)


def _rw_scan_kernel(*refs):
    ins = refs[:16]
    yf_ref, yb_ref, s_ref = refs[16:]
    c = pl.program_id(1)

    @pl.when(c == 0)
    def _():
        s_ref[...] = jnp.zeros(s_ref.shape, F32)

    t_pos = lax.broadcasted_iota(jnp.int32, (CHUNK, GW), 0)
    s_pos = lax.broadcasted_iota(jnp.int32, (CHUNK, GW), 1) % CHUNK
    eye_c = (t_pos == s_pos).astype(F32)
    ri = lax.broadcasted_iota(jnp.int32, (GW, GW), 0)
    ci = lax.broadcasted_iota(jnp.int32, (GW, GW), 1)
    head_mask = (ri // CHUNK) == (ci // RW_HEAD)
    stack = lambda a: jnp.where(head_mask, jnp.concatenate([a] * GROUP, axis=0), jnp.zeros((), BF16))

    def prepare(orders):
        items = []
        for d, gi, order in ((d, gi, o) for d in range(2) for gi in range(RW_HEADS // GROUP) for o in orders):
            kkt, rt, kh, bh, kw, bw, v, wt = ins[8 * d:8 * d + 8]
            strict = (s_pos < t_pos) if d == 0 else (s_pos > t_pos)
            incl = (s_pos <= t_pos) if d == 0 else (s_pos >= t_pos)
            sub = order if d == 0 else SCAN_CHUNKS - 1 - order
            rows = slice(CHUNK * sub, CHUNK * (sub + 1))
            sl = slice(GW * gi, GW * (gi + 1))
            it = dict(y_ref=(yf_ref, yb_ref)[d], sl=sl, rows=rows, s_ref=s_ref.at[d, gi], wtot=wt[sub][:, sl],
                      order=order)
            v_c = v[rows, sl]
            kkt_rt = jnp.concatenate([kkt[rows, sl], rt[rows, sl]], axis=0)
            a_all = _dot_nt(kkt_rt, jnp.concatenate([stack(kh[rows, sl]), stack(bh[rows, sl])], axis=0))
            ak = jnp.where(strict, a_all[:CHUNK, :GW], 0.0).astype(BF16)
            bk = jnp.where(incl, a_all[CHUNK:, :GW], 0.0).astype(BF16)
            it.update(kkt_rt=kkt_rt, v=v_c, kbw=jnp.concatenate([kw[rows, sl], bw[rows, sl]], axis=0),
                      abk=jnp.concatenate([ak, bk], axis=0), bb=jnp.where(incl, a_all[CHUNK:, GW:], 0.0).astype(BF16))
            ab = jnp.where(strict, a_all[:CHUNK, GW:], 0.0)
            it.update(tinv=eye_c - ab, pw=ab.astype(BF16))
            items.append(it)
        for it in items:
            it["abk_v"] = _dot(it["abk"], stack(it["v"]))
        for it in items:
            it["pw"] = _dot(it["pw"], stack(it["pw"])).astype(BF16)
        for k in range(5):
            for it in items:
                sk = stack(it["pw"])
                if k < 4:
                    res = _dot(jnp.concatenate([it["pw"], it["tinv"].astype(BF16)], axis=0), sk)
                    it["pw"] = res[:CHUNK].astype(BF16)
                    it["tinv"] = it["tinv"] + res[CHUNK:]
                else:
                    it["tinv"] = (it["tinv"] + _dot(it["tinv"].astype(BF16), sk)).astype(BF16)
        return items

    def advance(chains):
        for ch in chains:
            sb = ch["s_ref"][...].astype(BF16)
            ch["zs"] = _dot_nt(ch["kkt_rt"], sb) + ch["abk_v"]
        for ch in chains:
            ch["u"] = (-_dot(ch["tinv"], stack(ch["zs"][:CHUNK].astype(BF16)))).astype(BF16)
        for ch in chains:
            ch["y_ref"][ch["rows"], ch["sl"]] = ch["zs"][CHUNK:] + _dot(ch["bb"], stack(ch["u"]))
        for ch in chains:
            upd = _dot_tn(jnp.concatenate([ch["v"], ch["u"]], axis=0), ch["kbw"])
            ch["s_ref"][...] = ch["s_ref"][...] * ch["wtot"] + jnp.where(head_mask, upd, 0.0)

    prepared = prepare(range(SCAN_CHUNKS))
    for order in range(SCAN_CHUNKS):
        advance([it for it in prepared if it["order"] == order])


def _rw_scan(v, kkt, rt, kh, bh, kw, bw, wt, B, T):
    ntok = B * T
    rows = SCAN_CHUNKS * CHUNK
    nb = T // rows
    in_specs = []
    args = []
    for d in range(2):
        if d == 0:
            cm = lambda b, c: b * nb + c
        else:
            cm = lambda b, c: b * nb + (nb - 1 - c)
        for arr in (kkt, rt, kh, bh, kw, bw):
            in_specs.append(pl.BlockSpec((None, rows, RW_DIM), lambda b, c, cm=cm, d=d: (d, cm(b, c), 0)))
            args.append(arr)
        in_specs.append(pl.BlockSpec((rows, RW_DIM), lambda b, c, cm=cm: (cm(b, c), 0)))
        args.append(v)
        in_specs.append(pl.BlockSpec((None, SCAN_CHUNKS, 1, RW_DIM), lambda b, c, cm=cm, d=d: (d, cm(b, c), 0, 0)))
        args.append(wt)
    return pl.pallas_call(
        _rw_scan_kernel,
        grid=(B, nb),
        in_specs=in_specs,
        out_specs=[pl.BlockSpec((rows, RW_DIM), lambda b, c: (b * nb + c, 0)),
                   pl.BlockSpec((rows, RW_DIM), lambda b, c: (b * nb + (nb - 1 - c), 0))],
        out_shape=[jax.ShapeDtypeStruct((ntok, RW_DIM), F32), jax.ShapeDtypeStruct((ntok, RW_DIM), F32)],
        scratch_shapes=[pltpu.VMEM((2, RW_HEADS // GROUP, GW, GW), F32)],
        compiler_params=_cparams("parallel", "arbitrary"),
        name="rw_scan",
    )(*args)


def _mem_kv_kernel(m_ref, g_ref, wk_ref, wv_ref, gk_ref, k_out, v_out):
    m = _rms(m_ref[...], g_ref[...]).astype(BF16)
    k = _dot(m, wk_ref[...])
    v_out[...] = _dot(m, wv_ref[...]).astype(BF16)
    for hd in range(X_HEADS):
        sl = slice(X_HEAD * hd, X_HEAD * (hd + 1))
        k_out[:, sl] = _rms(k[:, sl], gk_ref[...]).astype(BF16)


def _mem_kv(mem2, w):
    n = mem2.shape[0]
    row = lambda i: (i, 0)
    return pl.pallas_call(
        _mem_kv_kernel,
        grid=(n // N_MEM,),
        in_specs=[pl.BlockSpec((N_MEM, D_MODEL), row), _full((1, D_MODEL)), _full((D_MODEL, X_DIM)),
                  _full((D_MODEL, X_DIM)), _full((1, X_HEAD))],
        out_specs=[pl.BlockSpec((N_MEM, X_DIM), row), pl.BlockSpec((N_MEM, X_DIM), row)],
        out_shape=[jax.ShapeDtypeStruct((n, X_DIM), BF16), jax.ShapeDtypeStruct((n, X_DIM), BF16)],
        compiler_params=_cparams("parallel"),
        name="mem_kv",
    )(mem2, w["mem_norm_g"], w["w_mk"], w["w_mv"], w["x_kn_g"])


def _xattn_kernel(x_ref, g_ref, wq_ref, gq_ref, mk_ref, mv_ref, o_ref):
    h = _rms(x_ref[...], g_ref[...]).astype(BF16)
    q = _dot(h, wq_ref[...])
    for hd in range(X_HEADS):
        sl = slice(X_HEAD * hd, X_HEAD * (hd + 1))
        qh = _rms(q[:, sl], gq_ref[...]).astype(BF16)
        s = _dot_nt(qh, mk_ref[:, sl])
        p = jnp.exp(s - jnp.max(s, axis=-1, keepdims=True))
        o = _dot(p.astype(BF16), mv_ref[:, sl]) / jnp.sum(p, axis=-1, keepdims=True)
        o_ref[:, sl] = o.astype(BF16)


def _xattn(x2, mk, mv, B, T, w):
    tq = min(512, T)
    nq = T // tq
    return pl.pallas_call(
        _xattn_kernel,
        grid=(B, nq),
        in_specs=[pl.BlockSpec((tq, D_MODEL), lambda b, i: (b * nq + i, 0)), _full((1, D_MODEL)),
                  _full((D_MODEL, X_DIM)), _full((1, X_HEAD)),
                  pl.BlockSpec((N_MEM, X_DIM), lambda b, i: (b, 0)), pl.BlockSpec((N_MEM, X_DIM), lambda b, i: (b, 0))],
        out_specs=pl.BlockSpec((tq, X_DIM), lambda b, i: (b * nq + i, 0)),
        out_shape=jax.ShapeDtypeStruct((B * T, X_DIM), BF16),
        compiler_params=_cparams("parallel", "parallel"),
        name="xattn",
    )(x2, w["norm_mix_g"], w["w_xq"], w["x_qn_g_scaled"], mk, mv)


def _merge_kernel(x_ref, oa_ref, yf_ref, yb_ref, bonus_ref, g7_ref, oc_ref, g_ref, wg_ref, woa_ref, wob_ref,
                  woc_ref, wout_ref, lng_ref, lnb_ref, seg_ref, o_ref):
    x = x_ref[...]
    h = _rms(x, g_ref[...]).astype(BF16)
    seg = seg_ref[...]
    gate = lambda bi: _sigmoid(_dot(h, wg_ref[:, D_MODEL * bi:D_MODEL * (bi + 1)]))
    y7 = yf_ref[...] + yb_ref[...] + bonus_ref[...]
    mu = _dot_0_1(y7, seg) * (1.0 / RW_HEAD)
    merged = gate(0) * _dot(oa_ref[...], woa_ref[...])
    dy = y7 - mu
    var = _dot_0_1(dy * dy, seg) * (1.0 / RW_HEAD)
    merged = merged + gate(2) * _dot(oc_ref[...], woc_ref[...])
    gate_b = gate(1)
    y7 = dy * lax.rsqrt(var + LNX_EPS) * lng_ref[...] + lnb_ref[...]
    merged = merged + gate_b * _dot((y7 * g7_ref[...]).astype(BF16), wob_ref[...])
    o_ref[...] = x + _dot(merged.astype(BF16), wout_ref[...])


def _merge(x2, oa, yf, yb, bonus, g7, oc, T, w):
    ntok = x2.shape[0]
    tm = min(512, T)
    row = lambda i: (i, 0)
    half = lambda: pl.BlockSpec((tm, RW_DIM), row)
    return pl.pallas_call(
        _merge_kernel,
        grid=(ntok // tm,),
        in_specs=[pl.BlockSpec((tm, D_MODEL), row), half(), half(), half(), half(), half(), half(),
                  _full((1, D_MODEL)), _full((D_MODEL, 3 * D_MODEL)), _full((RW_DIM, D_MODEL)),
                  _full((RW_DIM, D_MODEL)), _full((RW_DIM, D_MODEL)), _full((D_MODEL, D_MODEL)),
                  _full((1, RW_DIM)), _full((1, RW_DIM)), _full((RW_DIM, RW_DIM))],
        out_specs=pl.BlockSpec((tm, D_MODEL), row),
        out_shape=jax.ShapeDtypeStruct((ntok, D_MODEL), F32),
        compiler_params=_cparams("parallel"),
        name="merge",
    )(x2, oa, yf, yb, bonus, g7, oc, w["norm_mix_g"], w["w_gate"], w["w_o_a"], w["w_o_b"], w["w_o_c"],
      w["w_out"], w["lnx_g"], w["lnx_b"], w["seg"])


MXU_TILE = 256
FFN_SPLITS = (0, 5 * MXU_TILE, D_FF)


def _ffn_kernel(x_ref, xp_ref, xn_ref, g_ref, wug_ref, wuv_ref, cw_ref, cb_ref, wd_ref, o_ref, h_scr, ug_scr,
                *, tm, npos):
    i = pl.program_id(0)
    not_first = (i % npos != 0).astype(F32)
    not_last = (i % npos != npos - 1).astype(F32)
    g = g_ref[...]
    h_scr[0:SUBLANES, :] = (_rms(xp_ref[...], g) * not_first).astype(BF16)
    h_scr[SUBLANES:SUBLANES + tm, :] = _rms(x_ref[...], g).astype(BF16)
    h_scr[SUBLANES + tm:, :] = (_rms(xn_ref[...], g) * not_last).astype(BF16)
    out = x_ref[...]
    for lo, hi in zip(FFN_SPLITS[:-1], FFN_SPLITS[1:]):
        n = hi - lo
        ug_scr[:, 0:n] = _dot(h_scr[...], wug_ref[:, lo:hi])
        uv = _dot(h_scr[SUBLANES:SUBLANES + tm, :], wuv_ref[:, lo:hi])
        cw = cw_ref[:, lo:hi]
        c = (cw[0:1] * ug_scr[SUBLANES - 1:SUBLANES - 1 + tm, 0:n] + cw[1:2] * ug_scr[SUBLANES:SUBLANES + tm, 0:n]
             + cw[2:3] * ug_scr[SUBLANES + 1:SUBLANES + 1 + tm, 0:n] + cb_ref[:, lo:hi])
        act = 0.5 * c * (1.0 + lax.erf(c * np.float32(1.0 / np.sqrt(2.0)))) * uv
        out = out + _dot(act.astype(BF16), wd_ref[lo:hi, :])
    o_ref[...] = out


def _ffn(x2, T, w):
    ntok = x2.shape[0]
    tm = min(512, T)
    fc = max(hi - lo for lo, hi in zip(FFN_SPLITS[:-1], FFN_SPLITS[1:]))
    npos = T // tm
    nblk8 = ntok // SUBLANES
    r8 = tm // SUBLANES
    row = lambda i: (i, 0)
    return pl.pallas_call(
        functools.partial(_ffn_kernel, tm=tm, npos=npos),
        grid=(ntok // tm,),
        in_specs=[pl.BlockSpec((tm, D_MODEL), row),
                  pl.BlockSpec((SUBLANES, D_MODEL), lambda i: (jnp.maximum(i * r8 - 1, 0), 0)),
                  pl.BlockSpec((SUBLANES, D_MODEL), lambda i: (jnp.minimum((i + 1) * r8, nblk8 - 1), 0)),
                  _full((1, D_MODEL)), _full((D_MODEL, D_FF)), _full((D_MODEL, D_FF)), _full((3, D_FF)),
                  _full((1, D_FF)), _full((D_FF, D_MODEL))],
        out_specs=pl.BlockSpec((tm, D_MODEL), row),
        out_shape=jax.ShapeDtypeStruct((ntok, D_MODEL), F32),
        scratch_shapes=[pltpu.VMEM((tm + 2 * SUBLANES, D_MODEL), BF16), pltpu.VMEM((tm + 2 * SUBLANES, fc), F32)],
        compiler_params=_cparams("parallel"),
        name="conv_ffn",
    )(x2, x2, x2, w["norm_ffn_g"], w["w_up_gate"], w["w_up_val"], w["conv_w"], w["conv_b"], w["w_down"])


def _prep_weights(p):
    w = {}
    row = lambda a: a.reshape(1, -1).astype(F32)
    w_in = p["w_in"]
    o = np.cumsum([0, Q_LORA, KV_LORA, MLA_ROPE, RW_COLS, X_DIM, 3 * D_MODEL])
    seg = lambda i: w_in[:, o[i]:o[i + 1]]
    zc = lambda n: jnp.zeros((D_MODEL, n), F32)
    w["w_c"] = jnp.concatenate([seg(0), seg(1), zc(MLA_NOPE), seg(2), zc(LANES - MLA_QK)], axis=1).astype(BF16)
    w["w_rw"] = seg(3).astype(BF16)
    w["w_xq"] = seg(4).astype(BF16)
    w["w_gate"] = seg(5).astype(BF16)
    w["norm_mix_g"] = row(p["norm_mix_g"])
    w["q_norm_g"] = row(p["q_norm_g"])
    w["kv_norm_g"] = row(p["kv_norm_g"])
    pad_slot = lambda a: jnp.pad(a, ((0, 0), (0, 0), (0, LANES - a.shape[-1]))).reshape(a.shape[0], -1)
    uq = p["w_uq"].reshape(Q_LORA, MLA_HEADS, MLA_QK)
    w["w_uq"] = pad_slot(uq).astype(BF16)
    half = MLA_ROPE // 2
    uq_rot = jnp.concatenate([jnp.zeros_like(uq[:, :, :MLA_NOPE]), -uq[:, :, MLA_NOPE + half:],
                              uq[:, :, MLA_NOPE:MLA_NOPE + half]], axis=-1)
    w["w_uq_rot"] = pad_slot(uq_rot).astype(BF16)
    ukv =p["w_ukv"].reshape(KV_LORA, MLA_HEADS, MLA_NOPE + MLA_V)
    w["w_uk"] = pad_slot(ukv[:, :, :MLA_NOPE]).astype(BF16)
    w["w_uv"] = pad_slot(ukv[:, :, MLA_NOPE:]).T.astype(BF16)
    w["mla_qn_g_scaled"] = p["mla_qn_g"].reshape(-1) * np.float32(MLA_QK ** -0.5 * np.log2(np.e))
    w["mla_kn_g"] = p["mla_kn_g"].reshape(-1)
    for name in ("mu_prev", "mu_next", "k_k", "k_a", "r_k", "lnx_g", "lnx_b", "mem_norm_g", "x_kn_g", "norm_ffn_g",
                 "conv_b"):
        w[name] = row(p[name])
    w["w0"] = jnp.stack([p["w0_f"], p["w0_b"]]).reshape(2, 1, RW_DIM)
    w["a0"] = jnp.stack([p["a0_f"], p["a0_b"]]).reshape(2, 1, RW_DIM)
    zl = jnp.zeros((LORA, RW_DIM), F32)
    w["w2"] = jnp.stack([jnp.concatenate([p["w2_f"], zl]), jnp.concatenate([zl, p["w2_b"]])]).astype(BF16)
    w["a2"] = jnp.stack([jnp.concatenate([p["a2_f"], zl]), jnp.concatenate([zl, p["a2_b"]])]).astype(BF16)
    w["g2"] = p["g2"].astype(BF16)
    hid = np.arange(RW_DIM) // RW_HEAD
    w["seg"] = jnp.asarray((hid[:, None] == hid[None, :]).astype(np.float32), BF16)
    mkv = p["w_mkv"].reshape(D_MODEL, X_HEADS, 2 * X_HEAD)
    w["w_mk"] = mkv[:, :, :X_HEAD].reshape(D_MODEL, X_DIM).astype(BF16)
    w["w_mv"] = mkv[:, :, X_HEAD:].reshape(D_MODEL, X_DIM).astype(BF16)
    w["x_qn_g_scaled"] = row(p["x_qn_g"]) * np.float32(X_HEAD ** -0.5)
    for name in ("w_o_a", "w_o_b", "w_o_c", "w_out", "w_down"):
        w[name] = p[name].astype(BF16)
    w["w_up_gate"] = p["w_up"][:, :D_FF].astype(BF16)
    w["w_up_val"] = p["w_up"][:, D_FF:].astype(BF16)
    w["conv_w"] = p["conv_w"].astype(F32)
    return w


def _rope_tables(T, gq, gk):
    half = MLA_ROPE // 2
    inv = jnp.power(ROPE_THETA, -jnp.arange(half, dtype=F32) / half)
    ang = jnp.arange(T, dtype=F32)[:, None] * inv[None, :]
    cos, sin = jnp.cos(ang), jnp.sin(ang)
    z = lambda n: jnp.zeros((T, n), F32)
    pad = z(LANES - MLA_QK)

    def own(g):
        g1, g2 = g[MLA_NOPE:MLA_NOPE + half], g[MLA_NOPE + half:]
        return jnp.concatenate([jnp.broadcast_to(g[:MLA_NOPE], (T, MLA_NOPE)), cos * g1, cos * g2, pad], axis=1)

    g1, g2 = gq[MLA_NOPE:MLA_NOPE + half], gq[MLA_NOPE + half:]
    qb = jnp.concatenate([z(MLA_NOPE), sin * g2, sin * g1, pad], axis=1)
    g1, g2 = gk[MLA_NOPE:MLA_NOPE + half], gk[MLA_NOPE + half:]
    kba = jnp.concatenate([z(MLA_NOPE), -sin * g2, z(half), pad], axis=1)
    kbb = jnp.concatenate([z(MLA_NOPE), z(half), sin * g1, pad], axis=1)
    return own(gq), qb, own(gk), kba, kbb


def _layer(x, mem, w):
    B, T, _ = x.shape
    x2 = x.reshape(B * T, D_MODEL)
    q, k, v = _mla_prep(x2, T, w, _rope_tables(T, w["mla_qn_g_scaled"], w["mla_kn_g"]))
    o_a = _flash(q, k, v, B, T)
    v7, kkt, rt, kh, bh, kw, bw, wt, bonus, g7 = _rw_prep(x2, T, w)
    y_f, y_b = _rw_scan(v7, kkt, rt, kh, bh, kw, bw, wt, B, T)
    mk, mv = _mem_kv(mem.reshape(B * N_MEM, D_MODEL), w)
    o_c = _xattn(x2, mk, mv, B, T, w)
    x1 = _merge(x2, o_a, y_f, y_b, bonus, g7, o_c, T, w)
    return _ffn(x1, T, w).reshape(B, T, D_MODEL)


def kernel(x_prompt, x_sample, mem_prompt, mem_sample, norm_mix_g, w_in, q_norm_g, w_uq, kv_norm_g, w_ukv, mla_qn_g, mla_kn_g, w_o_a, mu_prev, mu_next, w0_f, w2_f, a0_f, a2_f, w0_b, w2_b, a0_b, a2_b, g2, k_k, k_a, r_k, lnx_g, lnx_b, w_o_b, mem_norm_g, w_mkv, x_qn_g, x_kn_g, w_o_c, w_out, norm_ffn_g, w_up, conv_w, conv_b, w_down):
    p = dict(norm_mix_g=norm_mix_g, w_in=w_in, q_norm_g=q_norm_g, w_uq=w_uq, kv_norm_g=kv_norm_g, w_ukv=w_ukv,
             mla_qn_g=mla_qn_g, mla_kn_g=mla_kn_g, w_o_a=w_o_a, mu_prev=mu_prev, mu_next=mu_next,
             w0_f=w0_f, w2_f=w2_f, a0_f=a0_f, a2_f=a2_f, w0_b=w0_b, w2_b=w2_b, a0_b=a0_b, a2_b=a2_b,
             g2=g2, k_k=k_k, k_a=k_a, r_k=r_k, lnx_g=lnx_g, lnx_b=lnx_b, w_o_b=w_o_b,
             mem_norm_g=mem_norm_g, w_mkv=w_mkv, x_qn_g=x_qn_g, x_kn_g=x_kn_g, w_o_c=w_o_c, w_out=w_out,
             norm_ffn_g=norm_ffn_g, w_up=w_up, conv_w=conv_w, conv_b=conv_b, w_down=w_down)
    w = _prep_weights({name: a[0] for name, a in p.items()})
    return (_layer(x_prompt, mem_prompt, w), _layer(x_sample, mem_sample, w))
```

```python
import functools

import numpy as np
import jax
import jax.numpy as jnp
from jax import lax
from jax.experimental import pallas as pl
from jax.experimental.pallas import tpu as pltpu

F32 = jnp.float32
BF16 = jnp.bfloat16

D_MODEL = 1024
RMS_EPS = 1e-6
N_MEM = 256
MLA_HEADS = 8
MLA_NOPE = 64
MLA_ROPE = 32
MLA_QK = MLA_NOPE + MLA_ROPE
MLA_V = 64
Q_LORA = 384
KV_LORA = 256
ROPE_THETA = 10000.0
RW_HEADS = 8
RW_HEAD = 64
RW_DIM = RW_HEADS * RW_HEAD
LORA = 64
GATE_LORA = 128
RW_COLS = 3 * RW_DIM + 4 * LORA + GATE_LORA
LNX_EPS = 64e-5
X_HEADS = 4
X_HEAD = 128
X_DIM = X_HEADS * X_HEAD
D_FF = 2816

LANES = 128
SUBLANES = 8
CHUNK = 64
GROUP = 2
GW = GROUP * RW_HEAD
SLOT_COLS = MLA_HEADS * LANES
MLA_C_COLS = Q_LORA + KV_LORA + LANES
SCAN_CHUNKS = 8
VMEM_LIMIT = 56 * 1024 * 1024


def _cparams(*sem):
    return pltpu.CompilerParams(dimension_semantics=sem, vmem_limit_bytes=VMEM_LIMIT)


def _rms(x, g, eps=RMS_EPS):
    return x * lax.rsqrt(jnp.mean(x * x, axis=-1, keepdims=True) + eps) * g


def _dot(a, b):
    return jnp.dot(a, b, preferred_element_type=F32)


def _dot_nt(a, b):
    return lax.dot_general(a, b, (((1,), (1,)), ((), ())), preferred_element_type=F32)


def _dot_tn(a, b):
    return lax.dot_general(a, b, (((0,), (0,)), ((), ())), preferred_element_type=F32)


def _dot_0_1(x, e):
    return _dot(x.astype(BF16), e)


def _dot_0_1_l(e, x):
    hi = x.astype(BF16)
    lo = (x - hi.astype(F32)).astype(BF16)
    return _dot(e, hi) + _dot(e, lo)


def _sigmoid(z):
    return 1.0 / (1.0 + jnp.exp(-z))


def _full(shape):
    nd = len(shape)
    return pl.BlockSpec(shape, lambda *_: (0,) * nd, pipeline_mode=pl.Buffered(1))


def _mla_prep_kernel(x_ref, g_ref, wc_ref, gq_ref, gkv_ref, wuq_ref, wuqr_ref, wuk_ref, wuv_ref,
                     qa_ref, qb_ref, ka_ref, kba_ref, kbb_ref, q_out, k_out, v_out, h_out):
    tm = x_ref.shape[0]
    half = MLA_ROPE // 2
    halves = [slice(0, tm // 2), slice(tm // 2, tm)]
    cs = []
    for r in halves:
        h = _rms(x_ref[r, :], g_ref[...]).astype(BF16)
        h_out[r, :] = h
        cs.append(_dot(h, wc_ref[...]))
    proj = []
    for c in cs:
        cq = _rms(c[:, :Q_LORA], gq_ref[...]).astype(BF16)
        ckv = _rms(c[:, Q_LORA:Q_LORA + KV_LORA], gkv_ref[...]).astype(BF16)
        proj.append((_dot(cq, wuq_ref[...]), _dot(cq, wuqr_ref[...]), _dot(ckv, wuk_ref[...]),
                     _dot_nt(wuv_ref[...], ckv)))
    for r, c, (q, qp, k, vt) in zip(halves, cs, proj):
        slot_row = lax.broadcasted_iota(jnp.int32, vt.shape, 0) % LANES
        v_out[0, :, r] = jnp.where(slot_row == MLA_V, 1.0, vt).astype(BF16)
        kr = c[:, Q_LORA + KV_LORA:]
        qa = qa_ref[r, :]
        qb = qb_ref[r, :]
        ka = ka_ref[r, :]
        kr_roped = (kr * ka + pltpu.roll(kr, LANES - half, 1) * kba_ref[r, :]
                    + pltpu.roll(kr, half, 1) * kbb_ref[r, :])
        kr_ss = jnp.sum(kr * kr, axis=-1, keepdims=True)
        for hd in range(MLA_HEADS):
            sl = slice(LANES * hd, LANES * (hd + 1))
            t = q[:, sl]
            ss = jnp.sum(t * t, axis=-1, keepdims=True) * (1.0 / MLA_QK)
            q_out[r, sl] = ((t * qa + qp[:, sl] * qb) * lax.rsqrt(ss + RMS_EPS)).astype(BF16)
            t = k[:, sl]
            ss = (jnp.sum(t * t, axis=-1, keepdims=True) + kr_ss) * (1.0 / MLA_QK)
            k_out[r, sl] = ((t * ka + kr_roped) * lax.rsqrt(ss + RMS_EPS)).astype(BF16)


def _mla_prep(x2, T, w, tabs):
    ntok = x2.shape[0]
    tm = _attn_tile(T)
    npos = T // tm
    row = lambda i: (i, 0)
    pos = lambda i: (i % npos, 0)
    return pl.pallas_call(
        _mla_prep_kernel,
        grid=(ntok // tm,),
        in_specs=[pl.BlockSpec((tm, D_MODEL), row), _full((1, D_MODEL)), _full((D_MODEL, MLA_C_COLS)),
                  _full((1, Q_LORA)), _full((1, KV_LORA)), _full((Q_LORA, SLOT_COLS)), _full((Q_LORA, SLOT_COLS)),
                  _full((KV_LORA, SLOT_COLS)), _full((SLOT_COLS, KV_LORA))] + [pl.BlockSpec((tm, LANES), pos)] * 5,
        out_specs=[pl.BlockSpec((tm, SLOT_COLS), row), pl.BlockSpec((tm, SLOT_COLS), row),
                   pl.BlockSpec((1, SLOT_COLS, tm), lambda i: (i, 0, 0)), pl.BlockSpec((tm, D_MODEL), row)],
        out_shape=[jax.ShapeDtypeStruct((ntok, SLOT_COLS), BF16), jax.ShapeDtypeStruct((ntok, SLOT_COLS), BF16),
                   jax.ShapeDtypeStruct((ntok // tm, SLOT_COLS, tm), BF16),
                   jax.ShapeDtypeStruct((ntok, D_MODEL), BF16)],
        compiler_params=_cparams("parallel"),
        name="mla_prep",
    )(x2, w["norm_mix_g"], w["w_c"], w["q_norm_g"], w["kv_norm_g"], w["w_uq"], w["w_uq_rot"], w["w_uk"],
      w["w_uv"], *tabs)


def _flash_kernel(q_ref, k_ref, vt_ref, o_ref, s0_ref, s1_ref, *, tk, nk):
    tq = q_ref.shape[0]
    slots = [slice(LANES * h, LANES * (h + 1)) for h in range(2)]
    per = tk // vt_ref.shape[2]

    def scores(j, s_ref):
        kblk = k_ref[pl.ds(pl.multiple_of(j * tk, tk), tk), :]
        for h in range(2):
            s_ref[h] = _dot_nt(kblk[:, slots[h]], q_ref[:, slots[h]])

    def consume(j, s_ref, state):
        vt = jnp.concatenate([vt_ref[per * j + r] for r in range(per)], axis=1)
        m_news = [jnp.maximum(state[h][0], jnp.max(s_ref[h], axis=0, keepdims=True)) for h in range(2)]
        pts = [jnp.exp2((s_ref[h] - m_news[h]).astype(BF16)) for h in range(2)]
        return tuple((m_news[h], jnp.exp2(state[h][0] - m_news[h]) * state[h][1] + _dot(vt[slots[h], :], pts[h]))
                     for h in range(2))

    def body(t, state):
        j = 2 * t
        scores(j + 1, s1_ref)
        state = consume(j, s0_ref, state)
        scores(j + 2, s0_ref)
        return consume(j + 1, s1_ref, state)

    state = tuple((jnp.full((1, tq), -jnp.inf, F32), jnp.zeros((LANES, tq), F32)) for _ in range(2))
    scores(0, s0_ref)
    state = lax.fori_loop(0, nk // 2 - 1, body, state)
    scores(nk - 1, s1_ref)
    state = consume(nk - 2, s0_ref, state)
    state = consume(nk - 1, s1_ref, state)
    outs = [acc[0:MLA_V, :] / acc[MLA_V:MLA_V + 1, :] for _, acc in state]
    o_ref[...] = jnp.concatenate(outs, axis=0).T.astype(BF16)


def _attn_tile(T):
    return min(512, T // 4)


def _flash(q, k, vt, B, T):
    tv = _attn_tile(T)
    tq, tk = (tv // 2, 2 * tv) if T // tv >= 16 else (2 * tv, tv)
    nq = T // tq
    nk = T // tk
    assert nk % 2 == 0 and tk % tv == 0, (T, tk, tv)
    return pl.pallas_call(
        functools.partial(_flash_kernel, tk=tk, nk=nk),
        grid=(B, MLA_HEADS // 2, nq),
        in_specs=[pl.BlockSpec((tq, 2 * LANES), lambda b, hp, i: (b * nq + i, hp)),
                  pl.BlockSpec((T, 2 * LANES), lambda b, hp, i: (b, hp)),
                  pl.BlockSpec((T // tv, 2 * LANES, tv), lambda b, hp, i: (b, hp, 0))],
        out_specs=pl.BlockSpec((tq, LANES), lambda b, hp, i: (b * nq + i, hp)),
        out_shape=jax.ShapeDtypeStruct((B * T, MLA_HEADS * MLA_V), BF16),
        scratch_shapes=[pltpu.VMEM((2, tk, tq), F32), pltpu.VMEM((2, tk, tq), F32)],
        compiler_params=_cparams("parallel", "parallel", "arbitrary"),
        name="mla_flash",
    )(q, k, vt)


HALO = 16


def _rw_prep_kernel(h_ref, hp_ref, hn_ref, wrw_ref, mup_ref, mun_ref, kk_ref, ka_ref, rk_ref,
                    w0_ref, w2_ref, a0_ref, a2_ref, g2_ref, seg_ref, tri_ref,
                    v_out, kkt_out, rt_out, kh_out, bh_out, kw_out, bw_out, wt_out, bonus_out, g7_out,
                    h_scr, rw_scr, *, tm, npos):
    i = pl.program_id(0)
    h_scr[0:HALO, :] = jnp.where(i % npos != 0, hp_ref[...], jnp.zeros((), BF16))
    h_scr[HALO:HALO + tm, :] = h_ref[...]
    h_scr[HALO + tm:, :] = jnp.where(i % npos != npos - 1, hn_ref[...], jnp.zeros((), BF16))
    rw_scr[...] = _dot(h_scr[...], wrw_ref[...])
    mup = mup_ref[...]
    mun = mun_ref[...]
    cur = rw_scr[HALO:HALO + tm, :]
    prev = rw_scr[HALO - 1:HALO - 1 + tm, :]
    nxt = rw_scr[HALO + 1:HALO + 1 + tm, :]
    rwf = cur * (1.0 - mup - mun) + mup * prev + mun * nxt
    r7 = rwf[:, 0:RW_DIM]
    k7 = rwf[:, RW_DIM:2 * RW_DIM]
    v7 = rwf[:, 2 * RW_DIM:3 * RW_DIM]
    wl = rwf[:, 3 * RW_DIM:3 * RW_DIM + 2 * LORA]
    al = rwf[:, 3 * RW_DIM + 2 * LORA:3 * RW_DIM + 4 * LORA]
    gl = rwf[:, 3 * RW_DIM + 4 * LORA:]
    seg = seg_ref[...]
    kx = k7 * kk_ref[...]
    kkn = kx * lax.rsqrt(jnp.maximum(_dot_0_1(kx * kx, seg), 1e-24))
    v_out[...] = v7.astype(BF16)
    g7_out[...] = _dot(_sigmoid(gl).astype(BF16), g2_ref[...])
    tw = jnp.tanh(wl).astype(BF16)
    alb = al.astype(BF16)
    bonus = jnp.zeros((tm, RW_DIM), F32)
    for d in range(2):
        z = -(w0_ref[d] + _dot(tw, w2_ref[d]))
        softplus = jnp.maximum(z, 0.0) + jnp.log(1.0 + jnp.exp(-jnp.abs(z)))
        lw = -jnp.exp(-softplus - 0.5)
        a = _sigmoid(a0_ref[d] + _dot(alb, a2_ref[d]))
        kd = k7 * (1.0 + (a - 1.0) * ka_ref[...])
        b = kkn * a
        bonus = bonus + _dot_0_1(r7 * kd * rk_ref[...], seg) * v7
        cum = _dot_0_1_l(tri_ref[d], lw)
        last = CHUNK - 1 if d == 0 else 0
        wtot_rows = [jnp.exp(cum[CHUNK * ci + last:CHUNK * ci + last + 1, :]) for ci in range(tm // CHUNK)]
        wtot = jnp.concatenate([jnp.broadcast_to(r, (CHUNK, RW_DIM)) for r in wtot_rows], axis=0)
        w_incl = jnp.exp(cum)
        w_excl = jnp.exp(cum - lw)
        w_inv = 1.0 / w_incl
        w_rest = wtot * w_inv
        kkt_out[d] = (kkn * w_excl).astype(BF16)
        rt_out[d] = (r7 * w_incl).astype(BF16)
        kh_out[d] = (kd * w_inv).astype(BF16)
        bh_out[d] = (b * w_inv).astype(BF16)
        kw_out[d] = (kd * w_rest).astype(BF16)
        bw_out[d] = (b * w_rest).astype(BF16)
        for ci in range(tm // CHUNK):
            wt_out[d, ci] = wtot_rows[ci]
    bonus_out[...] = bonus


def _rw_prep(h2, T, w):
    ntok = h2.shape[0]
    tm = min(512, T)
    npos = T // tm
    nhalo = ntok // HALO
    rh = tm // HALO
    row = lambda i: (i, 0)
    drow = lambda i: (0, i, 0)
    dspec = pl.BlockSpec((2, tm, RW_DIM), drow)
    dshape = jax.ShapeDtypeStruct((2, ntok, RW_DIM), BF16)
    tri = _chunk_masks(tm)
    outs = pl.pallas_call(
        functools.partial(_rw_prep_kernel, tm=tm, npos=npos),
        grid=(ntok // tm,),
        in_specs=[pl.BlockSpec((tm, D_MODEL), row),
                  pl.BlockSpec((HALO, D_MODEL), lambda i: (jnp.maximum(i * rh - 1, 0), 0)),
                  pl.BlockSpec((HALO, D_MODEL), lambda i: (jnp.minimum((i + 1) * rh, nhalo - 1), 0)),
                  _full((D_MODEL, RW_COLS)), _full((1, RW_COLS)), _full((1, RW_COLS)),
                  _full((1, RW_DIM)), _full((1, RW_DIM)), _full((1, RW_DIM)),
                  _full((2, 1, RW_DIM)), _full((2, 2 * LORA, RW_DIM)), _full((2, 1, RW_DIM)),
                  _full((2, 2 * LORA, RW_DIM)), _full((GATE_LORA, RW_DIM)), _full((RW_DIM, RW_DIM)),
                  _full((2, tm, tm))],
        out_specs=[pl.BlockSpec((tm, RW_DIM), row), dspec, dspec, dspec, dspec, dspec, dspec,
                   pl.BlockSpec((2, tm // CHUNK, 1, RW_DIM), lambda i: (0, i, 0, 0)),
                   pl.BlockSpec((tm, RW_DIM), row), pl.BlockSpec((tm, RW_DIM), row)],
        out_shape=[jax.ShapeDtypeStruct((ntok, RW_DIM), BF16), dshape, dshape, dshape, dshape, dshape, dshape,
                   jax.ShapeDtypeStruct((2, ntok // CHUNK, 1, RW_DIM), F32),
                   jax.ShapeDtypeStruct((ntok, RW_DIM), F32), jax.ShapeDtypeStruct((ntok, RW_DIM), F32)],
        scratch_shapes=[pltpu.VMEM((tm + 2 * HALO, D_MODEL), BF16), pltpu.VMEM((tm + 2 * HALO, RW_COLS), F32)],
        compiler_params=_cparams("parallel"),
        name="rw_prep",
    )(h2, h2, h2, w["w_rw"], w["mu_prev"], w["mu_next"], w["k_k"], w["k_a"], w["r_k"],
      w["w0"], w["w2"], w["a0"], w["a2"], w["g2"], w["seg"], tri)
    return outs


def _chunk_masks(tm):
    t = np.arange(tm)
    same = (t[:, None] // CHUNK) == (t[None, :] // CHUNK)
    fwd = same & (t[None, :] <= t[:, None])
    bwd = same & (t[None, :] >= t[:, None])
    return jnp.asarray(np.stack([fwd, bwd]).astype(np.float32), BF16)


def _rw_scan_kernel(*refs):
    ins = refs[:16]
    yf_ref, yb_ref, s_ref = refs[16:]
    c = pl.program_id(1)

    @pl.when(c == 0)
    def _():
        s_ref[...] = jnp.zeros(s_ref.shape, F32)

    t_pos = lax.broadcasted_iota(jnp.int32, (CHUNK, GW), 0)
    s_pos = lax.broadcasted_iota(jnp.int32, (CHUNK, GW), 1) % CHUNK
    eye_c = (t_pos == s_pos).astype(F32)
    ri = lax.broadcasted_iota(jnp.int32, (GW, GW), 0)
    ci = lax.broadcasted_iota(jnp.int32, (GW, GW), 1)
    head_mask = (ri // CHUNK) == (ci // RW_HEAD)
    stack = lambda a: jnp.where(head_mask, jnp.concatenate([a] * GROUP, axis=0), jnp.zeros((), BF16))

    def prepare(orders):
        items = []
        for d, gi, order in ((d, gi, o) for d in range(2) for gi in range(RW_HEADS // GROUP) for o in orders):
            kkt, rt, kh, bh, kw, bw, v, wt = ins[8 * d:8 * d + 8]
            strict = (s_pos < t_pos) if d == 0 else (s_pos > t_pos)
            incl = (s_pos <= t_pos) if d == 0 else (s_pos >= t_pos)
            sub = order if d == 0 else SCAN_CHUNKS - 1 - order
            rows = slice(CHUNK * sub, CHUNK * (sub + 1))
            sl = slice(GW * gi, GW * (gi + 1))
            it = dict(y_ref=(yf_ref, yb_ref)[d], sl=sl, rows=rows, s_ref=s_ref.at[d, gi], wtot=wt[sub][:, sl],
                      order=order)
            v_c = v[rows, sl]
            kkt_rt = jnp.concatenate([kkt[rows, sl], rt[rows, sl]], axis=0)
            a_all = _dot_nt(kkt_rt, jnp.concatenate([stack(kh[rows, sl]), stack(bh[rows, sl])], axis=0))
            ak = jnp.where(strict, a_all[:CHUNK, :GW], 0.0).astype(BF16)
            bk = jnp.where(incl, a_all[CHUNK:, :GW], 0.0).astype(BF16)
            it.update(kkt_rt=kkt_rt, v=v_c, kbw=jnp.concatenate([kw[rows, sl], bw[rows, sl]], axis=0),
                      abk=jnp.concatenate([ak, bk], axis=0), bb=jnp.where(incl, a_all[CHUNK:, GW:], 0.0).astype(BF16))
            ab = jnp.where(strict, a_all[:CHUNK, GW:], 0.0)
            it.update(tinv=eye_c - ab, pw=ab.astype(BF16))
            items.append(it)
        for it in items:
            it["abk_v"] = _dot(it["abk"], stack(it["v"]))
        for it in items:
            it["pw"] = _dot(it["pw"], stack(it["pw"])).astype(BF16)
        for k in range(5):
            for it in items:
                sk = stack(it["pw"])
                if k < 4:
                    res = _dot(jnp.concatenate([it["pw"], it["tinv"].astype(BF16)], axis=0), sk)
                    it["pw"] = res[:CHUNK].astype(BF16)
                    it["tinv"] = it["tinv"] + res[CHUNK:]
                else:
                    it["tinv"] = (it["tinv"] + _dot(it["tinv"].astype(BF16), sk)).astype(BF16)
        return items

    def advance(chains):
        for ch in chains:
            sb = ch["s_ref"][...].astype(BF16)
            ch["zs"] = _dot_nt(ch["kkt_rt"], sb) + ch["abk_v"]
        for ch in chains:
            ch["u"] = (-_dot(ch["tinv"], stack(ch["zs"][:CHUNK].astype(BF16)))).astype(BF16)
        for ch in chains:
            ch["y_ref"][ch["rows"], ch["sl"]] = ch["zs"][CHUNK:] + _dot(ch["bb"], stack(ch["u"]))
        for ch in chains:
            upd = _dot_tn(jnp.concatenate([ch["v"], ch["u"]], axis=0), ch["kbw"])
            ch["s_ref"][...] = ch["s_ref"][...] * ch["wtot"] + jnp.where(head_mask, upd, 0.0)

    prepared = prepare(range(SCAN_CHUNKS))
    for order in range(SCAN_CHUNKS):
        advance([it for it in prepared if it["order"] == order])


def _rw_scan(v, kkt, rt, kh, bh, kw, bw, wt, B, T):
    ntok = B * T
    rows = SCAN_CHUNKS * CHUNK
    nb = T // rows
    in_specs = []
    args = []
    for d in range(2):
        if d == 0:
            cm = lambda b, c: b * nb + c
        else:
            cm = lambda b, c: b * nb + (nb - 1 - c)
        for arr in (kkt, rt, kh, bh, kw, bw):
            in_specs.append(pl.BlockSpec((None, rows, RW_DIM), lambda b, c, cm=cm, d=d: (d, cm(b, c), 0)))
            args.append(arr)
        in_specs.append(pl.BlockSpec((rows, RW_DIM), lambda b, c, cm=cm: (cm(b, c), 0)))
        args.append(v)
        in_specs.append(pl.BlockSpec((None, SCAN_CHUNKS, 1, RW_DIM), lambda b, c, cm=cm, d=d: (d, cm(b, c), 0, 0)))
        args.append(wt)
    return pl.pallas_call(
        _rw_scan_kernel,
        grid=(B, nb),
        in_specs=in_specs,
        out_specs=[pl.BlockSpec((rows, RW_DIM), lambda b, c: (b * nb + c, 0)),
                   pl.BlockSpec((rows, RW_DIM), lambda b, c: (b * nb + (nb - 1 - c), 0))],
        out_shape=[jax.ShapeDtypeStruct((ntok, RW_DIM), F32), jax.ShapeDtypeStruct((ntok, RW_DIM), F32)],
        scratch_shapes=[pltpu.VMEM((2, RW_HEADS // GROUP, GW, GW), F32)],
        compiler_params=_cparams("parallel", "arbitrary"),
        name="rw_scan",
    )(*args)


def _mem_kv_kernel(m_ref, g_ref, wk_ref, wv_ref, gk_ref, k_out, v_out):
    m = _rms(m_ref[...], g_ref[...]).astype(BF16)
    k = _dot(m, wk_ref[...])
    v_out[...] = _dot(m, wv_ref[...]).astype(BF16)
    for hd in range(X_HEADS):
        sl = slice(X_HEAD * hd, X_HEAD * (hd + 1))
        k_out[:, sl] = _rms(k[:, sl], gk_ref[...]).astype(BF16)


def _mem_kv(mem2, w):
    n = mem2.shape[0]
    row = lambda i: (i, 0)
    return pl.pallas_call(
        _mem_kv_kernel,
        grid=(n // N_MEM,),
        in_specs=[pl.BlockSpec((N_MEM, D_MODEL), row), _full((1, D_MODEL)), _full((D_MODEL, X_DIM)),
                  _full((D_MODEL, X_DIM)), _full((1, X_HEAD))],
        out_specs=[pl.BlockSpec((N_MEM, X_DIM), row), pl.BlockSpec((N_MEM, X_DIM), row)],
        out_shape=[jax.ShapeDtypeStruct((n, X_DIM), BF16), jax.ShapeDtypeStruct((n, X_DIM), BF16)],
        compiler_params=_cparams("parallel"),
        name="mem_kv",
    )(mem2, w["mem_norm_g"], w["w_mk"], w["w_mv"], w["x_kn_g"])


def _xattn_kernel(h_ref, wq_ref, gq_ref, mk_ref, mv_ref, o_ref):
    q = _dot(h_ref[...], wq_ref[...])
    for hd in range(X_HEADS):
        sl = slice(X_HEAD * hd, X_HEAD * (hd + 1))
        qh = _rms(q[:, sl], gq_ref[...]).astype(BF16)
        s = _dot_nt(qh, mk_ref[:, sl])
        p = jnp.exp(s - jnp.max(s, axis=-1, keepdims=True))
        o = _dot(p.astype(BF16), mv_ref[:, sl]) / jnp.sum(p, axis=-1, keepdims=True)
        o_ref[:, sl] = o.astype(BF16)


def _xattn(h2, mk, mv, B, T, w):
    tq = min(512, T)
    nq = T // tq
    return pl.pallas_call(
        _xattn_kernel,
        grid=(B, nq),
        in_specs=[pl.BlockSpec((tq, D_MODEL), lambda b, i: (b * nq + i, 0)),
                  _full((D_MODEL, X_DIM)), _full((1, X_HEAD)),
                  pl.BlockSpec((N_MEM, X_DIM), lambda b, i: (b, 0)), pl.BlockSpec((N_MEM, X_DIM), lambda b, i: (b, 0))],
        out_specs=pl.BlockSpec((tq, X_DIM), lambda b, i: (b * nq + i, 0)),
        out_shape=jax.ShapeDtypeStruct((B * T, X_DIM), BF16),
        compiler_params=_cparams("parallel", "parallel"),
        name="xattn",
    )(h2, w["w_xq"], w["x_qn_g_scaled"], mk, mv)


def _merge_kernel(x_ref, oa_ref, yf_ref, yb_ref, bonus_ref, g7_ref, oc_ref, g_ref, wg_ref, woa_ref, wob_ref,
                  woc_ref, wout_ref, lng_ref, lnb_ref, seg_ref, o_ref):
    x = x_ref[...]
    h = _rms(x, g_ref[...]).astype(BF16)
    seg = seg_ref[...]
    gate = lambda bi: _sigmoid(_dot(h, wg_ref[:, D_MODEL * bi:D_MODEL * (bi + 1)]))
    y7 = yf_ref[...] + yb_ref[...] + bonus_ref[...]
    mu = _dot_0_1(y7, seg) * (1.0 / RW_HEAD)
    merged = gate(0) * _dot(oa_ref[...], woa_ref[...])
    dy = y7 - mu
    var = _dot_0_1(dy * dy, seg) * (1.0 / RW_HEAD)
    merged = merged + gate(2) * _dot(oc_ref[...], woc_ref[...])
    gate_b = gate(1)
    y7 = dy * lax.rsqrt(var + LNX_EPS) * lng_ref[...] + lnb_ref[...]
    merged = merged + gate_b * _dot((y7 * g7_ref[...]).astype(BF16), wob_ref[...])
    o_ref[...] = x + _dot(merged.astype(BF16), wout_ref[...])


def _merge(x2, oa, yf, yb, bonus, g7, oc, T, w):
    ntok = x2.shape[0]
    tm = min(512, T)
    row = lambda i: (i, 0)
    half = lambda: pl.BlockSpec((tm, RW_DIM), row)
    return pl.pallas_call(
        _merge_kernel,
        grid=(ntok // tm,),
        in_specs=[pl.BlockSpec((tm, D_MODEL), row), half(), half(), half(), half(), half(), half(),
                  _full((1, D_MODEL)), _full((D_MODEL, 3 * D_MODEL)), _full((RW_DIM, D_MODEL)),
                  _full((RW_DIM, D_MODEL)), _full((RW_DIM, D_MODEL)), _full((D_MODEL, D_MODEL)),
                  _full((1, RW_DIM)), _full((1, RW_DIM)), _full((RW_DIM, RW_DIM))],
        out_specs=pl.BlockSpec((tm, D_MODEL), row),
        out_shape=jax.ShapeDtypeStruct((ntok, D_MODEL), F32),
        compiler_params=_cparams("parallel"),
        name="merge",
    )(x2, oa, yf, yb, bonus, g7, oc, w["norm_mix_g"], w["w_gate"], w["w_o_a"], w["w_o_b"], w["w_o_c"],
      w["w_out"], w["lnx_g"], w["lnx_b"], w["seg"])


MXU_TILE = 256
FFN_SPLITS = (0, 5 * MXU_TILE, D_FF)


def _ffn_kernel(x_ref, xp_ref, xn_ref, g_ref, wug_ref, wuv_ref, cw_ref, cb_ref, wd_ref, o_ref, h_scr, ug_scr,
                *, tm, npos):
    i = pl.program_id(0)
    not_first = (i % npos != 0).astype(F32)
    not_last = (i % npos != npos - 1).astype(F32)
    g = g_ref[...]
    h_scr[0:SUBLANES, :] = (_rms(xp_ref[...], g) * not_first).astype(BF16)
    h_scr[SUBLANES:SUBLANES + tm, :] = _rms(x_ref[...], g).astype(BF16)
    h_scr[SUBLANES + tm:, :] = (_rms(xn_ref[...], g) * not_last).astype(BF16)
    out = x_ref[...]
    for lo, hi in zip(FFN_SPLITS[:-1], FFN_SPLITS[1:]):
        n = hi - lo
        ug_scr[:, 0:n] = _dot(h_scr[...], wug_ref[:, lo:hi])
        uv = _dot(h_scr[SUBLANES:SUBLANES + tm, :], wuv_ref[:, lo:hi])
        cw = cw_ref[:, lo:hi]
        c = (cw[0:1] * ug_scr[SUBLANES - 1:SUBLANES - 1 + tm, 0:n] + cw[1:2] * ug_scr[SUBLANES:SUBLANES + tm, 0:n]
             + cw[2:3] * ug_scr[SUBLANES + 1:SUBLANES + 1 + tm, 0:n] + cb_ref[:, lo:hi])
        act = 0.5 * c * (1.0 + lax.erf(c * np.float32(1.0 / np.sqrt(2.0)))) * uv
        out = out + _dot(act.astype(BF16), wd_ref[lo:hi, :])
    o_ref[...] = out


def _ffn(x2, T, w):
    ntok = x2.shape[0]
    tm = min(512, T)
    fc = max(hi - lo for lo, hi in zip(FFN_SPLITS[:-1], FFN_SPLITS[1:]))
    npos = T // tm
    nblk8 = ntok // SUBLANES
    r8 = tm // SUBLANES
    row = lambda i: (i, 0)
    return pl.pallas_call(
        functools.partial(_ffn_kernel, tm=tm, npos=npos),
        grid=(ntok // tm,),
        in_specs=[pl.BlockSpec((tm, D_MODEL), row),
                  pl.BlockSpec((SUBLANES, D_MODEL), lambda i: (jnp.maximum(i * r8 - 1, 0), 0)),
                  pl.BlockSpec((SUBLANES, D_MODEL), lambda i: (jnp.minimum((i + 1) * r8, nblk8 - 1), 0)),
                  _full((1, D_MODEL)), _full((D_MODEL, D_FF)), _full((D_MODEL, D_FF)), _full((3, D_FF)),
                  _full((1, D_FF)), _full((D_FF, D_MODEL))],
        out_specs=pl.BlockSpec((tm, D_MODEL), row),
        out_shape=jax.ShapeDtypeStruct((ntok, D_MODEL), F32),
        scratch_shapes=[pltpu.VMEM((tm + 2 * SUBLANES, D_MODEL), BF16), pltpu.VMEM((tm + 2 * SUBLANES, fc), F32)],
        compiler_params=_cparams("parallel"),
        name="conv_ffn",
    )(x2, x2, x2, w["norm_ffn_g"], w["w_up_gate"], w["w_up_val"], w["conv_w"], w["conv_b"], w["w_down"])


def _prep_weights(p):
    w = {}
    row = lambda a: a.reshape(1, -1).astype(F32)
    w_in = p["w_in"]
    o = np.cumsum([0, Q_LORA, KV_LORA, MLA_ROPE, RW_COLS, X_DIM, 3 * D_MODEL])
    seg = lambda i: w_in[:, o[i]:o[i + 1]]
    zc = lambda n: jnp.zeros((D_MODEL, n), F32)
    w["w_c"] = jnp.concatenate([seg(0), seg(1), zc(MLA_NOPE), seg(2), zc(LANES - MLA_QK)], axis=1).astype(BF16)
    w["w_rw"] = seg(3).astype(BF16)
    w["w_xq"] = seg(4).astype(BF16)
    w["w_gate"] = seg(5).astype(BF16)
    w["norm_mix_g"] = row(p["norm_mix_g"])
    w["q_norm_g"] = row(p["q_norm_g"])
    w["kv_norm_g"] = row(p["kv_norm_g"])
    pad_slot = lambda a: jnp.pad(a, ((0, 0), (0, 0), (0, LANES - a.shape[-1]))).reshape(a.shape[0], -1)
    uq = p["w_uq"].reshape(Q_LORA, MLA_HEADS, MLA_QK)
    w["w_uq"] = pad_slot(uq).astype(BF16)
    half = MLA_ROPE // 2
    uq_rot = jnp.concatenate([jnp.zeros_like(uq[:, :, :MLA_NOPE]), -uq[:, :, MLA_NOPE + half:],
                              uq[:, :, MLA_NOPE:MLA_NOPE + half]], axis=-1)
    w["w_uq_rot"] = pad_slot(uq_rot).astype(BF16)
    ukv =p["w_ukv"].reshape(KV_LORA, MLA_HEADS, MLA_NOPE + MLA_V)
    w["w_uk"] = pad_slot(ukv[:, :, :MLA_NOPE]).astype(BF16)
    w["w_uv"] = pad_slot(ukv[:, :, MLA_NOPE:]).T.astype(BF16)
    w["mla_qn_g_scaled"] = p["mla_qn_g"].reshape(-1) * np.float32(MLA_QK ** -0.5 * np.log2(np.e))
    w["mla_kn_g"] = p["mla_kn_g"].reshape(-1)
    for name in ("mu_prev", "mu_next", "k_k", "k_a", "r_k", "lnx_g", "lnx_b", "mem_norm_g", "x_kn_g", "norm_ffn_g",
                 "conv_b"):
        w[name] = row(p[name])
    w["w0"] = jnp.stack([p["w0_f"], p["w0_b"]]).reshape(2, 1, RW_DIM)
    w["a0"] = jnp.stack([p["a0_f"], p["a0_b"]]).reshape(2, 1, RW_DIM)
    zl = jnp.zeros((LORA, RW_DIM), F32)
    w["w2"] = jnp.stack([jnp.concatenate([p["w2_f"], zl]), jnp.concatenate([zl, p["w2_b"]])]).astype(BF16)
    w["a2"] = jnp.stack([jnp.concatenate([p["a2_f"], zl]), jnp.concatenate([zl, p["a2_b"]])]).astype(BF16)
    w["g2"] = p["g2"].astype(BF16)
    hid = np.arange(RW_DIM) // RW_HEAD
    w["seg"] = jnp.asarray((hid[:, None] == hid[None, :]).astype(np.float32), BF16)
    mkv = p["w_mkv"].reshape(D_MODEL, X_HEADS, 2 * X_HEAD)
    w["w_mk"] = mkv[:, :, :X_HEAD].reshape(D_MODEL, X_DIM).astype(BF16)
    w["w_mv"] = mkv[:, :, X_HEAD:].reshape(D_MODEL, X_DIM).astype(BF16)
    w["x_qn_g_scaled"] = row(p["x_qn_g"]) * np.float32(X_HEAD ** -0.5)
    for name in ("w_o_a", "w_o_b", "w_o_c", "w_out", "w_down"):
        w[name] = p[name].astype(BF16)
    w["w_up_gate"] = p["w_up"][:, :D_FF].astype(BF16)
    w["w_up_val"] = p["w_up"][:, D_FF:].astype(BF16)
    w["conv_w"] = p["conv_w"].astype(F32)
    return w


def _rope_tables(T, gq, gk):
    half = MLA_ROPE // 2
    inv = jnp.power(ROPE_THETA, -jnp.arange(half, dtype=F32) / half)
    ang = jnp.arange(T, dtype=F32)[:, None] * inv[None, :]
    cos, sin = jnp.cos(ang), jnp.sin(ang)
    z = lambda n: jnp.zeros((T, n), F32)
    pad = z(LANES - MLA_QK)

    def own(g):
        g1, g2 = g[MLA_NOPE:MLA_NOPE + half], g[MLA_NOPE + half:]
        return jnp.concatenate([jnp.broadcast_to(g[:MLA_NOPE], (T, MLA_NOPE)), cos * g1, cos * g2, pad], axis=1)

    g1, g2 = gq[MLA_NOPE:MLA_NOPE + half], gq[MLA_NOPE + half:]
    qb = jnp.concatenate([z(MLA_NOPE), sin * g2, sin * g1, pad], axis=1)
    g1, g2 = gk[MLA_NOPE:MLA_NOPE + half], gk[MLA_NOPE + half:]
    kba = jnp.concatenate([z(MLA_NOPE), -sin * g2, z(half), pad], axis=1)
    kbb = jnp.concatenate([z(MLA_NOPE), z(half), sin * g1, pad], axis=1)
    return own(gq), qb, own(gk), kba, kbb


def _layer(x, mem, w):
    B, T, _ = x.shape
    x2 = x.reshape(B * T, D_MODEL)
    q, k, v, h2 = _mla_prep(x2, T, w, _rope_tables(T, w["mla_qn_g_scaled"], w["mla_kn_g"]))
    o_a = _flash(q, k, v, B, T)
    v7, kkt, rt, kh, bh, kw, bw, wt, bonus, g7 = _rw_prep(h2, T, w)
    y_f, y_b = _rw_scan(v7, kkt, rt, kh, bh, kw, bw, wt, B, T)
    mk, mv = _mem_kv(mem.reshape(B * N_MEM, D_MODEL), w)
    o_c = _xattn(h2, mk, mv, B, T, w)
    x1 = _merge(x2, o_a, y_f, y_b, bonus, g7, o_c, T, w)
    return _ffn(x1, T, w).reshape(B, T, D_MODEL)


def kernel(x_prompt, x_sample, mem_prompt, mem_sample, norm_mix_g, w_in, q_norm_g, w_uq, kv_norm_g, w_ukv, mla_qn_g, mla_kn_g, w_o_a, mu_prev, mu_next, w0_f, w2_f, a0_f, a2_f, w0_b, w2_b, a0_b, a2_b, g2, k_k, k_a, r_k, lnx_g, lnx_b, w_o_b, mem_norm_g, w_mkv, x_qn_g, x_kn_g, w_o_c, w_out, norm_ffn_g, w_up, conv_w, conv_b, w_down):
    p = dict(norm_mix_g=norm_mix_g, w_in=w_in, q_norm_g=q_norm_g, w_uq=w_uq, kv_norm_g=kv_norm_g, w_ukv=w_ukv,
             mla_qn_g=mla_qn_g, mla_kn_g=mla_kn_g, w_o_a=w_o_a, mu_prev=mu_prev, mu_next=mu_next,
             w0_f=w0_f, w2_f=w2_f, a0_f=a0_f, a2_f=a2_f, w0_b=w0_b, w2_b=w2_b, a0_b=a0_b, a2_b=a2_b,
             g2=g2, k_k=k_k, k_a=k_a, r_k=r_k, lnx_g=lnx_g, lnx_b=lnx_b, w_o_b=w_o_b,
             mem_norm_g=mem_norm_g, w_mkv=w_mkv, x_qn_g=x_qn_g, x_kn_g=x_kn_g, w_o_c=w_o_c, w_out=w_out,
             norm_ffn_g=norm_ffn_g, w_up=w_up, conv_w=conv_w, conv_b=conv_b, w_down=w_down)
    w = _prep_weights({name: a[0] for name, a in p.items()})
    return (_layer(x_prompt, mem_prompt, w), _layer(x_sample, mem_sample, w))
```

```python
import functools

import numpy as np
import jax
import jax.numpy as jnp
from jax import lax
from jax.experimental import pallas as pl
from jax.experimental.pallas import tpu as pltpu

F32 = jnp.float32
BF16 = jnp.bfloat16

D_MODEL = 1024
RMS_EPS = 1e-6
N_MEM = 256
MLA_HEADS = 8
MLA_NOPE = 64
MLA_ROPE = 32
MLA_QK = MLA_NOPE + MLA_ROPE
MLA_V = 64
Q_LORA = 384
KV_LORA = 256
ROPE_THETA = 10000.0
RW_HEADS = 8
RW_HEAD = 64
RW_DIM = RW_HEADS * RW_HEAD
LORA = 64
GATE_LORA = 128
RW_COLS = 3 * RW_DIM + 4 * LORA + GATE_LORA
LNX_EPS = 64e-5
X_HEADS = 4
X_HEAD = 128
X_DIM = X_HEADS * X_HEAD
D_FF = 2816

LANES = 128
SUBLANES = 8
CHUNK = 64
GROUP = 2
GW = GROUP * RW_HEAD
SLOT_COLS = MLA_HEADS * LANES
MLA_C_COLS = Q_LORA + KV_LORA + LANES
SCAN_CHUNKS = 8
VMEM_LIMIT = 56 * 1024 * 1024


def _cparams(*sem):
    return pltpu.CompilerParams(dimension_semantics=sem, vmem_limit_bytes=VMEM_LIMIT)


def _rms(x, g, eps=RMS_EPS):
    return x * lax.rsqrt(jnp.mean(x * x, axis=-1, keepdims=True) + eps) * g


def _dot(a, b):
    return jnp.dot(a, b, preferred_element_type=F32)


def _dot_nt(a, b):
    return lax.dot_general(a, b, (((1,), (1,)), ((), ())), preferred_element_type=F32)


def _dot_tn(a, b):
    return lax.dot_general(a, b, (((0,), (0,)), ((), ())), preferred_element_type=F32)


def _dot_0_1(x, e):
    return _dot(x.astype(BF16), e)


def _dot_0_1_l(e, x):
    hi = x.astype(BF16)
    lo = (x - hi.astype(F32)).astype(BF16)
    return _dot(e, hi) + _dot(e, lo)


def _sigmoid(z):
    return 1.0 / (1.0 + jnp.exp(-z))


def _full(shape):
    nd = len(shape)
    return pl.BlockSpec(shape, lambda *_: (0,) * nd, pipeline_mode=pl.Buffered(1))


def _mla_prep_kernel(x_ref, g_ref, wc_ref, gq_ref, gkv_ref, wuq_ref, wuqr_ref, wuk_ref, wuv_ref,
                     qa_ref, qb_ref, ka_ref, kba_ref, kbb_ref, q_out, k_out, v_out, h_out):
    tm = x_ref.shape[0]
    half = MLA_ROPE // 2
    halves = [slice(0, tm // 2), slice(tm // 2, tm)]
    cs = []
    for r in halves:
        h = _rms(x_ref[r, :], g_ref[...]).astype(BF16)
        h_out[r, :] = h
        cs.append(_dot(h, wc_ref[...]))
    proj = []
    for c in cs:
        cq = _rms(c[:, :Q_LORA], gq_ref[...]).astype(BF16)
        ckv = _rms(c[:, Q_LORA:Q_LORA + KV_LORA], gkv_ref[...]).astype(BF16)
        proj.append((_dot(cq, wuq_ref[...]), _dot(cq, wuqr_ref[...]), _dot(ckv, wuk_ref[...]),
                     _dot_nt(wuv_ref[...], ckv)))
    for r, c, (q, qp, k, vt) in zip(halves, cs, proj):
        slot_row = lax.broadcasted_iota(jnp.int32, vt.shape, 0) % LANES
        v_out[0, :, r] = jnp.where(slot_row == MLA_V, 1.0, vt).astype(BF16)
        kr = c[:, Q_LORA + KV_LORA:]
        qa = qa_ref[r, :]
        qb = qb_ref[r, :]
        ka = ka_ref[r, :]
        kr_roped = (kr * ka + pltpu.roll(kr, LANES - half, 1) * kba_ref[r, :]
                    + pltpu.roll(kr, half, 1) * kbb_ref[r, :])
        kr_ss = jnp.sum(kr * kr, axis=-1, keepdims=True)
        for hd in range(MLA_HEADS):
            sl = slice(LANES * hd, LANES * (hd + 1))
            t = q[:, sl]
            ss = jnp.sum(t * t, axis=-1, keepdims=True) * (1.0 / MLA_QK)
            q_out[r, sl] = ((t * qa + qp[:, sl] * qb) * lax.rsqrt(ss + RMS_EPS)).astype(BF16)
            t = k[:, sl]
            ss = (jnp.sum(t * t, axis=-1, keepdims=True) + kr_ss) * (1.0 / MLA_QK)
            k_out[r, sl] = ((t * ka + kr_roped) * lax.rsqrt(ss + RMS_EPS)).astype(BF16)


def _mla_prep(x2, T, w, tabs):
    ntok = x2.shape[0]
    tm = _attn_tile(T)
    npos = T // tm
    row = lambda i: (i, 0)
    pos = lambda i: (i % npos, 0)
    return pl.pallas_call(
        _mla_prep_kernel,
        grid=(ntok // tm,),
        in_specs=[pl.BlockSpec((tm, D_MODEL), row), _full((1, D_MODEL)), _full((D_MODEL, MLA_C_COLS)),
                  _full((1, Q_LORA)), _full((1, KV_LORA)), _full((Q_LORA, SLOT_COLS)), _full((Q_LORA, SLOT_COLS)),
                  _full((KV_LORA, SLOT_COLS)), _full((SLOT_COLS, KV_LORA))] + [pl.BlockSpec((tm, LANES), pos)] * 5,
        out_specs=[pl.BlockSpec((tm, SLOT_COLS), row), pl.BlockSpec((tm, SLOT_COLS), row),
                   pl.BlockSpec((1, SLOT_COLS, tm), lambda i: (i, 0, 0)), pl.BlockSpec((tm, D_MODEL), row)],
        out_shape=[jax.ShapeDtypeStruct((ntok, SLOT_COLS), BF16), jax.ShapeDtypeStruct((ntok, SLOT_COLS), BF16),
                   jax.ShapeDtypeStruct((ntok // tm, SLOT_COLS, tm), BF16),
                   jax.ShapeDtypeStruct((ntok, D_MODEL), BF16)],
        compiler_params=_cparams("parallel"),
        name="mla_prep",
    )(x2, w["norm_mix_g"], w["w_c"], w["q_norm_g"], w["kv_norm_g"], w["w_uq"], w["w_uq_rot"], w["w_uk"],
      w["w_uv"], *tabs)


def _flash_kernel(q_ref, k_ref, vt_ref, o_ref, s0_ref, s1_ref, *, tk, nk):
    tq = q_ref.shape[0]
    slots = [slice(LANES * h, LANES * (h + 1)) for h in range(2)]
    per = tk // vt_ref.shape[2]

    def scores(j, s_ref):
        kblk = k_ref[pl.ds(pl.multiple_of(j * tk, tk), tk), :]
        for h in range(2):
            s_ref[h] = _dot_nt(kblk[:, slots[h]], q_ref[:, slots[h]])

    def consume(j, s_ref, state):
        vt = jnp.concatenate([vt_ref[per * j + r] for r in range(per)], axis=1)
        m_news = [jnp.maximum(state[h][0], jnp.max(s_ref[h], axis=0, keepdims=True)) for h in range(2)]
        pts = [jnp.exp2((s_ref[h] - m_news[h]).astype(BF16)) for h in range(2)]
        return tuple((m_news[h], jnp.exp2(state[h][0] - m_news[h]) * state[h][1] + _dot(vt[slots[h], :], pts[h]))
                     for h in range(2))

    def body(t, state):
        j = 2 * t
        scores(j + 1, s1_ref)
        state = consume(j, s0_ref, state)
        scores(j + 2, s0_ref)
        return consume(j + 1, s1_ref, state)

    state = tuple((jnp.full((1, tq), -jnp.inf, F32), jnp.zeros((LANES, tq), F32)) for _ in range(2))
    scores(0, s0_ref)
    state = lax.fori_loop(0, nk // 2 - 1, body, state)
    scores(nk - 1, s1_ref)
    state = consume(nk - 2, s0_ref, state)
    state = consume(nk - 1, s1_ref, state)
    outs = [acc[0:MLA_V, :] / acc[MLA_V:MLA_V + 1, :] for _, acc in state]
    o_ref[...] = jnp.concatenate(outs, axis=0).T.astype(BF16)


def _attn_tile(T):
    return min(512, T // 4)


def _flash(q, k, vt, B, T):
    tv = _attn_tile(T)
    tq, tk = (tv // 2, 2 * tv) if T // tv >= 16 else (2 * tv, tv)
    nq = T // tq
    nk = T // tk
    assert nk % 2 == 0 and tk % tv == 0, (T, tk, tv)
    return pl.pallas_call(
        functools.partial(_flash_kernel, tk=tk, nk=nk),
        grid=(B, MLA_HEADS // 2, nq),
        in_specs=[pl.BlockSpec((tq, 2 * LANES), lambda b, hp, i: (b * nq + i, hp)),
                  pl.BlockSpec((T, 2 * LANES), lambda b, hp, i: (b, hp)),
                  pl.BlockSpec((T // tv, 2 * LANES, tv), lambda b, hp, i: (b, hp, 0))],
        out_specs=pl.BlockSpec((tq, LANES), lambda b, hp, i: (b * nq + i, hp)),
        out_shape=jax.ShapeDtypeStruct((B * T, MLA_HEADS * MLA_V), BF16),
        scratch_shapes=[pltpu.VMEM((2, tk, tq), F32), pltpu.VMEM((2, tk, tq), F32)],
        compiler_params=_cparams("parallel", "parallel", "arbitrary"),
        name="mla_flash",
    )(q, k, vt)


HALO = 16


def _rw_prep_kernel(h_ref, hp_ref, hn_ref, wrw_ref, mup_ref, mun_ref, kk_ref, ka_ref, rk_ref,
                    w0_ref, w2_ref, a0_ref, a2_ref, g2_ref, seg_ref, tri_ref,
                    v_out, kkt_out, rt_out, kh_out, bh_out, kw_out, bw_out, wt_out, bonus_out, g7_out,
                    h_scr, rw_scr, *, tm, npos):
    i = pl.program_id(0)
    h_scr[0:HALO, :] = jnp.where(i % npos != 0, hp_ref[...], jnp.zeros((), BF16))
    h_scr[HALO:HALO + tm, :] = h_ref[...]
    h_scr[HALO + tm:, :] = jnp.where(i % npos != npos - 1, hn_ref[...], jnp.zeros((), BF16))
    rb = tri_ref.shape[1]
    edges = [0] + [HALO + rb * (blk + 1) + HALO for blk in range(tm // rb - 1)] + [tm + 2 * HALO]
    mup = mup_ref[...]
    mun = mun_ref[...]
    mu0 = 1.0 - mup - mun
    seg = seg_ref[...]
    for blk in range(tm // rb):
        lo, hi = edges[blk], edges[blk + 1]
        rw_scr[lo:hi, :] = _dot(h_scr[lo:hi, :], wrw_ref[...])
        r0 = HALO + rb * blk
        rows = slice(rb * blk, rb * (blk + 1))
        rwf = rw_scr[r0:r0 + rb, :] * mu0 + mup * rw_scr[r0 - 1:r0 - 1 + rb, :] + mun * rw_scr[r0 + 1:r0 + 1 + rb, :]
        r7 = rwf[:, 0:RW_DIM]
        k7 = rwf[:, RW_DIM:2 * RW_DIM]
        v7 = rwf[:, 2 * RW_DIM:3 * RW_DIM]
        wl = rwf[:, 3 * RW_DIM:3 * RW_DIM + 2 * LORA]
        al = rwf[:, 3 * RW_DIM + 2 * LORA:3 * RW_DIM + 4 * LORA]
        gl = rwf[:, 3 * RW_DIM + 4 * LORA:]
        kx = k7 * kk_ref[...]
        kkn = kx * lax.rsqrt(jnp.maximum(_dot_0_1(kx * kx, seg), 1e-24))
        v_out[rows, :] = v7.astype(BF16)
        g7_out[rows, :] = _dot(_sigmoid(gl).astype(BF16), g2_ref[...])
        tw = jnp.tanh(wl).astype(BF16)
        alb = al.astype(BF16)
        bonus = jnp.zeros((rb, RW_DIM), F32)
        for d in range(2):
            z = -(w0_ref[d] + _dot(tw, w2_ref[d]))
            softplus = jnp.maximum(z, 0.0) + jnp.log(1.0 + jnp.exp(-jnp.abs(z)))
            lw = -jnp.exp(-softplus - 0.5)
            a = _sigmoid(a0_ref[d] + _dot(alb, a2_ref[d]))
            kd = k7 * (1.0 + (a - 1.0) * ka_ref[...])
            b = kkn * a
            bonus = bonus + _dot_0_1(r7 * kd * rk_ref[...], seg) * v7
            cum = _dot_0_1_l(tri_ref[d], lw)
            last = CHUNK - 1 if d == 0 else 0
            wtot_rows = [jnp.exp(cum[CHUNK * ci + last:CHUNK * ci + last + 1, :]) for ci in range(rb // CHUNK)]
            wtot = jnp.concatenate([jnp.broadcast_to(r, (CHUNK, RW_DIM)) for r in wtot_rows], axis=0)
            w_incl = jnp.exp(cum)
            w_excl = jnp.exp(cum - lw)
            w_inv = 1.0 / w_incl
            w_rest = wtot * w_inv
            kkt_out[d, rows, :] = (kkn * w_excl).astype(BF16)
            rt_out[d, rows, :] = (r7 * w_incl).astype(BF16)
            kh_out[d, rows, :] = (kd * w_inv).astype(BF16)
            bh_out[d, rows, :] = (b * w_inv).astype(BF16)
            kw_out[d, rows, :] = (kd * w_rest).astype(BF16)
            bw_out[d, rows, :] = (b * w_rest).astype(BF16)
            for ci in range(rb // CHUNK):
                wt_out[d, blk * (rb // CHUNK) + ci] = wtot_rows[ci]
        bonus_out[rows, :] = bonus


def _rw_prep(h2, T, w):
    ntok = h2.shape[0]
    tm = min(512, T)
    npos = T // tm
    nhalo = ntok // HALO
    rh = tm // HALO
    row = lambda i: (i, 0)
    drow = lambda i: (0, i, 0)
    dspec = pl.BlockSpec((2, tm, RW_DIM), drow)
    dshape = jax.ShapeDtypeStruct((2, ntok, RW_DIM), BF16)
    rb = min(256, tm)
    tri = _chunk_masks(rb)
    outs = pl.pallas_call(
        functools.partial(_rw_prep_kernel, tm=tm, npos=npos),
        grid=(ntok // tm,),
        in_specs=[pl.BlockSpec((tm, D_MODEL), row),
                  pl.BlockSpec((HALO, D_MODEL), lambda i: (jnp.maximum(i * rh - 1, 0), 0)),
                  pl.BlockSpec((HALO, D_MODEL), lambda i: (jnp.minimum((i + 1) * rh, nhalo - 1), 0)),
                  _full((D_MODEL, RW_COLS)), _full((1, RW_COLS)), _full((1, RW_COLS)),
                  _full((1, RW_DIM)), _full((1, RW_DIM)), _full((1, RW_DIM)),
                  _full((2, 1, RW_DIM)), _full((2, 2 * LORA, RW_DIM)), _full((2, 1, RW_DIM)),
                  _full((2, 2 * LORA, RW_DIM)), _full((GATE_LORA, RW_DIM)), _full((RW_DIM, RW_DIM)),
                  _full((2, rb, rb))],
        out_specs=[pl.BlockSpec((tm, RW_DIM), row), dspec, dspec, dspec, dspec, dspec, dspec,
                   pl.BlockSpec((2, tm // CHUNK, 1, RW_DIM), lambda i: (0, i, 0, 0)),
                   pl.BlockSpec((tm, RW_DIM), row), pl.BlockSpec((tm, RW_DIM), row)],
        out_shape=[jax.ShapeDtypeStruct((ntok, RW_DIM), BF16), dshape, dshape, dshape, dshape, dshape, dshape,
                   jax.ShapeDtypeStruct((2, ntok // CHUNK, 1, RW_DIM), F32),
                   jax.ShapeDtypeStruct((ntok, RW_DIM), F32), jax.ShapeDtypeStruct((ntok, RW_DIM), F32)],
        scratch_shapes=[pltpu.VMEM((tm + 2 * HALO, D_MODEL), BF16), pltpu.VMEM((tm + 2 * HALO, RW_COLS), F32)],
        compiler_params=_cparams("parallel"),
        name="rw_prep",
    )(h2, h2, h2, w["w_rw"], w["mu_prev"], w["mu_next"], w["k_k"], w["k_a"], w["r_k"],
      w["w0"], w["w2"], w["a0"], w["a2"], w["g2"], w["seg"], tri)
    return outs


def _chunk_masks(tm):
    t = np.arange(tm)
    same = (t[:, None] // CHUNK) == (t[None, :] // CHUNK)
    fwd = same & (t[None, :] <= t[:, None])
    bwd = same & (t[None, :] >= t[:, None])
    return jnp.asarray(np.stack([fwd, bwd]).astype(np.float32), BF16)


def _rw_scan_kernel(*refs):
    ins = refs[:16]
    yf_ref, yb_ref, s_ref = refs[16:]
    c = pl.program_id(1)

    @pl.when(c == 0)
    def _():
        s_ref[...] = jnp.zeros(s_ref.shape, F32)

    t_pos = lax.broadcasted_iota(jnp.int32, (CHUNK, GW), 0)
    s_pos = lax.broadcasted_iota(jnp.int32, (CHUNK, GW), 1) % CHUNK
    eye_c = (t_pos == s_pos).astype(F32)
    ri = lax.broadcasted_iota(jnp.int32, (GW, GW), 0)
    ci = lax.broadcasted_iota(jnp.int32, (GW, GW), 1)
    head_mask = (ri // CHUNK) == (ci // RW_HEAD)
    stack = lambda a: jnp.where(head_mask, jnp.concatenate([a] * GROUP, axis=0), jnp.zeros((), BF16))

    def prepare(orders):
        items = []
        for d, gi, order in ((d, gi, o) for d in range(2) for gi in range(RW_HEADS // GROUP) for o in orders):
            kkt, rt, kh, bh, kw, bw, v, wt = ins[8 * d:8 * d + 8]
            strict = (s_pos < t_pos) if d == 0 else (s_pos > t_pos)
            incl = (s_pos <= t_pos) if d == 0 else (s_pos >= t_pos)
            sub = order if d == 0 else SCAN_CHUNKS - 1 - order
            rows = slice(CHUNK * sub, CHUNK * (sub + 1))
            sl = slice(GW * gi, GW * (gi + 1))
            it = dict(y_ref=(yf_ref, yb_ref)[d], sl=sl, rows=rows, s_ref=s_ref.at[d, gi], wtot=wt[sub][:, sl],
                      order=order)
            v_c = v[rows, sl]
            kkt_rt = jnp.concatenate([kkt[rows, sl], rt[rows, sl]], axis=0)
            a_all = _dot_nt(kkt_rt, jnp.concatenate([stack(kh[rows, sl]), stack(bh[rows, sl])], axis=0))
            ak = jnp.where(strict, a_all[:CHUNK, :GW], 0.0).astype(BF16)
            bk = jnp.where(incl, a_all[CHUNK:, :GW], 0.0).astype(BF16)
            it.update(kkt_rt=kkt_rt, v=v_c, kbw=jnp.concatenate([kw[rows, sl], bw[rows, sl]], axis=0),
                      abk=jnp.concatenate([ak, bk], axis=0), bb=jnp.where(incl, a_all[CHUNK:, GW:], 0.0).astype(BF16))
            ab = jnp.where(strict, a_all[:CHUNK, GW:], 0.0)
            it.update(tinv=eye_c - ab, pw=ab.astype(BF16))
            items.append(it)
        for it in items:
            it["abk_v"] = _dot(it["abk"], stack(it["v"]))
        for it in items:
            it["pw"] = _dot(it["pw"], stack(it["pw"])).astype(BF16)
        for k in range(5):
            for it in items:
                sk = stack(it["pw"])
                if k < 4:
                    res = _dot(jnp.concatenate([it["pw"], it["tinv"].astype(BF16)], axis=0), sk)
                    it["pw"] = res[:CHUNK].astype(BF16)
                    it["tinv"] = it["tinv"] + res[CHUNK:]
                else:
                    it["tinv"] = (it["tinv"] + _dot(it["tinv"].astype(BF16), sk)).astype(BF16)
        return items

    def advance(chains):
        for ch in chains:
            sb = ch["s_ref"][...].astype(BF16)
            ch["zs"] = _dot_nt(ch["kkt_rt"], sb) + ch["abk_v"]
        for ch in chains:
            ch["u"] = (-_dot(ch["tinv"], stack(ch["zs"][:CHUNK].astype(BF16)))).astype(BF16)
        for ch in chains:
            ch["y_ref"][ch["rows"], ch["sl"]] = ch["zs"][CHUNK:] + _dot(ch["bb"], stack(ch["u"]))
        for ch in chains:
            upd = _dot_tn(jnp.concatenate([ch["v"], ch["u"]], axis=0), ch["kbw"])
            ch["s_ref"][...] = ch["s_ref"][...] * ch["wtot"] + jnp.where(head_mask, upd, 0.0)

    prepared = prepare(range(SCAN_CHUNKS))
    for order in range(SCAN_CHUNKS):
        advance([it for it in prepared if it["order"] == order])


def _rw_scan(v, kkt, rt, kh, bh, kw, bw, wt, B, T):
    ntok = B * T
    rows = SCAN_CHUNKS * CHUNK
    nb = T // rows
    in_specs = []
    args = []
    for d in range(2):
        if d == 0:
            cm = lambda b, c: b * nb + c
        else:
            cm = lambda b, c: b * nb + (nb - 1 - c)
        for arr in (kkt, rt, kh, bh, kw, bw):
            in_specs.append(pl.BlockSpec((None, rows, RW_DIM), lambda b, c, cm=cm, d=d: (d, cm(b, c), 0)))
            args.append(arr)
        in_specs.append(pl.BlockSpec((rows, RW_DIM), lambda b, c, cm=cm: (cm(b, c), 0)))
        args.append(v)
        in_specs.append(pl.BlockSpec((None, SCAN_CHUNKS, 1, RW_DIM), lambda b, c, cm=cm, d=d: (d, cm(b, c), 0, 0)))
        args.append(wt)
    return pl.pallas_call(
        _rw_scan_kernel,
        grid=(B, nb),
        in_specs=in_specs,
        out_specs=[pl.BlockSpec((rows, RW_DIM), lambda b, c: (b * nb + c, 0)),
                   pl.BlockSpec((rows, RW_DIM), lambda b, c: (b * nb + (nb - 1 - c), 0))],
        out_shape=[jax.ShapeDtypeStruct((ntok, RW_DIM), F32), jax.ShapeDtypeStruct((ntok, RW_DIM), F32)],
        scratch_shapes=[pltpu.VMEM((2, RW_HEADS // GROUP, GW, GW), F32)],
        compiler_params=_cparams("parallel", "arbitrary"),
        name="rw_scan",
    )(*args)


def _mem_kv_kernel(m_ref, g_ref, wk_ref, wv_ref, gk_ref, k_out, v_out):
    m = _rms(m_ref[...], g_ref[...]).astype(BF16)
    k = _dot(m, wk_ref[...])
    v_out[...] = _dot(m, wv_ref[...]).astype(BF16)
    for hd in range(X_HEADS):
        sl = slice(X_HEAD * hd, X_HEAD * (hd + 1))
        k_out[:, sl] = _rms(k[:, sl], gk_ref[...]).astype(BF16)


def _mem_kv(mem2, w):
    n = mem2.shape[0]
    row = lambda i: (i, 0)
    return pl.pallas_call(
        _mem_kv_kernel,
        grid=(n // N_MEM,),
        in_specs=[pl.BlockSpec((N_MEM, D_MODEL), row), _full((1, D_MODEL)), _full((D_MODEL, X_DIM)),
                  _full((D_MODEL, X_DIM)), _full((1, X_HEAD))],
        out_specs=[pl.BlockSpec((N_MEM, X_DIM), row), pl.BlockSpec((N_MEM, X_DIM), row)],
        out_shape=[jax.ShapeDtypeStruct((n, X_DIM), BF16), jax.ShapeDtypeStruct((n, X_DIM), BF16)],
        compiler_params=_cparams("parallel"),
        name="mem_kv",
    )(mem2, w["mem_norm_g"], w["w_mk"], w["w_mv"], w["x_kn_g"])


def _xattn_kernel(h_ref, wq_ref, gq_ref, mk_ref, mv_ref, o_ref):
    q = _dot(h_ref[...], wq_ref[...])
    for hd in range(X_HEADS):
        sl = slice(X_HEAD * hd, X_HEAD * (hd + 1))
        qh = _rms(q[:, sl], gq_ref[...]).astype(BF16)
        s = _dot_nt(qh, mk_ref[:, sl])
        p = jnp.exp(s - jnp.max(s, axis=-1, keepdims=True))
        o = _dot(p.astype(BF16), mv_ref[:, sl]) / jnp.sum(p, axis=-1, keepdims=True)
        o_ref[:, sl] = o.astype(BF16)


def _xattn(h2, mk, mv, B, T, w):
    tq = min(512, T)
    nq = T // tq
    return pl.pallas_call(
        _xattn_kernel,
        grid=(B, nq),
        in_specs=[pl.BlockSpec((tq, D_MODEL), lambda b, i: (b * nq + i, 0)),
                  _full((D_MODEL, X_DIM)), _full((1, X_HEAD)),
                  pl.BlockSpec((N_MEM, X_DIM), lambda b, i: (b, 0)), pl.BlockSpec((N_MEM, X_DIM), lambda b, i: (b, 0))],
        out_specs=pl.BlockSpec((tq, X_DIM), lambda b, i: (b * nq + i, 0)),
        out_shape=jax.ShapeDtypeStruct((B * T, X_DIM), BF16),
        compiler_params=_cparams("parallel", "parallel"),
        name="xattn",
    )(h2, w["w_xq"], w["x_qn_g_scaled"], mk, mv)


def _merge_kernel(x_ref, h_ref, oa_ref, yf_ref, yb_ref, bonus_ref, g7_ref, oc_ref, wg_ref, woa_ref, wob_ref,
                  woc_ref, wout_ref, lng_ref, lnb_ref, seg_ref, o_ref):
    h = h_ref[...]
    seg = seg_ref[...]
    gate = lambda bi: _sigmoid(_dot(h, wg_ref[:, D_MODEL * bi:D_MODEL * (bi + 1)]))
    y7 = yf_ref[...] + yb_ref[...] + bonus_ref[...]
    mu = _dot_0_1(y7, seg) * (1.0 / RW_HEAD)
    merged = gate(0) * _dot(oa_ref[...], woa_ref[...])
    dy = y7 - mu
    var = _dot_0_1(dy * dy, seg) * (1.0 / RW_HEAD)
    merged = merged + gate(2) * _dot(oc_ref[...], woc_ref[...])
    gate_b = gate(1)
    y7 = dy * lax.rsqrt(var + LNX_EPS) * lng_ref[...] + lnb_ref[...]
    merged = merged + gate_b * _dot((y7 * g7_ref[...]).astype(BF16), wob_ref[...])
    o_ref[...] = x_ref[...] + _dot(merged.astype(BF16), wout_ref[...])


def _merge(x2, h2, oa, yf, yb, bonus, g7, oc, T, w):
    ntok = x2.shape[0]
    tm = min(512, T)
    row = lambda i: (i, 0)
    half = lambda: pl.BlockSpec((tm, RW_DIM), row)
    return pl.pallas_call(
        _merge_kernel,
        grid=(ntok // tm,),
        in_specs=[pl.BlockSpec((tm, D_MODEL), row), pl.BlockSpec((tm, D_MODEL), row),
                  half(), half(), half(), half(), half(), half(),
                  _full((D_MODEL, 3 * D_MODEL)), _full((RW_DIM, D_MODEL)),
                  _full((RW_DIM, D_MODEL)), _full((RW_DIM, D_MODEL)), _full((D_MODEL, D_MODEL)),
                  _full((1, RW_DIM)), _full((1, RW_DIM)), _full((RW_DIM, RW_DIM))],
        out_specs=pl.BlockSpec((tm, D_MODEL), row),
        out_shape=jax.ShapeDtypeStruct((ntok, D_MODEL), F32),
        compiler_params=_cparams("parallel"),
        name="merge",
    )(x2, h2, oa, yf, yb, bonus, g7, oc, w["w_gate"], w["w_o_a"], w["w_o_b"], w["w_o_c"],
      w["w_out"], w["lnx_g"], w["lnx_b"], w["seg"])


MXU_TILE = 256
FFN_SPLITS = (0, 5 * MXU_TILE, D_FF)


def _ffn_kernel(x_ref, xp_ref, xn_ref, g_ref, wug_ref, wuv_ref, cw_ref, cb_ref, wd_ref, o_ref, h_scr, ug_scr,
                *, tm, npos):
    i = pl.program_id(0)
    not_first = (i % npos != 0).astype(F32)
    not_last = (i % npos != npos - 1).astype(F32)
    g = g_ref[...]
    h_scr[0:SUBLANES, :] = (_rms(xp_ref[...], g) * not_first).astype(BF16)
    h_scr[SUBLANES:SUBLANES + tm, :] = _rms(x_ref[...], g).astype(BF16)
    h_scr[SUBLANES + tm:, :] = (_rms(xn_ref[...], g) * not_last).astype(BF16)
    out = x_ref[...]
    for lo, hi in zip(FFN_SPLITS[:-1], FFN_SPLITS[1:]):
        n = hi - lo
        ug_scr[:, 0:n] = _dot(h_scr[...], wug_ref[:, lo:hi])
        uv = _dot(h_scr[SUBLANES:SUBLANES + tm, :], wuv_ref[:, lo:hi])
        cw = cw_ref[:, lo:hi]
        c = (cw[0:1] * ug_scr[SUBLANES - 1:SUBLANES - 1 + tm, 0:n] + cw[1:2] * ug_scr[SUBLANES:SUBLANES + tm, 0:n]
             + cw[2:3] * ug_scr[SUBLANES + 1:SUBLANES + 1 + tm, 0:n] + cb_ref[:, lo:hi])
        act = 0.5 * c * (1.0 + lax.erf(c * np.float32(1.0 / np.sqrt(2.0)))) * uv
        out = out + _dot(act.astype(BF16), wd_ref[lo:hi, :])
    o_ref[...] = out


def _ffn(x2, T, w):
    ntok = x2.shape[0]
    tm = min(512, T)
    fc = max(hi - lo for lo, hi in zip(FFN_SPLITS[:-1], FFN_SPLITS[1:]))
    npos = T // tm
    nblk8 = ntok // SUBLANES
    r8 = tm // SUBLANES
    row = lambda i: (i, 0)
    return pl.pallas_call(
        functools.partial(_ffn_kernel, tm=tm, npos=npos),
        grid=(ntok // tm,),
        in_specs=[pl.BlockSpec((tm, D_MODEL), row),
                  pl.BlockSpec((SUBLANES, D_MODEL), lambda i: (jnp.maximum(i * r8 - 1, 0), 0)),
                  pl.BlockSpec((SUBLANES, D_MODEL), lambda i: (jnp.minimum((i + 1) * r8, nblk8 - 1), 0)),
                  _full((1, D_MODEL)), _full((D_MODEL, D_FF)), _full((D_MODEL, D_FF)), _full((3, D_FF)),
                  _full((1, D_FF)), _full((D_FF, D_MODEL))],
        out_specs=pl.BlockSpec((tm, D_MODEL), row),
        out_shape=jax.ShapeDtypeStruct((ntok, D_MODEL), F32),
        scratch_shapes=[pltpu.VMEM((tm + 2 * SUBLANES, D_MODEL), BF16), pltpu.VMEM((tm + 2 * SUBLANES, fc), F32)],
        compiler_params=_cparams("parallel"),
        name="conv_ffn",
    )(x2, x2, x2, w["norm_ffn_g"], w["w_up_gate"], w["w_up_val"], w["conv_w"], w["conv_b"], w["w_down"])


def _prep_weights(p):
    w = {}
    row = lambda a: a.reshape(1, -1).astype(F32)
    w_in = p["w_in"]
    o = np.cumsum([0, Q_LORA, KV_LORA, MLA_ROPE, RW_COLS, X_DIM, 3 * D_MODEL])
    seg = lambda i: w_in[:, o[i]:o[i + 1]]
    zc = lambda n: jnp.zeros((D_MODEL, n), F32)
    w["w_c"] = jnp.concatenate([seg(0), seg(1), zc(MLA_NOPE), seg(2), zc(LANES - MLA_QK)], axis=1).astype(BF16)
    w["w_rw"] = seg(3).astype(BF16)
    w["w_xq"] = seg(4).astype(BF16)
    w["w_gate"] = seg(5).astype(BF16)
    w["norm_mix_g"] = row(p["norm_mix_g"])
    w["q_norm_g"] = row(p["q_norm_g"])
    w["kv_norm_g"] = row(p["kv_norm_g"])
    pad_slot = lambda a: jnp.pad(a, ((0, 0), (0, 0), (0, LANES - a.shape[-1]))).reshape(a.shape[0], -1)
    uq = p["w_uq"].reshape(Q_LORA, MLA_HEADS, MLA_QK)
    w["w_uq"] = pad_slot(uq).astype(BF16)
    half = MLA_ROPE // 2
    uq_rot = jnp.concatenate([jnp.zeros_like(uq[:, :, :MLA_NOPE]), -uq[:, :, MLA_NOPE + half:],
                              uq[:, :, MLA_NOPE:MLA_NOPE + half]], axis=-1)
    w["w_uq_rot"] = pad_slot(uq_rot).astype(BF16)
    ukv =p["w_ukv"].reshape(KV_LORA, MLA_HEADS, MLA_NOPE + MLA_V)
    w["w_uk"] = pad_slot(ukv[:, :, :MLA_NOPE]).astype(BF16)
    w["w_uv"] = pad_slot(ukv[:, :, MLA_NOPE:]).T.astype(BF16)
    w["mla_qn_g_scaled"] = p["mla_qn_g"].reshape(-1) * np.float32(MLA_QK ** -0.5 * np.log2(np.e))
    w["mla_kn_g"] = p["mla_kn_g"].reshape(-1)
    for name in ("mu_prev", "mu_next", "k_k", "k_a", "r_k", "lnx_g", "lnx_b", "mem_norm_g", "x_kn_g", "norm_ffn_g",
                 "conv_b"):
        w[name] = row(p[name])
    w["w0"] = jnp.stack([p["w0_f"], p["w0_b"]]).reshape(2, 1, RW_DIM)
    w["a0"] = jnp.stack([p["a0_f"], p["a0_b"]]).reshape(2, 1, RW_DIM)
    zl = jnp.zeros((LORA, RW_DIM), F32)
    w["w2"] = jnp.stack([jnp.concatenate([p["w2_f"], zl]), jnp.concatenate([zl, p["w2_b"]])]).astype(BF16)
    w["a2"] = jnp.stack([jnp.concatenate([p["a2_f"], zl]), jnp.concatenate([zl, p["a2_b"]])]).astype(BF16)
    w["g2"] = p["g2"].astype(BF16)
    hid = np.arange(RW_DIM) // RW_HEAD
    w["seg"] = jnp.asarray((hid[:, None] == hid[None, :]).astype(np.float32), BF16)
    mkv = p["w_mkv"].reshape(D_MODEL, X_HEADS, 2 * X_HEAD)
    w["w_mk"] = mkv[:, :, :X_HEAD].reshape(D_MODEL, X_DIM).astype(BF16)
    w["w_mv"] = mkv[:, :, X_HEAD:].reshape(D_MODEL, X_DIM).astype(BF16)
    w["x_qn_g_scaled"] = row(p["x_qn_g"]) * np.float32(X_HEAD ** -0.5)
    for name in ("w_o_a", "w_o_b", "w_o_c", "w_out", "w_down"):
        w[name] = p[name].astype(BF16)
    w["w_up_gate"] = p["w_up"][:, :D_FF].astype(BF16)
    w["w_up_val"] = p["w_up"][:, D_FF:].astype(BF16)
    w["conv_w"] = p["conv_w"].astype(F32)
    return w


def _rope_tables(T, gq, gk):
    half = MLA_ROPE // 2
    inv = jnp.power(ROPE_THETA, -jnp.arange(half, dtype=F32) / half)
    ang = jnp.arange(T, dtype=F32)[:, None] * inv[None, :]
    cos, sin = jnp.cos(ang), jnp.sin(ang)
    z = lambda n: jnp.zeros((T, n), F32)
    pad = z(LANES - MLA_QK)

    def own(g):
        g1, g2 = g[MLA_NOPE:MLA_NOPE + half], g[MLA_NOPE + half:]
        return jnp.concatenate([jnp.broadcast_to(g[:MLA_NOPE], (T, MLA_NOPE)), cos * g1, cos * g2, pad], axis=1)

    g1, g2 = gq[MLA_NOPE:MLA_NOPE + half], gq[MLA_NOPE + half:]
    qb = jnp.concatenate([z(MLA_NOPE), sin * g2, sin * g1, pad], axis=1)
    g1, g2 = gk[MLA_NOPE:MLA_NOPE + half], gk[MLA_NOPE + half:]
    kba = jnp.concatenate([z(MLA_NOPE), -sin * g2, z(half), pad], axis=1)
    kbb = jnp.concatenate([z(MLA_NOPE), z(half), sin * g1, pad], axis=1)
    return own(gq), qb, own(gk), kba, kbb


def _layer(x, mem, w):
    B, T, _ = x.shape
    x2 = x.reshape(B * T, D_MODEL)
    q, k, v, h2 = _mla_prep(x2, T, w, _rope_tables(T, w["mla_qn_g_scaled"], w["mla_kn_g"]))
    o_a = _flash(q, k, v, B, T)
    v7, kkt, rt, kh, bh, kw, bw, wt, bonus, g7 = _rw_prep(h2, T, w)
    y_f, y_b = _rw_scan(v7, kkt, rt, kh, bh, kw, bw, wt, B, T)
    mk, mv = _mem_kv(mem.reshape(B * N_MEM, D_MODEL), w)
    o_c = _xattn(h2, mk, mv, B, T, w)
    x1 = _merge(x2, h2, o_a, y_f, y_b, bonus, g7, o_c, T, w)
    return _ffn(x1, T, w).reshape(B, T, D_MODEL)


def kernel(x_prompt, x_sample, mem_prompt, mem_sample, norm_mix_g, w_in, q_norm_g, w_uq, kv_norm_g, w_ukv, mla_qn_g, mla_kn_g, w_o_a, mu_prev, mu_next, w0_f, w2_f, a0_f, a2_f, w0_b, w2_b, a0_b, a2_b, g2, k_k, k_a, r_k, lnx_g, lnx_b, w_o_b, mem_norm_g, w_mkv, x_qn_g, x_kn_g, w_o_c, w_out, norm_ffn_g, w_up, conv_w, conv_b, w_down):
    p = dict(norm_mix_g=norm_mix_g, w_in=w_in, q_norm_g=q_norm_g, w_uq=w_uq, kv_norm_g=kv_norm_g, w_ukv=w_ukv,
             mla_qn_g=mla_qn_g, mla_kn_g=mla_kn_g, w_o_a=w_o_a, mu_prev=mu_prev, mu_next=mu_next,
             w0_f=w0_f, w2_f=w2_f, a0_f=a0_f, a2_f=a2_f, w0_b=w0_b, w2_b=w2_b, a0_b=a0_b, a2_b=a2_b,
             g2=g2, k_k=k_k, k_a=k_a, r_k=r_k, lnx_g=lnx_g, lnx_b=lnx_b, w_o_b=w_o_b,
             mem_norm_g=mem_norm_g, w_mkv=w_mkv, x_qn_g=x_qn_g, x_kn_g=x_kn_g, w_o_c=w_o_c, w_out=w_out,
             norm_ffn_g=norm_ffn_g, w_up=w_up, conv_w=conv_w, conv_b=conv_b, w_down=w_down)
    w = _prep_weights({name: a[0] for name, a in p.items()})
    return (_layer(x_prompt, mem_prompt, w), _layer(x_sample, mem_sample, w))
```

```python
import functools

import numpy as np
import jax
import jax.numpy as jnp
from jax import lax
from jax.experimental import pallas as pl
from jax.experimental.pallas import tpu as pltpu

F32 = jnp.float32
BF16 = jnp.bfloat16

D_MODEL = 1024
RMS_EPS = 1e-6
N_MEM = 256
MLA_HEADS = 8
MLA_NOPE = 64
MLA_ROPE = 32
MLA_QK = MLA_NOPE + MLA_ROPE
MLA_V = 64
Q_LORA = 384
KV_LORA = 256
ROPE_THETA = 10000.0
RW_HEADS = 8
RW_HEAD = 64
RW_DIM = RW_HEADS * RW_HEAD
LORA = 64
GATE_LORA = 128
RW_COLS = 3 * RW_DIM + 4 * LORA + GATE_LORA
LNX_EPS = 64e-5
X_HEADS = 4
X_HEAD = 128
X_DIM = X_HEADS * X_HEAD
D_FF = 2816

LANES = 128
SUBLANES = 8
CHUNK = 64
GROUP = 2
GW = GROUP * RW_HEAD
SLOT_COLS = MLA_HEADS * LANES
MLA_C_COLS = Q_LORA + KV_LORA + LANES
SCAN_CHUNKS = 8
VMEM_LIMIT = 56 * 1024 * 1024


def _cparams(*sem):
    return pltpu.CompilerParams(dimension_semantics=sem, vmem_limit_bytes=VMEM_LIMIT)


def _rms(x, g, eps=RMS_EPS):
    return x * lax.rsqrt(jnp.mean(x * x, axis=-1, keepdims=True) + eps) * g


def _dot(a, b):
    return jnp.dot(a, b, preferred_element_type=F32)


def _dot_nt(a, b):
    return lax.dot_general(a, b, (((1,), (1,)), ((), ())), preferred_element_type=F32)


def _dot_tn(a, b):
    return lax.dot_general(a, b, (((0,), (0,)), ((), ())), preferred_element_type=F32)


def _dot_0_1(x, e):
    return _dot(x.astype(BF16), e)


def _dot_0_1_l(e, x):
    hi = x.astype(BF16)
    lo = (x - hi.astype(F32)).astype(BF16)
    return _dot(e, hi) + _dot(e, lo)


def _sigmoid(z):
    return 1.0 / (1.0 + jnp.exp(-z))


def _full(shape):
    nd = len(shape)
    return pl.BlockSpec(shape, lambda *_: (0,) * nd, pipeline_mode=pl.Buffered(1))


def _mla_prep_kernel(x_ref, g_ref, wc_ref, gq_ref, gkv_ref, wuq_ref, wuqr_ref, wuk_ref, wuv_ref,
                     qa_ref, qb_ref, ka_ref, kba_ref, kbb_ref, q_out, k_out, v_out, h_out):
    tm = x_ref.shape[0]
    half = MLA_ROPE // 2
    halves = [slice(0, tm // 2), slice(tm // 2, tm)]
    cs = []
    for r in halves:
        h = _rms(x_ref[r, :], g_ref[...]).astype(BF16)
        h_out[r, :] = h
        cs.append(_dot(h, wc_ref[...]))
    proj = []
    for c in cs:
        cq = _rms(c[:, :Q_LORA], gq_ref[...]).astype(BF16)
        ckv = _rms(c[:, Q_LORA:Q_LORA + KV_LORA], gkv_ref[...]).astype(BF16)
        proj.append((_dot(cq, wuq_ref[...]), _dot(cq, wuqr_ref[...]), _dot(ckv, wuk_ref[...]),
                     _dot_nt(wuv_ref[...], ckv)))
    for r, c, (q, qp, k, vt) in zip(halves, cs, proj):
        slot_row = lax.broadcasted_iota(jnp.int32, vt.shape, 0) % LANES
        v_out[0, :, r] = jnp.where(slot_row == MLA_V, 1.0, vt).astype(BF16)
        kr = c[:, Q_LORA + KV_LORA:]
        qa = qa_ref[r, :]
        qb = qb_ref[r, :]
        ka = ka_ref[r, :]
        kr_roped = (kr * ka + pltpu.roll(kr, LANES - half, 1) * kba_ref[r, :]
                    + pltpu.roll(kr, half, 1) * kbb_ref[r, :])
        kr_ss = jnp.sum(kr * kr, axis=-1, keepdims=True)
        for hd in range(MLA_HEADS):
            sl = slice(LANES * hd, LANES * (hd + 1))
            t = q[:, sl]
            ss = jnp.sum(t * t, axis=-1, keepdims=True) * (1.0 / MLA_QK)
            q_out[r, sl] = ((t * qa + qp[:, sl] * qb) * lax.rsqrt(ss + RMS_EPS)).astype(BF16)
            t = k[:, sl]
            ss = (jnp.sum(t * t, axis=-1, keepdims=True) + kr_ss) * (1.0 / MLA_QK)
            k_out[r, sl] = ((t * ka + kr_roped) * lax.rsqrt(ss + RMS_EPS)).astype(BF16)


def _mla_prep(x2, T, w, tabs):
    ntok = x2.shape[0]
    tm = _attn_tile(T)
    npos = T // tm
    row = lambda i: (i, 0)
    pos = lambda i: (i % npos, 0)
    return pl.pallas_call(
        _mla_prep_kernel,
        grid=(ntok // tm,),
        in_specs=[pl.BlockSpec((tm, D_MODEL), row), _full((1, D_MODEL)), _full((D_MODEL, MLA_C_COLS)),
                  _full((1, Q_LORA)), _full((1, KV_LORA)), _full((Q_LORA, SLOT_COLS)), _full((Q_LORA, SLOT_COLS)),
                  _full((KV_LORA, SLOT_COLS)), _full((SLOT_COLS, KV_LORA))] + [pl.BlockSpec((tm, LANES), pos)] * 5,
        out_specs=[pl.BlockSpec((tm, SLOT_COLS), row), pl.BlockSpec((tm, SLOT_COLS), row),
                   pl.BlockSpec((1, SLOT_COLS, tm), lambda i: (i, 0, 0)), pl.BlockSpec((tm, D_MODEL), row)],
        out_shape=[jax.ShapeDtypeStruct((ntok, SLOT_COLS), BF16), jax.ShapeDtypeStruct((ntok, SLOT_COLS), BF16),
                   jax.ShapeDtypeStruct((ntok // tm, SLOT_COLS, tm), BF16),
                   jax.ShapeDtypeStruct((ntok, D_MODEL), BF16)],
        compiler_params=_cparams("parallel"),
        name="mla_prep",
    )(x2, w["norm_mix_g"], w["w_c"], w["q_norm_g"], w["kv_norm_g"], w["w_uq"], w["w_uq_rot"], w["w_uk"],
      w["w_uv"], *tabs)


def _flash_kernel(q_ref, k_ref, vt_ref, o_ref, s0_ref, s1_ref, *, tk, nk):
    tq = q_ref.shape[0]
    slots = [slice(LANES * h, LANES * (h + 1)) for h in range(2)]
    per = tk // vt_ref.shape[2]

    def scores(j, s_ref):
        kblk = k_ref[pl.ds(pl.multiple_of(j * tk, tk), tk), :]
        for h in range(2):
            s_ref[h] = _dot_nt(kblk[:, slots[h]], q_ref[:, slots[h]]).astype(BF16)

    def consume(j, s_ref, state):
        vt = jnp.concatenate([vt_ref[per * j + r] for r in range(per)], axis=1)
        m_news = [jnp.maximum(state[h][0], jnp.max(s_ref[h], axis=0, keepdims=True).astype(F32)) for h in range(2)]
        pts = [jnp.exp2(s_ref[h] - m_news[h].astype(BF16)) for h in range(2)]
        return tuple((m_news[h], jnp.exp2(state[h][0] - m_news[h]) * state[h][1] + _dot(vt[slots[h], :], pts[h]))
                     for h in range(2))

    def body(t, state):
        j = 2 * t
        scores(j + 1, s1_ref)
        state = consume(j, s0_ref, state)
        scores(j + 2, s0_ref)
        return consume(j + 1, s1_ref, state)

    state = tuple((jnp.full((1, tq), -jnp.inf, F32), jnp.zeros((LANES, tq), F32)) for _ in range(2))
    scores(0, s0_ref)
    state = lax.fori_loop(0, nk // 2 - 1, body, state)
    scores(nk - 1, s1_ref)
    state = consume(nk - 2, s0_ref, state)
    state = consume(nk - 1, s1_ref, state)
    outs = [acc[0:MLA_V, :] / acc[MLA_V:MLA_V + 1, :] for _, acc in state]
    o_ref[...] = jnp.concatenate(outs, axis=0).T.astype(BF16)


def _attn_tile(T):
    return min(512, T // 4)


def _flash(q, k, vt, B, T):
    tv = _attn_tile(T)
    tq, tk = (tv // 2, 2 * tv) if T // tv >= 16 else (2 * tv, tv)
    nq = T // tq
    nk = T // tk
    assert nk % 2 == 0 and tk % tv == 0, (T, tk, tv)
    return pl.pallas_call(
        functools.partial(_flash_kernel, tk=tk, nk=nk),
        grid=(B, MLA_HEADS // 2, nq),
        in_specs=[pl.BlockSpec((tq, 2 * LANES), lambda b, hp, i: (b * nq + i, hp)),
                  pl.BlockSpec((T, 2 * LANES), lambda b, hp, i: (b, hp)),
                  pl.BlockSpec((T // tv, 2 * LANES, tv), lambda b, hp, i: (b, hp, 0))],
        out_specs=pl.BlockSpec((tq, LANES), lambda b, hp, i: (b * nq + i, hp)),
        out_shape=jax.ShapeDtypeStruct((B * T, MLA_HEADS * MLA_V), BF16),
        scratch_shapes=[pltpu.VMEM((2, tk, tq), BF16), pltpu.VMEM((2, tk, tq), BF16)],
        compiler_params=_cparams("parallel", "parallel", "arbitrary"),
        name="mla_flash",
    )(q, k, vt)


HALO = 16


def _rw_prep_kernel(h_ref, hp_ref, hn_ref, wrw_ref, mup_ref, mun_ref, kk_ref, ka_ref, rk_ref,
                    w0_ref, w2_ref, a0_ref, a2_ref, g2_ref, seg_ref, tri_ref,
                    v_out, kkt_out, rt_out, kh_out, bh_out, kw_out, bw_out, wt_out, bonus_out, g7_out,
                    h_scr, rw_scr, *, tm, npos):
    i = pl.program_id(0)
    h_scr[0:HALO, :] = jnp.where(i % npos != 0, hp_ref[...], jnp.zeros((), BF16))
    h_scr[HALO:HALO + tm, :] = h_ref[...]
    h_scr[HALO + tm:, :] = jnp.where(i % npos != npos - 1, hn_ref[...], jnp.zeros((), BF16))
    rb = tri_ref.shape[1]
    edges = [0] + [HALO + rb * (blk + 1) + HALO for blk in range(tm // rb - 1)] + [tm + 2 * HALO]
    mup = mup_ref[...]
    mun = mun_ref[...]
    mu0 = 1.0 - mup - mun
    seg = seg_ref[...]
    for blk in range(tm // rb):
        lo, hi = edges[blk], edges[blk + 1]
        rw_scr[lo:hi, :] = _dot(h_scr[lo:hi, :], wrw_ref[...])
        r0 = HALO + rb * blk
        rows = slice(rb * blk, rb * (blk + 1))
        rwf = rw_scr[r0:r0 + rb, :] * mu0 + mup * rw_scr[r0 - 1:r0 - 1 + rb, :] + mun * rw_scr[r0 + 1:r0 + 1 + rb, :]
        r7 = rwf[:, 0:RW_DIM]
        k7 = rwf[:, RW_DIM:2 * RW_DIM]
        v7 = rwf[:, 2 * RW_DIM:3 * RW_DIM]
        wl = rwf[:, 3 * RW_DIM:3 * RW_DIM + 2 * LORA]
        al = rwf[:, 3 * RW_DIM + 2 * LORA:3 * RW_DIM + 4 * LORA]
        gl = rwf[:, 3 * RW_DIM + 4 * LORA:]
        kx = k7 * kk_ref[...]
        kkn = kx * lax.rsqrt(jnp.maximum(_dot_0_1(kx * kx, seg), 1e-24))
        v_out[rows, :] = v7.astype(BF16)
        g7_out[rows, :] = _dot(_sigmoid(gl).astype(BF16), g2_ref[...])
        tw = jnp.tanh(wl).astype(BF16)
        alb = al.astype(BF16)
        bonus = jnp.zeros((rb, RW_DIM), F32)
        for d in range(2):
            z = -(w0_ref[d] + _dot(tw, w2_ref[d]))
            softplus = jnp.maximum(z, 0.0) + jnp.log(1.0 + jnp.exp(-jnp.abs(z)))
            lw = -jnp.exp(-softplus - 0.5)
            a = _sigmoid(a0_ref[d] + _dot(alb, a2_ref[d]))
            kd = k7 * (1.0 + (a - 1.0) * ka_ref[...])
            b = kkn * a
            bonus = bonus + _dot_0_1(r7 * kd * rk_ref[...], seg) * v7
            cum = _dot_0_1_l(tri_ref[d], lw)
            last = CHUNK - 1 if d == 0 else 0
            wtot_rows = [jnp.exp(cum[CHUNK * ci + last:CHUNK * ci + last + 1, :]) for ci in range(rb // CHUNK)]
            wtot = jnp.concatenate([jnp.broadcast_to(r, (CHUNK, RW_DIM)) for r in wtot_rows], axis=0)
            w_incl = jnp.exp(cum)
            w_excl = jnp.exp(cum - lw)
            w_inv = 1.0 / w_incl
            w_rest = wtot * w_inv
            kkt_out[d, rows, :] = (kkn * w_excl).astype(BF16)
            rt_out[d, rows, :] = (r7 * w_incl).astype(BF16)
            kh_out[d, rows, :] = (kd * w_inv).astype(BF16)
            bh_out[d, rows, :] = (b * w_inv).astype(BF16)
            kw_out[d, rows, :] = (kd * w_rest).astype(BF16)
            bw_out[d, rows, :] = (b * w_rest).astype(BF16)
            for ci in range(rb // CHUNK):
                wt_out[d, blk * (rb // CHUNK) + ci] = wtot_rows[ci]
        bonus_out[rows, :] = bonus


def _rw_prep(h2, T, w):
    ntok = h2.shape[0]
    tm = min(512, T)
    npos = T // tm
    nhalo = ntok // HALO
    rh = tm // HALO
    row = lambda i: (i, 0)
    drow = lambda i: (0, i, 0)
    dspec = pl.BlockSpec((2, tm, RW_DIM), drow)
    dshape = jax.ShapeDtypeStruct((2, ntok, RW_DIM), BF16)
    rb = min(256, tm)
    tri = _chunk_masks(rb)
    outs = pl.pallas_call(
        functools.partial(_rw_prep_kernel, tm=tm, npos=npos),
        grid=(ntok // tm,),
        in_specs=[pl.BlockSpec((tm, D_MODEL), row),
                  pl.BlockSpec((HALO, D_MODEL), lambda i: (jnp.maximum(i * rh - 1, 0), 0)),
                  pl.BlockSpec((HALO, D_MODEL), lambda i: (jnp.minimum((i + 1) * rh, nhalo - 1), 0)),
                  _full((D_MODEL, RW_COLS)), _full((1, RW_COLS)), _full((1, RW_COLS)),
                  _full((1, RW_DIM)), _full((1, RW_DIM)), _full((1, RW_DIM)),
                  _full((2, 1, RW_DIM)), _full((2, 2 * LORA, RW_DIM)), _full((2, 1, RW_DIM)),
                  _full((2, 2 * LORA, RW_DIM)), _full((GATE_LORA, RW_DIM)), _full((RW_DIM, RW_DIM)),
                  _full((2, rb, rb))],
        out_specs=[pl.BlockSpec((tm, RW_DIM), row), dspec, dspec, dspec, dspec, dspec, dspec,
                   pl.BlockSpec((2, tm // CHUNK, 1, RW_DIM), lambda i: (0, i, 0, 0)),
                   pl.BlockSpec((tm, RW_DIM), row), pl.BlockSpec((tm, RW_DIM), row)],
        out_shape=[jax.ShapeDtypeStruct((ntok, RW_DIM), BF16), dshape, dshape, dshape, dshape, dshape, dshape,
                   jax.ShapeDtypeStruct((2, ntok // CHUNK, 1, RW_DIM), F32),
                   jax.ShapeDtypeStruct((ntok, RW_DIM), F32), jax.ShapeDtypeStruct((ntok, RW_DIM), F32)],
        scratch_shapes=[pltpu.VMEM((tm + 2 * HALO, D_MODEL), BF16), pltpu.VMEM((tm + 2 * HALO, RW_COLS), F32)],
        compiler_params=_cparams("parallel"),
        name="rw_prep",
    )(h2, h2, h2, w["w_rw"], w["mu_prev"], w["mu_next"], w["k_k"], w["k_a"], w["r_k"],
      w["w0"], w["w2"], w["a0"], w["a2"], w["g2"], w["seg"], tri)
    return outs


def _chunk_masks(tm):
    t = np.arange(tm)
    same = (t[:, None] // CHUNK) == (t[None, :] // CHUNK)
    fwd = same & (t[None, :] <= t[:, None])
    bwd = same & (t[None, :] >= t[:, None])
    return jnp.asarray(np.stack([fwd, bwd]).astype(np.float32), BF16)


def _rw_scan_kernel(*refs):
    ins = refs[:16]
    yf_ref, yb_ref, s_ref = refs[16:]
    c = pl.program_id(1)

    @pl.when(c == 0)
    def _():
        s_ref[...] = jnp.zeros(s_ref.shape, F32)

    t_pos = lax.broadcasted_iota(jnp.int32, (CHUNK, GW), 0)
    s_pos = lax.broadcasted_iota(jnp.int32, (CHUNK, GW), 1) % CHUNK
    eye_c = (t_pos == s_pos).astype(F32)
    ri = lax.broadcasted_iota(jnp.int32, (GW, GW), 0)
    ci = lax.broadcasted_iota(jnp.int32, (GW, GW), 1)
    head_mask = (ri // CHUNK) == (ci // RW_HEAD)
    stack = lambda a: jnp.where(head_mask, jnp.concatenate([a] * GROUP, axis=0), jnp.zeros((), BF16))

    def prepare(orders):
        items = []
        for d, gi, order in ((d, gi, o) for d in range(2) for gi in range(RW_HEADS // GROUP) for o in orders):
            kkt, rt, kh, bh, kw, bw, v, wt = ins[8 * d:8 * d + 8]
            strict = (s_pos < t_pos) if d == 0 else (s_pos > t_pos)
            incl = (s_pos <= t_pos) if d == 0 else (s_pos >= t_pos)
            sub = order if d == 0 else SCAN_CHUNKS - 1 - order
            rows = slice(CHUNK * sub, CHUNK * (sub + 1))
            sl = slice(GW * gi, GW * (gi + 1))
            it = dict(y_ref=(yf_ref, yb_ref)[d], sl=sl, rows=rows, s_ref=s_ref.at[d, gi], wtot=wt[sub][:, sl],
                      order=order)
            v_c = v[rows, sl]
            kkt_rt = jnp.concatenate([kkt[rows, sl], rt[rows, sl]], axis=0)
            a_all = _dot_nt(kkt_rt, jnp.concatenate([stack(kh[rows, sl]), stack(bh[rows, sl])], axis=0))
            ak = jnp.where(strict, a_all[:CHUNK, :GW], 0.0).astype(BF16)
            bk = jnp.where(incl, a_all[CHUNK:, :GW], 0.0).astype(BF16)
            it.update(kkt_rt=kkt_rt, v=v_c, kbw=jnp.concatenate([kw[rows, sl], bw[rows, sl]], axis=0),
                      abk=jnp.concatenate([ak, bk], axis=0), bb=jnp.where(incl, a_all[CHUNK:, GW:], 0.0).astype(BF16))
            ab = jnp.where(strict, a_all[:CHUNK, GW:], 0.0)
            it.update(tinv=eye_c - ab, pw=ab.astype(BF16))
            items.append(it)
        for it in items:
            it["abk_v"] = _dot(it["abk"], stack(it["v"]))
        for it in items:
            it["pw"] = _dot(it["pw"], stack(it["pw"])).astype(BF16)
        for k in range(5):
            for it in items:
                sk = stack(it["pw"])
                if k < 4:
                    res = _dot(jnp.concatenate([it["pw"], it["tinv"].astype(BF16)], axis=0), sk)
                    it["pw"] = res[:CHUNK].astype(BF16)
                    it["tinv"] = it["tinv"] + res[CHUNK:]
                else:
                    it["tinv"] = (it["tinv"] + _dot(it["tinv"].astype(BF16), sk)).astype(BF16)
        return items

    def advance(chains):
        for ch in chains:
            sb = ch["s_ref"][...].astype(BF16)
            ch["zs"] = _dot_nt(ch["kkt_rt"], sb) + ch["abk_v"]
        for ch in chains:
            ch["u"] = (-_dot(ch["tinv"], stack(ch["zs"][:CHUNK].astype(BF16)))).astype(BF16)
        for ch in chains:
            ch["y_ref"][ch["rows"], ch["sl"]] = ch["zs"][CHUNK:] + _dot(ch["bb"], stack(ch["u"]))
        for ch in chains:
            upd = _dot_tn(jnp.concatenate([ch["v"], ch["u"]], axis=0), ch["kbw"])
            ch["s_ref"][...] = ch["s_ref"][...] * ch["wtot"] + jnp.where(head_mask, upd, 0.0)

    prepared = prepare(range(SCAN_CHUNKS))
    for order in range(SCAN_CHUNKS):
        advance([it for it in prepared if it["order"] == order])


def _rw_scan(v, kkt, rt, kh, bh, kw, bw, wt, B, T):
    ntok = B * T
    rows = SCAN_CHUNKS * CHUNK
    nb = T // rows
    in_specs = []
    args = []
    for d in range(2):
        if d == 0:
            cm = lambda b, c: b * nb + c
        else:
            cm = lambda b, c: b * nb + (nb - 1 - c)
        for arr in (kkt, rt, kh, bh, kw, bw):
            in_specs.append(pl.BlockSpec((None, rows, RW_DIM), lambda b, c, cm=cm, d=d: (d, cm(b, c), 0)))
            args.append(arr)
        in_specs.append(pl.BlockSpec((rows, RW_DIM), lambda b, c, cm=cm: (cm(b, c), 0)))
        args.append(v)
        in_specs.append(pl.BlockSpec((None, SCAN_CHUNKS, 1, RW_DIM), lambda b, c, cm=cm, d=d: (d, cm(b, c), 0, 0)))
        args.append(wt)
    return pl.pallas_call(
        _rw_scan_kernel,
        grid=(B, nb),
        in_specs=in_specs,
        out_specs=[pl.BlockSpec((rows, RW_DIM), lambda b, c: (b * nb + c, 0)),
                   pl.BlockSpec((rows, RW_DIM), lambda b, c: (b * nb + (nb - 1 - c), 0))],
        out_shape=[jax.ShapeDtypeStruct((ntok, RW_DIM), F32), jax.ShapeDtypeStruct((ntok, RW_DIM), F32)],
        scratch_shapes=[pltpu.VMEM((2, RW_HEADS // GROUP, GW, GW), F32)],
        compiler_params=_cparams("parallel", "arbitrary"),
        name="rw_scan",
    )(*args)


def _mem_kv_kernel(m_ref, g_ref, wk_ref, wv_ref, gk_ref, k_out, v_out):
    m = _rms(m_ref[...], g_ref[...]).astype(BF16)
    k = _dot(m, wk_ref[...])
    v_out[...] = _dot(m, wv_ref[...]).astype(BF16)
    for hd in range(X_HEADS):
        sl = slice(X_HEAD * hd, X_HEAD * (hd + 1))
        k_out[:, sl] = _rms(k[:, sl], gk_ref[...]).astype(BF16)


def _mem_kv(mem2, w):
    n = mem2.shape[0]
    row = lambda i: (i, 0)
    return pl.pallas_call(
        _mem_kv_kernel,
        grid=(n // N_MEM,),
        in_specs=[pl.BlockSpec((N_MEM, D_MODEL), row), _full((1, D_MODEL)), _full((D_MODEL, X_DIM)),
                  _full((D_MODEL, X_DIM)), _full((1, X_HEAD))],
        out_specs=[pl.BlockSpec((N_MEM, X_DIM), row), pl.BlockSpec((N_MEM, X_DIM), row)],
        out_shape=[jax.ShapeDtypeStruct((n, X_DIM), BF16), jax.ShapeDtypeStruct((n, X_DIM), BF16)],
        compiler_params=_cparams("parallel"),
        name="mem_kv",
    )(mem2, w["mem_norm_g"], w["w_mk"], w["w_mv"], w["x_kn_g"])


def _xattn_kernel(h_ref, wq_ref, gq_ref, mk_ref, mv_ref, o_ref):
    q = _dot(h_ref[...], wq_ref[...])
    for hd in range(X_HEADS):
        sl = slice(X_HEAD * hd, X_HEAD * (hd + 1))
        qh = _rms(q[:, sl], gq_ref[...]).astype(BF16)
        s = _dot_nt(qh, mk_ref[:, sl])
        p = jnp.exp(s - jnp.max(s, axis=-1, keepdims=True))
        o = _dot(p.astype(BF16), mv_ref[:, sl]) / jnp.sum(p, axis=-1, keepdims=True)
        o_ref[:, sl] = o.astype(BF16)


def _xattn(h2, mk, mv, B, T, w):
    tq = min(512, T)
    nq = T // tq
    return pl.pallas_call(
        _xattn_kernel,
        grid=(B, nq),
        in_specs=[pl.BlockSpec((tq, D_MODEL), lambda b, i: (b * nq + i, 0)),
                  _full((D_MODEL, X_DIM)), _full((1, X_HEAD)),
                  pl.BlockSpec((N_MEM, X_DIM), lambda b, i: (b, 0)), pl.BlockSpec((N_MEM, X_DIM), lambda b, i: (b, 0))],
        out_specs=pl.BlockSpec((tq, X_DIM), lambda b, i: (b * nq + i, 0)),
        out_shape=jax.ShapeDtypeStruct((B * T, X_DIM), BF16),
        compiler_params=_cparams("parallel", "parallel"),
        name="xattn",
    )(h2, w["w_xq"], w["x_qn_g_scaled"], mk, mv)


def _merge_kernel(x_ref, h_ref, oa_ref, yf_ref, yb_ref, bonus_ref, g7_ref, oc_ref, wg_ref, woa_ref, wob_ref,
                  woc_ref, wout_ref, lng_ref, lnb_ref, seg_ref, o_ref):
    h = h_ref[...]
    seg = seg_ref[...]
    gate = lambda bi: _sigmoid(_dot(h, wg_ref[:, D_MODEL * bi:D_MODEL * (bi + 1)]))
    y7 = yf_ref[...] + yb_ref[...] + bonus_ref[...]
    mu = _dot_0_1(y7, seg) * (1.0 / RW_HEAD)
    merged = gate(0) * _dot(oa_ref[...], woa_ref[...])
    dy = y7 - mu
    var = _dot_0_1(dy * dy, seg) * (1.0 / RW_HEAD)
    merged = merged + gate(2) * _dot(oc_ref[...], woc_ref[...])
    gate_b = gate(1)
    y7 = dy * lax.rsqrt(var + LNX_EPS) * lng_ref[...] + lnb_ref[...]
    merged = merged + gate_b * _dot((y7 * g7_ref[...]).astype(BF16), wob_ref[...])
    o_ref[...] = x_ref[...] + _dot(merged.astype(BF16), wout_ref[...])


def _merge(x2, h2, oa, yf, yb, bonus, g7, oc, T, w):
    ntok = x2.shape[0]
    tm = min(512, T)
    row = lambda i: (i, 0)
    half = lambda: pl.BlockSpec((tm, RW_DIM), row)
    return pl.pallas_call(
        _merge_kernel,
        grid=(ntok // tm,),
        in_specs=[pl.BlockSpec((tm, D_MODEL), row), pl.BlockSpec((tm, D_MODEL), row),
                  half(), half(), half(), half(), half(), half(),
                  _full((D_MODEL, 3 * D_MODEL)), _full((RW_DIM, D_MODEL)),
                  _full((RW_DIM, D_MODEL)), _full((RW_DIM, D_MODEL)), _full((D_MODEL, D_MODEL)),
                  _full((1, RW_DIM)), _full((1, RW_DIM)), _full((RW_DIM, RW_DIM))],
        out_specs=pl.BlockSpec((tm, D_MODEL), row),
        out_shape=jax.ShapeDtypeStruct((ntok, D_MODEL), F32),
        compiler_params=_cparams("parallel"),
        name="merge",
    )(x2, h2, oa, yf, yb, bonus, g7, oc, w["w_gate"], w["w_o_a"], w["w_o_b"], w["w_o_c"],
      w["w_out"], w["lnx_g"], w["lnx_b"], w["seg"])


MXU_TILE = 256
FFN_SPLITS = (0, 5 * MXU_TILE, D_FF)


def _ffn_kernel(x_ref, xp_ref, xn_ref, g_ref, wug_ref, wuv_ref, cw_ref, cb_ref, wd_ref, o_ref, h_scr, ug_scr,
                *, tm, npos):
    i = pl.program_id(0)
    not_first = (i % npos != 0).astype(F32)
    not_last = (i % npos != npos - 1).astype(F32)
    g = g_ref[...]
    h_scr[0:SUBLANES, :] = (_rms(xp_ref[...], g) * not_first).astype(BF16)
    h_scr[SUBLANES:SUBLANES + tm, :] = _rms(x_ref[...], g).astype(BF16)
    h_scr[SUBLANES + tm:, :] = (_rms(xn_ref[...], g) * not_last).astype(BF16)
    out = x_ref[...]
    for lo, hi in zip(FFN_SPLITS[:-1], FFN_SPLITS[1:]):
        n = hi - lo
        ug_scr[:, 0:n] = _dot(h_scr[...], wug_ref[:, lo:hi])
        uv = _dot(h_scr[SUBLANES:SUBLANES + tm, :], wuv_ref[:, lo:hi])
        cw = cw_ref[:, lo:hi]
        c = (cw[0:1] * ug_scr[SUBLANES - 1:SUBLANES - 1 + tm, 0:n] + cw[1:2] * ug_scr[SUBLANES:SUBLANES + tm, 0:n]
             + cw[2:3] * ug_scr[SUBLANES + 1:SUBLANES + 1 + tm, 0:n] + cb_ref[:, lo:hi])
        act = 0.5 * c * (1.0 + lax.erf(c * np.float32(1.0 / np.sqrt(2.0)))) * uv
        out = out + _dot(act.astype(BF16), wd_ref[lo:hi, :])
    o_ref[...] = out


def _ffn(x2, T, w):
    ntok = x2.shape[0]
    tm = min(512, T)
    fc = max(hi - lo for lo, hi in zip(FFN_SPLITS[:-1], FFN_SPLITS[1:]))
    npos = T // tm
    nblk8 = ntok // SUBLANES
    r8 = tm // SUBLANES
    row = lambda i: (i, 0)
    return pl.pallas_call(
        functools.partial(_ffn_kernel, tm=tm, npos=npos),
        grid=(ntok // tm,),
        in_specs=[pl.BlockSpec((tm, D_MODEL), row),
                  pl.BlockSpec((SUBLANES, D_MODEL), lambda i: (jnp.maximum(i * r8 - 1, 0), 0)),
                  pl.BlockSpec((SUBLANES, D_MODEL), lambda i: (jnp.minimum((i + 1) * r8, nblk8 - 1), 0)),
                  _full((1, D_MODEL)), _full((D_MODEL, D_FF)), _full((D_MODEL, D_FF)), _full((3, D_FF)),
                  _full((1, D_FF)), _full((D_FF, D_MODEL))],
        out_specs=pl.BlockSpec((tm, D_MODEL), row),
        out_shape=jax.ShapeDtypeStruct((ntok, D_MODEL), F32),
        scratch_shapes=[pltpu.VMEM((tm + 2 * SUBLANES, D_MODEL), BF16), pltpu.VMEM((tm + 2 * SUBLANES, fc), F32)],
        compiler_params=_cparams("parallel"),
        name="conv_ffn",
    )(x2, x2, x2, w["norm_ffn_g"], w["w_up_gate"], w["w_up_val"], w["conv_w"], w["conv_b"], w["w_down"])


def _prep_weights(p):
    w = {}
    row = lambda a: a.reshape(1, -1).astype(F32)
    w_in = p["w_in"]
    o = np.cumsum([0, Q_LORA, KV_LORA, MLA_ROPE, RW_COLS, X_DIM, 3 * D_MODEL])
    seg = lambda i: w_in[:, o[i]:o[i + 1]]
    zc = lambda n: jnp.zeros((D_MODEL, n), F32)
    w["w_c"] = jnp.concatenate([seg(0), seg(1), zc(MLA_NOPE), seg(2), zc(LANES - MLA_QK)], axis=1).astype(BF16)
    w["w_rw"] = seg(3).astype(BF16)
    w["w_xq"] = seg(4).astype(BF16)
    w["w_gate"] = seg(5).astype(BF16)
    w["norm_mix_g"] = row(p["norm_mix_g"])
    w["q_norm_g"] = row(p["q_norm_g"])
    w["kv_norm_g"] = row(p["kv_norm_g"])
    pad_slot = lambda a: jnp.pad(a, ((0, 0), (0, 0), (0, LANES - a.shape[-1]))).reshape(a.shape[0], -1)
    uq = p["w_uq"].reshape(Q_LORA, MLA_HEADS, MLA_QK)
    w["w_uq"] = pad_slot(uq).astype(BF16)
    half = MLA_ROPE // 2
    uq_rot = jnp.concatenate([jnp.zeros_like(uq[:, :, :MLA_NOPE]), -uq[:, :, MLA_NOPE + half:],
                              uq[:, :, MLA_NOPE:MLA_NOPE + half]], axis=-1)
    w["w_uq_rot"] = pad_slot(uq_rot).astype(BF16)
    ukv =p["w_ukv"].reshape(KV_LORA, MLA_HEADS, MLA_NOPE + MLA_V)
    w["w_uk"] = pad_slot(ukv[:, :, :MLA_NOPE]).astype(BF16)
    w["w_uv"] = pad_slot(ukv[:, :, MLA_NOPE:]).T.astype(BF16)
    w["mla_qn_g_scaled"] = p["mla_qn_g"].reshape(-1) * np.float32(MLA_QK ** -0.5 * np.log2(np.e))
    w["mla_kn_g"] = p["mla_kn_g"].reshape(-1)
    for name in ("mu_prev", "mu_next", "k_k", "k_a", "r_k", "lnx_g", "lnx_b", "mem_norm_g", "x_kn_g", "norm_ffn_g",
                 "conv_b"):
        w[name] = row(p[name])
    w["w0"] = jnp.stack([p["w0_f"], p["w0_b"]]).reshape(2, 1, RW_DIM)
    w["a0"] = jnp.stack([p["a0_f"], p["a0_b"]]).reshape(2, 1, RW_DIM)
    zl = jnp.zeros((LORA, RW_DIM), F32)
    w["w2"] = jnp.stack([jnp.concatenate([p["w2_f"], zl]), jnp.concatenate([zl, p["w2_b"]])]).astype(BF16)
    w["a2"] = jnp.stack([jnp.concatenate([p["a2_f"], zl]), jnp.concatenate([zl, p["a2_b"]])]).astype(BF16)
    w["g2"] = p["g2"].astype(BF16)
    hid = np.arange(RW_DIM) // RW_HEAD
    w["seg"] = jnp.asarray((hid[:, None] == hid[None, :]).astype(np.float32), BF16)
    mkv = p["w_mkv"].reshape(D_MODEL, X_HEADS, 2 * X_HEAD)
    w["w_mk"] = mkv[:, :, :X_HEAD].reshape(D_MODEL, X_DIM).astype(BF16)
    w["w_mv"] = mkv[:, :, X_HEAD:].reshape(D_MODEL, X_DIM).astype(BF16)
    w["x_qn_g_scaled"] = row(p["x_qn_g"]) * np.float32(X_HEAD ** -0.5)
    for name in ("w_o_a", "w_o_b", "w_o_c", "w_out", "w_down"):
        w[name] = p[name].astype(BF16)
    w["w_up_gate"] = p["w_up"][:, :D_FF].astype(BF16)
    w["w_up_val"] = p["w_up"][:, D_FF:].astype(BF16)
    w["conv_w"] = p["conv_w"].astype(F32)
    return w


def _rope_tables(T, gq, gk):
    half = MLA_ROPE // 2
    inv = jnp.power(ROPE_THETA, -jnp.arange(half, dtype=F32) / half)
    ang = jnp.arange(T, dtype=F32)[:, None] * inv[None, :]
    cos, sin = jnp.cos(ang), jnp.sin(ang)
    z = lambda n: jnp.zeros((T, n), F32)
    pad = z(LANES - MLA_QK)

    def own(g):
        g1, g2 = g[MLA_NOPE:MLA_NOPE + half], g[MLA_NOPE + half:]
        return jnp.concatenate([jnp.broadcast_to(g[:MLA_NOPE], (T, MLA_NOPE)), cos * g1, cos * g2, pad], axis=1)

    g1, g2 = gq[MLA_NOPE:MLA_NOPE + half], gq[MLA_NOPE + half:]
    qb = jnp.concatenate([z(MLA_NOPE), sin * g2, sin * g1, pad], axis=1)
    g1, g2 = gk[MLA_NOPE:MLA_NOPE + half], gk[MLA_NOPE + half:]
    kba = jnp.concatenate([z(MLA_NOPE), -sin * g2, z(half), pad], axis=1)
    kbb = jnp.concatenate([z(MLA_NOPE), z(half), sin * g1, pad], axis=1)
    return own(gq), qb, own(gk), kba, kbb


def _layer(x, mem, w):
    B, T, _ = x.shape
    x2 = x.reshape(B * T, D_MODEL)
    q, k, v, h2 = _mla_prep(x2, T, w, _rope_tables(T, w["mla_qn_g_scaled"], w["mla_kn_g"]))
    o_a = _flash(q, k, v, B, T)
    v7, kkt, rt, kh, bh, kw, bw, wt, bonus, g7 = _rw_prep(h2, T, w)
    y_f, y_b = _rw_scan(v7, kkt, rt, kh, bh, kw, bw, wt, B, T)
    mk, mv = _mem_kv(mem.reshape(B * N_MEM, D_MODEL), w)
    o_c = _xattn(h2, mk, mv, B, T, w)
    x1 = _merge(x2, h2, o_a, y_f, y_b, bonus, g7, o_c, T, w)
    return _ffn(x1, T, w).reshape(B, T, D_MODEL)


def kernel(x_prompt, x_sample, mem_prompt, mem_sample, norm_mix_g, w_in, q_norm_g, w_uq, kv_norm_g, w_ukv, mla_qn_g, mla_kn_g, w_o_a, mu_prev, mu_next, w0_f, w2_f, a0_f, a2_f, w0_b, w2_b, a0_b, a2_b, g2, k_k, k_a, r_k, lnx_g, lnx_b, w_o_b, mem_norm_g, w_mkv, x_qn_g, x_kn_g, w_o_c, w_out, norm_ffn_g, w_up, conv_w, conv_b, w_down):
    p = dict(norm_mix_g=norm_mix_g, w_in=w_in, q_norm_g=q_norm_g, w_uq=w_uq, kv_norm_g=kv_norm_g, w_ukv=w_ukv,
             mla_qn_g=mla_qn_g, mla_kn_g=mla_kn_g, w_o_a=w_o_a, mu_prev=mu_prev, mu_next=mu_next,
             w0_f=w0_f, w2_f=w2_f, a0_f=a0_f, a2_f=a2_f, w0_b=w0_b, w2_b=w2_b, a0_b=a0_b, a2_b=a2_b,
             g2=g2, k_k=k_k, k_a=k_a, r_k=r_k, lnx_g=lnx_g, lnx_b=lnx_b, w_o_b=w_o_b,
             mem_norm_g=mem_norm_g, w_mkv=w_mkv, x_qn_g=x_qn_g, x_kn_g=x_kn_g, w_o_c=w_o_c, w_out=w_out,
             norm_ffn_g=norm_ffn_g, w_up=w_up, conv_w=conv_w, conv_b=conv_b, w_down=w_down)
    w = _prep_weights({name: a[0] for name, a in p.items()})
    return (_layer(x_prompt, mem_prompt, w), _layer(x_sample, mem_sample, w))
```

```python
import functools

import numpy as np
import jax
import jax.numpy as jnp
from jax import lax
from jax.experimental import pallas as pl
from jax.experimental.pallas import tpu as pltpu

F32 = jnp.float32
BF16 = jnp.bfloat16

D_MODEL = 1024
RMS_EPS = 1e-6
N_MEM = 256
MLA_HEADS = 8
MLA_NOPE = 64
MLA_ROPE = 32
MLA_QK = MLA_NOPE + MLA_ROPE
MLA_V = 64
Q_LORA = 384
KV_LORA = 256
ROPE_THETA = 10000.0
RW_HEADS = 8
RW_HEAD = 64
RW_DIM = RW_HEADS * RW_HEAD
LORA = 64
GATE_LORA = 128
RW_COLS = 3 * RW_DIM + 4 * LORA + GATE_LORA
LNX_EPS = 64e-5
X_HEADS = 4
X_HEAD = 128
X_DIM = X_HEADS * X_HEAD
D_FF = 2816

LANES = 128
SUBLANES = 8
CHUNK = 64
GROUP = 2
GW = GROUP * RW_HEAD
SLOT_COLS = MLA_HEADS * LANES
MLA_C_COLS = Q_LORA + KV_LORA + LANES
SCAN_CHUNKS = 8
VMEM_LIMIT = 56 * 1024 * 1024


def _cparams(*sem):
    return pltpu.CompilerParams(dimension_semantics=sem, vmem_limit_bytes=VMEM_LIMIT)


def _rms(x, g, eps=RMS_EPS):
    return x * lax.rsqrt(jnp.mean(x * x, axis=-1, keepdims=True) + eps) * g


def _dot(a, b):
    return jnp.dot(a, b, preferred_element_type=F32)


def _dot_nt(a, b):
    return lax.dot_general(a, b, (((1,), (1,)), ((), ())), preferred_element_type=F32)


def _dot_tn(a, b):
    return lax.dot_general(a, b, (((0,), (0,)), ((), ())), preferred_element_type=F32)


def _dot_0_1(x, e):
    return _dot(x.astype(BF16), e)


def _dot_0_1_l(e, x):
    hi = x.astype(BF16)
    lo = (x - hi.astype(F32)).astype(BF16)
    return _dot(e, hi) + _dot(e, lo)


def _sigmoid(z):
    return 1.0 / (1.0 + jnp.exp(-z))


def _full(shape):
    nd = len(shape)
    return pl.BlockSpec(shape, lambda *_: (0,) * nd, pipeline_mode=pl.Buffered(1))


def _mla_prep_kernel(x_ref, g_ref, wc_ref, gq_ref, gkv_ref, wuq_ref, wuqr_ref, wuk_ref, wuv_ref,
                     qa_ref, qb_ref, ka_ref, kba_ref, kbb_ref, q_out, k_out, v_out, h_out):
    tm = x_ref.shape[0]
    half = MLA_ROPE // 2
    halves = [slice(0, tm // 2), slice(tm // 2, tm)]
    cs = []
    for r in halves:
        h = _rms(x_ref[r, :], g_ref[...]).astype(BF16)
        h_out[r, :] = h
        cs.append(_dot(h, wc_ref[...]))
    proj = []
    for c in cs:
        cq = _rms(c[:, :Q_LORA], gq_ref[...]).astype(BF16)
        ckv = _rms(c[:, Q_LORA:Q_LORA + KV_LORA], gkv_ref[...]).astype(BF16)
        proj.append((_dot(cq, wuq_ref[...]), _dot(cq, wuqr_ref[...]), _dot(ckv, wuk_ref[...]),
                     _dot_nt(wuv_ref[...], ckv)))
    for r, c, (q, qp, k, vt) in zip(halves, cs, proj):
        slot_row = lax.broadcasted_iota(jnp.int32, vt.shape, 0) % LANES
        v_out[0, :, r] = jnp.where(slot_row == MLA_V, 1.0, vt).astype(BF16)
        kr = c[:, Q_LORA + KV_LORA:]
        qa = qa_ref[r, :]
        qb = qb_ref[r, :]
        ka = ka_ref[r, :]
        kr_roped = (kr * ka + pltpu.roll(kr, LANES - half, 1) * kba_ref[r, :]
                    + pltpu.roll(kr, half, 1) * kbb_ref[r, :])
        kr_ss = jnp.sum(kr * kr, axis=-1, keepdims=True)
        for hd in range(MLA_HEADS):
            sl = slice(LANES * hd, LANES * (hd + 1))
            t = q[:, sl]
            ss = jnp.sum(t * t, axis=-1, keepdims=True) * (1.0 / MLA_QK)
            q_out[r, sl] = ((t * qa + qp[:, sl] * qb) * lax.rsqrt(ss + RMS_EPS)).astype(BF16)
            t = k[:, sl]
            ss = (jnp.sum(t * t, axis=-1, keepdims=True) + kr_ss) * (1.0 / MLA_QK)
            k_out[r, sl] = ((t * ka + kr_roped) * lax.rsqrt(ss + RMS_EPS)).astype(BF16)


def _mla_prep(x2, T, w, tabs):
    ntok = x2.shape[0]
    tm = _attn_tile(T)
    npos = T // tm
    row = lambda i: (i, 0)
    pos = lambda i: (i % npos, 0)
    return pl.pallas_call(
        _mla_prep_kernel,
        grid=(ntok // tm,),
        in_specs=[pl.BlockSpec((tm, D_MODEL), row), _full((1, D_MODEL)), _full((D_MODEL, MLA_C_COLS)),
                  _full((1, Q_LORA)), _full((1, KV_LORA)), _full((Q_LORA, SLOT_COLS)), _full((Q_LORA, SLOT_COLS)),
                  _full((KV_LORA, SLOT_COLS)), _full((SLOT_COLS, KV_LORA))] + [pl.BlockSpec((tm, LANES), pos)] * 5,
        out_specs=[pl.BlockSpec((tm, SLOT_COLS), row), pl.BlockSpec((tm, SLOT_COLS), row),
                   pl.BlockSpec((1, SLOT_COLS, tm), lambda i: (i, 0, 0)), pl.BlockSpec((tm, D_MODEL), row)],
        out_shape=[jax.ShapeDtypeStruct((ntok, SLOT_COLS), BF16), jax.ShapeDtypeStruct((ntok, SLOT_COLS), BF16),
                   jax.ShapeDtypeStruct((ntok // tm, SLOT_COLS, tm), BF16),
                   jax.ShapeDtypeStruct((ntok, D_MODEL), BF16)],
        compiler_params=_cparams("parallel"),
        name="mla_prep",
    )(x2, w["norm_mix_g"], w["w_c"], w["q_norm_g"], w["kv_norm_g"], w["w_uq"], w["w_uq_rot"], w["w_uk"],
      w["w_uv"], *tabs)


def _flash_kernel(q_ref, k_ref, vt_ref, o_ref, s0_ref, s1_ref, *, tk, nk):
    tq = q_ref.shape[0]
    slots = [slice(LANES * h, LANES * (h + 1)) for h in range(2)]
    per = tk // vt_ref.shape[2]

    def scores(j, s_ref):
        kblk = k_ref[pl.ds(pl.multiple_of(j * tk, tk), tk), :]
        for h in range(2):
            s_ref[h] = _dot_nt(kblk[:, slots[h]], q_ref[:, slots[h]]).astype(BF16)

    def consume(j, s_ref, state):
        vt = jnp.concatenate([vt_ref[per * j + r] for r in range(per)], axis=1)
        m_news = [jnp.maximum(state[h][0], jnp.max(s_ref[h], axis=0, keepdims=True).astype(F32)) for h in range(2)]
        pts = [jnp.exp2(s_ref[h] - m_news[h].astype(BF16)) for h in range(2)]
        return tuple((m_news[h], jnp.exp2(state[h][0] - m_news[h]) * state[h][1] + _dot(vt[slots[h], :], pts[h]))
                     for h in range(2))

    def body(t, state):
        j = 2 * t
        scores(j + 1, s1_ref)
        state = consume(j, s0_ref, state)
        scores(j + 2, s0_ref)
        return consume(j + 1, s1_ref, state)

    state = tuple((jnp.full((1, tq), -jnp.inf, F32), jnp.zeros((LANES, tq), F32)) for _ in range(2))
    scores(0, s0_ref)
    state = lax.fori_loop(0, nk // 2 - 1, body, state)
    scores(nk - 1, s1_ref)
    state = consume(nk - 2, s0_ref, state)
    state = consume(nk - 1, s1_ref, state)
    outs = [acc[0:MLA_V, :] / acc[MLA_V:MLA_V + 1, :] for _, acc in state]
    o_ref[...] = jnp.concatenate(outs, axis=0).T.astype(BF16)


def _attn_tile(T):
    return min(512, T // 4)


def _flash(q, k, vt, B, T):
    tv = _attn_tile(T)
    tq, tk = (tv, 2 * tv) if T // tv >= 16 else (2 * tv, tv)
    nq = T // tq
    nk = T // tk
    assert nk % 2 == 0 and tk % tv == 0, (T, tk, tv)
    return pl.pallas_call(
        functools.partial(_flash_kernel, tk=tk, nk=nk),
        grid=(B, MLA_HEADS // 2, nq),
        in_specs=[pl.BlockSpec((tq, 2 * LANES), lambda b, hp, i: (b * nq + i, hp)),
                  pl.BlockSpec((T, 2 * LANES), lambda b, hp, i: (b, hp)),
                  pl.BlockSpec((T // tv, 2 * LANES, tv), lambda b, hp, i: (b, hp, 0))],
        out_specs=pl.BlockSpec((tq, LANES), lambda b, hp, i: (b * nq + i, hp)),
        out_shape=jax.ShapeDtypeStruct((B * T, MLA_HEADS * MLA_V), BF16),
        scratch_shapes=[pltpu.VMEM((2, tk, tq), BF16), pltpu.VMEM((2, tk, tq), BF16)],
        compiler_params=_cparams("parallel", "parallel", "arbitrary"),
        name="mla_flash",
    )(q, k, vt)


HALO = 16


def _rw_prep_kernel(h_ref, hp_ref, hn_ref, wrw_ref, mup_ref, mun_ref, kk_ref, ka_ref, rk_ref,
                    w0_ref, w2_ref, a0_ref, a2_ref, g2_ref, seg_ref, tri_ref,
                    v_out, kkt_out, rt_out, kh_out, bh_out, kw_out, bw_out, wt_out, bonus_out, g7_out,
                    h_scr, rw_scr, *, tm, npos):
    i = pl.program_id(0)
    h_scr[0:HALO, :] = jnp.where(i % npos != 0, hp_ref[...], jnp.zeros((), BF16))
    h_scr[HALO:HALO + tm, :] = h_ref[...]
    h_scr[HALO + tm:, :] = jnp.where(i % npos != npos - 1, hn_ref[...], jnp.zeros((), BF16))
    rb = tri_ref.shape[1]
    edges = [0] + [HALO + rb * (blk + 1) + HALO for blk in range(tm // rb - 1)] + [tm + 2 * HALO]
    mup = mup_ref[...]
    mun = mun_ref[...]
    mu0 = 1.0 - mup - mun
    seg = seg_ref[...]
    for blk in range(tm // rb):
        lo, hi = edges[blk], edges[blk + 1]
        rw_scr[lo:hi, :] = _dot(h_scr[lo:hi, :], wrw_ref[...])
        r0 = HALO + rb * blk
        rows = slice(rb * blk, rb * (blk + 1))
        rwf = rw_scr[r0:r0 + rb, :] * mu0 + mup * rw_scr[r0 - 1:r0 - 1 + rb, :] + mun * rw_scr[r0 + 1:r0 + 1 + rb, :]
        r7 = rwf[:, 0:RW_DIM]
        k7 = rwf[:, RW_DIM:2 * RW_DIM]
        v7 = rwf[:, 2 * RW_DIM:3 * RW_DIM]
        wl = rwf[:, 3 * RW_DIM:3 * RW_DIM + 2 * LORA]
        al = rwf[:, 3 * RW_DIM + 2 * LORA:3 * RW_DIM + 4 * LORA]
        gl = rwf[:, 3 * RW_DIM + 4 * LORA:]
        kx = k7 * kk_ref[...]
        kkn = kx * lax.rsqrt(jnp.maximum(_dot_0_1(kx * kx, seg), 1e-24))
        v_out[rows, :] = v7.astype(BF16)
        g7_out[rows, :] = _dot(_sigmoid(gl).astype(BF16), g2_ref[...])
        tw = jnp.tanh(wl).astype(BF16)
        alb = al.astype(BF16)
        bonus = jnp.zeros((rb, RW_DIM), F32)
        for d in range(2):
            z = -(w0_ref[d] + _dot(tw, w2_ref[d]))
            softplus = jnp.maximum(z, 0.0) + jnp.log(1.0 + jnp.exp(-jnp.abs(z)))
            lw = -jnp.exp(-softplus - 0.5)
            a = _sigmoid(a0_ref[d] + _dot(alb, a2_ref[d]))
            kd = k7 * (1.0 + (a - 1.0) * ka_ref[...])
            b = kkn * a
            bonus = bonus + _dot_0_1(r7 * kd * rk_ref[...], seg) * v7
            cum = _dot_0_1_l(tri_ref[d], lw)
            last = CHUNK - 1 if d == 0 else 0
            wtot_rows = [jnp.exp(cum[CHUNK * ci + last:CHUNK * ci + last + 1, :]) for ci in range(rb // CHUNK)]
            wtot = jnp.concatenate([jnp.broadcast_to(r, (CHUNK, RW_DIM)) for r in wtot_rows], axis=0)
            w_incl = jnp.exp(cum)
            w_excl = jnp.exp(cum - lw)
            w_inv = 1.0 / w_incl
            w_rest = wtot * w_inv
            kkt_out[d, rows, :] = (kkn * w_excl).astype(BF16)
            rt_out[d, rows, :] = (r7 * w_incl).astype(BF16)
            kh_out[d, rows, :] = (kd * w_inv).astype(BF16)
            bh_out[d, rows, :] = (b * w_inv).astype(BF16)
            kw_out[d, rows, :] = (kd * w_rest).astype(BF16)
            bw_out[d, rows, :] = (b * w_rest).astype(BF16)
            for ci in range(rb // CHUNK):
                wt_out[d, blk * (rb // CHUNK) + ci] = wtot_rows[ci]
        bonus_out[rows, :] = bonus


def _rw_prep(h2, T, w):
    ntok = h2.shape[0]
    tm = min(512, T)
    npos = T // tm
    nhalo = ntok // HALO
    rh = tm // HALO
    row = lambda i: (i, 0)
    drow = lambda i: (0, i, 0)
    dspec = pl.BlockSpec((2, tm, RW_DIM), drow)
    dshape = jax.ShapeDtypeStruct((2, ntok, RW_DIM), BF16)
    rb = min(256, tm)
    tri = _chunk_masks(rb)
    outs = pl.pallas_call(
        functools.partial(_rw_prep_kernel, tm=tm, npos=npos),
        grid=(ntok // tm,),
        in_specs=[pl.BlockSpec((tm, D_MODEL), row),
                  pl.BlockSpec((HALO, D_MODEL), lambda i: (jnp.maximum(i * rh - 1, 0), 0)),
                  pl.BlockSpec((HALO, D_MODEL), lambda i: (jnp.minimum((i + 1) * rh, nhalo - 1), 0)),
                  _full((D_MODEL, RW_COLS)), _full((1, RW_COLS)), _full((1, RW_COLS)),
                  _full((1, RW_DIM)), _full((1, RW_DIM)), _full((1, RW_DIM)),
                  _full((2, 1, RW_DIM)), _full((2, 2 * LORA, RW_DIM)), _full((2, 1, RW_DIM)),
                  _full((2, 2 * LORA, RW_DIM)), _full((GATE_LORA, RW_DIM)), _full((RW_DIM, RW_DIM)),
                  _full((2, rb, rb))],
        out_specs=[pl.BlockSpec((tm, RW_DIM), row), dspec, dspec, dspec, dspec, dspec, dspec,
                   pl.BlockSpec((2, tm // CHUNK, 1, RW_DIM), lambda i: (0, i, 0, 0)),
                   pl.BlockSpec((tm, RW_DIM), row), pl.BlockSpec((tm, RW_DIM), row)],
        out_shape=[jax.ShapeDtypeStruct((ntok, RW_DIM), BF16), dshape, dshape, dshape, dshape, dshape, dshape,
                   jax.ShapeDtypeStruct((2, ntok // CHUNK, 1, RW_DIM), F32),
                   jax.ShapeDtypeStruct((ntok, RW_DIM), F32), jax.ShapeDtypeStruct((ntok, RW_DIM), F32)],
        scratch_shapes=[pltpu.VMEM((tm + 2 * HALO, D_MODEL), BF16), pltpu.VMEM((tm + 2 * HALO, RW_COLS), F32)],
        compiler_params=_cparams("parallel"),
        name="rw_prep",
    )(h2, h2, h2, w["w_rw"], w["mu_prev"], w["mu_next"], w["k_k"], w["k_a"], w["r_k"],
      w["w0"], w["w2"], w["a0"], w["a2"], w["g2"], w["seg"], tri)
    return outs


def _chunk_masks(tm):
    t = np.arange(tm)
    same = (t[:, None] // CHUNK) == (t[None, :] // CHUNK)
    fwd = same & (t[None, :] <= t[:, None])
    bwd = same & (t[None, :] >= t[:, None])
    return jnp.asarray(np.stack([fwd, bwd]).astype(np.float32), BF16)


def _rw_scan_kernel(*refs):
    ins = refs[:16]
    yf_ref, yb_ref, s_ref = refs[16:]
    c = pl.program_id(1)

    @pl.when(c == 0)
    def _():
        s_ref[...] = jnp.zeros(s_ref.shape, F32)

    t_pos = lax.broadcasted_iota(jnp.int32, (CHUNK, GW), 0)
    s_pos = lax.broadcasted_iota(jnp.int32, (CHUNK, GW), 1) % CHUNK
    eye_c = (t_pos == s_pos).astype(F32)
    ri = lax.broadcasted_iota(jnp.int32, (GW, GW), 0)
    ci = lax.broadcasted_iota(jnp.int32, (GW, GW), 1)
    head_mask = (ri // CHUNK) == (ci // RW_HEAD)
    stack = lambda a: jnp.where(head_mask, jnp.concatenate([a] * GROUP, axis=0), jnp.zeros((), BF16))

    def prepare(orders):
        items = []
        for d, gi, order in ((d, gi, o) for d in range(2) for gi in range(RW_HEADS // GROUP) for o in orders):
            kkt, rt, kh, bh, kw, bw, v, wt = ins[8 * d:8 * d + 8]
            strict = (s_pos < t_pos) if d == 0 else (s_pos > t_pos)
            incl = (s_pos <= t_pos) if d == 0 else (s_pos >= t_pos)
            sub = order if d == 0 else SCAN_CHUNKS - 1 - order
            rows = slice(CHUNK * sub, CHUNK * (sub + 1))
            sl = slice(GW * gi, GW * (gi + 1))
            it = dict(y_ref=(yf_ref, yb_ref)[d], sl=sl, rows=rows, s_ref=s_ref.at[d, gi], wtot=wt[sub][:, sl],
                      order=order)
            v_c = v[rows, sl]
            kkt_rt = jnp.concatenate([kkt[rows, sl], rt[rows, sl]], axis=0)
            a_all = _dot_nt(kkt_rt, jnp.concatenate([stack(kh[rows, sl]), stack(bh[rows, sl])], axis=0))
            ak = jnp.where(strict, a_all[:CHUNK, :GW], 0.0).astype(BF16)
            bk = jnp.where(incl, a_all[CHUNK:, :GW], 0.0).astype(BF16)
            it.update(kkt_rt=kkt_rt, v=v_c, kbw=jnp.concatenate([kw[rows, sl], bw[rows, sl]], axis=0),
                      abk=jnp.concatenate([ak, bk], axis=0), bb=jnp.where(incl, a_all[CHUNK:, GW:], 0.0).astype(BF16))
            ab = jnp.where(strict, a_all[:CHUNK, GW:], 0.0)
            it.update(tinv=eye_c - ab, pw=ab.astype(BF16))
            items.append(it)
        for it in items:
            it["abk_v"] = _dot(it["abk"], stack(it["v"]))
        for it in items:
            it["pw"] = _dot(it["pw"], stack(it["pw"])).astype(BF16)
        for k in range(5):
            for it in items:
                sk = stack(it["pw"])
                if k < 4:
                    res = _dot(jnp.concatenate([it["pw"], it["tinv"].astype(BF16)], axis=0), sk)
                    it["pw"] = res[:CHUNK].astype(BF16)
                    it["tinv"] = it["tinv"] + res[CHUNK:]
                else:
                    it["tinv"] = (it["tinv"] + _dot(it["tinv"].astype(BF16), sk)).astype(BF16)
        return items

    def advance(chains):
        for ch in chains:
            sb = ch["s_ref"][...].astype(BF16)
            ch["zs"] = _dot_nt(ch["kkt_rt"], sb) + ch["abk_v"]
        for ch in chains:
            ch["u"] = (-_dot(ch["tinv"], stack(ch["zs"][:CHUNK].astype(BF16)))).astype(BF16)
        for ch in chains:
            ch["y_ref"][ch["rows"], ch["sl"]] = ch["zs"][CHUNK:] + _dot(ch["bb"], stack(ch["u"]))
        for ch in chains:
            upd = _dot_tn(jnp.concatenate([ch["v"], ch["u"]], axis=0), ch["kbw"])
            ch["s_ref"][...] = ch["s_ref"][...] * ch["wtot"] + jnp.where(head_mask, upd, 0.0)

    prepared = prepare(range(SCAN_CHUNKS))
    for order in range(SCAN_CHUNKS):
        advance([it for it in prepared if it["order"] == order])


def _rw_scan(v, kkt, rt, kh, bh, kw, bw, wt, B, T):
    ntok = B * T
    rows = SCAN_CHUNKS * CHUNK
    nb = T // rows
    in_specs = []
    args = []
    for d in range(2):
        if d == 0:
            cm = lambda b, c: b * nb + c
        else:
            cm = lambda b, c: b * nb + (nb - 1 - c)
        for arr in (kkt, rt, kh, bh, kw, bw):
            in_specs.append(pl.BlockSpec((None, rows, RW_DIM), lambda b, c, cm=cm, d=d: (d, cm(b, c), 0)))
            args.append(arr)
        in_specs.append(pl.BlockSpec((rows, RW_DIM), lambda b, c, cm=cm: (cm(b, c), 0)))
        args.append(v)
        in_specs.append(pl.BlockSpec((None, SCAN_CHUNKS, 1, RW_DIM), lambda b, c, cm=cm, d=d: (d, cm(b, c), 0, 0)))
        args.append(wt)
    return pl.pallas_call(
        _rw_scan_kernel,
        grid=(B, nb),
        in_specs=in_specs,
        out_specs=[pl.BlockSpec((rows, RW_DIM), lambda b, c: (b * nb + c, 0)),
                   pl.BlockSpec((rows, RW_DIM), lambda b, c: (b * nb + (nb - 1 - c), 0))],
        out_shape=[jax.ShapeDtypeStruct((ntok, RW_DIM), F32), jax.ShapeDtypeStruct((ntok, RW_DIM), F32)],
        scratch_shapes=[pltpu.VMEM((2, RW_HEADS // GROUP, GW, GW), F32)],
        compiler_params=_cparams("parallel", "arbitrary"),
        name="rw_scan",
    )(*args)


def _mem_kv_kernel(m_ref, g_ref, wk_ref, wv_ref, gk_ref, k_out, v_out):
    m = _rms(m_ref[...], g_ref[...]).astype(BF16)
    k = _dot(m, wk_ref[...])
    v_out[...] = _dot(m, wv_ref[...]).astype(BF16)
    for hd in range(X_HEADS):
        sl = slice(X_HEAD * hd, X_HEAD * (hd + 1))
        k_out[:, sl] = _rms(k[:, sl], gk_ref[...]).astype(BF16)


def _mem_kv(mem2, w):
    n = mem2.shape[0]
    row = lambda i: (i, 0)
    return pl.pallas_call(
        _mem_kv_kernel,
        grid=(n // N_MEM,),
        in_specs=[pl.BlockSpec((N_MEM, D_MODEL), row), _full((1, D_MODEL)), _full((D_MODEL, X_DIM)),
                  _full((D_MODEL, X_DIM)), _full((1, X_HEAD))],
        out_specs=[pl.BlockSpec((N_MEM, X_DIM), row), pl.BlockSpec((N_MEM, X_DIM), row)],
        out_shape=[jax.ShapeDtypeStruct((n, X_DIM), BF16), jax.ShapeDtypeStruct((n, X_DIM), BF16)],
        compiler_params=_cparams("parallel"),
        name="mem_kv",
    )(mem2, w["mem_norm_g"], w["w_mk"], w["w_mv"], w["x_kn_g"])


def _xattn_kernel(h_ref, wq_ref, gq_ref, mk_ref, mv_ref, o_ref):
    q = _dot(h_ref[...], wq_ref[...])
    for hd in range(X_HEADS):
        sl = slice(X_HEAD * hd, X_HEAD * (hd + 1))
        qh = _rms(q[:, sl], gq_ref[...]).astype(BF16)
        s = _dot_nt(qh, mk_ref[:, sl])
        p = jnp.exp(s - jnp.max(s, axis=-1, keepdims=True))
        o = _dot(p.astype(BF16), mv_ref[:, sl]) / jnp.sum(p, axis=-1, keepdims=True)
        o_ref[:, sl] = o.astype(BF16)


def _xattn(h2, mk, mv, B, T, w):
    tq = min(512, T)
    nq = T // tq
    return pl.pallas_call(
        _xattn_kernel,
        grid=(B, nq),
        in_specs=[pl.BlockSpec((tq, D_MODEL), lambda b, i: (b * nq + i, 0)),
                  _full((D_MODEL, X_DIM)), _full((1, X_HEAD)),
                  pl.BlockSpec((N_MEM, X_DIM), lambda b, i: (b, 0)), pl.BlockSpec((N_MEM, X_DIM), lambda b, i: (b, 0))],
        out_specs=pl.BlockSpec((tq, X_DIM), lambda b, i: (b * nq + i, 0)),
        out_shape=jax.ShapeDtypeStruct((B * T, X_DIM), BF16),
        compiler_params=_cparams("parallel", "parallel"),
        name="xattn",
    )(h2, w["w_xq"], w["x_qn_g_scaled"], mk, mv)


def _merge_kernel(x_ref, h_ref, oa_ref, yf_ref, yb_ref, bonus_ref, g7_ref, oc_ref, wg_ref, woa_ref, wob_ref,
                  woc_ref, wout_ref, lng_ref, lnb_ref, seg_ref, o_ref):
    h = h_ref[...]
    seg = seg_ref[...]
    gate = lambda bi: _sigmoid(_dot(h, wg_ref[:, D_MODEL * bi:D_MODEL * (bi + 1)]))
    y7 = yf_ref[...] + yb_ref[...] + bonus_ref[...]
    mu = _dot_0_1(y7, seg) * (1.0 / RW_HEAD)
    merged = gate(0) * _dot(oa_ref[...], woa_ref[...])
    dy = y7 - mu
    var = _dot_0_1(dy * dy, seg) * (1.0 / RW_HEAD)
    merged = merged + gate(2) * _dot(oc_ref[...], woc_ref[...])
    gate_b = gate(1)
    y7 = dy * lax.rsqrt(var + LNX_EPS) * lng_ref[...] + lnb_ref[...]
    merged = merged + gate_b * _dot((y7 * g7_ref[...]).astype(BF16), wob_ref[...])
    o_ref[...] = x_ref[...] + _dot(merged.astype(BF16), wout_ref[...])


def _merge(x2, h2, oa, yf, yb, bonus, g7, oc, T, w):
    ntok = x2.shape[0]
    tm = min(512, T)
    row = lambda i: (i, 0)
    half = lambda: pl.BlockSpec((tm, RW_DIM), row)
    return pl.pallas_call(
        _merge_kernel,
        grid=(ntok // tm,),
        in_specs=[pl.BlockSpec((tm, D_MODEL), row), pl.BlockSpec((tm, D_MODEL), row),
                  half(), half(), half(), half(), half(), half(),
                  _full((D_MODEL, 3 * D_MODEL)), _full((RW_DIM, D_MODEL)),
                  _full((RW_DIM, D_MODEL)), _full((RW_DIM, D_MODEL)), _full((D_MODEL, D_MODEL)),
                  _full((1, RW_DIM)), _full((1, RW_DIM)), _full((RW_DIM, RW_DIM))],
        out_specs=pl.BlockSpec((tm, D_MODEL), row),
        out_shape=jax.ShapeDtypeStruct((ntok, D_MODEL), F32),
        compiler_params=_cparams("parallel"),
        name="merge",
    )(x2, h2, oa, yf, yb, bonus, g7, oc, w["w_gate"], w["w_o_a"], w["w_o_b"], w["w_o_c"],
      w["w_out"], w["lnx_g"], w["lnx_b"], w["seg"])


MXU_TILE = 256
FFN_SPLITS = (0, 5 * MXU_TILE, D_FF)


def _ffn_kernel(x_ref, xp_ref, xn_ref, g_ref, wug_ref, wuv_ref, cw_ref, cb_ref, wd_ref, o_ref, h_scr, ug_scr,
                *, tm, npos):
    i = pl.program_id(0)
    not_first = (i % npos != 0).astype(F32)
    not_last = (i % npos != npos - 1).astype(F32)
    g = g_ref[...]
    h_scr[0:SUBLANES, :] = (_rms(xp_ref[...], g) * not_first).astype(BF16)
    h_scr[SUBLANES:SUBLANES + tm, :] = _rms(x_ref[...], g).astype(BF16)
    h_scr[SUBLANES + tm:, :] = (_rms(xn_ref[...], g) * not_last).astype(BF16)
    out = x_ref[...]
    for lo, hi in zip(FFN_SPLITS[:-1], FFN_SPLITS[1:]):
        n = hi - lo
        ug_scr[:, 0:n] = _dot(h_scr[...], wug_ref[:, lo:hi])
        uv = _dot(h_scr[SUBLANES:SUBLANES + tm, :], wuv_ref[:, lo:hi])
        cw = cw_ref[:, lo:hi]
        c = (cw[0:1] * ug_scr[SUBLANES - 1:SUBLANES - 1 + tm, 0:n] + cw[1:2] * ug_scr[SUBLANES:SUBLANES + tm, 0:n]
             + cw[2:3] * ug_scr[SUBLANES + 1:SUBLANES + 1 + tm, 0:n] + cb_ref[:, lo:hi])
        act = 0.5 * c * (1.0 + lax.erf(c * np.float32(1.0 / np.sqrt(2.0)))) * uv
        out = out + _dot(act.astype(BF16), wd_ref[lo:hi, :])
    o_ref[...] = out


def _ffn(x2, T, w):
    ntok = x2.shape[0]
    tm = min(512, T)
    fc = max(hi - lo for lo, hi in zip(FFN_SPLITS[:-1], FFN_SPLITS[1:]))
    npos = T // tm
    nblk8 = ntok // SUBLANES
    r8 = tm // SUBLANES
    row = lambda i: (i, 0)
    return pl.pallas_call(
        functools.partial(_ffn_kernel, tm=tm, npos=npos),
        grid=(ntok // tm,),
        in_specs=[pl.BlockSpec((tm, D_MODEL), row),
                  pl.BlockSpec((SUBLANES, D_MODEL), lambda i: (jnp.maximum(i * r8 - 1, 0), 0)),
                  pl.BlockSpec((SUBLANES, D_MODEL), lambda i: (jnp.minimum((i + 1) * r8, nblk8 - 1), 0)),
                  _full((1, D_MODEL)), _full((D_MODEL, D_FF)), _full((D_MODEL, D_FF)), _full((3, D_FF)),
                  _full((1, D_FF)), _full((D_FF, D_MODEL))],
        out_specs=pl.BlockSpec((tm, D_MODEL), row),
        out_shape=jax.ShapeDtypeStruct((ntok, D_MODEL), F32),
        scratch_shapes=[pltpu.VMEM((tm + 2 * SUBLANES, D_MODEL), BF16), pltpu.VMEM((tm + 2 * SUBLANES, fc), F32)],
        compiler_params=_cparams("parallel"),
        name="conv_ffn",
    )(x2, x2, x2, w["norm_ffn_g"], w["w_up_gate"], w["w_up_val"], w["conv_w"], w["conv_b"], w["w_down"])


def _prep_weights(p):
    w = {}
    row = lambda a: a.reshape(1, -1).astype(F32)
    w_in = p["w_in"]
    o = np.cumsum([0, Q_LORA, KV_LORA, MLA_ROPE, RW_COLS, X_DIM, 3 * D_MODEL])
    seg = lambda i: w_in[:, o[i]:o[i + 1]]
    zc = lambda n: jnp.zeros((D_MODEL, n), F32)
    w["w_c"] = jnp.concatenate([seg(0), seg(1), zc(MLA_NOPE), seg(2), zc(LANES - MLA_QK)], axis=1).astype(BF16)
    w["w_rw"] = seg(3).astype(BF16)
    w["w_xq"] = seg(4).astype(BF16)
    w["w_gate"] = seg(5).astype(BF16)
    w["norm_mix_g"] = row(p["norm_mix_g"])
    w["q_norm_g"] = row(p["q_norm_g"])
    w["kv_norm_g"] = row(p["kv_norm_g"])
    pad_slot = lambda a: jnp.pad(a, ((0, 0), (0, 0), (0, LANES - a.shape[-1]))).reshape(a.shape[0], -1)
    uq = p["w_uq"].reshape(Q_LORA, MLA_HEADS, MLA_QK)
    w["w_uq"] = pad_slot(uq).astype(BF16)
    half = MLA_ROPE // 2
    uq_rot = jnp.concatenate([jnp.zeros_like(uq[:, :, :MLA_NOPE]), -uq[:, :, MLA_NOPE + half:],
                              uq[:, :, MLA_NOPE:MLA_NOPE + half]], axis=-1)
    w["w_uq_rot"] = pad_slot(uq_rot).astype(BF16)
    ukv =p["w_ukv"].reshape(KV_LORA, MLA_HEADS, MLA_NOPE + MLA_V)
    w["w_uk"] = pad_slot(ukv[:, :, :MLA_NOPE]).astype(BF16)
    w["w_uv"] = pad_slot(ukv[:, :, MLA_NOPE:]).T.astype(BF16)
    w["mla_qn_g_scaled"] = p["mla_qn_g"].reshape(-1) * np.float32(MLA_QK ** -0.5 * np.log2(np.e))
    w["mla_kn_g"] = p["mla_kn_g"].reshape(-1)
    for name in ("mu_prev", "mu_next", "k_k", "k_a", "r_k", "lnx_g", "lnx_b", "mem_norm_g", "x_kn_g", "norm_ffn_g",
                 "conv_b"):
        w[name] = row(p[name])
    w["w0"] = jnp.stack([p["w0_f"], p["w0_b"]]).reshape(2, 1, RW_DIM)
    w["a0"] = jnp.stack([p["a0_f"], p["a0_b"]]).reshape(2, 1, RW_DIM)
    zl = jnp.zeros((LORA, RW_DIM), F32)
    w["w2"] = jnp.stack([jnp.concatenate([p["w2_f"], zl]), jnp.concatenate([zl, p["w2_b"]])]).astype(BF16)
    w["a2"] = jnp.stack([jnp.concatenate([p["a2_f"], zl]), jnp.concatenate([zl, p["a2_b"]])]).astype(BF16)
    w["g2"] = p["g2"].astype(BF16)
    hid = np.arange(RW_DIM) // RW_HEAD
    w["seg"] = jnp.asarray((hid[:, None] == hid[None, :]).astype(np.float32), BF16)
    mkv = p["w_mkv"].reshape(D_MODEL, X_HEADS, 2 * X_HEAD)
    w["w_mk"] = mkv[:, :, :X_HEAD].reshape(D_MODEL, X_DIM).astype(BF16)
    w["w_mv"] = mkv[:, :, X_HEAD:].reshape(D_MODEL, X_DIM).astype(BF16)
    w["x_qn_g_scaled"] = row(p["x_qn_g"]) * np.float32(X_HEAD ** -0.5)
    for name in ("w_o_a", "w_o_b", "w_o_c", "w_out", "w_down"):
        w[name] = p[name].astype(BF16)
    w["w_up_gate"] = p["w_up"][:, :D_FF].astype(BF16)
    w["w_up_val"] = p["w_up"][:, D_FF:].astype(BF16)
    w["conv_w"] = p["conv_w"].astype(F32)
    return w


def _rope_tables(T, gq, gk):
    half = MLA_ROPE // 2
    inv = jnp.power(ROPE_THETA, -jnp.arange(half, dtype=F32) / half)
    ang = jnp.arange(T, dtype=F32)[:, None] * inv[None, :]
    cos, sin = jnp.cos(ang), jnp.sin(ang)
    z = lambda n: jnp.zeros((T, n), F32)
    pad = z(LANES - MLA_QK)

    def own(g):
        g1, g2 = g[MLA_NOPE:MLA_NOPE + half], g[MLA_NOPE + half:]
        return jnp.concatenate([jnp.broadcast_to(g[:MLA_NOPE], (T, MLA_NOPE)), cos * g1, cos * g2, pad], axis=1)

    g1, g2 = gq[MLA_NOPE:MLA_NOPE + half], gq[MLA_NOPE + half:]
    qb = jnp.concatenate([z(MLA_NOPE), sin * g2, sin * g1, pad], axis=1)
    g1, g2 = gk[MLA_NOPE:MLA_NOPE + half], gk[MLA_NOPE + half:]
    kba = jnp.concatenate([z(MLA_NOPE), -sin * g2, z(half), pad], axis=1)
    kbb = jnp.concatenate([z(MLA_NOPE), z(half), sin * g1, pad], axis=1)
    return own(gq), qb, own(gk), kba, kbb


def _layer(x, mem, w):
    B, T, _ = x.shape
    x2 = x.reshape(B * T, D_MODEL)
    q, k, v, h2 = _mla_prep(x2, T, w, _rope_tables(T, w["mla_qn_g_scaled"], w["mla_kn_g"]))
    o_a = _flash(q, k, v, B, T)
    v7, kkt, rt, kh, bh, kw, bw, wt, bonus, g7 = _rw_prep(h2, T, w)
    y_f, y_b = _rw_scan(v7, kkt, rt, kh, bh, kw, bw, wt, B, T)
    mk, mv = _mem_kv(mem.reshape(B * N_MEM, D_MODEL), w)
    o_c = _xattn(h2, mk, mv, B, T, w)
    x1 = _merge(x2, h2, o_a, y_f, y_b, bonus, g7, o_c, T, w)
    return _ffn(x1, T, w).reshape(B, T, D_MODEL)


def kernel(x_prompt, x_sample, mem_prompt, mem_sample, norm_mix_g, w_in, q_norm_g, w_uq, kv_norm_g, w_ukv, mla_qn_g, mla_kn_g, w_o_a, mu_prev, mu_next, w0_f, w2_f, a0_f, a2_f, w0_b, w2_b, a0_b, a2_b, g2, k_k, k_a, r_k, lnx_g, lnx_b, w_o_b, mem_norm_g, w_mkv, x_qn_g, x_kn_g, w_o_c, w_out, norm_ffn_g, w_up, conv_w, conv_b, w_down):
    p = dict(norm_mix_g=norm_mix_g, w_in=w_in, q_norm_g=q_norm_g, w_uq=w_uq, kv_norm_g=kv_norm_g, w_ukv=w_ukv,
             mla_qn_g=mla_qn_g, mla_kn_g=mla_kn_g, w_o_a=w_o_a, mu_prev=mu_prev, mu_next=mu_next,
             w0_f=w0_f, w2_f=w2_f, a0_f=a0_f, a2_f=a2_f, w0_b=w0_b, w2_b=w2_b, a0_b=a0_b, a2_b=a2_b,
             g2=g2, k_k=k_k, k_a=k_a, r_k=r_k, lnx_g=lnx_g, lnx_b=lnx_b, w_o_b=w_o_b,
             mem_norm_g=mem_norm_g, w_mkv=w_mkv, x_qn_g=x_qn_g, x_kn_g=x_kn_g, w_o_c=w_o_c, w_out=w_out,
             norm_ffn_g=norm_ffn_g, w_up=w_up, conv_w=conv_w, conv_b=conv_b, w_down=w_down)
    w = _prep_weights({name: a[0] for name, a in p.items()})
    return (_layer(x_prompt, mem_prompt, w), _layer(x_sample, mem_sample, w))
```

```python
import functools

import numpy as np
import jax
import jax.numpy as jnp
from jax import lax
from jax.experimental import pallas as pl
from jax.experimental.pallas import tpu as pltpu

F32 = jnp.float32
BF16 = jnp.bfloat16

D_MODEL = 1024
RMS_EPS = 1e-6
N_MEM = 256
MLA_HEADS = 8
MLA_NOPE = 64
MLA_ROPE = 32
MLA_QK = MLA_NOPE + MLA_ROPE
MLA_V = 64
Q_LORA = 384
KV_LORA = 256
ROPE_THETA = 10000.0
RW_HEADS = 8
RW_HEAD = 64
RW_DIM = RW_HEADS * RW_HEAD
LORA = 64
GATE_LORA = 128
RW_COLS = 3 * RW_DIM + 4 * LORA + GATE_LORA
LNX_EPS = 64e-5
X_HEADS = 4
X_HEAD = 128
X_DIM = X_HEADS * X_HEAD
D_FF = 2816

LANES = 128
SUBLANES = 8
CHUNK = 64
GROUP = 2
GW = GROUP * RW_HEAD
SLOT_COLS = MLA_HEADS * LANES
MLA_C_COLS = Q_LORA + KV_LORA + LANES
SCAN_CHUNKS = 8
VMEM_LIMIT = 56 * 1024 * 1024


def _cparams(*sem):
    return pltpu.CompilerParams(dimension_semantics=sem, vmem_limit_bytes=VMEM_LIMIT)


def _rms(x, g, eps=RMS_EPS):
    return x * lax.rsqrt(jnp.mean(x * x, axis=-1, keepdims=True) + eps) * g


def _dot(a, b):
    return jnp.dot(a, b, preferred_element_type=F32)


def _dot_nt(a, b):
    return lax.dot_general(a, b, (((1,), (1,)), ((), ())), preferred_element_type=F32)


def _dot_tn(a, b):
    return lax.dot_general(a, b, (((0,), (0,)), ((), ())), preferred_element_type=F32)


def _dot_0_1(x, e):
    return _dot(x.astype(BF16), e)


def _dot_0_1_l(e, x):
    hi = x.astype(BF16)
    lo = (x - hi.astype(F32)).astype(BF16)
    return _dot(e, hi) + _dot(e, lo)


def _sigmoid(z):
    return 1.0 / (1.0 + jnp.exp(-z))


def _full(shape):
    nd = len(shape)
    return pl.BlockSpec(shape, lambda *_: (0,) * nd, pipeline_mode=pl.Buffered(1))


def _mla_prep_kernel(x_ref, g_ref, wc_ref, gq_ref, gkv_ref, wuq_ref, wuqr_ref, wuk_ref, wuv_ref,
                     qa_ref, qb_ref, ka_ref, kba_ref, kbb_ref, q_out, k_out, v_out, h_out):
    tm = x_ref.shape[0]
    half = MLA_ROPE // 2
    halves = [slice(0, tm // 2), slice(tm // 2, tm)]
    cs = []
    for r in halves:
        h = _rms(x_ref[r, :], g_ref[...]).astype(BF16)
        h_out[r, :] = h
        cs.append(_dot(h, wc_ref[...]))
    proj = []
    for c in cs:
        cq = _rms(c[:, :Q_LORA], gq_ref[...]).astype(BF16)
        ckv = _rms(c[:, Q_LORA:Q_LORA + KV_LORA], gkv_ref[...]).astype(BF16)
        proj.append((_dot(cq, wuq_ref[...]), _dot(cq, wuqr_ref[...]), _dot(ckv, wuk_ref[...]),
                     _dot_nt(wuv_ref[...], ckv)))
    for r, c, (q, qp, k, vt) in zip(halves, cs, proj):
        slot_row = lax.broadcasted_iota(jnp.int32, vt.shape, 0) % LANES
        v_out[0, :, r] = jnp.where(slot_row == MLA_V, 1.0, vt).astype(BF16)
        kr = c[:, Q_LORA + KV_LORA:]
        qa = qa_ref[r, :]
        qb = qb_ref[r, :]
        ka = ka_ref[r, :]
        kr_roped = (kr * ka + pltpu.roll(kr, LANES - half, 1) * kba_ref[r, :]
                    + pltpu.roll(kr, half, 1) * kbb_ref[r, :])
        kr_ss = jnp.sum(kr * kr, axis=-1, keepdims=True)
        for hd in range(MLA_HEADS):
            sl = slice(LANES * hd, LANES * (hd + 1))
            t = q[:, sl]
            ss = jnp.sum(t * t, axis=-1, keepdims=True) * (1.0 / MLA_QK)
            q_out[r, sl] = ((t * qa + qp[:, sl] * qb) * lax.rsqrt(ss + RMS_EPS)).astype(BF16)
            t = k[:, sl]
            ss = (jnp.sum(t * t, axis=-1, keepdims=True) + kr_ss) * (1.0 / MLA_QK)
            k_out[r, sl] = ((t * ka + kr_roped) * lax.rsqrt(ss + RMS_EPS)).astype(BF16)


def _mla_prep(x2, T, w, tabs):
    ntok = x2.shape[0]
    tm = _attn_tile(T)
    npos = T // tm
    row = lambda i: (i, 0)
    pos = lambda i: (i % npos, 0)
    return pl.pallas_call(
        _mla_prep_kernel,
        grid=(ntok // tm,),
        in_specs=[pl.BlockSpec((tm, D_MODEL), row), _full((1, D_MODEL)), _full((D_MODEL, MLA_C_COLS)),
                  _full((1, Q_LORA)), _full((1, KV_LORA)), _full((Q_LORA, SLOT_COLS)), _full((Q_LORA, SLOT_COLS)),
                  _full((KV_LORA, SLOT_COLS)), _full((SLOT_COLS, KV_LORA))] + [pl.BlockSpec((tm, LANES), pos)] * 5,
        out_specs=[pl.BlockSpec((tm, SLOT_COLS), row), pl.BlockSpec((tm, SLOT_COLS), row),
                   pl.BlockSpec((1, SLOT_COLS, tm), lambda i: (i, 0, 0)), pl.BlockSpec((tm, D_MODEL), row)],
        out_shape=[jax.ShapeDtypeStruct((ntok, SLOT_COLS), BF16), jax.ShapeDtypeStruct((ntok, SLOT_COLS), BF16),
                   jax.ShapeDtypeStruct((ntok // tm, SLOT_COLS, tm), BF16),
                   jax.ShapeDtypeStruct((ntok, D_MODEL), BF16)],
        compiler_params=_cparams("parallel"),
        name="mla_prep",
    )(x2, w["norm_mix_g"], w["w_c"], w["q_norm_g"], w["kv_norm_g"], w["w_uq"], w["w_uq_rot"], w["w_uk"],
      w["w_uv"], *tabs)


def _flash_kernel(q_ref, k_ref, vt_ref, o_ref, s0_ref, s1_ref, *, tk, nk):
    tq = q_ref.shape[0]
    slots = [slice(LANES * h, LANES * (h + 1)) for h in range(2)]
    per = tk // vt_ref.shape[2]

    def scores(j, s_ref):
        kblk = k_ref[pl.ds(pl.multiple_of(j * tk, tk), tk), :]
        for h in range(2):
            s_ref[h] = _dot_nt(kblk[:, slots[h]], q_ref[:, slots[h]]).astype(BF16)

    def consume(j, s_ref, state):
        vt = jnp.concatenate([vt_ref[per * j + r] for r in range(per)], axis=1)
        m_news = [jnp.maximum(state[h][0], jnp.max(s_ref[h], axis=0, keepdims=True).astype(F32)) for h in range(2)]
        pts = [jnp.exp2(s_ref[h] - m_news[h].astype(BF16)) for h in range(2)]
        return tuple((m_news[h], jnp.exp2(state[h][0] - m_news[h]) * state[h][1] + _dot(vt[slots[h], :], pts[h]))
                     for h in range(2))

    def body(t, state):
        j = 2 * t
        scores(j + 1, s1_ref)
        state = consume(j, s0_ref, state)
        scores(j + 2, s0_ref)
        return consume(j + 1, s1_ref, state)

    state = tuple((jnp.full((1, tq), -jnp.inf, F32), jnp.zeros((LANES, tq), F32)) for _ in range(2))
    scores(0, s0_ref)
    state = lax.fori_loop(0, nk // 2 - 1, body, state)
    scores(nk - 1, s1_ref)
    state = consume(nk - 2, s0_ref, state)
    state = consume(nk - 1, s1_ref, state)
    outs = [acc[0:MLA_V, :] / acc[MLA_V:MLA_V + 1, :] for _, acc in state]
    o_ref[...] = jnp.concatenate(outs, axis=0).T.astype(BF16)


def _attn_tile(T):
    return min(512, T // 4)


def _flash(q, k, vt, B, T):
    tv = _attn_tile(T)
    tq, tk = (2 * tv, 2 * tv) if T // tv >= 16 else (2 * tv, tv)
    nq = T // tq
    nk = T // tk
    assert nk % 2 == 0 and tk % tv == 0, (T, tk, tv)
    return pl.pallas_call(
        functools.partial(_flash_kernel, tk=tk, nk=nk),
        grid=(B, MLA_HEADS // 2, nq),
        in_specs=[pl.BlockSpec((tq, 2 * LANES), lambda b, hp, i: (b * nq + i, hp)),
                  pl.BlockSpec((T, 2 * LANES), lambda b, hp, i: (b, hp)),
                  pl.BlockSpec((T // tv, 2 * LANES, tv), lambda b, hp, i: (b, hp, 0))],
        out_specs=pl.BlockSpec((tq, LANES), lambda b, hp, i: (b * nq + i, hp)),
        out_shape=jax.ShapeDtypeStruct((B * T, MLA_HEADS * MLA_V), BF16),
        scratch_shapes=[pltpu.VMEM((2, tk, tq), BF16), pltpu.VMEM((2, tk, tq), BF16)],
        compiler_params=_cparams("parallel", "parallel", "arbitrary"),
        name="mla_flash",
    )(q, k, vt)


HALO = 16


def _rw_prep_kernel(h_ref, hp_ref, hn_ref, wrw_ref, mup_ref, mun_ref, kk_ref, ka_ref, rk_ref,
                    w0_ref, w2_ref, a0_ref, a2_ref, g2_ref, seg_ref, tri_ref,
                    v_out, kkt_out, rt_out, kh_out, bh_out, kw_out, bw_out, wt_out, bonus_out, g7_out,
                    h_scr, rw_scr, *, tm, npos):
    i = pl.program_id(0)
    h_scr[0:HALO, :] = jnp.where(i % npos != 0, hp_ref[...], jnp.zeros((), BF16))
    h_scr[HALO:HALO + tm, :] = h_ref[...]
    h_scr[HALO + tm:, :] = jnp.where(i % npos != npos - 1, hn_ref[...], jnp.zeros((), BF16))
    rb = tri_ref.shape[1]
    edges = [0] + [HALO + rb * (blk + 1) + HALO for blk in range(tm // rb - 1)] + [tm + 2 * HALO]
    mup = mup_ref[...]
    mun = mun_ref[...]
    mu0 = 1.0 - mup - mun
    seg = seg_ref[...]
    for blk in range(tm // rb):
        lo, hi = edges[blk], edges[blk + 1]
        rw_scr[lo:hi, :] = _dot(h_scr[lo:hi, :], wrw_ref[...])
        r0 = HALO + rb * blk
        rows = slice(rb * blk, rb * (blk + 1))
        rwf = rw_scr[r0:r0 + rb, :] * mu0 + mup * rw_scr[r0 - 1:r0 - 1 + rb, :] + mun * rw_scr[r0 + 1:r0 + 1 + rb, :]
        r7 = rwf[:, 0:RW_DIM]
        k7 = rwf[:, RW_DIM:2 * RW_DIM]
        v7 = rwf[:, 2 * RW_DIM:3 * RW_DIM]
        wl = rwf[:, 3 * RW_DIM:3 * RW_DIM + 2 * LORA]
        al = rwf[:, 3 * RW_DIM + 2 * LORA:3 * RW_DIM + 4 * LORA]
        gl = rwf[:, 3 * RW_DIM + 4 * LORA:]
        kx = k7 * kk_ref[...]
        kkn = kx * lax.rsqrt(jnp.maximum(_dot_0_1(kx * kx, seg), 1e-24))
        v_out[rows, :] = v7.astype(BF16)
        g7_out[rows, :] = _dot(_sigmoid(gl).astype(BF16), g2_ref[...])
        tw = jnp.tanh(wl).astype(BF16)
        alb = al.astype(BF16)
        bonus = jnp.zeros((rb, RW_DIM), F32)
        for d in range(2):
            z = -(w0_ref[d] + _dot(tw, w2_ref[d]))
            softplus = jnp.maximum(z, 0.0) + jnp.log(1.0 + jnp.exp(-jnp.abs(z)))
            lw = -jnp.exp(-softplus - 0.5)
            a = _sigmoid(a0_ref[d] + _dot(alb, a2_ref[d]))
            kd = k7 * (1.0 + (a - 1.0) * ka_ref[...])
            b = kkn * a
            bonus = bonus + _dot_0_1(r7 * kd * rk_ref[...], seg) * v7
            cum = _dot_0_1_l(tri_ref[d], lw)
            last = CHUNK - 1 if d == 0 else 0
            wtot_rows = [jnp.exp(cum[CHUNK * ci + last:CHUNK * ci + last + 1, :]) for ci in range(rb // CHUNK)]
            wtot = jnp.concatenate([jnp.broadcast_to(r, (CHUNK, RW_DIM)) for r in wtot_rows], axis=0)
            w_incl = jnp.exp(cum)
            w_excl = jnp.exp(cum - lw)
            w_inv = 1.0 / w_incl
            w_rest = wtot * w_inv
            kkt_out[d, rows, :] = (kkn * w_excl).astype(BF16)
            rt_out[d, rows, :] = (r7 * w_incl).astype(BF16)
            kh_out[d, rows, :] = (kd * w_inv).astype(BF16)
            bh_out[d, rows, :] = (b * w_inv).astype(BF16)
            kw_out[d, rows, :] = (kd * w_rest).astype(BF16)
            bw_out[d, rows, :] = (b * w_rest).astype(BF16)
            for ci in range(rb // CHUNK):
                wt_out[d, blk * (rb // CHUNK) + ci] = wtot_rows[ci]
        bonus_out[rows, :] = bonus


def _rw_prep(h2, T, w):
    ntok = h2.shape[0]
    tm = min(512, T)
    npos = T // tm
    nhalo = ntok // HALO
    rh = tm // HALO
    row = lambda i: (i, 0)
    drow = lambda i: (0, i, 0)
    dspec = pl.BlockSpec((2, tm, RW_DIM), drow)
    dshape = jax.ShapeDtypeStruct((2, ntok, RW_DIM), BF16)
    rb = min(256, tm)
    tri = _chunk_masks(rb)
    outs = pl.pallas_call(
        functools.partial(_rw_prep_kernel, tm=tm, npos=npos),
        grid=(ntok // tm,),
        in_specs=[pl.BlockSpec((tm, D_MODEL), row),
                  pl.BlockSpec((HALO, D_MODEL), lambda i: (jnp.maximum(i * rh - 1, 0), 0)),
                  pl.BlockSpec((HALO, D_MODEL), lambda i: (jnp.minimum((i + 1) * rh, nhalo - 1), 0)),
                  _full((D_MODEL, RW_COLS)), _full((1, RW_COLS)), _full((1, RW_COLS)),
                  _full((1, RW_DIM)), _full((1, RW_DIM)), _full((1, RW_DIM)),
                  _full((2, 1, RW_DIM)), _full((2, 2 * LORA, RW_DIM)), _full((2, 1, RW_DIM)),
                  _full((2, 2 * LORA, RW_DIM)), _full((GATE_LORA, RW_DIM)), _full((RW_DIM, RW_DIM)),
                  _full((2, rb, rb))],
        out_specs=[pl.BlockSpec((tm, RW_DIM), row), dspec, dspec, dspec, dspec, dspec, dspec,
                   pl.BlockSpec((2, tm // CHUNK, 1, RW_DIM), lambda i: (0, i, 0, 0)),
                   pl.BlockSpec((tm, RW_DIM), row), pl.BlockSpec((tm, RW_DIM), row)],
        out_shape=[jax.ShapeDtypeStruct((ntok, RW_DIM), BF16), dshape, dshape, dshape, dshape, dshape, dshape,
                   jax.ShapeDtypeStruct((2, ntok // CHUNK, 1, RW_DIM), F32),
                   jax.ShapeDtypeStruct((ntok, RW_DIM), F32), jax.ShapeDtypeStruct((ntok, RW_DIM), F32)],
        scratch_shapes=[pltpu.VMEM((tm + 2 * HALO, D_MODEL), BF16), pltpu.VMEM((tm + 2 * HALO, RW_COLS), F32)],
        compiler_params=_cparams("parallel"),
        name="rw_prep",
    )(h2, h2, h2, w["w_rw"], w["mu_prev"], w["mu_next"], w["k_k"], w["k_a"], w["r_k"],
      w["w0"], w["w2"], w["a0"], w["a2"], w["g2"], w["seg"], tri)
    return outs


def _chunk_masks(tm):
    t = np.arange(tm)
    same = (t[:, None] // CHUNK) == (t[None, :] // CHUNK)
    fwd = same & (t[None, :] <= t[:, None])
    bwd = same & (t[None, :] >= t[:, None])
    return jnp.asarray(np.stack([fwd, bwd]).astype(np.float32), BF16)


def _rw_scan_kernel(*refs):
    ins = refs[:16]
    yf_ref, yb_ref, s_ref = refs[16:]
    c = pl.program_id(1)

    @pl.when(c == 0)
    def _():
        s_ref[...] = jnp.zeros(s_ref.shape, F32)

    t_pos = lax.broadcasted_iota(jnp.int32, (CHUNK, GW), 0)
    s_pos = lax.broadcasted_iota(jnp.int32, (CHUNK, GW), 1) % CHUNK
    eye_c = (t_pos == s_pos).astype(F32)
    ri = lax.broadcasted_iota(jnp.int32, (GW, GW), 0)
    ci = lax.broadcasted_iota(jnp.int32, (GW, GW), 1)
    head_mask = (ri // CHUNK) == (ci // RW_HEAD)
    stack = lambda a: jnp.where(head_mask, jnp.concatenate([a] * GROUP, axis=0), jnp.zeros((), BF16))

    def prepare(orders):
        items = []
        for d, gi, order in ((d, gi, o) for d in range(2) for gi in range(RW_HEADS // GROUP) for o in orders):
            kkt, rt, kh, bh, kw, bw, v, wt = ins[8 * d:8 * d + 8]
            strict = (s_pos < t_pos) if d == 0 else (s_pos > t_pos)
            incl = (s_pos <= t_pos) if d == 0 else (s_pos >= t_pos)
            sub = order if d == 0 else SCAN_CHUNKS - 1 - order
            rows = slice(CHUNK * sub, CHUNK * (sub + 1))
            sl = slice(GW * gi, GW * (gi + 1))
            it = dict(y_ref=(yf_ref, yb_ref)[d], sl=sl, rows=rows, s_ref=s_ref.at[d, gi], wtot=wt[sub][:, sl],
                      order=order)
            v_c = v[rows, sl]
            kkt_rt = jnp.concatenate([kkt[rows, sl], rt[rows, sl]], axis=0)
            a_all = _dot_nt(kkt_rt, jnp.concatenate([stack(kh[rows, sl]), stack(bh[rows, sl])], axis=0))
            ak = jnp.where(strict, a_all[:CHUNK, :GW], 0.0).astype(BF16)
            bk = jnp.where(incl, a_all[CHUNK:, :GW], 0.0).astype(BF16)
            it.update(kkt_rt=kkt_rt, v=v_c, kbw=jnp.concatenate([kw[rows, sl], bw[rows, sl]], axis=0),
                      abk=jnp.concatenate([ak, bk], axis=0), bb=jnp.where(incl, a_all[CHUNK:, GW:], 0.0).astype(BF16))
            ab = jnp.where(strict, a_all[:CHUNK, GW:], 0.0)
            it.update(tinv=eye_c - ab, pw=ab.astype(BF16))
            items.append(it)
        for it in items:
            it["abk_v"] = _dot(it["abk"], stack(it["v"]))
        for it in items:
            it["pw"] = _dot(it["pw"], stack(it["pw"])).astype(BF16)
        for k in range(5):
            for it in items:
                sk = stack(it["pw"])
                if k < 4:
                    res = _dot(jnp.concatenate([it["pw"], it["tinv"].astype(BF16)], axis=0), sk)
                    it["pw"] = res[:CHUNK].astype(BF16)
                    it["tinv"] = it["tinv"] + res[CHUNK:]
                else:
                    it["tinv"] = (it["tinv"] + _dot(it["tinv"].astype(BF16), sk)).astype(BF16)
        return items

    def advance(chains):
        for ch in chains:
            sb = ch["s_ref"][...].astype(BF16)
            ch["zs"] = _dot_nt(ch["kkt_rt"], sb) + ch["abk_v"]
        for ch in chains:
            ch["u"] = (-_dot(ch["tinv"], stack(ch["zs"][:CHUNK].astype(BF16)))).astype(BF16)
        for ch in chains:
            ch["y_ref"][ch["rows"], ch["sl"]] = ch["zs"][CHUNK:] + _dot(ch["bb"], stack(ch["u"]))
        for ch in chains:
            upd = _dot_tn(jnp.concatenate([ch["v"], ch["u"]], axis=0), ch["kbw"])
            ch["s_ref"][...] = ch["s_ref"][...] * ch["wtot"] + jnp.where(head_mask, upd, 0.0)

    prepared = prepare(range(SCAN_CHUNKS))
    for order in range(SCAN_CHUNKS):
        advance([it for it in prepared if it["order"] == order])


def _rw_scan(v, kkt, rt, kh, bh, kw, bw, wt, B, T):
    ntok = B * T
    rows = SCAN_CHUNKS * CHUNK
    nb = T // rows
    in_specs = []
    args = []
    for d in range(2):
        if d == 0:
            cm = lambda b, c: b * nb + c
        else:
            cm = lambda b, c: b * nb + (nb - 1 - c)
        for arr in (kkt, rt, kh, bh, kw, bw):
            in_specs.append(pl.BlockSpec((None, rows, RW_DIM), lambda b, c, cm=cm, d=d: (d, cm(b, c), 0)))
            args.append(arr)
        in_specs.append(pl.BlockSpec((rows, RW_DIM), lambda b, c, cm=cm: (cm(b, c), 0)))
        args.append(v)
        in_specs.append(pl.BlockSpec((None, SCAN_CHUNKS, 1, RW_DIM), lambda b, c, cm=cm, d=d: (d, cm(b, c), 0, 0)))
        args.append(wt)
    return pl.pallas_call(
        _rw_scan_kernel,
        grid=(B, nb),
        in_specs=in_specs,
        out_specs=[pl.BlockSpec((rows, RW_DIM), lambda b, c: (b * nb + c, 0)),
                   pl.BlockSpec((rows, RW_DIM), lambda b, c: (b * nb + (nb - 1 - c), 0))],
        out_shape=[jax.ShapeDtypeStruct((ntok, RW_DIM), F32), jax.ShapeDtypeStruct((ntok, RW_DIM), F32)],
        scratch_shapes=[pltpu.VMEM((2, RW_HEADS // GROUP, GW, GW), F32)],
        compiler_params=_cparams("parallel", "arbitrary"),
        name="rw_scan",
    )(*args)


def _mem_kv_kernel(m_ref, g_ref, wk_ref, wv_ref, gk_ref, k_out, v_out):
    m = _rms(m_ref[...], g_ref[...]).astype(BF16)
    k = _dot(m, wk_ref[...])
    v_out[...] = _dot(m, wv_ref[...]).astype(BF16)
    for hd in range(X_HEADS):
        sl = slice(X_HEAD * hd, X_HEAD * (hd + 1))
        k_out[:, sl] = _rms(k[:, sl], gk_ref[...]).astype(BF16)


def _mem_kv(mem2, w):
    n = mem2.shape[0]
    row = lambda i: (i, 0)
    return pl.pallas_call(
        _mem_kv_kernel,
        grid=(n // N_MEM,),
        in_specs=[pl.BlockSpec((N_MEM, D_MODEL), row), _full((1, D_MODEL)), _full((D_MODEL, X_DIM)),
                  _full((D_MODEL, X_DIM)), _full((1, X_HEAD))],
        out_specs=[pl.BlockSpec((N_MEM, X_DIM), row), pl.BlockSpec((N_MEM, X_DIM), row)],
        out_shape=[jax.ShapeDtypeStruct((n, X_DIM), BF16), jax.ShapeDtypeStruct((n, X_DIM), BF16)],
        compiler_params=_cparams("parallel"),
        name="mem_kv",
    )(mem2, w["mem_norm_g"], w["w_mk"], w["w_mv"], w["x_kn_g"])


def _xattn_kernel(h_ref, wq_ref, gq_ref, mk_ref, mv_ref, o_ref):
    q = _dot(h_ref[...], wq_ref[...])
    for hd in range(X_HEADS):
        sl = slice(X_HEAD * hd, X_HEAD * (hd + 1))
        qh = _rms(q[:, sl], gq_ref[...]).astype(BF16)
        s = _dot_nt(qh, mk_ref[:, sl])
        p = jnp.exp(s - jnp.max(s, axis=-1, keepdims=True))
        o = _dot(p.astype(BF16), mv_ref[:, sl]) / jnp.sum(p, axis=-1, keepdims=True)
        o_ref[:, sl] = o.astype(BF16)


def _xattn(h2, mk, mv, B, T, w):
    tq = min(512, T)
    nq = T // tq
    return pl.pallas_call(
        _xattn_kernel,
        grid=(B, nq),
        in_specs=[pl.BlockSpec((tq, D_MODEL), lambda b, i: (b * nq + i, 0)),
                  _full((D_MODEL, X_DIM)), _full((1, X_HEAD)),
                  pl.BlockSpec((N_MEM, X_DIM), lambda b, i: (b, 0)), pl.BlockSpec((N_MEM, X_DIM), lambda b, i: (b, 0))],
        out_specs=pl.BlockSpec((tq, X_DIM), lambda b, i: (b * nq + i, 0)),
        out_shape=jax.ShapeDtypeStruct((B * T, X_DIM), BF16),
        compiler_params=_cparams("parallel", "parallel"),
        name="xattn",
    )(h2, w["w_xq"], w["x_qn_g_scaled"], mk, mv)


def _merge_kernel(x_ref, h_ref, oa_ref, yf_ref, yb_ref, bonus_ref, g7_ref, oc_ref, wg_ref, woa_ref, wob_ref,
                  woc_ref, wout_ref, lng_ref, lnb_ref, seg_ref, o_ref):
    h = h_ref[...]
    seg = seg_ref[...]
    gate = lambda bi: _sigmoid(_dot(h, wg_ref[:, D_MODEL * bi:D_MODEL * (bi + 1)]))
    y7 = yf_ref[...] + yb_ref[...] + bonus_ref[...]
    mu = _dot_0_1(y7, seg) * (1.0 / RW_HEAD)
    merged = gate(0) * _dot(oa_ref[...], woa_ref[...])
    dy = y7 - mu
    var = _dot_0_1(dy * dy, seg) * (1.0 / RW_HEAD)
    merged = merged + gate(2) * _dot(oc_ref[...], woc_ref[...])
    gate_b = gate(1)
    y7 = dy * lax.rsqrt(var + LNX_EPS) * lng_ref[...] + lnb_ref[...]
    merged = merged + gate_b * _dot((y7 * g7_ref[...]).astype(BF16), wob_ref[...])
    o_ref[...] = x_ref[...] + _dot(merged.astype(BF16), wout_ref[...])


def _merge(x2, h2, oa, yf, yb, bonus, g7, oc, T, w):
    ntok = x2.shape[0]
    tm = min(512, T)
    row = lambda i: (i, 0)
    half = lambda: pl.BlockSpec((tm, RW_DIM), row)
    return pl.pallas_call(
        _merge_kernel,
        grid=(ntok // tm,),
        in_specs=[pl.BlockSpec((tm, D_MODEL), row), pl.BlockSpec((tm, D_MODEL), row),
                  half(), half(), half(), half(), half(), half(),
                  _full((D_MODEL, 3 * D_MODEL)), _full((RW_DIM, D_MODEL)),
                  _full((RW_DIM, D_MODEL)), _full((RW_DIM, D_MODEL)), _full((D_MODEL, D_MODEL)),
                  _full((1, RW_DIM)), _full((1, RW_DIM)), _full((RW_DIM, RW_DIM))],
        out_specs=pl.BlockSpec((tm, D_MODEL), row),
        out_shape=jax.ShapeDtypeStruct((ntok, D_MODEL), F32),
        compiler_params=_cparams("parallel"),
        name="merge",
    )(x2, h2, oa, yf, yb, bonus, g7, oc, w["w_gate"], w["w_o_a"], w["w_o_b"], w["w_o_c"],
      w["w_out"], w["lnx_g"], w["lnx_b"], w["seg"])


MXU_TILE = 256
FFN_SPLITS = (0, 5 * MXU_TILE, D_FF)


def _ffn_kernel(x_ref, xp_ref, xn_ref, g_ref, wug_ref, wuv_ref, cw_ref, cb_ref, wd_ref, o_ref, h_scr, ug_scr,
                *, tm, npos):
    i = pl.program_id(0)
    not_first = (i % npos != 0).astype(F32)
    not_last = (i % npos != npos - 1).astype(F32)
    g = g_ref[...]
    h_scr[0:SUBLANES, :] = (_rms(xp_ref[...], g) * not_first).astype(BF16)
    h_scr[SUBLANES:SUBLANES + tm, :] = _rms(x_ref[...], g).astype(BF16)
    h_scr[SUBLANES + tm:, :] = (_rms(xn_ref[...], g) * not_last).astype(BF16)
    out = x_ref[...]
    for lo, hi in zip(FFN_SPLITS[:-1], FFN_SPLITS[1:]):
        n = hi - lo
        ug_scr[:, 0:n] = _dot(h_scr[...], wug_ref[:, lo:hi])
        uv = _dot(h_scr[SUBLANES:SUBLANES + tm, :], wuv_ref[:, lo:hi])
        cw = cw_ref[:, lo:hi]
        c = (cw[0:1] * ug_scr[SUBLANES - 1:SUBLANES - 1 + tm, 0:n] + cw[1:2] * ug_scr[SUBLANES:SUBLANES + tm, 0:n]
             + cw[2:3] * ug_scr[SUBLANES + 1:SUBLANES + 1 + tm, 0:n] + cb_ref[:, lo:hi])
        act = 0.5 * c * (1.0 + lax.erf(c * np.float32(1.0 / np.sqrt(2.0)))) * uv
        out = out + _dot(act.astype(BF16), wd_ref[lo:hi, :])
    o_ref[...] = out


def _ffn(x2, T, w):
    ntok = x2.shape[0]
    tm = min(512, T)
    fc = max(hi - lo for lo, hi in zip(FFN_SPLITS[:-1], FFN_SPLITS[1:]))
    npos = T // tm
    nblk8 = ntok // SUBLANES
    r8 = tm // SUBLANES
    row = lambda i: (i, 0)
    return pl.pallas_call(
        functools.partial(_ffn_kernel, tm=tm, npos=npos),
        grid=(ntok // tm,),
        in_specs=[pl.BlockSpec((tm, D_MODEL), row),
                  pl.BlockSpec((SUBLANES, D_MODEL), lambda i: (jnp.maximum(i * r8 - 1, 0), 0)),
                  pl.BlockSpec((SUBLANES, D_MODEL), lambda i: (jnp.minimum((i + 1) * r8, nblk8 - 1), 0)),
                  _full((1, D_MODEL)), _full((D_MODEL, D_FF)), _full((D_MODEL, D_FF)), _full((3, D_FF)),
                  _full((1, D_FF)), _full((D_FF, D_MODEL))],
        out_specs=pl.BlockSpec((tm, D_MODEL), row),
        out_shape=jax.ShapeDtypeStruct((ntok, D_MODEL), F32),
        scratch_shapes=[pltpu.VMEM((tm + 2 * SUBLANES, D_MODEL), BF16), pltpu.VMEM((tm + 2 * SUBLANES, fc), F32)],
        compiler_params=_cparams("parallel"),
        name="conv_ffn",
    )(x2, x2, x2, w["norm_ffn_g"], w["w_up_gate"], w["w_up_val"], w["conv_w"], w["conv_b"], w["w_down"])


def _prep_weights(p):
    w = {}
    row = lambda a: a.reshape(1, -1).astype(F32)
    w_in = p["w_in"]
    o = np.cumsum([0, Q_LORA, KV_LORA, MLA_ROPE, RW_COLS, X_DIM, 3 * D_MODEL])
    seg = lambda i: w_in[:, o[i]:o[i + 1]]
    zc = lambda n: jnp.zeros((D_MODEL, n), F32)
    w["w_c"] = jnp.concatenate([seg(0), seg(1), zc(MLA_NOPE), seg(2), zc(LANES - MLA_QK)], axis=1).astype(BF16)
    w["w_rw"] = seg(3).astype(BF16)
    w["w_xq"] = seg(4).astype(BF16)
    w["w_gate"] = seg(5).astype(BF16)
    w["norm_mix_g"] = row(p["norm_mix_g"])
    w["q_norm_g"] = row(p["q_norm_g"])
    w["kv_norm_g"] = row(p["kv_norm_g"])
    pad_slot = lambda a: jnp.pad(a, ((0, 0), (0, 0), (0, LANES - a.shape[-1]))).reshape(a.shape[0], -1)
    uq = p["w_uq"].reshape(Q_LORA, MLA_HEADS, MLA_QK)
    w["w_uq"] = pad_slot(uq).astype(BF16)
    half = MLA_ROPE // 2
    uq_rot = jnp.concatenate([jnp.zeros_like(uq[:, :, :MLA_NOPE]), -uq[:, :, MLA_NOPE + half:],
                              uq[:, :, MLA_NOPE:MLA_NOPE + half]], axis=-1)
    w["w_uq_rot"] = pad_slot(uq_rot).astype(BF16)
    ukv =p["w_ukv"].reshape(KV_LORA, MLA_HEADS, MLA_NOPE + MLA_V)
    w["w_uk"] = pad_slot(ukv[:, :, :MLA_NOPE]).astype(BF16)
    w["w_uv"] = pad_slot(ukv[:, :, MLA_NOPE:]).T.astype(BF16)
    w["mla_qn_g_scaled"] = p["mla_qn_g"].reshape(-1) * np.float32(MLA_QK ** -0.5 * np.log2(np.e))
    w["mla_kn_g"] = p["mla_kn_g"].reshape(-1)
    for name in ("mu_prev", "mu_next", "k_k", "k_a", "r_k", "lnx_g", "lnx_b", "mem_norm_g", "x_kn_g", "norm_ffn_g",
                 "conv_b"):
        w[name] = row(p[name])
    w["w0"] = jnp.stack([p["w0_f"], p["w0_b"]]).reshape(2, 1, RW_DIM)
    w["a0"] = jnp.stack([p["a0_f"], p["a0_b"]]).reshape(2, 1, RW_DIM)
    zl = jnp.zeros((LORA, RW_DIM), F32)
    w["w2"] = jnp.stack([jnp.concatenate([p["w2_f"], zl]), jnp.concatenate([zl, p["w2_b"]])]).astype(BF16)
    w["a2"] = jnp.stack([jnp.concatenate([p["a2_f"], zl]), jnp.concatenate([zl, p["a2_b"]])]).astype(BF16)
    w["g2"] = p["g2"].astype(BF16)
    hid = np.arange(RW_DIM) // RW_HEAD
    w["seg"] = jnp.asarray((hid[:, None] == hid[None, :]).astype(np.float32), BF16)
    mkv = p["w_mkv"].reshape(D_MODEL, X_HEADS, 2 * X_HEAD)
    w["w_mk"] = mkv[:, :, :X_HEAD].reshape(D_MODEL, X_DIM).astype(BF16)
    w["w_mv"] = mkv[:, :, X_HEAD:].reshape(D_MODEL, X_DIM).astype(BF16)
    w["x_qn_g_scaled"] = row(p["x_qn_g"]) * np.float32(X_HEAD ** -0.5)
    for name in ("w_o_a", "w_o_b", "w_o_c", "w_out", "w_down"):
        w[name] = p[name].astype(BF16)
    w["w_up_gate"] = p["w_up"][:, :D_FF].astype(BF16)
    w["w_up_val"] = p["w_up"][:, D_FF:].astype(BF16)
    w["conv_w"] = p["conv_w"].astype(F32)
    return w


def _rope_tables(T, gq, gk):
    half = MLA_ROPE // 2
    inv = jnp.power(ROPE_THETA, -jnp.arange(half, dtype=F32) / half)
    ang = jnp.arange(T, dtype=F32)[:, None] * inv[None, :]
    cos, sin = jnp.cos(ang), jnp.sin(ang)
    z = lambda n: jnp.zeros((T, n), F32)
    pad = z(LANES - MLA_QK)

    def own(g):
        g1, g2 = g[MLA_NOPE:MLA_NOPE + half], g[MLA_NOPE + half:]
        return jnp.concatenate([jnp.broadcast_to(g[:MLA_NOPE], (T, MLA_NOPE)), cos * g1, cos * g2, pad], axis=1)

    g1, g2 = gq[MLA_NOPE:MLA_NOPE + half], gq[MLA_NOPE + half:]
    qb = jnp.concatenate([z(MLA_NOPE), sin * g2, sin * g1, pad], axis=1)
    g1, g2 = gk[MLA_NOPE:MLA_NOPE + half], gk[MLA_NOPE + half:]
    kba = jnp.concatenate([z(MLA_NOPE), -sin * g2, z(half), pad], axis=1)
    kbb = jnp.concatenate([z(MLA_NOPE), z(half), sin * g1, pad], axis=1)
    return own(gq), qb, own(gk), kba, kbb


def _layer(x, mem, w):
    B, T, _ = x.shape
    x2 = x.reshape(B * T, D_MODEL)
    q, k, v, h2 = _mla_prep(x2, T, w, _rope_tables(T, w["mla_qn_g_scaled"], w["mla_kn_g"]))
    o_a = _flash(q, k, v, B, T)
    v7, kkt, rt, kh, bh, kw, bw, wt, bonus, g7 = _rw_prep(h2, T, w)
    y_f, y_b = _rw_scan(v7, kkt, rt, kh, bh, kw, bw, wt, B, T)
    mk, mv = _mem_kv(mem.reshape(B * N_MEM, D_MODEL), w)
    o_c = _xattn(h2, mk, mv, B, T, w)
    x1 = _merge(x2, h2, o_a, y_f, y_b, bonus, g7, o_c, T, w)
    return _ffn(x1, T, w).reshape(B, T, D_MODEL)


def kernel(x_prompt, x_sample, mem_prompt, mem_sample, norm_mix_g, w_in, q_norm_g, w_uq, kv_norm_g, w_ukv, mla_qn_g, mla_kn_g, w_o_a, mu_prev, mu_next, w0_f, w2_f, a0_f, a2_f, w0_b, w2_b, a0_b, a2_b, g2, k_k, k_a, r_k, lnx_g, lnx_b, w_o_b, mem_norm_g, w_mkv, x_qn_g, x_kn_g, w_o_c, w_out, norm_ffn_g, w_up, conv_w, conv_b, w_down):
    p = dict(norm_mix_g=norm_mix_g, w_in=w_in, q_norm_g=q_norm_g, w_uq=w_uq, kv_norm_g=kv_norm_g, w_ukv=w_ukv,
             mla_qn_g=mla_qn_g, mla_kn_g=mla_kn_g, w_o_a=w_o_a, mu_prev=mu_prev, mu_next=mu_next,
             w0_f=w0_f, w2_f=w2_f, a0_f=a0_f, a2_f=a2_f, w0_b=w0_b, w2_b=w2_b, a0_b=a0_b, a2_b=a2_b,
             g2=g2, k_k=k_k, k_a=k_a, r_k=r_k, lnx_g=lnx_g, lnx_b=lnx_b, w_o_b=w_o_b,
             mem_norm_g=mem_norm_g, w_mkv=w_mkv, x_qn_g=x_qn_g, x_kn_g=x_kn_g, w_o_c=w_o_c, w_out=w_out,
             norm_ffn_g=norm_ffn_g, w_up=w_up, conv_w=conv_w, conv_b=conv_b, w_down=w_down)
    w = _prep_weights({name: a[0] for name, a in p.items()})
    return (_layer(x_prompt, mem_prompt, w), _layer(x_sample, mem_sample, w))
```

```python
import functools

import numpy as np
import jax
import jax.numpy as jnp
from jax import lax
from jax.experimental import pallas as pl
from jax.experimental.pallas import tpu as pltpu

F32 = jnp.float32
BF16 = jnp.bfloat16

D_MODEL = 1024
RMS_EPS = 1e-6
N_MEM = 256
MLA_HEADS = 8
MLA_NOPE = 64
MLA_ROPE = 32
MLA_QK = MLA_NOPE + MLA_ROPE
MLA_V = 64
Q_LORA = 384
KV_LORA = 256
ROPE_THETA = 10000.0
RW_HEADS = 8
RW_HEAD = 64
RW_DIM = RW_HEADS * RW_HEAD
LORA = 64
GATE_LORA = 128
RW_COLS = 3 * RW_DIM + 4 * LORA + GATE_LORA
LNX_EPS = 64e-5
X_HEADS = 4
X_HEAD = 128
X_DIM = X_HEADS * X_HEAD
D_FF = 2816

LANES = 128
SUBLANES = 8
CHUNK = 64
GROUP = 2
GW = GROUP * RW_HEAD
SLOT_COLS = MLA_HEADS * LANES
MLA_C_COLS = Q_LORA + KV_LORA + LANES
SCAN_CHUNKS = 8
VMEM_LIMIT = 56 * 1024 * 1024


def _cparams(*sem):
    return pltpu.CompilerParams(dimension_semantics=sem, vmem_limit_bytes=VMEM_LIMIT)


def _rms(x, g, eps=RMS_EPS):
    return x * lax.rsqrt(jnp.mean(x * x, axis=-1, keepdims=True) + eps) * g


def _dot(a, b):
    return jnp.dot(a, b, preferred_element_type=F32)


def _dot_nt(a, b):
    return lax.dot_general(a, b, (((1,), (1,)), ((), ())), preferred_element_type=F32)


def _dot_tn(a, b):
    return lax.dot_general(a, b, (((0,), (0,)), ((), ())), preferred_element_type=F32)


def _dot_0_1(x, e):
    return _dot(x.astype(BF16), e)


def _dot_0_1_l(e, x):
    hi = x.astype(BF16)
    lo = (x - hi.astype(F32)).astype(BF16)
    return _dot(e, hi) + _dot(e, lo)


def _sigmoid(z):
    return 1.0 / (1.0 + jnp.exp(-z))


def _full(shape):
    nd = len(shape)
    return pl.BlockSpec(shape, lambda *_: (0,) * nd, pipeline_mode=pl.Buffered(1))


def _mla_prep_kernel(x_ref, g_ref, wc_ref, gq_ref, gkv_ref, wuq_ref, wuqr_ref, wuk_ref, wuv_ref,
                     qa_ref, qb_ref, ka_ref, kba_ref, kbb_ref, q_out, k_out, v_out, h_out):
    tm = x_ref.shape[0]
    half = MLA_ROPE // 2
    halves = [slice(0, tm // 2), slice(tm // 2, tm)]
    cs = []
    for r in halves:
        h = _rms(x_ref[r, :], g_ref[...]).astype(BF16)
        h_out[r, :] = h
        cs.append(_dot(h, wc_ref[...]))
    proj = []
    for c in cs:
        cq = _rms(c[:, :Q_LORA], gq_ref[...]).astype(BF16)
        ckv = _rms(c[:, Q_LORA:Q_LORA + KV_LORA], gkv_ref[...]).astype(BF16)
        proj.append((_dot(cq, wuq_ref[...]), _dot(cq, wuqr_ref[...]), _dot(ckv, wuk_ref[...]),
                     _dot_nt(wuv_ref[...], ckv)))
    for r, c, (q, qp, k, vt) in zip(halves, cs, proj):
        slot_row = lax.broadcasted_iota(jnp.int32, vt.shape, 0) % LANES
        v_out[0, :, r] = jnp.where(slot_row == MLA_V, 1.0, vt).astype(BF16)
        kr = c[:, Q_LORA + KV_LORA:]
        qa = qa_ref[r, :]
        qb = qb_ref[r, :]
        ka = ka_ref[r, :]
        kr_roped = (kr * ka + pltpu.roll(kr, LANES - half, 1) * kba_ref[r, :]
                    + pltpu.roll(kr, half, 1) * kbb_ref[r, :])
        kr_ss = jnp.sum(kr * kr, axis=-1, keepdims=True)
        for hd in range(MLA_HEADS):
            sl = slice(LANES * hd, LANES * (hd + 1))
            t = q[:, sl]
            ss = jnp.sum(t * t, axis=-1, keepdims=True) * (1.0 / MLA_QK)
            q_out[r, sl] = ((t * qa + qp[:, sl] * qb) * lax.rsqrt(ss + RMS_EPS)).astype(BF16)
            t = k[:, sl]
            ss = (jnp.sum(t * t, axis=-1, keepdims=True) + kr_ss) * (1.0 / MLA_QK)
            k_out[r, sl] = ((t * ka + kr_roped) * lax.rsqrt(ss + RMS_EPS)).astype(BF16)


def _mla_prep(x2, T, w, tabs):
    ntok = x2.shape[0]
    tm = _attn_tile(T)
    npos = T // tm
    row = lambda i: (i, 0)
    pos = lambda i: (i % npos, 0)
    return pl.pallas_call(
        _mla_prep_kernel,
        grid=(ntok // tm,),
        in_specs=[pl.BlockSpec((tm, D_MODEL), row), _full((1, D_MODEL)), _full((D_MODEL, MLA_C_COLS)),
                  _full((1, Q_LORA)), _full((1, KV_LORA)), _full((Q_LORA, SLOT_COLS)), _full((Q_LORA, SLOT_COLS)),
                  _full((KV_LORA, SLOT_COLS)), _full((SLOT_COLS, KV_LORA))] + [pl.BlockSpec((tm, LANES), pos)] * 5,
        out_specs=[pl.BlockSpec((tm, SLOT_COLS), row), pl.BlockSpec((tm, SLOT_COLS), row),
                   pl.BlockSpec((1, SLOT_COLS, tm), lambda i: (i, 0, 0)), pl.BlockSpec((tm, D_MODEL), row)],
        out_shape=[jax.ShapeDtypeStruct((ntok, SLOT_COLS), BF16), jax.ShapeDtypeStruct((ntok, SLOT_COLS), BF16),
                   jax.ShapeDtypeStruct((ntok // tm, SLOT_COLS, tm), BF16),
                   jax.ShapeDtypeStruct((ntok, D_MODEL), BF16)],
        compiler_params=_cparams("parallel"),
        name="mla_prep",
    )(x2, w["norm_mix_g"], w["w_c"], w["q_norm_g"], w["kv_norm_g"], w["w_uq"], w["w_uq_rot"], w["w_uk"],
      w["w_uv"], *tabs)


def _flash_kernel(q_ref, k_ref, vt_ref, o_ref, s0_ref, s1_ref, *, tk, nk):
    tq = q_ref.shape[0]
    slots = [slice(LANES * h, LANES * (h + 1)) for h in range(2)]
    per = tk // vt_ref.shape[2]

    def scores(j, s_ref):
        kblk = k_ref[pl.ds(pl.multiple_of(j * tk, tk), tk), :]
        for h in range(2):
            s_ref[h] = _dot_nt(kblk[:, slots[h]], q_ref[:, slots[h]]).astype(BF16)

    def consume(j, s_ref, state):
        vt = jnp.concatenate([vt_ref[per * j + r] for r in range(per)], axis=1)
        m_news = [jnp.maximum(state[h][0], jnp.max(s_ref[h], axis=0, keepdims=True).astype(F32)) for h in range(2)]
        pts = [jnp.exp2(s_ref[h] - m_news[h].astype(BF16)) for h in range(2)]
        return tuple((m_news[h], jnp.exp2(state[h][0] - m_news[h]) * state[h][1] + _dot(vt[slots[h], :], pts[h]))
                     for h in range(2))

    def body(t, state):
        j = 2 * t
        scores(j + 1, s1_ref)
        state = consume(j, s0_ref, state)
        scores(j + 2, s0_ref)
        return consume(j + 1, s1_ref, state)

    state = tuple((jnp.full((1, tq), -jnp.inf, F32), jnp.zeros((LANES, tq), F32)) for _ in range(2))
    scores(0, s0_ref)
    state = lax.fori_loop(0, nk // 2 - 1, body, state)
    scores(nk - 1, s1_ref)
    state = consume(nk - 2, s0_ref, state)
    state = consume(nk - 1, s1_ref, state)
    outs = [acc[0:MLA_V, :] / acc[MLA_V:MLA_V + 1, :] for _, acc in state]
    o_ref[...] = jnp.concatenate(outs, axis=0).T.astype(BF16)


def _attn_tile(T):
    return min(512, T // 4)


def _flash(q, k, vt, B, T):
    tv = _attn_tile(T)
    tq, tk = (4 * tv, 2 * tv) if T // tv >= 16 else (4 * tv, tv)
    nq = T // tq
    nk = T // tk
    assert nk % 2 == 0 and tk % tv == 0, (T, tk, tv)
    return pl.pallas_call(
        functools.partial(_flash_kernel, tk=tk, nk=nk),
        grid=(B, MLA_HEADS // 2, nq),
        in_specs=[pl.BlockSpec((tq, 2 * LANES), lambda b, hp, i: (b * nq + i, hp)),
                  pl.BlockSpec((T, 2 * LANES), lambda b, hp, i: (b, hp)),
                  pl.BlockSpec((T // tv, 2 * LANES, tv), lambda b, hp, i: (b, hp, 0))],
        out_specs=pl.BlockSpec((tq, LANES), lambda b, hp, i: (b * nq + i, hp)),
        out_shape=jax.ShapeDtypeStruct((B * T, MLA_HEADS * MLA_V), BF16),
        scratch_shapes=[pltpu.VMEM((2, tk, tq), BF16), pltpu.VMEM((2, tk, tq), BF16)],
        compiler_params=_cparams("parallel", "parallel", "arbitrary"),
        name="mla_flash",
    )(q, k, vt)


HALO = 16


def _rw_prep_kernel(h_ref, hp_ref, hn_ref, wrw_ref, mup_ref, mun_ref, kk_ref, ka_ref, rk_ref,
                    w0_ref, w2_ref, a0_ref, a2_ref, g2_ref, seg_ref, tri_ref,
                    v_out, kkt_out, rt_out, kh_out, bh_out, kw_out, bw_out, wt_out, bonus_out, g7_out,
                    h_scr, rw_scr, *, tm, npos):
    i = pl.program_id(0)
    h_scr[0:HALO, :] = jnp.where(i % npos != 0, hp_ref[...], jnp.zeros((), BF16))
    h_scr[HALO:HALO + tm, :] = h_ref[...]
    h_scr[HALO + tm:, :] = jnp.where(i % npos != npos - 1, hn_ref[...], jnp.zeros((), BF16))
    rb = tri_ref.shape[1]
    edges = [0] + [HALO + rb * (blk + 1) + HALO for blk in range(tm // rb - 1)] + [tm + 2 * HALO]
    mup = mup_ref[...]
    mun = mun_ref[...]
    mu0 = 1.0 - mup - mun
    seg = seg_ref[...]
    for blk in range(tm // rb):
        lo, hi = edges[blk], edges[blk + 1]
        rw_scr[lo:hi, :] = _dot(h_scr[lo:hi, :], wrw_ref[...])
        r0 = HALO + rb * blk
        rows = slice(rb * blk, rb * (blk + 1))
        rwf = rw_scr[r0:r0 + rb, :] * mu0 + mup * rw_scr[r0 - 1:r0 - 1 + rb, :] + mun * rw_scr[r0 + 1:r0 + 1 + rb, :]
        r7 = rwf[:, 0:RW_DIM]
        k7 = rwf[:, RW_DIM:2 * RW_DIM]
        v7 = rwf[:, 2 * RW_DIM:3 * RW_DIM]
        wl = rwf[:, 3 * RW_DIM:3 * RW_DIM + 2 * LORA]
        al = rwf[:, 3 * RW_DIM + 2 * LORA:3 * RW_DIM + 4 * LORA]
        gl = rwf[:, 3 * RW_DIM + 4 * LORA:]
        kx = k7 * kk_ref[...]
        kkn = kx * lax.rsqrt(jnp.maximum(_dot_0_1(kx * kx, seg), 1e-24))
        v_out[rows, :] = v7.astype(BF16)
        g7_out[rows, :] = _dot(_sigmoid(gl).astype(BF16), g2_ref[...])
        tw = jnp.tanh(wl).astype(BF16)
        alb = al.astype(BF16)
        bonus = jnp.zeros((rb, RW_DIM), F32)
        for d in range(2):
            z = -(w0_ref[d] + _dot(tw, w2_ref[d]))
            softplus = jnp.maximum(z, 0.0) + jnp.log(1.0 + jnp.exp(-jnp.abs(z)))
            lw = -jnp.exp(-softplus - 0.5)
            a = _sigmoid(a0_ref[d] + _dot(alb, a2_ref[d]))
            kd = k7 * (1.0 + (a - 1.0) * ka_ref[...])
            b = kkn * a
            bonus = bonus + _dot_0_1(r7 * kd * rk_ref[...], seg) * v7
            cum = _dot_0_1_l(tri_ref[d], lw)
            last = CHUNK - 1 if d == 0 else 0
            wtot_rows = [jnp.exp(cum[CHUNK * ci + last:CHUNK * ci + last + 1, :]) for ci in range(rb // CHUNK)]
            wtot = jnp.concatenate([jnp.broadcast_to(r, (CHUNK, RW_DIM)) for r in wtot_rows], axis=0)
            w_incl = jnp.exp(cum)
            w_excl = jnp.exp(cum - lw)
            w_inv = 1.0 / w_incl
            w_rest = wtot * w_inv
            kkt_out[d, rows, :] = (kkn * w_excl).astype(BF16)
            rt_out[d, rows, :] = (r7 * w_incl).astype(BF16)
            kh_out[d, rows, :] = (kd * w_inv).astype(BF16)
            bh_out[d, rows, :] = (b * w_inv).astype(BF16)
            kw_out[d, rows, :] = (kd * w_rest).astype(BF16)
            bw_out[d, rows, :] = (b * w_rest).astype(BF16)
            for ci in range(rb // CHUNK):
                wt_out[d, blk * (rb // CHUNK) + ci] = wtot_rows[ci]
        bonus_out[rows, :] = bonus


def _rw_prep(h2, T, w):
    ntok = h2.shape[0]
    tm = min(512, T)
    npos = T // tm
    nhalo = ntok // HALO
    rh = tm // HALO
    row = lambda i: (i, 0)
    drow = lambda i: (0, i, 0)
    dspec = pl.BlockSpec((2, tm, RW_DIM), drow)
    dshape = jax.ShapeDtypeStruct((2, ntok, RW_DIM), BF16)
    rb = min(256, tm)
    tri = _chunk_masks(rb)
    outs = pl.pallas_call(
        functools.partial(_rw_prep_kernel, tm=tm, npos=npos),
        grid=(ntok // tm,),
        in_specs=[pl.BlockSpec((tm, D_MODEL), row),
                  pl.BlockSpec((HALO, D_MODEL), lambda i: (jnp.maximum(i * rh - 1, 0), 0)),
                  pl.BlockSpec((HALO, D_MODEL), lambda i: (jnp.minimum((i + 1) * rh, nhalo - 1), 0)),
                  _full((D_MODEL, RW_COLS)), _full((1, RW_COLS)), _full((1, RW_COLS)),
                  _full((1, RW_DIM)), _full((1, RW_DIM)), _full((1, RW_DIM)),
                  _full((2, 1, RW_DIM)), _full((2, 2 * LORA, RW_DIM)), _full((2, 1, RW_DIM)),
                  _full((2, 2 * LORA, RW_DIM)), _full((GATE_LORA, RW_DIM)), _full((RW_DIM, RW_DIM)),
                  _full((2, rb, rb))],
        out_specs=[pl.BlockSpec((tm, RW_DIM), row), dspec, dspec, dspec, dspec, dspec, dspec,
                   pl.BlockSpec((2, tm // CHUNK, 1, RW_DIM), lambda i: (0, i, 0, 0)),
                   pl.BlockSpec((tm, RW_DIM), row), pl.BlockSpec((tm, RW_DIM), row)],
        out_shape=[jax.ShapeDtypeStruct((ntok, RW_DIM), BF16), dshape, dshape, dshape, dshape, dshape, dshape,
                   jax.ShapeDtypeStruct((2, ntok // CHUNK, 1, RW_DIM), F32),
                   jax.ShapeDtypeStruct((ntok, RW_DIM), F32), jax.ShapeDtypeStruct((ntok, RW_DIM), F32)],
        scratch_shapes=[pltpu.VMEM((tm + 2 * HALO, D_MODEL), BF16), pltpu.VMEM((tm + 2 * HALO, RW_COLS), F32)],
        compiler_params=_cparams("parallel"),
        name="rw_prep",
    )(h2, h2, h2, w["w_rw"], w["mu_prev"], w["mu_next"], w["k_k"], w["k_a"], w["r_k"],
      w["w0"], w["w2"], w["a0"], w["a2"], w["g2"], w["seg"], tri)
    return outs


def _chunk_masks(tm):
    t = np.arange(tm)
    same = (t[:, None] // CHUNK) == (t[None, :] // CHUNK)
    fwd = same & (t[None, :] <= t[:, None])
    bwd = same & (t[None, :] >= t[:, None])
    return jnp.asarray(np.stack([fwd, bwd]).astype(np.float32), BF16)


def _rw_scan_kernel(*refs):
    ins = refs[:16]
    yf_ref, yb_ref, s_ref = refs[16:]
    c = pl.program_id(1)

    @pl.when(c == 0)
    def _():
        s_ref[...] = jnp.zeros(s_ref.shape, F32)

    t_pos = lax.broadcasted_iota(jnp.int32, (CHUNK, GW), 0)
    s_pos = lax.broadcasted_iota(jnp.int32, (CHUNK, GW), 1) % CHUNK
    eye_c = (t_pos == s_pos).astype(F32)
    ri = lax.broadcasted_iota(jnp.int32, (GW, GW), 0)
    ci = lax.broadcasted_iota(jnp.int32, (GW, GW), 1)
    head_mask = (ri // CHUNK) == (ci // RW_HEAD)
    stack = lambda a: jnp.where(head_mask, jnp.concatenate([a] * GROUP, axis=0), jnp.zeros((), BF16))

    def prepare(orders):
        items = []
        for d, gi, order in ((d, gi, o) for d in range(2) for gi in range(RW_HEADS // GROUP) for o in orders):
            kkt, rt, kh, bh, kw, bw, v, wt = ins[8 * d:8 * d + 8]
            strict = (s_pos < t_pos) if d == 0 else (s_pos > t_pos)
            incl = (s_pos <= t_pos) if d == 0 else (s_pos >= t_pos)
            sub = order if d == 0 else SCAN_CHUNKS - 1 - order
            rows = slice(CHUNK * sub, CHUNK * (sub + 1))
            sl = slice(GW * gi, GW * (gi + 1))
            it = dict(y_ref=(yf_ref, yb_ref)[d], sl=sl, rows=rows, s_ref=s_ref.at[d, gi], wtot=wt[sub][:, sl],
                      order=order)
            v_c = v[rows, sl]
            kkt_rt = jnp.concatenate([kkt[rows, sl], rt[rows, sl]], axis=0)
            a_all = _dot_nt(kkt_rt, jnp.concatenate([stack(kh[rows, sl]), stack(bh[rows, sl])], axis=0))
            ak = jnp.where(strict, a_all[:CHUNK, :GW], 0.0).astype(BF16)
            bk = jnp.where(incl, a_all[CHUNK:, :GW], 0.0).astype(BF16)
            it.update(kkt_rt=kkt_rt, v=v_c, kbw=jnp.concatenate([kw[rows, sl], bw[rows, sl]], axis=0),
                      abk=jnp.concatenate([ak, bk], axis=0), bb=jnp.where(incl, a_all[CHUNK:, GW:], 0.0).astype(BF16))
            ab = jnp.where(strict, a_all[:CHUNK, GW:], 0.0)
            it.update(tinv=eye_c - ab, pw=ab.astype(BF16))
            items.append(it)
        for it in items:
            it["abk_v"] = _dot(it["abk"], stack(it["v"]))
        for it in items:
            it["pw"] = _dot(it["pw"], stack(it["pw"])).astype(BF16)
        for k in range(5):
            for it in items:
                sk = stack(it["pw"])
                if k < 4:
                    res = _dot(jnp.concatenate([it["pw"], it["tinv"].astype(BF16)], axis=0), sk)
                    it["pw"] = res[:CHUNK].astype(BF16)
                    it["tinv"] = it["tinv"] + res[CHUNK:]
                else:
                    it["tinv"] = (it["tinv"] + _dot(it["tinv"].astype(BF16), sk)).astype(BF16)
        return items

    def advance(chains):
        for ch in chains:
            sb = ch["s_ref"][...].astype(BF16)
            ch["zs"] = _dot_nt(ch["kkt_rt"], sb) + ch["abk_v"]
        for ch in chains:
            ch["u"] = (-_dot(ch["tinv"], stack(ch["zs"][:CHUNK].astype(BF16)))).astype(BF16)
        for ch in chains:
            ch["y_ref"][ch["rows"], ch["sl"]] = ch["zs"][CHUNK:] + _dot(ch["bb"], stack(ch["u"]))
        for ch in chains:
            upd = _dot_tn(jnp.concatenate([ch["v"], ch["u"]], axis=0), ch["kbw"])
            ch["s_ref"][...] = ch["s_ref"][...] * ch["wtot"] + jnp.where(head_mask, upd, 0.0)

    prepared = prepare(range(SCAN_CHUNKS))
    for order in range(SCAN_CHUNKS):
        advance([it for it in prepared if it["order"] == order])


def _rw_scan(v, kkt, rt, kh, bh, kw, bw, wt, B, T):
    ntok = B * T
    rows = SCAN_CHUNKS * CHUNK
    nb = T // rows
    in_specs = []
    args = []
    for d in range(2):
        if d == 0:
            cm = lambda b, c: b * nb + c
        else:
            cm = lambda b, c: b * nb + (nb - 1 - c)
        for arr in (kkt, rt, kh, bh, kw, bw):
            in_specs.append(pl.BlockSpec((None, rows, RW_DIM), lambda b, c, cm=cm, d=d: (d, cm(b, c), 0)))
            args.append(arr)
        in_specs.append(pl.BlockSpec((rows, RW_DIM), lambda b, c, cm=cm: (cm(b, c), 0)))
        args.append(v)
        in_specs.append(pl.BlockSpec((None, SCAN_CHUNKS, 1, RW_DIM), lambda b, c, cm=cm, d=d: (d, cm(b, c), 0, 0)))
        args.append(wt)
    return pl.pallas_call(
        _rw_scan_kernel,
        grid=(B, nb),
        in_specs=in_specs,
        out_specs=[pl.BlockSpec((rows, RW_DIM), lambda b, c: (b * nb + c, 0)),
                   pl.BlockSpec((rows, RW_DIM), lambda b, c: (b * nb + (nb - 1 - c), 0))],
        out_shape=[jax.ShapeDtypeStruct((ntok, RW_DIM), F32), jax.ShapeDtypeStruct((ntok, RW_DIM), F32)],
        scratch_shapes=[pltpu.VMEM((2, RW_HEADS // GROUP, GW, GW), F32)],
        compiler_params=_cparams("parallel", "arbitrary"),
        name="rw_scan",
    )(*args)


def _mem_kv_kernel(m_ref, g_ref, wk_ref, wv_ref, gk_ref, k_out, v_out):
    m = _rms(m_ref[...], g_ref[...]).astype(BF16)
    k = _dot(m, wk_ref[...])
    v_out[...] = _dot(m, wv_ref[...]).astype(BF16)
    for hd in range(X_HEADS):
        sl = slice(X_HEAD * hd, X_HEAD * (hd + 1))
        k_out[:, sl] = _rms(k[:, sl], gk_ref[...]).astype(BF16)


def _mem_kv(mem2, w):
    n = mem2.shape[0]
    row = lambda i: (i, 0)
    return pl.pallas_call(
        _mem_kv_kernel,
        grid=(n // N_MEM,),
        in_specs=[pl.BlockSpec((N_MEM, D_MODEL), row), _full((1, D_MODEL)), _full((D_MODEL, X_DIM)),
                  _full((D_MODEL, X_DIM)), _full((1, X_HEAD))],
        out_specs=[pl.BlockSpec((N_MEM, X_DIM), row), pl.BlockSpec((N_MEM, X_DIM), row)],
        out_shape=[jax.ShapeDtypeStruct((n, X_DIM), BF16), jax.ShapeDtypeStruct((n, X_DIM), BF16)],
        compiler_params=_cparams("parallel"),
        name="mem_kv",
    )(mem2, w["mem_norm_g"], w["w_mk"], w["w_mv"], w["x_kn_g"])


def _xattn_kernel(h_ref, wq_ref, gq_ref, mk_ref, mv_ref, o_ref):
    q = _dot(h_ref[...], wq_ref[...])
    for hd in range(X_HEADS):
        sl = slice(X_HEAD * hd, X_HEAD * (hd + 1))
        qh = _rms(q[:, sl], gq_ref[...]).astype(BF16)
        s = _dot_nt(qh, mk_ref[:, sl])
        p = jnp.exp(s - jnp.max(s, axis=-1, keepdims=True))
        o = _dot(p.astype(BF16), mv_ref[:, sl]) / jnp.sum(p, axis=-1, keepdims=True)
        o_ref[:, sl] = o.astype(BF16)


def _xattn(h2, mk, mv, B, T, w):
    tq = min(512, T)
    nq = T // tq
    return pl.pallas_call(
        _xattn_kernel,
        grid=(B, nq),
        in_specs=[pl.BlockSpec((tq, D_MODEL), lambda b, i: (b * nq + i, 0)),
                  _full((D_MODEL, X_DIM)), _full((1, X_HEAD)),
                  pl.BlockSpec((N_MEM, X_DIM), lambda b, i: (b, 0)), pl.BlockSpec((N_MEM, X_DIM), lambda b, i: (b, 0))],
        out_specs=pl.BlockSpec((tq, X_DIM), lambda b, i: (b * nq + i, 0)),
        out_shape=jax.ShapeDtypeStruct((B * T, X_DIM), BF16),
        compiler_params=_cparams("parallel", "parallel"),
        name="xattn",
    )(h2, w["w_xq"], w["x_qn_g_scaled"], mk, mv)


def _merge_kernel(x_ref, h_ref, oa_ref, yf_ref, yb_ref, bonus_ref, g7_ref, oc_ref, wg_ref, woa_ref, wob_ref,
                  woc_ref, wout_ref, lng_ref, lnb_ref, seg_ref, o_ref):
    h = h_ref[...]
    seg = seg_ref[...]
    gate = lambda bi: _sigmoid(_dot(h, wg_ref[:, D_MODEL * bi:D_MODEL * (bi + 1)]))
    y7 = yf_ref[...] + yb_ref[...] + bonus_ref[...]
    mu = _dot_0_1(y7, seg) * (1.0 / RW_HEAD)
    merged = gate(0) * _dot(oa_ref[...], woa_ref[...])
    dy = y7 - mu
    var = _dot_0_1(dy * dy, seg) * (1.0 / RW_HEAD)
    merged = merged + gate(2) * _dot(oc_ref[...], woc_ref[...])
    gate_b = gate(1)
    y7 = dy * lax.rsqrt(var + LNX_EPS) * lng_ref[...] + lnb_ref[...]
    merged = merged + gate_b * _dot((y7 * g7_ref[...]).astype(BF16), wob_ref[...])
    o_ref[...] = x_ref[...] + _dot(merged.astype(BF16), wout_ref[...])


def _merge(x2, h2, oa, yf, yb, bonus, g7, oc, T, w):
    ntok = x2.shape[0]
    tm = min(512, T)
    row = lambda i: (i, 0)
    half = lambda: pl.BlockSpec((tm, RW_DIM), row)
    return pl.pallas_call(
        _merge_kernel,
        grid=(ntok // tm,),
        in_specs=[pl.BlockSpec((tm, D_MODEL), row), pl.BlockSpec((tm, D_MODEL), row),
                  half(), half(), half(), half(), half(), half(),
                  _full((D_MODEL, 3 * D_MODEL)), _full((RW_DIM, D_MODEL)),
                  _full((RW_DIM, D_MODEL)), _full((RW_DIM, D_MODEL)), _full((D_MODEL, D_MODEL)),
                  _full((1, RW_DIM)), _full((1, RW_DIM)), _full((RW_DIM, RW_DIM))],
        out_specs=pl.BlockSpec((tm, D_MODEL), row),
        out_shape=jax.ShapeDtypeStruct((ntok, D_MODEL), F32),
        compiler_params=_cparams("parallel"),
        name="merge",
    )(x2, h2, oa, yf, yb, bonus, g7, oc, w["w_gate"], w["w_o_a"], w["w_o_b"], w["w_o_c"],
      w["w_out"], w["lnx_g"], w["lnx_b"], w["seg"])


MXU_TILE = 256
FFN_SPLITS = (0, 5 * MXU_TILE, D_FF)


def _ffn_kernel(x_ref, xp_ref, xn_ref, g_ref, wug_ref, wuv_ref, cw_ref, cb_ref, wd_ref, o_ref, h_scr, ug_scr,
                *, tm, npos):
    i = pl.program_id(0)
    not_first = (i % npos != 0).astype(F32)
    not_last = (i % npos != npos - 1).astype(F32)
    g = g_ref[...]
    h_scr[0:SUBLANES, :] = (_rms(xp_ref[...], g) * not_first).astype(BF16)
    h_scr[SUBLANES:SUBLANES + tm, :] = _rms(x_ref[...], g).astype(BF16)
    h_scr[SUBLANES + tm:, :] = (_rms(xn_ref[...], g) * not_last).astype(BF16)
    out = x_ref[...]
    for lo, hi in zip(FFN_SPLITS[:-1], FFN_SPLITS[1:]):
        n = hi - lo
        ug_scr[:, 0:n] = _dot(h_scr[...], wug_ref[:, lo:hi])
        uv = _dot(h_scr[SUBLANES:SUBLANES + tm, :], wuv_ref[:, lo:hi])
        cw = cw_ref[:, lo:hi]
        c = (cw[0:1] * ug_scr[SUBLANES - 1:SUBLANES - 1 + tm, 0:n] + cw[1:2] * ug_scr[SUBLANES:SUBLANES + tm, 0:n]
             + cw[2:3] * ug_scr[SUBLANES + 1:SUBLANES + 1 + tm, 0:n] + cb_ref[:, lo:hi])
        act = 0.5 * c * (1.0 + lax.erf(c * np.float32(1.0 / np.sqrt(2.0)))) * uv
        out = out + _dot(act.astype(BF16), wd_ref[lo:hi, :])
    o_ref[...] = out


def _ffn(x2, T, w):
    ntok = x2.shape[0]
    tm = min(512, T)
    fc = max(hi - lo for lo, hi in zip(FFN_SPLITS[:-1], FFN_SPLITS[1:]))
    npos = T // tm
    nblk8 = ntok // SUBLANES
    r8 = tm // SUBLANES
    row = lambda i: (i, 0)
    return pl.pallas_call(
        functools.partial(_ffn_kernel, tm=tm, npos=npos),
        grid=(ntok // tm,),
        in_specs=[pl.BlockSpec((tm, D_MODEL), row),
                  pl.BlockSpec((SUBLANES, D_MODEL), lambda i: (jnp.maximum(i * r8 - 1, 0), 0)),
                  pl.BlockSpec((SUBLANES, D_MODEL), lambda i: (jnp.minimum((i + 1) * r8, nblk8 - 1), 0)),
                  _full((1, D_MODEL)), _full((D_MODEL, D_FF)), _full((D_MODEL, D_FF)), _full((3, D_FF)),
                  _full((1, D_FF)), _full((D_FF, D_MODEL))],
        out_specs=pl.BlockSpec((tm, D_MODEL), row),
        out_shape=jax.ShapeDtypeStruct((ntok, D_MODEL), F32),
        scratch_shapes=[pltpu.VMEM((tm + 2 * SUBLANES, D_MODEL), BF16), pltpu.VMEM((tm + 2 * SUBLANES, fc), F32)],
        compiler_params=_cparams("parallel"),
        name="conv_ffn",
    )(x2, x2, x2, w["norm_ffn_g"], w["w_up_gate"], w["w_up_val"], w["conv_w"], w["conv_b"], w["w_down"])


def _prep_weights(p):
    w = {}
    row = lambda a: a.reshape(1, -1).astype(F32)
    w_in = p["w_in"]
    o = np.cumsum([0, Q_LORA, KV_LORA, MLA_ROPE, RW_COLS, X_DIM, 3 * D_MODEL])
    seg = lambda i: w_in[:, o[i]:o[i + 1]]
    zc = lambda n: jnp.zeros((D_MODEL, n), F32)
    w["w_c"] = jnp.concatenate([seg(0), seg(1), zc(MLA_NOPE), seg(2), zc(LANES - MLA_QK)], axis=1).astype(BF16)
    w["w_rw"] = seg(3).astype(BF16)
    w["w_xq"] = seg(4).astype(BF16)
    w["w_gate"] = seg(5).astype(BF16)
    w["norm_mix_g"] = row(p["norm_mix_g"])
    w["q_norm_g"] = row(p["q_norm_g"])
    w["kv_norm_g"] = row(p["kv_norm_g"])
    pad_slot = lambda a: jnp.pad(a, ((0, 0), (0, 0), (0, LANES - a.shape[-1]))).reshape(a.shape[0], -1)
    uq = p["w_uq"].reshape(Q_LORA, MLA_HEADS, MLA_QK)
    w["w_uq"] = pad_slot(uq).astype(BF16)
    half = MLA_ROPE // 2
    uq_rot = jnp.concatenate([jnp.zeros_like(uq[:, :, :MLA_NOPE]), -uq[:, :, MLA_NOPE + half:],
                              uq[:, :, MLA_NOPE:MLA_NOPE + half]], axis=-1)
    w["w_uq_rot"] = pad_slot(uq_rot).astype(BF16)
    ukv =p["w_ukv"].reshape(KV_LORA, MLA_HEADS, MLA_NOPE + MLA_V)
    w["w_uk"] = pad_slot(ukv[:, :, :MLA_NOPE]).astype(BF16)
    w["w_uv"] = pad_slot(ukv[:, :, MLA_NOPE:]).T.astype(BF16)
    w["mla_qn_g_scaled"] = p["mla_qn_g"].reshape(-1) * np.float32(MLA_QK ** -0.5 * np.log2(np.e))
    w["mla_kn_g"] = p["mla_kn_g"].reshape(-1)
    for name in ("mu_prev", "mu_next", "k_k", "k_a", "r_k", "lnx_g", "lnx_b", "mem_norm_g", "x_kn_g", "norm_ffn_g",
                 "conv_b"):
        w[name] = row(p[name])
    w["w0"] = jnp.stack([p["w0_f"], p["w0_b"]]).reshape(2, 1, RW_DIM)
    w["a0"] = jnp.stack([p["a0_f"], p["a0_b"]]).reshape(2, 1, RW_DIM)
    zl = jnp.zeros((LORA, RW_DIM), F32)
    w["w2"] = jnp.stack([jnp.concatenate([p["w2_f"], zl]), jnp.concatenate([zl, p["w2_b"]])]).astype(BF16)
    w["a2"] = jnp.stack([jnp.concatenate([p["a2_f"], zl]), jnp.concatenate([zl, p["a2_b"]])]).astype(BF16)
    w["g2"] = p["g2"].astype(BF16)
    hid = np.arange(RW_DIM) // RW_HEAD
    w["seg"] = jnp.asarray((hid[:, None] == hid[None, :]).astype(np.float32), BF16)
    mkv = p["w_mkv"].reshape(D_MODEL, X_HEADS, 2 * X_HEAD)
    w["w_mk"] = mkv[:, :, :X_HEAD].reshape(D_MODEL, X_DIM).astype(BF16)
    w["w_mv"] = mkv[:, :, X_HEAD:].reshape(D_MODEL, X_DIM).astype(BF16)
    w["x_qn_g_scaled"] = row(p["x_qn_g"]) * np.float32(X_HEAD ** -0.5)
    for name in ("w_o_a", "w_o_b", "w_o_c", "w_out", "w_down"):
        w[name] = p[name].astype(BF16)
    w["w_up_gate"] = p["w_up"][:, :D_FF].astype(BF16)
    w["w_up_val"] = p["w_up"][:, D_FF:].astype(BF16)
    w["conv_w"] = p["conv_w"].astype(F32)
    return w


def _rope_tables(T, gq, gk):
    half = MLA_ROPE // 2
    inv = jnp.power(ROPE_THETA, -jnp.arange(half, dtype=F32) / half)
    ang = jnp.arange(T, dtype=F32)[:, None] * inv[None, :]
    cos, sin = jnp.cos(ang), jnp.sin(ang)
    z = lambda n: jnp.zeros((T, n), F32)
    pad = z(LANES - MLA_QK)

    def own(g):
        g1, g2 = g[MLA_NOPE:MLA_NOPE + half], g[MLA_NOPE + half:]
        return jnp.concatenate([jnp.broadcast_to(g[:MLA_NOPE], (T, MLA_NOPE)), cos * g1, cos * g2, pad], axis=1)

    g1, g2 = gq[MLA_NOPE:MLA_NOPE + half], gq[MLA_NOPE + half:]
    qb = jnp.concatenate([z(MLA_NOPE), sin * g2, sin * g1, pad], axis=1)
    g1, g2 = gk[MLA_NOPE:MLA_NOPE + half], gk[MLA_NOPE + half:]
    kba = jnp.concatenate([z(MLA_NOPE), -sin * g2, z(half), pad], axis=1)
    kbb = jnp.concatenate([z(MLA_NOPE), z(half), sin * g1, pad], axis=1)
    return own(gq), qb, own(gk), kba, kbb


def _layer(x, mem, w):
    B, T, _ = x.shape
    x2 = x.reshape(B * T, D_MODEL)
    q, k, v, h2 = _mla_prep(x2, T, w, _rope_tables(T, w["mla_qn_g_scaled"], w["mla_kn_g"]))
    o_a = _flash(q, k, v, B, T)
    v7, kkt, rt, kh, bh, kw, bw, wt, bonus, g7 = _rw_prep(h2, T, w)
    y_f, y_b = _rw_scan(v7, kkt, rt, kh, bh, kw, bw, wt, B, T)
    mk, mv = _mem_kv(mem.reshape(B * N_MEM, D_MODEL), w)
    o_c = _xattn(h2, mk, mv, B, T, w)
    x1 = _merge(x2, h2, o_a, y_f, y_b, bonus, g7, o_c, T, w)
    return _ffn(x1, T, w).reshape(B, T, D_MODEL)


def kernel(x_prompt, x_sample, mem_prompt, mem_sample, norm_mix_g, w_in, q_norm_g, w_uq, kv_norm_g, w_ukv, mla_qn_g, mla_kn_g, w_o_a, mu_prev, mu_next, w0_f, w2_f, a0_f, a2_f, w0_b, w2_b, a0_b, a2_b, g2, k_k, k_a, r_k, lnx_g, lnx_b, w_o_b, mem_norm_g, w_mkv, x_qn_g, x_kn_g, w_o_c, w_out, norm_ffn_g, w_up, conv_w, conv_b, w_down):
    p = dict(norm_mix_g=norm_mix_g, w_in=w_in, q_norm_g=q_norm_g, w_uq=w_uq, kv_norm_g=kv_norm_g, w_ukv=w_ukv,
             mla_qn_g=mla_qn_g, mla_kn_g=mla_kn_g, w_o_a=w_o_a, mu_prev=mu_prev, mu_next=mu_next,
             w0_f=w0_f, w2_f=w2_f, a0_f=a0_f, a2_f=a2_f, w0_b=w0_b, w2_b=w2_b, a0_b=a0_b, a2_b=a2_b,
             g2=g2, k_k=k_k, k_a=k_a, r_k=r_k, lnx_g=lnx_g, lnx_b=lnx_b, w_o_b=w_o_b,
             mem_norm_g=mem_norm_g, w_mkv=w_mkv, x_qn_g=x_qn_g, x_kn_g=x_kn_g, w_o_c=w_o_c, w_out=w_out,
             norm_ffn_g=norm_ffn_g, w_up=w_up, conv_w=conv_w, conv_b=conv_b, w_down=w_down)
    w = _prep_weights({name: a[0] for name, a in p.items()})
    return (_layer(x_prompt, mem_prompt, w), _layer(x_sample, mem_sample, w))
```
